```python
import math
import jax
import jax.numpy as jnp
from jax import lax
import numpy as np

D_MODEL = 1024
BATCH = 16
SEQ = 4096
DEPTH = 4

GRID_W = 64
CTX_LEN = 256
D_RG = 1024
RG_HEADS = 16
RG_HEAD_DIM = D_RG // RG_HEADS
RG_CONV = 4
RG_PAD_L = 1
RG_PAD_R = 2
RG_C = 8.0
D_HG = 1024
HG_EXPAND = 128
HG_HEADS = D_HG // HG_EXPAND
HG_VDIM = D_HG // HG_HEADS
HG_CHUNK = 32
D_FF = 2816
FFN_CONV = 3
N_MOD = 6
EPS = 1e-6
O_RGX = 0
O_RGG = O_RGX + D_RG
O_HG = O_RGG + D_RG
O_GA = O_HG + 5 * D_HG
O_GB = O_GA + D_MODEL
IN_WIDTH = O_GB + D_MODEL

kernel_name = 'hybrid_rglru_hgrn2_convffn_prefix_dit'


def rms_norm(x, g):
    x32 = x.astype(jnp.float32)
    y = x32 * lax.rsqrt(jnp.mean(x32 * x32, axis=-1, keepdims=True) + EPS)
    return (y * g.astype(jnp.float32)).astype(x.dtype)


def dwconv(x, w, b, pad_l, pad_r):
    L = x.shape[1]
    xp = jnp.pad(x, ((0, 0), (pad_l, pad_r), (0, 0)))
    return b + sum(xp[:, j:j + L] * w[j] for j in range(w.shape[0]))


def to_col(t, rows):
    B, L = t.shape[:2]
    return t.reshape(B, rows, GRID_W, *t.shape[2:]).swapaxes(1, 2).reshape(B, L, *t.shape[2:])


def from_col(t, rows):
    B, L = t.shape[:2]
    return t.reshape(B, GRID_W, rows, *t.shape[2:]).swapaxes(1, 2).reshape(B, L, *t.shape[2:])


def linear_scan(a, b, h0, reverse):
    if reverse:
        a, b = jnp.flip(a, 1), jnp.flip(b, 1)
    comb = lambda l, r: (l[0] * r[0], r[0] * l[1] + r[1])
    A, Bc = lax.associative_scan(comb, (a, b), axis=1)
    h = A * h0[:, None] + Bc
    final = h[:, -1]
    if reverse:
        h = jnp.flip(h, 1)
    return h, final


def rglru_coeffs(x, wa, ba, wx, bx, lam):
    B, L, _ = x.shape
    xh = x.reshape(B, L, RG_HEADS, RG_HEAD_DIM)
    r = jax.nn.sigmoid(jnp.einsum('blhi,hij->blhj', xh, wa).reshape(B, L, D_RG) + ba)
    i = jax.nn.sigmoid(jnp.einsum('blhi,hij->blhj', xh, wx).reshape(B, L, D_RG) + bx)
    log_a = -RG_C * jax.nn.softplus(-lam) * r
    a = jnp.exp(log_a)
    b = jnp.sqrt(-jnp.expm1(2.0 * log_a)) * (i * x)
    return a, b


def hgrn2_scan(q, logf, k, v, s0):
    B, L = q.shape[:2]
    n = L // HG_CHUNK
    chunks = lambda t: t.reshape(B, n, HG_CHUNK, *t.shape[2:]).swapaxes(0, 1)
    tri = jnp.tril(jnp.ones((HG_CHUNK, HG_CHUNK), dtype=bool))[None, :, :, None, None]

    def step(S, inp):
        qc, gc, kc, vc = inp
        bcum = jnp.cumsum(gc, axis=1)
        o_inter = jnp.einsum('bchk,bhkv->bchv', qc * jnp.exp(bcum), S)
        diff = bcum[:, :, None] - bcum[:, None, :]
        w = jnp.exp(jnp.where(tri, diff, -jnp.inf))
        A = jnp.einsum('bthk,btshk,bshk->bhts', qc, w, kc)
        o_intra = jnp.einsum('bhts,bshv->bthv', A, vc)
        blast = bcum[:, -1]
        S_new = jnp.exp(blast)[..., None] * S + jnp.einsum('bshk,bshv->bhkv', kc * jnp.exp(blast[:, None] - bcum), vc)
        return S_new.astype(S.dtype), o_inter + o_intra

    S_fin, o = lax.scan(step, s0, (chunks(q), chunks(logf), chunks(k), chunks(v)))
    return o.swapaxes(0, 1).reshape(B, L, HG_HEADS, HG_VDIM), S_fin


def hgrn2_dir(q, logf, k, v, s0, reverse):
    if reverse:
        q, logf, k, v = (jnp.flip(t, 1) for t in (q, logf, k, v))
    o, S = hgrn2_scan(q, logf, k, v, s0)
    if reverse:
        o = jnp.flip(o, 1)
    return o, S


def hg_features(zb, lb):
    B, L, _ = zb.shape
    hs = lambda t: t.reshape(B, L, HG_HEADS, HG_EXPAND)
    lb = lb.reshape(HG_HEADS, HG_EXPAND).astype(zb.dtype)
    q = hs(jax.nn.silu(zb[..., 0:D_HG]))
    dirs = []
    for j in (1, 2):
        zf = hs(zb[..., j * D_HG:(j + 1) * D_HG])
        logf = jnp.logaddexp(jnp.log(lb), jnp.log1p(-lb) + jax.nn.log_sigmoid(zf))
        k = (1.0 - lb) * jax.nn.sigmoid(-zf)
        dirs.append((logf, k))
    v = zb[..., 3 * D_HG:4 * D_HG].reshape(B, L, HG_HEADS, HG_VDIM)
    og = zb[..., 4 * D_HG:5 * D_HG]
    return q, dirs, v, og


def hg_out(o, og, gain):
    B, L = o.shape[:2]
    return rms_norm(o, gain.reshape(HG_HEADS, HG_VDIM)).reshape(B, L, D_HG) * jax.nn.silu(og)


def merge(z, r, y_hg, lp):
    y_rg = jax.nn.gelu(z[..., O_RGG:O_RGG + D_RG]) * r
    p_rg = y_rg @ lp['w_proj_rg']
    p_hg = y_hg @ lp['w_proj_hg']
    m = jax.nn.sigmoid(z[..., O_GA:O_GA + D_MODEL]) * p_rg + jax.nn.sigmoid(z[..., O_GB:O_GB + D_MODEL]) * p_hg
    return m @ lp['w_out']


def mixer(hc, hl, lp, lb, rows, need_ctx):
    B = hl.shape[0]
    zc = hc @ lp['w_in']
    zl = hl @ lp['w_in']
    xc = dwconv(zc[..., O_RGX:O_RGX + D_RG], lp['rg_conv_w'], lp['rg_conv_b'], RG_PAD_L, RG_PAD_R)
    xl = dwconv(zl[..., O_RGX:O_RGX + D_RG], lp['rg_conv_w'], lp['rg_conv_b'], RG_PAD_L, RG_PAD_R)
    rc, rl = 0.0, 0.0
    for d in range(2):
        prm = (lp['rg_wa'][d], lp['rg_ba'][d], lp['rg_wx'][d], lp['rg_bx'][d], lp['rg_lam'][d])
        rev = d == 1
        hcd, st = linear_scan(*rglru_coeffs(xc, *prm), jnp.zeros((B, D_RG), xc.dtype), rev)
        hld, _ = linear_scan(*rglru_coeffs(xl, *prm), st, rev)
        rc = rc + hcd
        rl = rl + hld
    qc, dirs_c, vc, ogc = hg_features(zc[..., O_HG:O_HG + 5 * D_HG], lb)
    ql, dirs_l, vl, ogl = hg_features(to_col(zl[..., O_HG:O_HG + 5 * D_HG], rows), lb)
    oc, ol = 0.0, 0.0
    for d in range(2):
        rev = d == 1
        s0 = jnp.zeros((B, HG_HEADS, HG_EXPAND, HG_VDIM), vc.dtype)
        ocd, st = hgrn2_dir(qc, dirs_c[d][0], dirs_c[d][1], vc, s0, rev)
        old, _ = hgrn2_dir(ql, dirs_l[d][0], dirs_l[d][1], vl, st, rev)
        oc = oc + ocd
        ol = ol + old
    yl = merge(zl, rl, from_col(hg_out(ol, ogl, lp['hg_out_norm']), rows), lp)
    yc = merge(zc, rc, hg_out(oc, ogc, lp['hg_out_norm']), lp) if need_ctx else None
    return yc, yl


def conv_ffn(h, w_up, cw, cb, w_down, rows):
    B, L, _ = h.shape
    u = h @ w_up
    g, v = u[..., :D_FF], u[..., D_FF:]
    if rows is None:
        g = dwconv(g, cw, cb, 1, 1)
    else:
        g = dwconv(g.reshape(B * rows, GRID_W, D_FF), cw, cb, 1, 1).reshape(B, L, D_FF)
    return (jax.nn.silu(g) * v) @ w_down


def setup_inputs(seed: int = 0) -> dict:
    key = jax.random.key(seed)
    ks = iter(jax.random.split(key, 40))
    f32 = jnp.float32
    nrm = lambda shape, fan_in: jax.random.normal(next(ks), shape, f32) * fan_in ** -0.5
    small = lambda shape: 0.01 * jax.random.normal(next(ks), shape, f32)
    gain = lambda shape: 1.0 + 0.02 * jax.random.normal(next(ks), shape, f32)
    x = jax.random.normal(next(ks), (BATCH, SEQ, D_MODEL), f32)
    c = jax.random.normal(next(ks), (BATCH, D_MODEL), f32)
    ctx = jax.random.normal(next(ks), (BATCH, CTX_LEN, D_MODEL), f32)
    c_ctx = jax.random.normal(next(ks), (D_MODEL,), f32)
    w_ada = 0.5 * nrm((DEPTH, D_MODEL, N_MOD * D_MODEL), D_MODEL)
    b_ada = small((DEPTH, N_MOD * D_MODEL))
    g_pre_mix = gain((DEPTH, D_MODEL))
    g_post_mix = gain((DEPTH, D_MODEL))
    g_pre_ffn = gain((DEPTH, D_MODEL))
    g_post_ffn = gain((DEPTH, D_MODEL))
    w_in = nrm((DEPTH, D_MODEL, IN_WIDTH), D_MODEL)
    rg_conv_w = nrm((DEPTH, RG_CONV, D_RG), RG_CONV)
    rg_conv_b = small((DEPTH, D_RG))
    rg_wa = nrm((DEPTH, 2, RG_HEADS, RG_HEAD_DIM, RG_HEAD_DIM), RG_HEAD_DIM)
    rg_ba = small((DEPTH, 2, D_RG))
    rg_wx = nrm((DEPTH, 2, RG_HEADS, RG_HEAD_DIM, RG_HEAD_DIM), RG_HEAD_DIM)
    rg_bx = small((DEPTH, 2, D_RG))
    u = jax.random.uniform(next(ks), (DEPTH, 2, D_RG), f32, minval=0.9, maxval=0.999)
    a0 = u ** (1.0 / RG_C)
    rg_lam = jnp.log(a0) - jnp.log1p(-a0)
    hg_lb_logits = 0.1 * jax.random.normal(next(ks), (DEPTH, D_HG), f32)
    hg_out_norm = gain((DEPTH, D_HG))
    w_proj_rg = nrm((DEPTH, D_RG, D_MODEL), D_RG)
    w_proj_hg = nrm((DEPTH, D_HG, D_MODEL), D_HG)
    w_out = nrm((DEPTH, D_MODEL, D_MODEL), D_MODEL)
    ffn_w_up = nrm((DEPTH, D_MODEL, 2 * D_FF), D_MODEL)
    ffn_conv_w = nrm((DEPTH, FFN_CONV, D_FF), FFN_CONV)
    ffn_conv_b = small((DEPTH, D_FF))
    ffn_w_down = nrm((DEPTH, D_FF, D_MODEL), D_FF)
    return {'x': x, 'c': c, 'ctx': ctx, 'c_ctx': c_ctx, 'w_ada': w_ada, 'b_ada': b_ada,
            'g_pre_mix': g_pre_mix, 'g_post_mix': g_post_mix, 'g_pre_ffn': g_pre_ffn, 'g_post_ffn': g_post_ffn,
            'w_in': w_in, 'rg_conv_w': rg_conv_w, 'rg_conv_b': rg_conv_b, 'rg_wa': rg_wa, 'rg_ba': rg_ba,
            'rg_wx': rg_wx, 'rg_bx': rg_bx, 'rg_lam': rg_lam, 'hg_lb_logits': hg_lb_logits,
            'hg_out_norm': hg_out_norm, 'w_proj_rg': w_proj_rg, 'w_proj_hg': w_proj_hg, 'w_out': w_out,
            'ffn_w_up': ffn_w_up, 'ffn_conv_w': ffn_conv_w, 'ffn_conv_b': ffn_conv_b, 'ffn_w_down': ffn_w_down}


def reference(x, c, ctx, c_ctx, w_ada, b_ada, g_pre_mix, g_post_mix, g_pre_ffn, g_post_ffn,
              w_in, rg_conv_w, rg_conv_b, rg_wa, rg_ba, rg_wx, rg_bx, rg_lam, hg_lb_logits,
              hg_out_norm, w_proj_rg, w_proj_hg, w_out, ffn_w_up, ffn_conv_w, ffn_conv_b, ffn_w_down):
    B, L, _ = x.shape
    rows = L // GRID_W
    p = jax.nn.softmax(hg_lb_logits.astype(jnp.float32), axis=0)
    cum = jnp.cumsum(p, axis=0)
    lb_all = cum - cum[0:1]
    sc = jax.nn.silu(c)
    scc = jax.nn.silu(c_ctx)
    xl, xc = x, ctx
    for l in range(DEPTH):
        need_ctx = l < DEPTH - 1
        ml = jnp.split((sc @ w_ada[l] + b_ada[l])[:, None, :], N_MOD, axis=-1)
        mc = jnp.split(scc @ w_ada[l] + b_ada[l], N_MOD, axis=-1)
        lp = dict(w_in=w_in[l], rg_conv_w=rg_conv_w[l], rg_conv_b=rg_conv_b[l], rg_wa=rg_wa[l],
                  rg_ba=rg_ba[l], rg_wx=rg_wx[l], rg_bx=rg_bx[l], rg_lam=rg_lam[l],
                  hg_out_norm=hg_out_norm[l], w_proj_rg=w_proj_rg[l], w_proj_hg=w_proj_hg[l], w_out=w_out[l])
        hl = rms_norm(xl, g_pre_mix[l]) * (1.0 + ml[1]) + ml[0]
        hc = rms_norm(xc, g_pre_mix[l]) * (1.0 + mc[1]) + mc[0]
        yc, yl = mixer(hc, hl, lp, lb_all[l], rows, need_ctx)
        xl = xl + ml[2] * rms_norm(yl, g_post_mix[l])
        hl = rms_norm(xl, g_pre_ffn[l]) * (1.0 + ml[4]) + ml[3]
        xl = xl + ml[5] * rms_norm(conv_ffn(hl, ffn_w_up[l], ffn_conv_w[l], ffn_conv_b[l], ffn_w_down[l], rows), g_post_ffn[l])
        if need_ctx:
            xc = xc + mc[2] * rms_norm(yc, g_post_mix[l])
            hc = rms_norm(xc, g_pre_ffn[l]) * (1.0 + mc[4]) + mc[3]
            xc = xc + mc[5] * rms_norm(conv_ffn(hc, ffn_w_up[l], ffn_conv_w[l], ffn_conv_b[l], ffn_w_down[l], None), g_post_ffn[l])
    return xl
```

```python
import functools

import jax
import jax.numpy as jnp
from jax import lax
from jax.experimental import pallas as pl
from jax.experimental.pallas import tpu as pltpu

GRID_W = 64
RG_HEADS = 16
RG_C = 8.0
HG_EXPAND = 128
N_MOD = 6
EPS = 1e-6
RG_CONV_PAD_L = 1
RG_CONV_PAD_R = 2

HG_CHUNK = 128
HG_DIAG = 8
HG_DIAG_CLAMP = 20.0
RG_COLS = 256
VMEM_LIMIT = 56 * 1024 * 1024

F32 = jnp.float32
BF16 = jnp.bfloat16


def _cparams(sem):
    return pltpu.CompilerParams(dimension_semantics=sem, vmem_limit_bytes=VMEM_LIMIT)


def _sigmoid(x):
    return jax.nn.sigmoid(x)


def _silu(x):
    return x * jax.nn.sigmoid(x)


def _gelu_tanh(x):
    c = 0.7978845608028654
    return 0.5 * x * (1.0 + jnp.tanh(c * (x + 0.044715 * (x * x * x))))


def _rms(x, g):
    ms = jnp.mean(x * x, axis=-1, keepdims=True)
    return x * lax.rsqrt(ms + EPS) * g


def _per_batch(x, vec, nb):
    r, d = x.shape
    return x.reshape(r // nb, nb, d), vec[None]


def _norm_mod(x, g, shift, scale, nb):
    y = _rms(x, g)
    y3, sc = _per_batch(y, scale, nb)
    h = y3 * (1.0 + sc) + shift[None]
    return h.reshape(x.shape)


def _ada_kernel(c_ref, w_ref, b_ref, o_ref):
    s = _silu(c_ref[...]).astype(BF16)
    o_ref[0] = jnp.dot(s, w_ref[0], preferred_element_type=F32) + b_ref[0]


def _ada_mod(cc, w_ada, b_ada):
    depth, d, n = w_ada.shape
    tn = 1536 if n % 1536 == 0 else n
    return pl.pallas_call(
        _ada_kernel,
        out_shape=jax.ShapeDtypeStruct((depth, cc.shape[0], n), F32),
        grid=(depth, n // tn),
        in_specs=[pl.BlockSpec(cc.shape, lambda l, j: (0, 0)),
                  pl.BlockSpec((1, d, tn), lambda l, j: (l, 0, j)),
                  pl.BlockSpec((1, 1, tn), lambda l, j: (l, 0, j))],
        out_specs=pl.BlockSpec((1, cc.shape[0], tn), lambda l, j: (l, 0, j)),
        compiler_params=_cparams(("parallel", "parallel")),
        name="ada_mod",
    )(cc, w_ada, b_ada.reshape(depth, 1, n))


def _inproj_kernel(x_ref, mod_ref, g_ref, w_ref, o_ref, h_scr, *, nb, d):
    @pl.when(pl.program_id(1) == 0)
    def _():
        h = _norm_mod(x_ref[...], g_ref[...], mod_ref[:, 0:d], mod_ref[:, d:2 * d], nb)
        h_scr[...] = h.astype(BF16)

    res = jnp.dot(h_scr[...], w_ref[...], preferred_element_type=F32)
    o_ref[...] = res.reshape(o_ref.shape).astype(o_ref.dtype)


def _inproj(x, mod, g, w, col_groups, out_shape, out_spec, tm, nb):
    n, d = x.shape

    def wmap(i, j):
        idx = jnp.int32(col_groups[0])
        for k in range(1, len(col_groups)):
            idx = jnp.where(j == k, jnp.int32(col_groups[k]), idx)
        return (0, idx)

    return pl.pallas_call(
        functools.partial(_inproj_kernel, nb=nb, d=d),
        out_shape=out_shape,
        grid=(n // tm, len(col_groups)),
        in_specs=[pl.BlockSpec((tm, d), lambda i, j: (i, 0)),
                  pl.BlockSpec(mod.shape, lambda i, j: (0, 0)),
                  pl.BlockSpec((1, d), lambda i, j: (0, 0)),
                  pl.BlockSpec((d, d), wmap)],
        out_specs=out_spec,
        scratch_shapes=[pltpu.VMEM((tm, d), BF16)],
        compiler_params=_cparams(("parallel", "arbitrary")),
        name="inproj",
    )(x, mod, g, w)


def _neg_expm1(y):
    small = y * (1.0 + y * (1.0 / 2.0) * (1.0 + y * (1.0 / 3.0) * (1.0 + y * (1.0 / 4.0) * (
        1.0 + y * (1.0 / 5.0) * (1.0 + y * (1.0 / 6.0) * (1.0 + y * (1.0 / 7.0)))))))
    return -jnp.where(y > -0.1, small, jnp.exp(y) - 1.0)


def _rg_kernel(*refs, reverse, nb, tt, nt):
    if reverse:
        (x_ref, xp_ref, xn_ref, cw_ref, cb_ref, wg_ref, bg_ref, lam_ref, h0_ref, hf_ref, gg_ref,
         y_ref, hfin_ref, a_scr, b_scr, h_scr) = refs
    else:
        (x_ref, xp_ref, xn_ref, cw_ref, cb_ref, wg_ref, bg_ref, lam_ref, h0_ref,
         y_ref, hfin_ref, a_scr, b_scr, h_scr) = refs
    s = pl.program_id(1)
    tile = (nt - 1 - s) if reverse else s
    rows = tt * nb
    cols = x_ref.shape[1]

    @pl.when(s == 0)
    def _():
        h_scr[...] = h0_ref[...]

    prev = jnp.where(tile > 0, xp_ref[...], 0.0)
    nxt = jnp.where(tile < nt - 1, xn_ref[...], 0.0)
    xe = jnp.concatenate([prev, x_ref[...], nxt], axis=0)
    xl = cb_ref[...] + xe[0:rows] * cw_ref[0:1, :]
    for j in range(1, 4):
        xl = xl + xe[j * nb:j * nb + rows] * cw_ref[j:j + 1, :]

    gates = jnp.dot(xl.astype(BF16), wg_ref[0], preferred_element_type=F32) + bg_ref[0]
    r = _sigmoid(gates[:, :cols])
    ig = _sigmoid(gates[:, cols:])
    log_a = (-RG_C * jax.nn.softplus(-lam_ref[...])) * r
    a_scr[...] = jnp.exp(log_a)
    b_scr[...] = jnp.sqrt(_neg_expm1(2.0 * log_a)) * (ig * xl)

    def body(t, h):
        tloc = (tt - 1 - t) if reverse else t
        off = pl.multiple_of(tloc * nb, nb)
        h = a_scr[pl.ds(off, nb), :] * h + b_scr[pl.ds(off, nb), :]
        b_scr[pl.ds(off, nb), :] = h
        return h

    h = lax.fori_loop(0, tt, body, h_scr[...], unroll=8)
    h_scr[...] = h
    hfin_ref[...] = h
    if reverse:
        y_ref[...] = (_gelu_tanh(gg_ref[...]) * (hf_ref[...] + b_scr[...])).astype(y_ref.dtype)
    else:
        y_ref[...] = b_scr[...]


def _rg_dir(zrm, hf, conv_w, conv_b, wg, bg, lam, h0, *, reverse, nb, tt, d):
    n = zrm.shape[0]
    nt = n // (tt * nb)
    ncb = d // RG_COLS
    rows = tt * nb
    ppb = rows // nb
    npb = rows // (2 * nb)
    n_next_blocks = n // (2 * nb)

    def tile_of(s):
        return (nt - 1 - s) if reverse else s

    in_specs = [
        pl.BlockSpec((rows, RG_COLS), lambda c, s: (tile_of(s), c)),
        pl.BlockSpec((nb, RG_COLS), lambda c, s: (jnp.maximum(tile_of(s) * ppb - 1, 0), c)),
        pl.BlockSpec((2 * nb, RG_COLS),
                     lambda c, s: (jnp.minimum((tile_of(s) + 1) * npb, n_next_blocks - 1), c)),
        pl.BlockSpec((4, RG_COLS), lambda c, s: (0, c)),
        pl.BlockSpec((1, RG_COLS), lambda c, s: (0, c)),
        pl.BlockSpec((1, RG_COLS, 2 * RG_COLS), lambda c, s: (c, 0, 0)),
        pl.BlockSpec((1, 1, 2 * RG_COLS), lambda c, s: (c, 0, 0)),
        pl.BlockSpec((1, RG_COLS), lambda c, s: (0, c)),
        pl.BlockSpec((nb, RG_COLS), lambda c, s: (0, c)),
    ]
    args = [zrm, zrm, zrm, conv_w, conv_b, wg, bg, lam, h0]
    if reverse:
        in_specs += [pl.BlockSpec((rows, RG_COLS), lambda c, s: (tile_of(s), c)),
                     pl.BlockSpec((rows, RG_COLS), lambda c, s: (tile_of(s), ncb + c))]
        args += [hf, zrm]
    out_dtype = BF16 if reverse else F32
    return pl.pallas_call(
        functools.partial(_rg_kernel, reverse=reverse, nb=nb, tt=tt, nt=nt),
        out_shape=(jax.ShapeDtypeStruct((n, d), out_dtype), jax.ShapeDtypeStruct((nb, d), F32)),
        grid=(ncb, nt),
        in_specs=in_specs,
        out_specs=(pl.BlockSpec((rows, RG_COLS), lambda c, s: (tile_of(s), c)),
                   pl.BlockSpec((nb, RG_COLS), lambda c, s: (0, c))),
        scratch_shapes=[pltpu.VMEM((rows, RG_COLS), F32), pltpu.VMEM((rows, RG_COLS), F32),
                        pltpu.VMEM((nb, RG_COLS), F32)],
        compiler_params=_cparams(("parallel", "arbitrary")),
        name="rg_bwd" if reverse else "rg_fwd",
    )(*args)


def _cumsum_time(x, reverse, block):
    c = x.shape[0]
    x4 = x.reshape((c // block, block) + x.shape[1:])
    cols = [None] * block
    order = range(block - 1, -1, -1) if reverse else range(block)
    run = None
    for i in order:
        run = x4[:, i] if run is None else run + x4[:, i]
        cols[i] = run
    return jnp.stack(cols, axis=1).reshape(x.shape)


def _cumsum_chunk(x, reverse):
    c = x.shape[0]
    blk = 8
    x4 = _cumsum_time(x, reverse, blk).reshape((c // blk, blk) + x.shape[1:])
    nblk = c // blk
    tot = x4[:, 0] if reverse else x4[:, blk - 1]
    offs = [None] * nblk
    order = range(nblk - 1, -1, -1) if reverse else range(nblk)
    run = None
    for i in order:
        offs[i] = run
        run = tot[i] if run is None else run + tot[i]
    first = nblk - 1 if reverse else 0
    parts = [x4[i] if i == first else x4[i] + offs[i][None] for i in range(nblk)]
    return jnp.stack(parts, axis=0).reshape(x.shape)


def _pivot_time(c, block, p):
    c4 = c.reshape((c.shape[0] // block, block) + c.shape[1:])
    return jnp.broadcast_to(c4[:, p:p + 1], c4.shape).reshape(c.shape)


def _hg_levels(c):
    out, m = [], c // 2
    while m >= HG_DIAG:
        out.append(m)
        m //= 2
    return out


def _hg_kernel(*refs, reverse, final):
    if final:
        (zq_ref, zf_ref, v_ref, of_ref, og_ref, par_ref, s0_ref, y_ref, sout_ref,
         st_scr, ops_scr, o_scr) = refs
    else:
        (zq_ref, zf_ref, v_ref, par_ref, s0_ref, y_ref, sout_ref, st_scr, ops_scr) = refs
    j = pl.program_id(1)
    nj = pl.num_programs(1)
    c, nb, kd = zq_ref.shape

    @pl.when(j == 0)
    def _():
        st_scr[...] = s0_ref[...]

    zq = zq_ref[...]
    zf = zf_ref[...]
    loglb = par_ref[0:1, :][None]
    l1m = par_ref[1:2, :][None]
    oml = par_ref[2:3, :][None]

    q = _silu(zq)
    e = jnp.exp(-jnp.abs(zf))
    logsig = jnp.minimum(zf, 0.0) - jnp.log1p(e)
    x2 = l1m + logsig
    g = jnp.maximum(loglb, x2) + jnp.log1p(jnp.exp(-jnp.abs(loglb - x2)))
    k = oml * (jnp.where(zf >= 0.0, e, 1.0) / (1.0 + e))

    cs = _cumsum_chunk(g, reverse)
    ctot = cs[0:1] if reverse else cs[c - 1:c]
    ops_scr[0] = q * jnp.exp(cs)
    ops_scr[1] = k * jnp.exp(ctot - cs)
    decay = jnp.exp(ctot)[0]

    levels = _hg_levels(c)
    for n, m in enumerate(levels):
        piv = m if reverse else m - 1
        el = jnp.exp(-jnp.abs(cs - _pivot_time(cs, 2 * m, piv)))
        ops_scr[2 + 2 * n] = q * el
        ops_scr[3 + 2 * n] = k * el
    nd = 2 + 2 * len(levels)
    cd = _cumsum_time(jnp.maximum(g, -HG_DIAG_CLAMP), reverse, HG_DIAG)
    cdp = _pivot_time(cd, HG_DIAG, HG_DIAG // 2 if reverse else HG_DIAG // 2 - 1)
    ops_scr[nd] = q * jnp.exp(cd - cdp)
    ops_scr[nd + 1] = k * jnp.exp(cdp - cd)

    ti = lax.broadcasted_iota(jnp.int32, (c, c), 0)
    si = lax.broadcasted_iota(jnp.int32, (c, c), 1)
    lev = ti ^ si
    valid = (ti <= si) if reverse else (ti >= si)
    nt_dims = (((1,), (1,)), ((), ()))
    tn_dims = (((0,), (0,)), ((), ()))

    def operand(n, b):
        return ops_scr[n, :, b, :].astype(BF16)

    for b in range(nb):
        st = st_scr[b]
        o = lax.dot_general(operand(0, b), st.astype(BF16), nt_dims, preferred_element_type=F32)
        a = lax.dot_general(operand(nd, b), operand(nd + 1, b), nt_dims, preferred_element_type=F32)
        for n in range(len(levels) - 1, -1, -1):
            p = lax.dot_general(operand(2 + 2 * n, b), operand(3 + 2 * n, b), nt_dims,
                                preferred_element_type=F32)
            a = jnp.where(lev < levels[n], a, p)
        a = jnp.where(valid, a, 0.0)
        vb = v_ref[:, b, :].astype(BF16)
        o = o + jnp.dot(a.astype(BF16), vb, preferred_element_type=F32)
        st_scr[b] = st * decay[b:b + 1, :] + lax.dot_general(vb, operand(1, b), tn_dims,
                                                             preferred_element_type=F32)
        if final:
            o_scr[:, b, :] = o
        else:
            y_ref[:, b, :] = o

    if final:
        gain = par_ref[3:4, :][None]
        y = _rms(o_scr[...] + of_ref[...], gain) * _silu(og_ref[...])
        y_ref[...] = y.astype(y_ref.dtype)

    @pl.when(j == nj - 1)
    def _():
        sout_ref[...] = st_scr[...]


def _hg_dir(zhg, of, par, s0, *, reverse, d):
    t, nb, _ = zhg.shape
    kd = HG_EXPAND
    heads = d // kd
    nj = t // HG_CHUNK
    final = reverse
    nops = 4 + 2 * len(_hg_levels(HG_CHUNK))

    def blk(j):
        return (nj - 1 - j) if reverse else j

    def zspec(gidx):
        return pl.BlockSpec((HG_CHUNK, nb, kd), lambda h, j: (blk(j), 0, gidx * heads + h))

    hspec = pl.BlockSpec((HG_CHUNK, nb, kd), lambda h, j: (blk(j), 0, h))
    sspec = pl.BlockSpec((nb, None, kd, kd), lambda h, j: (0, h, 0, 0))
    in_specs = [zspec(0), zspec(2 if reverse else 1), zspec(3)]
    args = [zhg, zhg, zhg]
    scratch = [pltpu.VMEM((nb, kd, kd), F32), pltpu.VMEM((nops, HG_CHUNK, nb, kd), F32)]
    if final:
        in_specs += [hspec, zspec(4)]
        args += [of, zhg]
        scratch += [pltpu.VMEM((HG_CHUNK, nb, kd), F32)]
    in_specs += [pl.BlockSpec((par.shape[0], kd), lambda h, j: (0, h)), sspec]
    args += [par, s0]
    return pl.pallas_call(
        functools.partial(_hg_kernel, reverse=reverse, final=final),
        out_shape=(jax.ShapeDtypeStruct((t, nb, d), BF16 if final else F32),
                   jax.ShapeDtypeStruct(s0.shape, F32)),
        grid=(heads, nj),
        in_specs=in_specs,
        out_specs=(hspec, sspec),
        scratch_shapes=scratch,
        compiler_params=_cparams(("parallel", "arbitrary")),
        name="hg_bwd" if reverse else "hg_fwd",
    )(*args)


def _merge_kernel(yrg_ref, yhg_ref, ga_ref, gb_ref, x_ref, mod_ref, g_ref, wr_ref, wh_ref, wo_ref,
                  o_ref, *, nb, d):
    tm = x_ref.shape[0]
    p_rg = jnp.dot(yrg_ref[...], wr_ref[...], preferred_element_type=F32)
    p_hg = jnp.dot(yhg_ref[...].reshape(tm, d), wh_ref[...], preferred_element_type=F32)
    m = _sigmoid(ga_ref[...]) * p_rg + _sigmoid(gb_ref[...]) * p_hg
    y = jnp.dot(m.astype(BF16), wo_ref[...], preferred_element_type=F32)
    n3, gate = _per_batch(_rms(y, g_ref[...]), mod_ref[:, 2 * d:3 * d], nb)
    o_ref[...] = x_ref[...] + (gate * n3).reshape(tm, d)


def _merge(yrg, yhg, yhg_spec, zrm, x, mod, g, wr, wh, wo, *, tm, nb):
    n, d = x.shape
    row = lambda i: (i, 0)
    const = lambda i: (0, 0)
    return pl.pallas_call(
        functools.partial(_merge_kernel, nb=nb, d=d),
        out_shape=jax.ShapeDtypeStruct((n, d), F32),
        grid=(n // tm,),
        in_specs=[pl.BlockSpec((tm, d), row), yhg_spec,
                  pl.BlockSpec((tm, d), lambda i: (i, 2)), pl.BlockSpec((tm, d), lambda i: (i, 3)),
                  pl.BlockSpec((tm, d), row), pl.BlockSpec(mod.shape, const),
                  pl.BlockSpec((1, d), const), pl.BlockSpec((d, d), const),
                  pl.BlockSpec((d, d), const), pl.BlockSpec((d, d), const)],
        out_specs=pl.BlockSpec((tm, d), row),
        compiler_params=_cparams(("parallel",)),
        name="merge",
    )(yrg, yhg, zrm, zrm, x, mod, g, wr, wh, wo)


def _ffn_kernel(*refs, nb, d, halo, nt):
    if halo:
        (x_ref, xp_ref, xn_ref, mod_ref, g1_ref, g2_ref, wg_ref, wv_ref, cw_ref, cb_ref, wd_ref,
         o_ref, h_scr, acc_scr) = refs
    else:
        (x_ref, mod_ref, g1_ref, g2_ref, wg_ref, wv_ref, cw_ref, cb_ref, wd_ref,
         o_ref, h_scr, acc_scr) = refs
    i = pl.program_id(0)
    kf = pl.program_id(1)
    nk = pl.num_programs(1)
    tm = x_ref.shape[0]
    shift = mod_ref[:, 3 * d:4 * d]
    scale = mod_ref[:, 4 * d:5 * d]

    @pl.when(kf == 0)
    def _():
        h = _norm_mod(x_ref[...], g1_ref[...], shift, scale, nb).astype(BF16)
        if halo:
            hp = _norm_mod(xp_ref[...], g1_ref[...], shift, scale, nb)
            hn = _norm_mod(xn_ref[...], g1_ref[...], shift, scale, nb)
            hp = jnp.where(i > 0, hp, 0.0).astype(BF16)
            hn = jnp.where(i < nt - 1, hn, 0.0).astype(BF16)
            h_scr[...] = jnp.concatenate([hp, h, hn], axis=0)
        else:
            h_scr[...] = h
        acc_scr[...] = jnp.zeros_like(acc_scr)

    if halo:
        ue = jnp.dot(h_scr[...], wg_ref[...], preferred_element_type=F32)
        hm = h_scr[nb:nb + tm, :]
    else:
        hm = h_scr[...]
        u = jnp.dot(hm, wg_ref[...], preferred_element_type=F32)
        z = jnp.zeros((nb, u.shape[1]), F32)
        ue = jnp.concatenate([z, u, z], axis=0)
    gc = cb_ref[...] + ue[0:tm] * cw_ref[0:1, :] + ue[nb:nb + tm] * cw_ref[1:2, :] \
        + ue[2 * nb:2 * nb + tm] * cw_ref[2:3, :]
    uv = jnp.dot(hm, wv_ref[...], preferred_element_type=F32)
    act = (_silu(gc) * uv).astype(BF16)
    acc_scr[...] += jnp.dot(act, wd_ref[...], preferred_element_type=F32)

    @pl.when(kf == nk - 1)
    def _():
        n3, gate = _per_batch(_rms(acc_scr[...], g2_ref[...]), mod_ref[:, 5 * d:6 * d], nb)
        o_ref[...] = x_ref[...] + (gate * n3).reshape(tm, d)


def _ffn(x, mod, g1, g2, w_up, cw, cb, w_down, *, tm, tf, nb, halo):
    n, d = x.shape
    dff = w_down.shape[0]
    nt = n // tm
    nkf = dff // tf
    hb = tm // nb
    row = lambda i, k: (i, 0)
    const = lambda i, k: (0, 0)
    in_specs = [pl.BlockSpec((tm, d), row)]
    args = [x]
    if halo:
        in_specs += [pl.BlockSpec((nb, d), lambda i, k: (jnp.maximum(i * hb - 1, 0), 0)),
                     pl.BlockSpec((nb, d), lambda i, k: (jnp.minimum((i + 1) * hb, n // nb - 1), 0))]
        args += [x, x]
    in_specs += [pl.BlockSpec(mod.shape, const), pl.BlockSpec((1, d), const),
                 pl.BlockSpec((1, d), const),
                 pl.BlockSpec((d, tf), lambda i, k: (0, k)),
                 pl.BlockSpec((d, tf), lambda i, k: (0, nkf + k)),
                 pl.BlockSpec((3, tf), lambda i, k: (0, k)),
                 pl.BlockSpec((1, tf), lambda i, k: (0, k)),
                 pl.BlockSpec((tf, d), lambda i, k: (k, 0))]
    args += [mod, g1, g2, w_up, w_up, cw, cb, w_down]
    hrows = tm + 2 * nb if halo else tm
    return pl.pallas_call(
        functools.partial(_ffn_kernel, nb=nb, d=d, halo=halo, nt=nt),
        out_shape=jax.ShapeDtypeStruct((n, d), F32),
        grid=(nt, nkf),
        in_specs=in_specs,
        out_specs=pl.BlockSpec((tm, d), row),
        scratch_shapes=[pltpu.VMEM((hrows, d), BF16), pltpu.VMEM((tm, d), F32)],
        compiler_params=_cparams(("parallel", "arbitrary")),
        name="ffn_ctx" if halo else "ffn",
    )(*args)


def _gate_weights(wa, wx):
    nd, heads, hd, _ = wa.shape
    per = RG_COLS // hd
    ncb = heads // per

    def bd(w):
        w = w.reshape(nd, ncb, per, hd, hd)
        eye = jnp.eye(per, dtype=w.dtype)
        return jnp.einsum('dcpij,pq->dcpiqj', w, eye).reshape(nd, ncb, RG_COLS, RG_COLS)

    return jnp.concatenate([bd(wa), bd(wx)], axis=-1).astype(BF16)


def _forward(x, c, ctx, c_ctx, w_ada, b_ada, g_pre_mix, g_post_mix, g_pre_ffn, g_post_ffn, w_in,
             rg_conv_w, rg_conv_b, rg_wa, rg_ba, rg_wx, rg_bx, rg_lam, hg_lb_logits, hg_out_norm,
             w_proj_rg, w_proj_hg, w_out, ffn_w_up, ffn_conv_w, ffn_conv_b, ffn_w_down, *, grid_w):
    nb, seq, d = x.shape
    ctx_len = ctx.shape[1]
    depth = w_in.shape[0]
    rows_g = seq // grid_w
    dff = ffn_w_down.shape[1]
    tile = grid_w * nb
    ngroups = 5
    heads = d // HG_EXPAND
    tf = 256 if dff % 256 == 0 else dff

    p = jax.nn.softmax(hg_lb_logits.astype(F32), axis=0)
    cum = jnp.cumsum(p, axis=0)
    lb_all = cum - cum[0:1]
    hg_par = jnp.stack([jnp.log(lb_all), jnp.log1p(-lb_all), 1.0 - lb_all, hg_out_norm], axis=1)
    hg_par = jnp.pad(hg_par, ((0, 0), (0, 4), (0, 0)))

    w_ada_b, w_in_b = w_ada.astype(BF16), w_in.astype(BF16)
    wr_b, wh_b, wo_b = w_proj_rg.astype(BF16), w_proj_hg.astype(BF16), w_out.astype(BF16)
    wup_b, wdn_b = ffn_w_up.astype(BF16), ffn_w_down.astype(BF16)

    cc = jnp.concatenate([c, jnp.broadcast_to(c_ctx[None], (nb, d))], axis=0)
    mod_all = _ada_mod(cc, w_ada_b, b_ada).reshape(depth, 2, nb, N_MOD * d)

    xl = jnp.swapaxes(x, 0, 1).reshape(seq * nb, d)
    xc = jnp.swapaxes(ctx, 0, 1).reshape(ctx_len * nb, d)

    for l in range(depth):
        need_ctx = l < depth - 1
        mod_l, mod_c = mod_all[l, 0], mod_all[l, 1]
        gpm = g_pre_mix[l].reshape(1, d)
        wg = _gate_weights(rg_wa[l], rg_wx[l])
        ncb = d // RG_COLS
        bg = jnp.concatenate([rg_ba[l].reshape(2, ncb, 1, RG_COLS),
                              rg_bx[l].reshape(2, ncb, 1, RG_COLS)], axis=-1)
        conv_b = rg_conv_b[l].reshape(1, d)

        rm_groups, hg_groups = (0, 1, 7, 8), (2, 3, 4, 5, 6)
        zc_rm = _inproj(xc, mod_c, gpm, w_in_b[l], rm_groups,
                        jax.ShapeDtypeStruct((ctx_len * nb, 4 * d), F32),
                        pl.BlockSpec((tile, d), lambda i, j: (i, j)), tile, nb)
        zl_rm = _inproj(xl, mod_l, gpm, w_in_b[l], rm_groups,
                        jax.ShapeDtypeStruct((seq * nb, 4 * d), F32),
                        pl.BlockSpec((tile, d), lambda i, j: (i, j)), tile, nb)
        zc_hg = _inproj(xc, mod_c, gpm, w_in_b[l], hg_groups,
                        jax.ShapeDtypeStruct((ctx_len, nb, ngroups * d), F32),
                        pl.BlockSpec((grid_w, nb, d), lambda i, j: (i, 0, j)), tile, nb)
        zl_hg = _inproj(xl, mod_l, gpm, w_in_b[l], hg_groups,
                        jax.ShapeDtypeStruct((grid_w, rows_g, nb, ngroups * d), F32),
                        pl.BlockSpec((grid_w, None, nb, d), lambda i, j: (0, i, 0, j)), tile, nb)
        zl_hg = zl_hg.reshape(seq, nb, ngroups * d)

        zero_h = jnp.zeros((nb, d), F32)
        rg = functools.partial(_rg_dir, conv_w=rg_conv_w[l], conv_b=conv_b, nb=nb, tt=grid_w, d=d)
        hf_c, st = rg(zc_rm, None, wg=wg[0], bg=bg[0], lam=rg_lam[l, 0:1], h0=zero_h, reverse=False)
        hf_l, _ = rg(zl_rm, None, wg=wg[0], bg=bg[0], lam=rg_lam[l, 0:1], h0=st, reverse=False)
        yrg_c, st = rg(zc_rm, hf_c, wg=wg[1], bg=bg[1], lam=rg_lam[l, 1:2], h0=zero_h, reverse=True)
        yrg_l, _ = rg(zl_rm, hf_l, wg=wg[1], bg=bg[1], lam=rg_lam[l, 1:2], h0=st, reverse=True)

        zero_s = jnp.zeros((nb, heads, HG_EXPAND, HG_EXPAND), F32)
        hg = functools.partial(_hg_dir, par=hg_par[l], d=d)
        of_c, st = hg(zc_hg, None, s0=zero_s, reverse=False)
        of_l, _ = hg(zl_hg, None, s0=st, reverse=False)
        yhg_c, st = hg(zc_hg, of_c, s0=zero_s, reverse=True)
        yhg_l, _ = hg(zl_hg, of_l, s0=st, reverse=True)

        gpo = g_post_mix[l].reshape(1, d)
        yhg_l4 = yhg_l.reshape(grid_w, rows_g, nb, d)
        xl = _merge(yrg_l, yhg_l4, pl.BlockSpec((grid_w, None, nb, d), lambda i: (0, i, 0, 0)),
                    zl_rm, xl, mod_l, gpo, wr_b[l], wh_b[l], wo_b[l], tm=tile, nb=nb)
        ffn = functools.partial(_ffn, g1=g_pre_ffn[l].reshape(1, d), g2=g_post_ffn[l].reshape(1, d),
                                w_up=wup_b[l], cw=ffn_conv_w[l], cb=ffn_conv_b[l].reshape(1, dff),
                                w_down=wdn_b[l], tm=tile, tf=tf, nb=nb)
        xl = ffn(xl, mod_l, halo=False)
        if need_ctx:
            xc = _merge(yrg_c, yhg_c, pl.BlockSpec((grid_w, nb, d), lambda i: (i, 0, 0)),
                        zc_rm, xc, mod_c, gpo, wr_b[l], wh_b[l], wo_b[l], tm=tile, nb=nb)
            xc = ffn(xc, mod_c, halo=True)

    return jnp.swapaxes(xl.reshape(seq, nb, d), 0, 1)


def kernel(x, c, ctx, c_ctx, w_ada, b_ada, g_pre_mix, g_post_mix, g_pre_ffn, g_post_ffn, w_in, rg_conv_w, rg_conv_b, rg_wa, rg_ba, rg_wx, rg_bx, rg_lam, hg_lb_logits, hg_out_norm, w_proj_rg, w_proj_hg, w_out, ffn_w_up, ffn_conv_w, ffn_conv_b, ffn_w_down):
    return _forward(x, c, ctx, c_ctx, w_ada, b_ada, g_pre_mix, g_post_mix, g_pre_ffn, g_post_ffn,
                    w_in, rg_conv_w, rg_conv_b, rg_wa, rg_ba, rg_wx, rg_bx, rg_lam, hg_lb_logits,
                    hg_out_norm, w_proj_rg, w_proj_hg, w_out, ffn_w_up, ffn_conv_w, ffn_conv_b,
                    ffn_w_down, grid_w=GRID_W)
```

```python
import functools

import jax
import jax.numpy as jnp
from jax import lax
from jax.experimental import pallas as pl
from jax.experimental.pallas import tpu as pltpu

GRID_W = 64
RG_HEADS = 16
RG_C = 8.0
HG_EXPAND = 128
N_MOD = 6
EPS = 1e-6
RG_CONV_PAD_L = 1
RG_CONV_PAD_R = 2

HG_CHUNK = 128
HG_DIAG = 8
HG_DIAG_CLAMP = 20.0
HG_PITCH = 24
RG_COLS = 256
VMEM_LIMIT = 56 * 1024 * 1024

F32 = jnp.float32
BF16 = jnp.bfloat16


def _cparams(sem):
    return pltpu.CompilerParams(dimension_semantics=sem, vmem_limit_bytes=VMEM_LIMIT)


def _sigmoid(x):
    return jax.nn.sigmoid(x)


def _silu(x):
    return x * jax.nn.sigmoid(x)


def _gelu_tanh(x):
    c = 0.7978845608028654
    return 0.5 * x * (1.0 + jnp.tanh(c * (x + 0.044715 * (x * x * x))))


def _rms(x, g):
    ms = jnp.mean(x * x, axis=-1, keepdims=True)
    return x * lax.rsqrt(ms + EPS) * g


def _per_batch(x, vec, nb):
    r, d = x.shape
    return x.reshape(r // nb, nb, d), vec[None]


def _norm_mod(x, g, shift, scale, nb):
    y = _rms(x, g)
    y3, sc = _per_batch(y, scale, nb)
    h = y3 * (1.0 + sc) + shift[None]
    return h.reshape(x.shape)


def _ada_kernel(c_ref, w_ref, b_ref, o_ref):
    s = _silu(c_ref[...]).astype(BF16)
    o_ref[0] = jnp.dot(s, w_ref[0], preferred_element_type=F32) + b_ref[0]


def _ada_mod(cc, w_ada, b_ada):
    depth, d, n = w_ada.shape
    tn = 1536 if n % 1536 == 0 else n
    return pl.pallas_call(
        _ada_kernel,
        out_shape=jax.ShapeDtypeStruct((depth, cc.shape[0], n), F32),
        grid=(depth, n // tn),
        in_specs=[pl.BlockSpec(cc.shape, lambda l, j: (0, 0)),
                  pl.BlockSpec((1, d, tn), lambda l, j: (l, 0, j)),
                  pl.BlockSpec((1, 1, tn), lambda l, j: (l, 0, j))],
        out_specs=pl.BlockSpec((1, cc.shape[0], tn), lambda l, j: (l, 0, j)),
        compiler_params=_cparams(("parallel", "parallel")),
        name="ada_mod",
    )(cc, w_ada, b_ada.reshape(depth, 1, n))


def _inproj_kernel(x_ref, mod_ref, g_ref, w_ref, o_ref, h_scr, *, nb, d):
    @pl.when(pl.program_id(1) == 0)
    def _():
        h = _norm_mod(x_ref[...], g_ref[...], mod_ref[:, 0:d], mod_ref[:, d:2 * d], nb)
        h_scr[...] = h.astype(BF16)

    res = jnp.dot(h_scr[...], w_ref[...], preferred_element_type=F32)
    o_ref[...] = res.reshape(o_ref.shape).astype(o_ref.dtype)


def _inproj(x, mod, g, w, col_groups, out_shape, out_spec, tm, nb):
    n, d = x.shape

    def wmap(i, j):
        idx = jnp.int32(col_groups[0])
        for k in range(1, len(col_groups)):
            idx = jnp.where(j == k, jnp.int32(col_groups[k]), idx)
        return (0, idx)

    return pl.pallas_call(
        functools.partial(_inproj_kernel, nb=nb, d=d),
        out_shape=out_shape,
        grid=(n // tm, len(col_groups)),
        in_specs=[pl.BlockSpec((tm, d), lambda i, j: (i, 0)),
                  pl.BlockSpec(mod.shape, lambda i, j: (0, 0)),
                  pl.BlockSpec((1, d), lambda i, j: (0, 0)),
                  pl.BlockSpec((d, d), wmap)],
        out_specs=out_spec,
        scratch_shapes=[pltpu.VMEM((tm, d), BF16)],
        compiler_params=_cparams(("parallel", "arbitrary")),
        name="inproj",
    )(x, mod, g, w)


def _neg_expm1(y):
    small = y * (1.0 + y * (1.0 / 2.0) * (1.0 + y * (1.0 / 3.0) * (1.0 + y * (1.0 / 4.0) * (
        1.0 + y * (1.0 / 5.0) * (1.0 + y * (1.0 / 6.0) * (1.0 + y * (1.0 / 7.0)))))))
    return -jnp.where(y > -0.1, small, jnp.exp(y) - 1.0)


def _rg_kernel(*refs, reverse, nb, tt, nt):
    if reverse:
        (x_ref, xp_ref, xn_ref, cw_ref, cb_ref, wg_ref, bg_ref, lam_ref, h0_ref, hf_ref, gg_ref,
         y_ref, hfin_ref, a_scr, b_scr, h_scr) = refs
    else:
        (x_ref, xp_ref, xn_ref, cw_ref, cb_ref, wg_ref, bg_ref, lam_ref, h0_ref,
         y_ref, hfin_ref, a_scr, b_scr, h_scr) = refs
    s = pl.program_id(1)
    tile = (nt - 1 - s) if reverse else s
    rows = tt * nb
    cols = x_ref.shape[1]

    @pl.when(s == 0)
    def _():
        h_scr[...] = h0_ref[...]

    prev = jnp.where(tile > 0, xp_ref[...], 0.0)
    nxt = jnp.where(tile < nt - 1, xn_ref[...], 0.0)
    xe = jnp.concatenate([prev, x_ref[...], nxt], axis=0)
    xl = cb_ref[...] + xe[0:rows] * cw_ref[0:1, :]
    for j in range(1, 4):
        xl = xl + xe[j * nb:j * nb + rows] * cw_ref[j:j + 1, :]

    gates = jnp.dot(xl.astype(BF16), wg_ref[0], preferred_element_type=F32) + bg_ref[0]
    r = _sigmoid(gates[:, :cols])
    ig = _sigmoid(gates[:, cols:])
    log_a = (-RG_C * jax.nn.softplus(-lam_ref[...])) * r
    a_scr[...] = jnp.exp(log_a)
    b_scr[...] = jnp.sqrt(_neg_expm1(2.0 * log_a)) * (ig * xl)

    def body(t, h):
        tloc = (tt - 1 - t) if reverse else t
        off = pl.multiple_of(tloc * nb, nb)
        h = a_scr[pl.ds(off, nb), :] * h + b_scr[pl.ds(off, nb), :]
        b_scr[pl.ds(off, nb), :] = h
        return h

    h = lax.fori_loop(0, tt, body, h_scr[...], unroll=8)
    h_scr[...] = h
    hfin_ref[...] = h
    if reverse:
        y_ref[...] = (_gelu_tanh(gg_ref[...]) * (hf_ref[...] + b_scr[...])).astype(y_ref.dtype)
    else:
        y_ref[...] = b_scr[...]


def _rg_dir(zrm, hf, conv_w, conv_b, wg, bg, lam, h0, *, reverse, nb, tt, d):
    n = zrm.shape[0]
    nt = n // (tt * nb)
    ncb = d // RG_COLS
    rows = tt * nb
    ppb = rows // nb
    npb = rows // (2 * nb)
    n_next_blocks = n // (2 * nb)

    def tile_of(s):
        return (nt - 1 - s) if reverse else s

    in_specs = [
        pl.BlockSpec((rows, RG_COLS), lambda c, s: (tile_of(s), c)),
        pl.BlockSpec((nb, RG_COLS), lambda c, s: (jnp.maximum(tile_of(s) * ppb - 1, 0), c)),
        pl.BlockSpec((2 * nb, RG_COLS),
                     lambda c, s: (jnp.minimum((tile_of(s) + 1) * npb, n_next_blocks - 1), c)),
        pl.BlockSpec((4, RG_COLS), lambda c, s: (0, c)),
        pl.BlockSpec((1, RG_COLS), lambda c, s: (0, c)),
        pl.BlockSpec((1, RG_COLS, 2 * RG_COLS), lambda c, s: (c, 0, 0)),
        pl.BlockSpec((1, 1, 2 * RG_COLS), lambda c, s: (c, 0, 0)),
        pl.BlockSpec((1, RG_COLS), lambda c, s: (0, c)),
        pl.BlockSpec((nb, RG_COLS), lambda c, s: (0, c)),
    ]
    args = [zrm, zrm, zrm, conv_w, conv_b, wg, bg, lam, h0]
    if reverse:
        in_specs += [pl.BlockSpec((rows, RG_COLS), lambda c, s: (tile_of(s), c)),
                     pl.BlockSpec((rows, RG_COLS), lambda c, s: (tile_of(s), ncb + c))]
        args += [hf, zrm]
    out_dtype = BF16 if reverse else F32
    return pl.pallas_call(
        functools.partial(_rg_kernel, reverse=reverse, nb=nb, tt=tt, nt=nt),
        out_shape=(jax.ShapeDtypeStruct((n, d), out_dtype), jax.ShapeDtypeStruct((nb, d), F32)),
        grid=(ncb, nt),
        in_specs=in_specs,
        out_specs=(pl.BlockSpec((rows, RG_COLS), lambda c, s: (tile_of(s), c)),
                   pl.BlockSpec((nb, RG_COLS), lambda c, s: (0, c))),
        scratch_shapes=[pltpu.VMEM((rows, RG_COLS), F32), pltpu.VMEM((rows, RG_COLS), F32),
                        pltpu.VMEM((nb, RG_COLS), F32)],
        compiler_params=_cparams(("parallel", "arbitrary")),
        name="rg_bwd" if reverse else "rg_fwd",
    )(*args)


def _cumsum_time(x, reverse, block):
    c = x.shape[0]
    x4 = x.reshape((c // block, block) + x.shape[1:])
    cols = [None] * block
    order = range(block - 1, -1, -1) if reverse else range(block)
    run = None
    for i in order:
        run = x4[:, i] if run is None else run + x4[:, i]
        cols[i] = run
    return jnp.stack(cols, axis=1).reshape(x.shape)


def _cumsum_chunk(x, reverse):
    c = x.shape[0]
    blk = 8
    x4 = _cumsum_time(x, reverse, blk).reshape((c // blk, blk) + x.shape[1:])
    nblk = c // blk
    tot = x4[:, 0] if reverse else x4[:, blk - 1]
    offs = [None] * nblk
    order = range(nblk - 1, -1, -1) if reverse else range(nblk)
    run = None
    for i in order:
        offs[i] = run
        run = tot[i] if run is None else run + tot[i]
    first = nblk - 1 if reverse else 0
    parts = [x4[i] if i == first else x4[i] + offs[i][None] for i in range(nblk)]
    return jnp.stack(parts, axis=0).reshape(x.shape)


def _pivot_time(c, block, p):
    c4 = c.reshape((c.shape[0] // block, block) + c.shape[1:])
    return jnp.broadcast_to(c4[:, p:p + 1], c4.shape).reshape(c.shape)


def _hg_levels(c):
    out, m = [], c // 2
    while m >= HG_DIAG:
        out.append(m)
        m //= 2
    return out


def _hg_kernel(*refs, reverse, final):
    if final:
        (zq_ref, zf_ref, v_ref, of_ref, og_ref, par_ref, s0_ref, y_ref, sout_ref,
         st_scr, tr_scr, o_scr) = refs
    else:
        (zq_ref, zf_ref, v_ref, par_ref, s0_ref, y_ref, sout_ref, st_scr, tr_scr, o_scr) = refs
    j = pl.program_id(1)
    nj = pl.num_programs(1)
    c, nb, kd = zq_ref.shape

    @pl.when(j == 0)
    def _():
        st_scr[...] = s0_ref[...]

    zq = zq_ref[...]
    zf = zf_ref[...]
    loglb = par_ref[0:1, :][None]
    l1m = par_ref[1:2, :][None]
    oml = par_ref[2:3, :][None]

    q = _silu(zq)
    e = jnp.exp(-jnp.abs(zf))
    logsig = jnp.minimum(zf, 0.0) - jnp.log1p(e)
    x2 = l1m + logsig
    g = jnp.maximum(loglb, x2) + jnp.log1p(jnp.exp(-jnp.abs(loglb - x2)))
    k = oml * (jnp.where(zf >= 0.0, e, 1.0) / (1.0 + e))

    cs = _cumsum_chunk(g, reverse)
    cd = _cumsum_time(jnp.maximum(g, -HG_DIAG_CLAMP), reverse, HG_DIAG)
    ctot = cs[0] if reverse else cs[c - 1]
    decay = jnp.exp(ctot)

    pad = jnp.zeros((c, HG_PITCH - nb, kd), F32)
    for n, arr in enumerate((q, k, cs, cd, v_ref[...])):
        tr_scr[n] = jnp.concatenate([arr, pad], axis=1).reshape(c * HG_PITCH, kd)

    ti = lax.broadcasted_iota(jnp.int32, (c, c), 0)
    si = lax.broadcasted_iota(jnp.int32, (c, c), 1)
    lev = ti ^ si
    valid = (ti <= si) if reverse else (ti >= si)
    nt_dims = (((1,), (1,)), ((), ()))
    tn_dims = (((0,), (0,)), ((), ()))
    levels = _hg_levels(c)

    for b in range(nb):
        rows = pl.ds(b, c, stride=HG_PITCH)
        q2, k2, cs2, cd2 = (tr_scr[n, rows, :] for n in range(4))
        vb = tr_scr[4, rows, :].astype(BF16)
        st = st_scr[b]
        q_in = (q2 * jnp.exp(cs2)).astype(BF16)
        o = lax.dot_general(q_in, st.astype(BF16), nt_dims, preferred_element_type=F32)
        cdp = _pivot_time(cd2, HG_DIAG, HG_DIAG // 2 if reverse else HG_DIAG // 2 - 1)
        a = lax.dot_general((q2 * jnp.exp(cd2 - cdp)).astype(BF16),
                            (k2 * jnp.exp(cdp - cd2)).astype(BF16), nt_dims,
                            preferred_element_type=F32)
        for m in levels[::-1]:
            piv = m if reverse else m - 1
            el = jnp.exp(-jnp.abs(cs2 - _pivot_time(cs2, 2 * m, piv)))
            p = lax.dot_general((q2 * el).astype(BF16), (k2 * el).astype(BF16), nt_dims,
                                preferred_element_type=F32)
            a = jnp.where(lev < m, a, p)
        a = jnp.where(valid, a, 0.0)
        o = o + jnp.dot(a.astype(BF16), vb, preferred_element_type=F32)
        k_out = (k2 * jnp.exp(ctot[b:b + 1, :] - cs2)).astype(BF16)
        st_scr[b] = st * decay[b:b + 1, :] + lax.dot_general(vb, k_out, tn_dims,
                                                             preferred_element_type=F32)
        o_scr[rows, :] = o

    o = o_scr[...].reshape(c, HG_PITCH, kd)[:, 0:nb, :]
    if final:
        gain = par_ref[3:4, :][None]
        y = _rms(o + of_ref[...], gain) * _silu(og_ref[...])
        y_ref[...] = y.astype(y_ref.dtype)
    else:
        y_ref[...] = o

    @pl.when(j == nj - 1)
    def _():
        sout_ref[...] = st_scr[...]


def _hg_dir(zhg, of, par, s0, *, reverse, d):
    t, nb, _ = zhg.shape
    kd = HG_EXPAND
    heads = d // kd
    nj = t // HG_CHUNK
    final = reverse
    assert nb <= HG_PITCH

    def blk(j):
        return (nj - 1 - j) if reverse else j

    def zspec(gidx):
        return pl.BlockSpec((HG_CHUNK, nb, kd), lambda h, j: (blk(j), 0, gidx * heads + h))

    hspec = pl.BlockSpec((HG_CHUNK, nb, kd), lambda h, j: (blk(j), 0, h))
    sspec = pl.BlockSpec((nb, None, kd, kd), lambda h, j: (0, h, 0, 0))
    in_specs = [zspec(0), zspec(2 if reverse else 1), zspec(3)]
    args = [zhg, zhg, zhg]
    scratch = [pltpu.VMEM((nb, kd, kd), F32), pltpu.VMEM((5, HG_CHUNK * HG_PITCH, kd), F32),
               pltpu.VMEM((HG_CHUNK * HG_PITCH, kd), F32)]
    if final:
        in_specs += [hspec, zspec(4)]
        args += [of, zhg]
    in_specs += [pl.BlockSpec((par.shape[0], kd), lambda h, j: (0, h)), sspec]
    args += [par, s0]
    return pl.pallas_call(
        functools.partial(_hg_kernel, reverse=reverse, final=final),
        out_shape=(jax.ShapeDtypeStruct((t, nb, d), BF16 if final else F32),
                   jax.ShapeDtypeStruct(s0.shape, F32)),
        grid=(heads, nj),
        in_specs=in_specs,
        out_specs=(hspec, sspec),
        scratch_shapes=scratch,
        compiler_params=_cparams(("parallel", "arbitrary")),
        name="hg_bwd" if reverse else "hg_fwd",
    )(*args)


def _merge_kernel(yrg_ref, yhg_ref, ga_ref, gb_ref, x_ref, mod_ref, g_ref, wr_ref, wh_ref, wo_ref,
                  o_ref, *, nb, d):
    tm = x_ref.shape[0]
    p_rg = jnp.dot(yrg_ref[...], wr_ref[...], preferred_element_type=F32)
    p_hg = jnp.dot(yhg_ref[...].reshape(tm, d), wh_ref[...], preferred_element_type=F32)
    m = _sigmoid(ga_ref[...]) * p_rg + _sigmoid(gb_ref[...]) * p_hg
    y = jnp.dot(m.astype(BF16), wo_ref[...], preferred_element_type=F32)
    n3, gate = _per_batch(_rms(y, g_ref[...]), mod_ref[:, 2 * d:3 * d], nb)
    o_ref[...] = x_ref[...] + (gate * n3).reshape(tm, d)


def _merge(yrg, yhg, yhg_spec, zrm, x, mod, g, wr, wh, wo, *, tm, nb):
    n, d = x.shape
    row = lambda i: (i, 0)
    const = lambda i: (0, 0)
    return pl.pallas_call(
        functools.partial(_merge_kernel, nb=nb, d=d),
        out_shape=jax.ShapeDtypeStruct((n, d), F32),
        grid=(n // tm,),
        in_specs=[pl.BlockSpec((tm, d), row), yhg_spec,
                  pl.BlockSpec((tm, d), lambda i: (i, 2)), pl.BlockSpec((tm, d), lambda i: (i, 3)),
                  pl.BlockSpec((tm, d), row), pl.BlockSpec(mod.shape, const),
                  pl.BlockSpec((1, d), const), pl.BlockSpec((d, d), const),
                  pl.BlockSpec((d, d), const), pl.BlockSpec((d, d), const)],
        out_specs=pl.BlockSpec((tm, d), row),
        compiler_params=_cparams(("parallel",)),
        name="merge",
    )(yrg, yhg, zrm, zrm, x, mod, g, wr, wh, wo)


def _ffn_kernel(*refs, nb, d, halo, nt):
    if halo:
        (x_ref, xp_ref, xn_ref, mod_ref, g1_ref, g2_ref, wg_ref, wv_ref, cw_ref, cb_ref, wd_ref,
         o_ref, h_scr, acc_scr) = refs
    else:
        (x_ref, mod_ref, g1_ref, g2_ref, wg_ref, wv_ref, cw_ref, cb_ref, wd_ref,
         o_ref, h_scr, acc_scr) = refs
    i = pl.program_id(0)
    kf = pl.program_id(1)
    nk = pl.num_programs(1)
    tm = x_ref.shape[0]
    shift = mod_ref[:, 3 * d:4 * d]
    scale = mod_ref[:, 4 * d:5 * d]

    @pl.when(kf == 0)
    def _():
        h = _norm_mod(x_ref[...], g1_ref[...], shift, scale, nb).astype(BF16)
        if halo:
            hp = _norm_mod(xp_ref[...], g1_ref[...], shift, scale, nb)
            hn = _norm_mod(xn_ref[...], g1_ref[...], shift, scale, nb)
            hp = jnp.where(i > 0, hp, 0.0).astype(BF16)
            hn = jnp.where(i < nt - 1, hn, 0.0).astype(BF16)
            h_scr[...] = jnp.concatenate([hp, h, hn], axis=0)
        else:
            h_scr[...] = h
        acc_scr[...] = jnp.zeros_like(acc_scr)

    if halo:
        ue = jnp.dot(h_scr[...], wg_ref[...], preferred_element_type=F32)
        hm = h_scr[nb:nb + tm, :]
    else:
        hm = h_scr[...]
        u = jnp.dot(hm, wg_ref[...], preferred_element_type=F32)
        z = jnp.zeros((nb, u.shape[1]), F32)
        ue = jnp.concatenate([z, u, z], axis=0)
    gc = cb_ref[...] + ue[0:tm] * cw_ref[0:1, :] + ue[nb:nb + tm] * cw_ref[1:2, :] \
        + ue[2 * nb:2 * nb + tm] * cw_ref[2:3, :]
    uv = jnp.dot(hm, wv_ref[...], preferred_element_type=F32)
    act = (_silu(gc) * uv).astype(BF16)
    acc_scr[...] += jnp.dot(act, wd_ref[...], preferred_element_type=F32)

    @pl.when(kf == nk - 1)
    def _():
        n3, gate = _per_batch(_rms(acc_scr[...], g2_ref[...]), mod_ref[:, 5 * d:6 * d], nb)
        o_ref[...] = x_ref[...] + (gate * n3).reshape(tm, d)


def _ffn(x, mod, g1, g2, w_up, cw, cb, w_down, *, tm, tf, nb, halo):
    n, d = x.shape
    dff = w_down.shape[0]
    nt = n // tm
    nkf = dff // tf
    hb = tm // nb
    row = lambda i, k: (i, 0)
    const = lambda i, k: (0, 0)
    in_specs = [pl.BlockSpec((tm, d), row)]
    args = [x]
    if halo:
        in_specs += [pl.BlockSpec((nb, d), lambda i, k: (jnp.maximum(i * hb - 1, 0), 0)),
                     pl.BlockSpec((nb, d), lambda i, k: (jnp.minimum((i + 1) * hb, n // nb - 1), 0))]
        args += [x, x]
    in_specs += [pl.BlockSpec(mod.shape, const), pl.BlockSpec((1, d), const),
                 pl.BlockSpec((1, d), const),
                 pl.BlockSpec((d, tf), lambda i, k: (0, k)),
                 pl.BlockSpec((d, tf), lambda i, k: (0, nkf + k)),
                 pl.BlockSpec((3, tf), lambda i, k: (0, k)),
                 pl.BlockSpec((1, tf), lambda i, k: (0, k)),
                 pl.BlockSpec((tf, d), lambda i, k: (k, 0))]
    args += [mod, g1, g2, w_up, w_up, cw, cb, w_down]
    hrows = tm + 2 * nb if halo else tm
    return pl.pallas_call(
        functools.partial(_ffn_kernel, nb=nb, d=d, halo=halo, nt=nt),
        out_shape=jax.ShapeDtypeStruct((n, d), F32),
        grid=(nt, nkf),
        in_specs=in_specs,
        out_specs=pl.BlockSpec((tm, d), row),
        scratch_shapes=[pltpu.VMEM((hrows, d), BF16), pltpu.VMEM((tm, d), F32)],
        compiler_params=_cparams(("parallel", "arbitrary")),
        name="ffn_ctx" if halo else "ffn",
    )(*args)


def _gate_weights(wa, wx):
    nd, heads, hd, _ = wa.shape
    per = RG_COLS // hd
    ncb = heads // per

    def bd(w):
        w = w.reshape(nd, ncb, per, hd, hd)
        eye = jnp.eye(per, dtype=w.dtype)
        return jnp.einsum('dcpij,pq->dcpiqj', w, eye).reshape(nd, ncb, RG_COLS, RG_COLS)

    return jnp.concatenate([bd(wa), bd(wx)], axis=-1).astype(BF16)


def _forward(x, c, ctx, c_ctx, w_ada, b_ada, g_pre_mix, g_post_mix, g_pre_ffn, g_post_ffn, w_in,
             rg_conv_w, rg_conv_b, rg_wa, rg_ba, rg_wx, rg_bx, rg_lam, hg_lb_logits, hg_out_norm,
             w_proj_rg, w_proj_hg, w_out, ffn_w_up, ffn_conv_w, ffn_conv_b, ffn_w_down, *, grid_w):
    nb, seq, d = x.shape
    ctx_len = ctx.shape[1]
    depth = w_in.shape[0]
    rows_g = seq // grid_w
    dff = ffn_w_down.shape[1]
    tile = grid_w * nb
    ngroups = 5
    heads = d // HG_EXPAND
    tf = 256 if dff % 256 == 0 else dff

    p = jax.nn.softmax(hg_lb_logits.astype(F32), axis=0)
    cum = jnp.cumsum(p, axis=0)
    lb_all = cum - cum[0:1]
    hg_par = jnp.stack([jnp.log(lb_all), jnp.log1p(-lb_all), 1.0 - lb_all, hg_out_norm], axis=1)
    hg_par = jnp.pad(hg_par, ((0, 0), (0, 4), (0, 0)))

    w_ada_b, w_in_b = w_ada.astype(BF16), w_in.astype(BF16)
    wr_b, wh_b, wo_b = w_proj_rg.astype(BF16), w_proj_hg.astype(BF16), w_out.astype(BF16)
    wup_b, wdn_b = ffn_w_up.astype(BF16), ffn_w_down.astype(BF16)

    cc = jnp.concatenate([c, jnp.broadcast_to(c_ctx[None], (nb, d))], axis=0)
    mod_all = _ada_mod(cc, w_ada_b, b_ada).reshape(depth, 2, nb, N_MOD * d)

    xl = jnp.swapaxes(x, 0, 1).reshape(seq * nb, d)
    xc = jnp.swapaxes(ctx, 0, 1).reshape(ctx_len * nb, d)

    for l in range(depth):
        need_ctx = l < depth - 1
        mod_l, mod_c = mod_all[l, 0], mod_all[l, 1]
        gpm = g_pre_mix[l].reshape(1, d)
        wg = _gate_weights(rg_wa[l], rg_wx[l])
        ncb = d // RG_COLS
        bg = jnp.concatenate([rg_ba[l].reshape(2, ncb, 1, RG_COLS),
                              rg_bx[l].reshape(2, ncb, 1, RG_COLS)], axis=-1)
        conv_b = rg_conv_b[l].reshape(1, d)

        rm_groups, hg_groups = (0, 1, 7, 8), (2, 3, 4, 5, 6)
        zc_rm = _inproj(xc, mod_c, gpm, w_in_b[l], rm_groups,
                        jax.ShapeDtypeStruct((ctx_len * nb, 4 * d), F32),
                        pl.BlockSpec((tile, d), lambda i, j: (i, j)), tile, nb)
        zl_rm = _inproj(xl, mod_l, gpm, w_in_b[l], rm_groups,
                        jax.ShapeDtypeStruct((seq * nb, 4 * d), F32),
                        pl.BlockSpec((tile, d), lambda i, j: (i, j)), tile, nb)
        zc_hg = _inproj(xc, mod_c, gpm, w_in_b[l], hg_groups,
                        jax.ShapeDtypeStruct((ctx_len, nb, ngroups * d), F32),
                        pl.BlockSpec((grid_w, nb, d), lambda i, j: (i, 0, j)), tile, nb)
        zl_hg = _inproj(xl, mod_l, gpm, w_in_b[l], hg_groups,
                        jax.ShapeDtypeStruct((grid_w, rows_g, nb, ngroups * d), F32),
                        pl.BlockSpec((grid_w, None, nb, d), lambda i, j: (0, i, 0, j)), tile, nb)
        zl_hg = zl_hg.reshape(seq, nb, ngroups * d)

        zero_h = jnp.zeros((nb, d), F32)
        rg = functools.partial(_rg_dir, conv_w=rg_conv_w[l], conv_b=conv_b, nb=nb, tt=grid_w, d=d)
        hf_c, st = rg(zc_rm, None, wg=wg[0], bg=bg[0], lam=rg_lam[l, 0:1], h0=zero_h, reverse=False)
        hf_l, _ = rg(zl_rm, None, wg=wg[0], bg=bg[0], lam=rg_lam[l, 0:1], h0=st, reverse=False)
        yrg_c, st = rg(zc_rm, hf_c, wg=wg[1], bg=bg[1], lam=rg_lam[l, 1:2], h0=zero_h, reverse=True)
        yrg_l, _ = rg(zl_rm, hf_l, wg=wg[1], bg=bg[1], lam=rg_lam[l, 1:2], h0=st, reverse=True)

        zero_s = jnp.zeros((nb, heads, HG_EXPAND, HG_EXPAND), F32)
        hg = functools.partial(_hg_dir, par=hg_par[l], d=d)
        of_c, st = hg(zc_hg, None, s0=zero_s, reverse=False)
        of_l, _ = hg(zl_hg, None, s0=st, reverse=False)
        yhg_c, st = hg(zc_hg, of_c, s0=zero_s, reverse=True)
        yhg_l, _ = hg(zl_hg, of_l, s0=st, reverse=True)

        gpo = g_post_mix[l].reshape(1, d)
        yhg_l4 = yhg_l.reshape(grid_w, rows_g, nb, d)
        xl = _merge(yrg_l, yhg_l4, pl.BlockSpec((grid_w, None, nb, d), lambda i: (0, i, 0, 0)),
                    zl_rm, xl, mod_l, gpo, wr_b[l], wh_b[l], wo_b[l], tm=tile, nb=nb)
        ffn = functools.partial(_ffn, g1=g_pre_ffn[l].reshape(1, d), g2=g_post_ffn[l].reshape(1, d),
                                w_up=wup_b[l], cw=ffn_conv_w[l], cb=ffn_conv_b[l].reshape(1, dff),
                                w_down=wdn_b[l], tm=tile, tf=tf, nb=nb)
        xl = ffn(xl, mod_l, halo=False)
        if need_ctx:
            xc = _merge(yrg_c, yhg_c, pl.BlockSpec((grid_w, nb, d), lambda i: (i, 0, 0)),
                        zc_rm, xc, mod_c, gpo, wr_b[l], wh_b[l], wo_b[l], tm=tile, nb=nb)
            xc = ffn(xc, mod_c, halo=True)

    return jnp.swapaxes(xl.reshape(seq, nb, d), 0, 1)


def kernel(x, c, ctx, c_ctx, w_ada, b_ada, g_pre_mix, g_post_mix, g_pre_ffn, g_post_ffn, w_in, rg_conv_w, rg_conv_b, rg_wa, rg_ba, rg_wx, rg_bx, rg_lam, hg_lb_logits, hg_out_norm, w_proj_rg, w_proj_hg, w_out, ffn_w_up, ffn_conv_w, ffn_conv_b, ffn_w_down):
    return _forward(x, c, ctx, c_ctx, w_ada, b_ada, g_pre_mix, g_post_mix, g_pre_ffn, g_post_ffn,
                    w_in, rg_conv_w, rg_conv_b, rg_wa, rg_ba, rg_wx, rg_bx, rg_lam, hg_lb_logits,
                    hg_out_norm, w_proj_rg, w_proj_hg, w_out, ffn_w_up, ffn_conv_w, ffn_conv_b,
                    ffn_w_down, grid_w=GRID_W)
```

```python
import functools

import jax
import jax.numpy as jnp
from jax import lax
from jax.experimental import pallas as pl
from jax.experimental.pallas import tpu as pltpu

GRID_W = 64
RG_HEADS = 16
RG_C = 8.0
HG_EXPAND = 128
N_MOD = 6
EPS = 1e-6
RG_CONV_PAD_L = 1
RG_CONV_PAD_R = 2

HG_CHUNK = 128
HG_DIAG = 8
HG_DIAG_CLAMP = 20.0
HG_F_FLOOR = 1e-37
HG_PITCH = 24
RG_COLS = 256
VMEM_LIMIT = 56 * 1024 * 1024

LOG2_E = 1.4426950408889634

F32 = jnp.float32
BF16 = jnp.bfloat16


def _cparams(sem):
    return pltpu.CompilerParams(dimension_semantics=sem, vmem_limit_bytes=VMEM_LIMIT)


def _sigmoid(x):
    return 0.5 + 0.5 * jnp.tanh(0.5 * x)


def _silu(x):
    return x * _sigmoid(x)


def _gelu_tanh(x):
    c = 0.7978845608028654
    return 0.5 * x * (1.0 + jnp.tanh(c * (x + 0.044715 * (x * x * x))))


def _rms(x, g):
    ms = jnp.mean(x * x, axis=-1, keepdims=True)
    return x * lax.rsqrt(ms + EPS) * g


def _per_batch(x, vec, nb):
    r, d = x.shape
    return x.reshape(r // nb, nb, d), vec[None]


def _norm_mod(x, g, shift, scale, nb):
    y = _rms(x, g)
    y3, sc = _per_batch(y, scale, nb)
    h = y3 * (1.0 + sc) + shift[None]
    return h.reshape(x.shape)


def _ada_kernel(c_ref, w_ref, b_ref, o_ref):
    s = _silu(c_ref[...]).astype(BF16)
    o_ref[0] = jnp.dot(s, w_ref[0], preferred_element_type=F32) + b_ref[0]


def _ada_mod(cc, w_ada, b_ada):
    depth, d, n = w_ada.shape
    tn = 1536 if n % 1536 == 0 else n
    return pl.pallas_call(
        _ada_kernel,
        out_shape=jax.ShapeDtypeStruct((depth, cc.shape[0], n), F32),
        grid=(depth, n // tn),
        in_specs=[pl.BlockSpec(cc.shape, lambda l, j: (0, 0)),
                  pl.BlockSpec((1, d, tn), lambda l, j: (l, 0, j)),
                  pl.BlockSpec((1, 1, tn), lambda l, j: (l, 0, j))],
        out_specs=pl.BlockSpec((1, cc.shape[0], tn), lambda l, j: (l, 0, j)),
        compiler_params=_cparams(("parallel", "parallel")),
        name="ada_mod",
    )(cc, w_ada, b_ada.reshape(depth, 1, n))


def _inproj_kernel(x_ref, mod_ref, g_ref, w_ref, o_ref, h_scr, *, nb, d):
    @pl.when(pl.program_id(1) == 0)
    def _():
        h = _norm_mod(x_ref[...], g_ref[...], mod_ref[:, 0:d], mod_ref[:, d:2 * d], nb)
        h_scr[...] = h.astype(BF16)

    res = jnp.dot(h_scr[...], w_ref[...], preferred_element_type=F32)
    o_ref[...] = res.reshape(o_ref.shape).astype(o_ref.dtype)


def _inproj(x, mod, g, w, col_groups, out_shape, out_spec, tm, nb):
    n, d = x.shape

    def wmap(i, j):
        idx = jnp.int32(col_groups[0])
        for k in range(1, len(col_groups)):
            idx = jnp.where(j == k, jnp.int32(col_groups[k]), idx)
        return (0, idx)

    return pl.pallas_call(
        functools.partial(_inproj_kernel, nb=nb, d=d),
        out_shape=out_shape,
        grid=(n // tm, len(col_groups)),
        in_specs=[pl.BlockSpec((tm, d), lambda i, j: (i, 0)),
                  pl.BlockSpec(mod.shape, lambda i, j: (0, 0)),
                  pl.BlockSpec((1, d), lambda i, j: (0, 0)),
                  pl.BlockSpec((d, d), wmap)],
        out_specs=out_spec,
        scratch_shapes=[pltpu.VMEM((tm, d), BF16)],
        compiler_params=_cparams(("parallel", "arbitrary")),
        name="inproj",
    )(x, mod, g, w)


def _rg_kernel(*refs, reverse, nb, tt, nt):
    if reverse:
        (x_ref, xp_ref, xn_ref, cw_ref, cb_ref, wg_ref, bg_ref, lam_ref, h0_ref, hf_ref, gg_ref,
         y_ref, hfin_ref, a_scr, b_scr, h_scr) = refs
    else:
        (x_ref, xp_ref, xn_ref, cw_ref, cb_ref, wg_ref, bg_ref, lam_ref, h0_ref,
         y_ref, hfin_ref, a_scr, b_scr, h_scr) = refs
    s = pl.program_id(1)
    tile = (nt - 1 - s) if reverse else s
    rows = tt * nb
    cols = x_ref.shape[1]

    @pl.when(s == 0)
    def _():
        h_scr[...] = h0_ref[...]

    prev = jnp.where(tile > 0, xp_ref[...].astype(F32), 0.0)
    nxt = jnp.where(tile < nt - 1, xn_ref[...].astype(F32), 0.0)
    xe = jnp.concatenate([prev, x_ref[...].astype(F32), nxt], axis=0)
    xl = cb_ref[...] + xe[0:rows] * cw_ref[0:1, :]
    for j in range(1, 4):
        xl = xl + xe[j * nb:j * nb + rows] * cw_ref[j:j + 1, :]

    gates = jnp.dot(xl.astype(BF16), wg_ref[0], preferred_element_type=F32) + bg_ref[0]
    r = _sigmoid(gates[:, :cols])
    ig = _sigmoid(gates[:, cols:])
    log_a = (-RG_C * jax.nn.softplus(-lam_ref[...])) * r
    a = jnp.exp(log_a)
    a_scr[...] = a
    b_scr[...] = jnp.sqrt((1.0 + a * a) * jnp.tanh(-log_a)) * (ig * xl)

    def body(t, h):
        tloc = (tt - 1 - t) if reverse else t
        off = pl.multiple_of(tloc * nb, nb)
        h = a_scr[pl.ds(off, nb), :] * h + b_scr[pl.ds(off, nb), :]
        b_scr[pl.ds(off, nb), :] = h
        return h

    h = lax.fori_loop(0, tt, body, h_scr[...], unroll=8)
    h_scr[...] = h
    hfin_ref[...] = h
    if reverse:
        gate = _gelu_tanh(gg_ref[...].astype(F32))
        y_ref[...] = (gate * (hf_ref[...] + b_scr[...])).astype(y_ref.dtype)
    else:
        y_ref[...] = b_scr[...]


def _rg_dir(zrm, hf, conv_w, conv_b, wg, bg, lam, h0, *, reverse, nb, tt, d):
    n = zrm.shape[0]
    nt = n // (tt * nb)
    ncb = d // RG_COLS
    rows = tt * nb
    ppb = rows // nb
    npb = rows // (2 * nb)
    n_next_blocks = n // (2 * nb)

    def tile_of(s):
        return (nt - 1 - s) if reverse else s

    in_specs = [
        pl.BlockSpec((rows, RG_COLS), lambda c, s: (tile_of(s), c)),
        pl.BlockSpec((nb, RG_COLS), lambda c, s: (jnp.maximum(tile_of(s) * ppb - 1, 0), c)),
        pl.BlockSpec((2 * nb, RG_COLS),
                     lambda c, s: (jnp.minimum((tile_of(s) + 1) * npb, n_next_blocks - 1), c)),
        pl.BlockSpec((4, RG_COLS), lambda c, s: (0, c)),
        pl.BlockSpec((1, RG_COLS), lambda c, s: (0, c)),
        pl.BlockSpec((1, RG_COLS, 2 * RG_COLS), lambda c, s: (c, 0, 0)),
        pl.BlockSpec((1, 1, 2 * RG_COLS), lambda c, s: (c, 0, 0)),
        pl.BlockSpec((1, RG_COLS), lambda c, s: (0, c)),
        pl.BlockSpec((nb, RG_COLS), lambda c, s: (0, c)),
    ]
    args = [zrm, zrm, zrm, conv_w, conv_b, wg, bg, lam, h0]
    if reverse:
        in_specs += [pl.BlockSpec((rows, RG_COLS), lambda c, s: (tile_of(s), c)),
                     pl.BlockSpec((rows, RG_COLS), lambda c, s: (tile_of(s), ncb + c))]
        args += [hf, zrm]
    out_dtype = BF16 if reverse else F32
    return pl.pallas_call(
        functools.partial(_rg_kernel, reverse=reverse, nb=nb, tt=tt, nt=nt),
        out_shape=(jax.ShapeDtypeStruct((n, d), out_dtype), jax.ShapeDtypeStruct((nb, d), F32)),
        grid=(ncb, nt),
        in_specs=in_specs,
        out_specs=(pl.BlockSpec((rows, RG_COLS), lambda c, s: (tile_of(s), c)),
                   pl.BlockSpec((nb, RG_COLS), lambda c, s: (0, c))),
        scratch_shapes=[pltpu.VMEM((rows, RG_COLS), F32), pltpu.VMEM((rows, RG_COLS), F32),
                        pltpu.VMEM((nb, RG_COLS), F32)],
        compiler_params=_cparams(("parallel", "arbitrary")),
        name="rg_bwd" if reverse else "rg_fwd",
    )(*args)


def _cumsum_time(x, reverse, block):
    c = x.shape[0]
    x4 = x.reshape((c // block, block) + x.shape[1:])
    cols = [None] * block
    order = range(block - 1, -1, -1) if reverse else range(block)
    run = None
    for i in order:
        run = x4[:, i] if run is None else run + x4[:, i]
        cols[i] = run
    return jnp.stack(cols, axis=1).reshape(x.shape)


def _cumsum_chunk(x, reverse):
    c = x.shape[0]
    blk = 8
    x4 = _cumsum_time(x, reverse, blk).reshape((c // blk, blk) + x.shape[1:])
    nblk = c // blk
    tot = x4[:, 0] if reverse else x4[:, blk - 1]
    offs = [None] * nblk
    order = range(nblk - 1, -1, -1) if reverse else range(nblk)
    run = None
    for i in order:
        offs[i] = run
        run = tot[i] if run is None else run + tot[i]
    first = nblk - 1 if reverse else 0
    parts = [x4[i] if i == first else x4[i] + offs[i][None] for i in range(nblk)]
    return jnp.stack(parts, axis=0).reshape(x.shape)


def _pivot_time(c, block, p):
    c4 = c.reshape((c.shape[0] // block, block) + c.shape[1:])
    return jnp.broadcast_to(c4[:, p:p + 1], c4.shape).reshape(c.shape)


def _hg_levels(c):
    out, m = [], c // 2
    while m >= HG_DIAG:
        out.append(m)
        m //= 2
    return out


def _hg_kernel(*refs, reverse, final):
    if final:
        (zq_ref, zf_ref, v_ref, of_ref, og_ref, par_ref, s0_ref, y_ref, sout_ref,
         st_scr, tr_scr, o_scr, mm_scr) = refs
    else:
        (zq_ref, zf_ref, v_ref, par_ref, s0_ref, y_ref, sout_ref, st_scr, tr_scr, o_scr,
         mm_scr) = refs
    j = pl.program_id(1)
    nj = pl.num_programs(1)
    c, nb, kd = zq_ref.shape

    @pl.when(j == 0)
    def _():
        st_scr[...] = s0_ref[...]

    zq = zq_ref[...].astype(F32)
    zf = zf_ref[...].astype(F32)
    lb = par_ref[0:1, :][None]
    oml = par_ref[1:2, :][None]

    q = _silu(zq)
    e = jnp.exp(-jnp.abs(zf))
    r = 1.0 / (1.0 + e)
    pos = zf >= 0.0
    f = lb + oml * (jnp.where(pos, 1.0, e) * r)
    k = oml * (jnp.where(pos, e, 1.0) * r)
    g = jnp.log2(jnp.maximum(f, HG_F_FLOOR))
    cs = _cumsum_chunk(g, reverse)
    cd = _cumsum_time(jnp.maximum(g, -HG_DIAG_CLAMP * LOG2_E), reverse, HG_DIAG)
    ctot = cs[0] if reverse else cs[c - 1]
    decay = jnp.exp2(ctot)

    pad = jnp.zeros((c, HG_PITCH - nb, kd), F32)
    for n, arr in enumerate((q, k, cs, cd, v_ref[...].astype(F32))):
        tr_scr[n] = jnp.concatenate([arr, pad], axis=1).reshape(c * HG_PITCH, kd)

    ti = lax.broadcasted_iota(jnp.int32, (c, c), 0)
    si = lax.broadcasted_iota(jnp.int32, (c, c), 1)
    lev = ti ^ si
    valid = (ti <= si) if reverse else (ti >= si)
    nt_dims = (((1,), (1,)), ((), ()))
    tn_dims = (((0,), (0,)), ((), ()))
    levels = _hg_levels(c)

    for b in range(nb):
        rows = pl.ds(b, c, stride=HG_PITCH)
        q2, k2, cs2, cd2 = (tr_scr[n, rows, :] for n in range(4))
        qb = q2.astype(BF16)
        kb = k2.astype(BF16)
        mm_scr[0, b] = qb * jnp.exp2(cs2).astype(BF16)
        mm_scr[1, b] = kb * jnp.exp2(ctot[b:b + 1, :] - cs2).astype(BF16)
        mm_scr[2, b] = tr_scr[4, rows, :].astype(BF16)
        cdd = cd2 - _pivot_time(cd2, HG_DIAG, HG_DIAG // 2 if reverse else HG_DIAG // 2 - 1)
        a = lax.dot_general(qb * jnp.exp2(cdd).astype(BF16), kb * jnp.exp2(-cdd).astype(BF16),
                            nt_dims, preferred_element_type=F32)
        for m in levels[::-1]:
            piv = m if reverse else m - 1
            el = jnp.exp2(-jnp.abs(cs2 - _pivot_time(cs2, 2 * m, piv))).astype(BF16)
            p = lax.dot_general(qb * el, kb * el, nt_dims, preferred_element_type=F32)
            a = jnp.where(lev < m, a, p)
        mm_scr[3, b] = jnp.where(valid, a, 0.0).astype(BF16)

    for b in range(nb):
        st = st_scr[b]
        vb = mm_scr[2, b]
        o = lax.dot_general(mm_scr[0, b], st.astype(BF16), nt_dims, preferred_element_type=F32)
        o = o + jnp.dot(mm_scr[3, b], vb, preferred_element_type=F32)
        st_scr[b] = st * decay[b:b + 1, :] + lax.dot_general(vb, mm_scr[1, b], tn_dims,
                                                             preferred_element_type=F32)
        o_scr[pl.ds(b, c, stride=HG_PITCH), :] = o

    o = o_scr[...].reshape(c, HG_PITCH, kd)[:, 0:nb, :]
    if final:
        gain = par_ref[3:4, :][None]
        y = _rms(o + of_ref[...], gain) * _silu(og_ref[...].astype(F32))
        y_ref[...] = y.astype(y_ref.dtype)
    else:
        y_ref[...] = o

    @pl.when(j == nj - 1)
    def _():
        sout_ref[...] = st_scr[...]


def _hg_dir(zhg, of, par, s0, *, reverse, d):
    t, nb, _ = zhg.shape
    kd = HG_EXPAND
    heads = d // kd
    nj = t // HG_CHUNK
    final = reverse
    assert nb <= HG_PITCH

    def blk(j):
        return (nj - 1 - j) if reverse else j

    def zspec(gidx):
        return pl.BlockSpec((HG_CHUNK, nb, kd), lambda h, j: (blk(j), 0, gidx * heads + h))

    hspec = pl.BlockSpec((HG_CHUNK, nb, kd), lambda h, j: (blk(j), 0, h))
    sspec = pl.BlockSpec((nb, None, kd, kd), lambda h, j: (0, h, 0, 0))
    in_specs = [zspec(0), zspec(2 if reverse else 1), zspec(3)]
    args = [zhg, zhg, zhg]
    scratch = [pltpu.VMEM((nb, kd, kd), F32), pltpu.VMEM((5, HG_CHUNK * HG_PITCH, kd), F32),
               pltpu.VMEM((HG_CHUNK * HG_PITCH, kd), F32),
               pltpu.VMEM((4, nb, HG_CHUNK, kd), BF16)]
    if final:
        in_specs += [hspec, zspec(4)]
        args += [of, zhg]
    in_specs += [pl.BlockSpec((par.shape[0], kd), lambda h, j: (0, h)), sspec]
    args += [par, s0]
    return pl.pallas_call(
        functools.partial(_hg_kernel, reverse=reverse, final=final),
        out_shape=(jax.ShapeDtypeStruct((t, nb, d), BF16 if final else F32),
                   jax.ShapeDtypeStruct(s0.shape, F32)),
        grid=(heads, nj),
        in_specs=in_specs,
        out_specs=(hspec, sspec),
        scratch_shapes=scratch,
        compiler_params=_cparams(("parallel", "arbitrary")),
        name="hg_bwd" if reverse else "hg_fwd",
    )(*args)


def _merge_kernel(yrg_ref, yhg_ref, ga_ref, gb_ref, x_ref, mod_ref, g_ref, wr_ref, wh_ref, wo_ref,
                  o_ref, *, nb, d):
    tm = x_ref.shape[0]
    p_rg = jnp.dot(yrg_ref[...], wr_ref[...], preferred_element_type=F32)
    p_hg = jnp.dot(yhg_ref[...].reshape(tm, d), wh_ref[...], preferred_element_type=F32)
    m = _sigmoid(ga_ref[...].astype(F32)) * p_rg + _sigmoid(gb_ref[...].astype(F32)) * p_hg
    y = jnp.dot(m.astype(BF16), wo_ref[...], preferred_element_type=F32)
    n3, gate = _per_batch(_rms(y, g_ref[...]), mod_ref[:, 2 * d:3 * d], nb)
    o_ref[...] = x_ref[...] + (gate * n3).reshape(tm, d)


def _merge(yrg, yhg, yhg_spec, zrm, x, mod, g, wr, wh, wo, *, tm, nb):
    n, d = x.shape
    row = lambda i: (i, 0)
    const = lambda i: (0, 0)
    return pl.pallas_call(
        functools.partial(_merge_kernel, nb=nb, d=d),
        out_shape=jax.ShapeDtypeStruct((n, d), F32),
        grid=(n // tm,),
        in_specs=[pl.BlockSpec((tm, d), row), yhg_spec,
                  pl.BlockSpec((tm, d), lambda i: (i, 2)), pl.BlockSpec((tm, d), lambda i: (i, 3)),
                  pl.BlockSpec((tm, d), row), pl.BlockSpec(mod.shape, const),
                  pl.BlockSpec((1, d), const), pl.BlockSpec((d, d), const),
                  pl.BlockSpec((d, d), const), pl.BlockSpec((d, d), const)],
        out_specs=pl.BlockSpec((tm, d), row),
        compiler_params=_cparams(("parallel",)),
        name="merge",
    )(yrg, yhg, zrm, zrm, x, mod, g, wr, wh, wo)


def _ffn_kernel(*refs, nb, d, dff, tf, halo, nt):
    if halo:
        (x_ref, xp_ref, xn_ref, mod_ref, g1_ref, g2_ref, wu_ref, cw_ref, cb_ref, wd_ref,
         o_ref, act_scr) = refs
    else:
        (x_ref, mod_ref, g1_ref, g2_ref, wu_ref, cw_ref, cb_ref, wd_ref, o_ref, act_scr) = refs
    i = pl.program_id(0)
    tm = x_ref.shape[0]
    shift = mod_ref[:, 3 * d:4 * d]
    scale = mod_ref[:, 4 * d:5 * d]

    hm = _norm_mod(x_ref[...], g1_ref[...], shift, scale, nb).astype(BF16)
    if halo:
        hp = _norm_mod(xp_ref[...], g1_ref[...], shift, scale, nb)
        hn = _norm_mod(xn_ref[...], g1_ref[...], shift, scale, nb)
        hp = jnp.where(i > 0, hp, 0.0).astype(BF16)
        hn = jnp.where(i < nt - 1, hn, 0.0).astype(BF16)
        he = jnp.concatenate([hp, hm, hn], axis=0)

    for kb in range(dff // tf):
        gs = slice(kb * tf, (kb + 1) * tf)
        vs = slice(dff + kb * tf, dff + (kb + 1) * tf)
        if halo:
            ue = jnp.dot(he, wu_ref[:, gs], preferred_element_type=F32)
        else:
            u = jnp.dot(hm, wu_ref[:, gs], preferred_element_type=F32)
            z = jnp.zeros((nb, tf), F32)
            ue = jnp.concatenate([z, u, z], axis=0)
        gc = cb_ref[:, gs] + ue[0:tm] * cw_ref[0:1, gs] + ue[nb:nb + tm] * cw_ref[1:2, gs] \
            + ue[2 * nb:2 * nb + tm] * cw_ref[2:3, gs]
        uv = jnp.dot(hm, wu_ref[:, vs], preferred_element_type=F32)
        act_scr[:, gs] = (_silu(gc) * uv).astype(BF16)

    y = jnp.dot(act_scr[...], wd_ref[...], preferred_element_type=F32)
    n3, gate = _per_batch(_rms(y, g2_ref[...]), mod_ref[:, 5 * d:6 * d], nb)
    o_ref[...] = x_ref[...] + (gate * n3).reshape(tm, d)


def _ffn(x, mod, g1, g2, w_up, cw, cb, w_down, *, tm, tf, nb, halo):
    n, d = x.shape
    dff = w_down.shape[0]
    nt = n // tm
    hb = tm // nb
    row = lambda i: (i, 0)
    const = lambda i: (0, 0)

    def resident(shape):
        return pl.BlockSpec(shape, const, pipeline_mode=pl.Buffered(1))

    in_specs = [pl.BlockSpec((tm, d), row)]
    args = [x]
    if halo:
        in_specs += [pl.BlockSpec((nb, d), lambda i: (jnp.maximum(i * hb - 1, 0), 0)),
                     pl.BlockSpec((nb, d), lambda i: (jnp.minimum((i + 1) * hb, n // nb - 1), 0))]
        args += [x, x]
    in_specs += [pl.BlockSpec(mod.shape, const), pl.BlockSpec((1, d), const),
                 pl.BlockSpec((1, d), const), resident(w_up.shape),
                 pl.BlockSpec(cw.shape, const), pl.BlockSpec(cb.shape, const),
                 resident(w_down.shape)]
    args += [mod, g1, g2, w_up, cw, cb, w_down]
    return pl.pallas_call(
        functools.partial(_ffn_kernel, nb=nb, d=d, dff=dff, tf=tf, halo=halo, nt=nt),
        out_shape=jax.ShapeDtypeStruct((n, d), F32),
        grid=(nt,),
        in_specs=in_specs,
        out_specs=pl.BlockSpec((tm, d), row),
        scratch_shapes=[pltpu.VMEM((tm, dff), BF16)],
        compiler_params=_cparams(("parallel",)),
        name="ffn_ctx" if halo else "ffn",
    )(*args)


def _gate_weights(wa, wx):
    nd, heads, hd, _ = wa.shape
    per = RG_COLS // hd
    ncb = heads // per

    def bd(w):
        w = w.reshape(nd, ncb, per, hd, hd)
        eye = jnp.eye(per, dtype=w.dtype)
        return jnp.einsum('dcpij,pq->dcpiqj', w, eye).reshape(nd, ncb, RG_COLS, RG_COLS)

    return jnp.concatenate([bd(wa), bd(wx)], axis=-1).astype(BF16)


def _forward(x, c, ctx, c_ctx, w_ada, b_ada, g_pre_mix, g_post_mix, g_pre_ffn, g_post_ffn, w_in,
             rg_conv_w, rg_conv_b, rg_wa, rg_ba, rg_wx, rg_bx, rg_lam, hg_lb_logits, hg_out_norm,
             w_proj_rg, w_proj_hg, w_out, ffn_w_up, ffn_conv_w, ffn_conv_b, ffn_w_down, *, grid_w):
    nb, seq, d = x.shape
    ctx_len = ctx.shape[1]
    depth = w_in.shape[0]
    rows_g = seq // grid_w
    dff = ffn_w_down.shape[1]
    tile = grid_w * nb
    ngroups = 5
    heads = d // HG_EXPAND
    tf = 256 if dff % 256 == 0 else dff

    p = jax.nn.softmax(hg_lb_logits.astype(F32), axis=0)
    cum = jnp.cumsum(p, axis=0)
    lb_all = cum - cum[0:1]
    hg_par = jnp.stack([lb_all, 1.0 - lb_all, jnp.zeros_like(lb_all), hg_out_norm], axis=1)
    hg_par = jnp.pad(hg_par, ((0, 0), (0, 4), (0, 0)))

    w_ada_b, w_in_b = w_ada.astype(BF16), w_in.astype(BF16)
    wr_b, wh_b, wo_b = w_proj_rg.astype(BF16), w_proj_hg.astype(BF16), w_out.astype(BF16)
    wup_b, wdn_b = ffn_w_up.astype(BF16), ffn_w_down.astype(BF16)

    cc = jnp.concatenate([c, jnp.broadcast_to(c_ctx[None], (nb, d))], axis=0)
    mod_all = _ada_mod(cc, w_ada_b, b_ada).reshape(depth, 2, nb, N_MOD * d)

    xl = jnp.swapaxes(x, 0, 1).reshape(seq * nb, d)
    xc = jnp.swapaxes(ctx, 0, 1).reshape(ctx_len * nb, d)

    for l in range(depth):
        need_ctx = l < depth - 1
        mod_l, mod_c = mod_all[l, 0], mod_all[l, 1]
        gpm = g_pre_mix[l].reshape(1, d)
        wg = _gate_weights(rg_wa[l], rg_wx[l])
        ncb = d // RG_COLS
        bg = jnp.concatenate([rg_ba[l].reshape(2, ncb, 1, RG_COLS),
                              rg_bx[l].reshape(2, ncb, 1, RG_COLS)], axis=-1)
        conv_b = rg_conv_b[l].reshape(1, d)

        rm_groups, hg_groups = (0, 1, 7, 8), (2, 3, 4, 5, 6)
        zc_rm = _inproj(xc, mod_c, gpm, w_in_b[l], rm_groups,
                        jax.ShapeDtypeStruct((ctx_len * nb, 4 * d), BF16),
                        pl.BlockSpec((tile, d), lambda i, j: (i, j)), tile, nb)
        zl_rm = _inproj(xl, mod_l, gpm, w_in_b[l], rm_groups,
                        jax.ShapeDtypeStruct((seq * nb, 4 * d), BF16),
                        pl.BlockSpec((tile, d), lambda i, j: (i, j)), tile, nb)
        zc_hg = _inproj(xc, mod_c, gpm, w_in_b[l], hg_groups,
                        jax.ShapeDtypeStruct((ctx_len, nb, ngroups * d), BF16),
                        pl.BlockSpec((grid_w, nb, d), lambda i, j: (i, 0, j)), tile, nb)
        zl_hg = _inproj(xl, mod_l, gpm, w_in_b[l], hg_groups,
                        jax.ShapeDtypeStruct((grid_w, rows_g, nb, ngroups * d), BF16),
                        pl.BlockSpec((grid_w, None, nb, d), lambda i, j: (0, i, 0, j)), tile, nb)
        zl_hg = zl_hg.reshape(seq, nb, ngroups * d)

        zero_h = jnp.zeros((nb, d), F32)
        rg = functools.partial(_rg_dir, conv_w=rg_conv_w[l], conv_b=conv_b, nb=nb, tt=grid_w, d=d)
        hf_c, st = rg(zc_rm, None, wg=wg[0], bg=bg[0], lam=rg_lam[l, 0:1], h0=zero_h, reverse=False)
        hf_l, _ = rg(zl_rm, None, wg=wg[0], bg=bg[0], lam=rg_lam[l, 0:1], h0=st, reverse=False)
        yrg_c, st = rg(zc_rm, hf_c, wg=wg[1], bg=bg[1], lam=rg_lam[l, 1:2], h0=zero_h, reverse=True)
        yrg_l, _ = rg(zl_rm, hf_l, wg=wg[1], bg=bg[1], lam=rg_lam[l, 1:2], h0=st, reverse=True)

        zero_s = jnp.zeros((nb, heads, HG_EXPAND, HG_EXPAND), F32)
        hg = functools.partial(_hg_dir, par=hg_par[l], d=d)
        of_c, st = hg(zc_hg, None, s0=zero_s, reverse=False)
        of_l, _ = hg(zl_hg, None, s0=st, reverse=False)
        yhg_c, st = hg(zc_hg, of_c, s0=zero_s, reverse=True)
        yhg_l, _ = hg(zl_hg, of_l, s0=st, reverse=True)

        gpo = g_post_mix[l].reshape(1, d)
        yhg_l4 = yhg_l.reshape(grid_w, rows_g, nb, d)
        xl = _merge(yrg_l, yhg_l4, pl.BlockSpec((grid_w, None, nb, d), lambda i: (0, i, 0, 0)),
                    zl_rm, xl, mod_l, gpo, wr_b[l], wh_b[l], wo_b[l], tm=tile, nb=nb)
        ffn = functools.partial(_ffn, g1=g_pre_ffn[l].reshape(1, d), g2=g_post_ffn[l].reshape(1, d),
                                w_up=wup_b[l], cw=ffn_conv_w[l], cb=ffn_conv_b[l].reshape(1, dff),
                                w_down=wdn_b[l], tm=tile, tf=tf, nb=nb)
        xl = ffn(xl, mod_l, halo=False)
        if need_ctx:
            xc = _merge(yrg_c, yhg_c, pl.BlockSpec((grid_w, nb, d), lambda i: (i, 0, 0)),
                        zc_rm, xc, mod_c, gpo, wr_b[l], wh_b[l], wo_b[l], tm=tile, nb=nb)
            xc = ffn(xc, mod_c, halo=True)

    return jnp.swapaxes(xl.reshape(seq, nb, d), 0, 1)


def kernel(x, c, ctx, c_ctx, w_ada, b_ada, g_pre_mix, g_post_mix, g_pre_ffn, g_post_ffn, w_in, rg_conv_w, rg_conv_b, rg_wa, rg_ba, rg_wx, rg_bx, rg_lam, hg_lb_logits, hg_out_norm, w_proj_rg, w_proj_hg, w_out, ffn_w_up, ffn_conv_w, ffn_conv_b, ffn_w_down):
    return _forward(x, c, ctx, c_ctx, w_ada, b_ada, g_pre_mix, g_post_mix, g_pre_ffn, g_post_ffn,
                    w_in, rg_conv_w, rg_conv_b, rg_wa, rg_ba, rg_wx, rg_bx, rg_lam, hg_lb_logits,
                    hg_out_norm, w_proj_rg, w_proj_hg, w_out, ffn_w_up, ffn_conv_w, ffn_conv_b,
                    ffn_w_down, grid_w=GRID_W)
```

```python
import functools

import jax
import jax.numpy as jnp
from jax import lax
from jax.experimental import pallas as pl
from jax.experimental.pallas import tpu as pltpu

GRID_W = 64
RG_HEADS = 16
RG_C = 8.0
HG_EXPAND = 128
N_MOD = 6
EPS = 1e-6
RG_CONV_PAD_L = 1
RG_CONV_PAD_R = 2

HG_CHUNK = 128
HG_DIAG = 8
HG_DIAG_CLAMP = 20.0
HG_F_FLOOR = 1e-37
HG_PITCH = 24
RG_TINY = 1e-30
RG_STEPS = 128
RG_COLS = 256
VMEM_LIMIT = 56 * 1024 * 1024

LOG2_E = 1.4426950408889634

F32 = jnp.float32
BF16 = jnp.bfloat16


def _cparams(sem):
    return pltpu.CompilerParams(dimension_semantics=sem, vmem_limit_bytes=VMEM_LIMIT)


def _sigmoid(x):
    return 0.5 + 0.5 * jnp.tanh(0.5 * x)


def _silu(x):
    return x * _sigmoid(x)


def _gelu_tanh(x):
    c = 0.7978845608028654
    return 0.5 * x * (1.0 + jnp.tanh(c * (x + 0.044715 * (x * x * x))))


def _rms(x, g):
    ms = jnp.mean(x * x, axis=-1, keepdims=True)
    return x * lax.rsqrt(ms + EPS) * g


def _per_batch(x, vec, nb):
    r, d = x.shape
    return x.reshape(r // nb, nb, d), vec[None]


def _norm_mod(x, g, shift, scale, nb):
    y = _rms(x, g)
    y3, sc = _per_batch(y, scale, nb)
    h = y3 * (1.0 + sc) + shift[None]
    return h.reshape(x.shape)


def _ada_kernel(c_ref, w_ref, b_ref, o_ref):
    s = _silu(c_ref[...]).astype(BF16)
    o_ref[0] = jnp.dot(s, w_ref[0], preferred_element_type=F32) + b_ref[0]


def _ada_mod(cc, w_ada, b_ada):
    depth, d, n = w_ada.shape
    tn = 1536 if n % 1536 == 0 else n
    return pl.pallas_call(
        _ada_kernel,
        out_shape=jax.ShapeDtypeStruct((depth, cc.shape[0], n), F32),
        grid=(depth, n // tn),
        in_specs=[pl.BlockSpec(cc.shape, lambda l, j: (0, 0)),
                  pl.BlockSpec((1, d, tn), lambda l, j: (l, 0, j)),
                  pl.BlockSpec((1, 1, tn), lambda l, j: (l, 0, j))],
        out_specs=pl.BlockSpec((1, cc.shape[0], tn), lambda l, j: (l, 0, j)),
        compiler_params=_cparams(("parallel", "parallel")),
        name="ada_mod",
    )(cc, w_ada, b_ada.reshape(depth, 1, n))


def _inproj_kernel(x_ref, mod_ref, g_ref, w_ref, o_ref, h_scr, *, nb, d):
    @pl.when(pl.program_id(1) == 0)
    def _():
        h = _norm_mod(x_ref[...], g_ref[...], mod_ref[:, 0:d], mod_ref[:, d:2 * d], nb)
        h_scr[...] = h.astype(BF16)

    res = jnp.dot(h_scr[...], w_ref[...], preferred_element_type=F32)
    o_ref[...] = res.reshape(o_ref.shape).astype(o_ref.dtype)


def _inproj(x, mod, g, w, col_groups, out_shape, out_spec, tm, nb):
    n, d = x.shape

    def wmap(i, j):
        idx = jnp.int32(col_groups[0])
        for k in range(1, len(col_groups)):
            idx = jnp.where(j == k, jnp.int32(col_groups[k]), idx)
        return (0, idx)

    return pl.pallas_call(
        functools.partial(_inproj_kernel, nb=nb, d=d),
        out_shape=out_shape,
        grid=(n // tm, len(col_groups)),
        in_specs=[pl.BlockSpec((tm, d), lambda i, j: (i, 0)),
                  pl.BlockSpec(mod.shape, lambda i, j: (0, 0)),
                  pl.BlockSpec((1, d), lambda i, j: (0, 0)),
                  pl.BlockSpec((d, d), wmap)],
        out_specs=out_spec,
        scratch_shapes=[pltpu.VMEM((tm, d), BF16)],
        compiler_params=_cparams(("parallel", "arbitrary")),
        name="inproj",
    )(x, mod, g, w)


def _rg_kernel(*refs, reverse, nb, tt, nt):
    if reverse:
        (x_ref, xp_ref, xn_ref, cw_ref, cb_ref, wg_ref, bg_ref, lam_ref, h0_ref, hf_ref, gg_ref,
         y_ref, hfin_ref, a_scr, b_scr, h_scr) = refs
    else:
        (x_ref, xp_ref, xn_ref, cw_ref, cb_ref, wg_ref, bg_ref, lam_ref, h0_ref,
         y_ref, hfin_ref, a_scr, b_scr, h_scr) = refs
    s = pl.program_id(1)
    tile = (nt - 1 - s) if reverse else s
    rows = tt * nb
    cols = x_ref.shape[1]

    @pl.when(s == 0)
    def _():
        h_scr[...] = h0_ref[...]

    prev = jnp.where(tile > 0, xp_ref[...].astype(F32), 0.0)
    nxt = jnp.where(tile < nt - 1, xn_ref[...].astype(F32), 0.0)
    xe = jnp.concatenate([prev, x_ref[...].astype(F32), nxt], axis=0)
    xl = cb_ref[...] + xe[0:rows] * cw_ref[0:1, :]
    for j in range(1, 4):
        xl = xl + xe[j * nb:j * nb + rows] * cw_ref[j:j + 1, :]

    th = jnp.tanh(jnp.dot(xl.astype(BF16), wg_ref[0], preferred_element_type=F32) + bg_ref[0])
    half_rate = (-0.5 * RG_C) * jax.nn.softplus(-lam_ref[...])
    log_a = half_rate + half_rate * th[:, :cols]
    ig = 0.5 + 0.5 * th[:, cols:]
    a = jnp.exp(log_a)
    a_scr[...] = a
    om = (1.0 + a * a) * jnp.tanh(-log_a)
    b_scr[...] = (om * lax.rsqrt(jnp.maximum(om, RG_TINY))) * (ig * xl)

    def body(t, h):
        tloc = (tt - 1 - t) if reverse else t
        off = pl.multiple_of(tloc * nb, nb)
        h = a_scr[pl.ds(off, nb), :] * h + b_scr[pl.ds(off, nb), :]
        b_scr[pl.ds(off, nb), :] = h
        return h

    h = lax.fori_loop(0, tt, body, h_scr[...], unroll=8)
    h_scr[...] = h
    hfin_ref[...] = h
    if reverse:
        gate = _gelu_tanh(gg_ref[...].astype(F32))
        y_ref[...] = (gate * (hf_ref[...] + b_scr[...])).astype(y_ref.dtype)
    else:
        y_ref[...] = b_scr[...]


def _rg_dir(zrm, hf, conv_w, conv_b, wg, bg, lam, h0, *, reverse, nb, tt, d):
    n = zrm.shape[0]
    nt = n // (tt * nb)
    ncb = d // RG_COLS
    rows = tt * nb
    ppb = rows // nb
    npb = rows // (2 * nb)
    n_next_blocks = n // (2 * nb)

    def tile_of(s):
        return (nt - 1 - s) if reverse else s

    in_specs = [
        pl.BlockSpec((rows, RG_COLS), lambda c, s: (tile_of(s), c)),
        pl.BlockSpec((nb, RG_COLS), lambda c, s: (jnp.maximum(tile_of(s) * ppb - 1, 0), c)),
        pl.BlockSpec((2 * nb, RG_COLS),
                     lambda c, s: (jnp.minimum((tile_of(s) + 1) * npb, n_next_blocks - 1), c)),
        pl.BlockSpec((4, RG_COLS), lambda c, s: (0, c)),
        pl.BlockSpec((1, RG_COLS), lambda c, s: (0, c)),
        pl.BlockSpec((1, RG_COLS, 2 * RG_COLS), lambda c, s: (c, 0, 0)),
        pl.BlockSpec((1, 1, 2 * RG_COLS), lambda c, s: (c, 0, 0)),
        pl.BlockSpec((1, RG_COLS), lambda c, s: (0, c)),
        pl.BlockSpec((nb, RG_COLS), lambda c, s: (0, c)),
    ]
    args = [zrm, zrm, zrm, conv_w, conv_b, wg, bg, lam, h0]
    if reverse:
        in_specs += [pl.BlockSpec((rows, RG_COLS), lambda c, s: (tile_of(s), c)),
                     pl.BlockSpec((rows, RG_COLS), lambda c, s: (tile_of(s), ncb + c))]
        args += [hf, zrm]
    out_dtype = BF16 if reverse else F32
    return pl.pallas_call(
        functools.partial(_rg_kernel, reverse=reverse, nb=nb, tt=tt, nt=nt),
        out_shape=(jax.ShapeDtypeStruct((n, d), out_dtype), jax.ShapeDtypeStruct((nb, d), F32)),
        grid=(ncb, nt),
        in_specs=in_specs,
        out_specs=(pl.BlockSpec((rows, RG_COLS), lambda c, s: (tile_of(s), c)),
                   pl.BlockSpec((nb, RG_COLS), lambda c, s: (0, c))),
        scratch_shapes=[pltpu.VMEM((rows, RG_COLS), F32), pltpu.VMEM((rows, RG_COLS), F32),
                        pltpu.VMEM((nb, RG_COLS), F32)],
        compiler_params=_cparams(("parallel", "arbitrary")),
        name="rg_bwd" if reverse else "rg_fwd",
    )(*args)


def _cumsum_time(x, reverse, block):
    c = x.shape[0]
    x4 = x.reshape((c // block, block) + x.shape[1:])
    cols = [None] * block
    order = range(block - 1, -1, -1) if reverse else range(block)
    run = None
    for i in order:
        run = x4[:, i] if run is None else run + x4[:, i]
        cols[i] = run
    return jnp.stack(cols, axis=1).reshape(x.shape)


def _cumsum_chunk(x, reverse):
    c = x.shape[0]
    blk = 8
    x4 = _cumsum_time(x, reverse, blk).reshape((c // blk, blk) + x.shape[1:])
    nblk = c // blk
    tot = x4[:, 0] if reverse else x4[:, blk - 1]
    offs = [None] * nblk
    order = range(nblk - 1, -1, -1) if reverse else range(nblk)
    run = None
    for i in order:
        offs[i] = run
        run = tot[i] if run is None else run + tot[i]
    first = nblk - 1 if reverse else 0
    parts = [x4[i] if i == first else x4[i] + offs[i][None] for i in range(nblk)]
    return jnp.stack(parts, axis=0).reshape(x.shape)


def _pivot_time(c, block, p):
    c4 = c.reshape((c.shape[0] // block, block) + c.shape[1:])
    return jnp.broadcast_to(c4[:, p:p + 1], c4.shape).reshape(c.shape)


def _hg_levels(c):
    out, m = [], c // 2
    while m >= HG_DIAG:
        out.append(m)
        m //= 2
    return out


def _hg_kernel(*refs, reverse, final):
    if final:
        (zq_ref, zf_ref, v_ref, of_ref, og_ref, par_ref, s0_ref, y_ref, sout_ref,
         st_scr, tr_scr, o_scr, mm_scr) = refs
    else:
        (zq_ref, zf_ref, v_ref, par_ref, s0_ref, y_ref, sout_ref, st_scr, tr_scr, o_scr,
         mm_scr) = refs
    j = pl.program_id(1)
    nj = pl.num_programs(1)
    c, nb, kd = zq_ref.shape

    @pl.when(j == 0)
    def _():
        st_scr[...] = s0_ref[...]

    zq = zq_ref[...].astype(F32)
    zf = zf_ref[...].astype(F32)
    lb = par_ref[0:1, :][None]
    oml = par_ref[1:2, :][None]

    q = _silu(zq)
    e = jnp.exp(-jnp.abs(zf))
    r = 1.0 / (1.0 + e)
    pos = zf >= 0.0
    f = lb + oml * (jnp.where(pos, 1.0, e) * r)
    k = oml * (jnp.where(pos, e, 1.0) * r)
    g = jnp.log2(jnp.maximum(f, HG_F_FLOOR))
    cs = _cumsum_chunk(g, reverse)
    cd = _cumsum_time(jnp.maximum(g, -HG_DIAG_CLAMP * LOG2_E), reverse, HG_DIAG)
    ctot = cs[0] if reverse else cs[c - 1]
    decay = jnp.exp2(ctot)

    pad = jnp.zeros((c, HG_PITCH - nb, kd), F32)
    for n, arr in enumerate((q, k, cs, cd, v_ref[...].astype(F32))):
        tr_scr[n] = jnp.concatenate([arr, pad], axis=1).reshape(c * HG_PITCH, kd)

    ti = lax.broadcasted_iota(jnp.int32, (c, c), 0)
    si = lax.broadcasted_iota(jnp.int32, (c, c), 1)
    lev = ti ^ si
    valid = (ti <= si) if reverse else (ti >= si)
    nt_dims = (((1,), (1,)), ((), ()))
    tn_dims = (((0,), (0,)), ((), ()))
    levels = _hg_levels(c)

    for b in range(nb):
        rows = pl.ds(b, c, stride=HG_PITCH)
        q2, k2, cs2, cd2 = (tr_scr[n, rows, :] for n in range(4))
        qb = q2.astype(BF16)
        kb = k2.astype(BF16)
        mm_scr[0, b] = qb * jnp.exp2(cs2).astype(BF16)
        mm_scr[1, b] = kb * jnp.exp2(ctot[b:b + 1, :] - cs2).astype(BF16)
        mm_scr[2, b] = tr_scr[4, rows, :].astype(BF16)
        cdd = cd2 - _pivot_time(cd2, HG_DIAG, HG_DIAG // 2 if reverse else HG_DIAG // 2 - 1)
        a = lax.dot_general(qb * jnp.exp2(cdd).astype(BF16), kb * jnp.exp2(-cdd).astype(BF16),
                            nt_dims, preferred_element_type=F32)
        for m in levels[::-1]:
            piv = m if reverse else m - 1
            el = jnp.exp2(-jnp.abs((cs2 - _pivot_time(cs2, 2 * m, piv)).astype(BF16)))
            p = lax.dot_general(qb * el, kb * el, nt_dims, preferred_element_type=F32)
            a = jnp.where(lev < m, a, p)
        mm_scr[3, b] = jnp.where(valid, a, 0.0).astype(BF16)

    for b in range(nb):
        st = st_scr[b]
        vb = mm_scr[2, b]
        o = lax.dot_general(mm_scr[0, b], st.astype(BF16), nt_dims, preferred_element_type=F32)
        o = o + jnp.dot(mm_scr[3, b], vb, preferred_element_type=F32)
        st_scr[b] = st * decay[b:b + 1, :] + lax.dot_general(vb, mm_scr[1, b], tn_dims,
                                                             preferred_element_type=F32)
        o_scr[pl.ds(b, c, stride=HG_PITCH), :] = o

    o = o_scr[...].reshape(c, HG_PITCH, kd)[:, 0:nb, :]
    if final:
        gain = par_ref[3:4, :][None]
        y = _rms(o + of_ref[...], gain) * _silu(og_ref[...].astype(F32))
        y_ref[...] = y.astype(y_ref.dtype)
    else:
        y_ref[...] = o

    @pl.when(j == nj - 1)
    def _():
        sout_ref[...] = st_scr[...]


def _hg_dir(zhg, of, par, s0, *, reverse, d):
    t, nb, _ = zhg.shape
    kd = HG_EXPAND
    heads = d // kd
    nj = t // HG_CHUNK
    final = reverse
    assert nb <= HG_PITCH

    def blk(j):
        return (nj - 1 - j) if reverse else j

    def zspec(gidx):
        return pl.BlockSpec((HG_CHUNK, nb, kd), lambda h, j: (blk(j), 0, gidx * heads + h))

    hspec = pl.BlockSpec((HG_CHUNK, nb, kd), lambda h, j: (blk(j), 0, h))
    sspec = pl.BlockSpec((nb, None, kd, kd), lambda h, j: (0, h, 0, 0))
    in_specs = [zspec(0), zspec(2 if reverse else 1), zspec(3)]
    args = [zhg, zhg, zhg]
    scratch = [pltpu.VMEM((nb, kd, kd), F32), pltpu.VMEM((5, HG_CHUNK * HG_PITCH, kd), F32),
               pltpu.VMEM((HG_CHUNK * HG_PITCH, kd), F32),
               pltpu.VMEM((4, nb, HG_CHUNK, kd), BF16)]
    if final:
        in_specs += [hspec, zspec(4)]
        args += [of, zhg]
    in_specs += [pl.BlockSpec((par.shape[0], kd), lambda h, j: (0, h)), sspec]
    args += [par, s0]
    return pl.pallas_call(
        functools.partial(_hg_kernel, reverse=reverse, final=final),
        out_shape=(jax.ShapeDtypeStruct((t, nb, d), BF16 if final else F32),
                   jax.ShapeDtypeStruct(s0.shape, F32)),
        grid=(heads, nj),
        in_specs=in_specs,
        out_specs=(hspec, sspec),
        scratch_shapes=scratch,
        compiler_params=_cparams(("parallel", "arbitrary")),
        name="hg_bwd" if reverse else "hg_fwd",
    )(*args)


def _merge_kernel(yrg_ref, yhg_ref, ga_ref, gb_ref, x_ref, mod_ref, g_ref, wr_ref, wh_ref, wo_ref,
                  o_ref, *, nb, d):
    tm = x_ref.shape[0]
    p_rg = jnp.dot(yrg_ref[...], wr_ref[...], preferred_element_type=F32)
    p_hg = jnp.dot(yhg_ref[...].reshape(tm, d), wh_ref[...], preferred_element_type=F32)
    m = _sigmoid(ga_ref[...].astype(F32)) * p_rg + _sigmoid(gb_ref[...].astype(F32)) * p_hg
    y = jnp.dot(m.astype(BF16), wo_ref[...], preferred_element_type=F32)
    n3, gate = _per_batch(_rms(y, g_ref[...]), mod_ref[:, 2 * d:3 * d], nb)
    o_ref[...] = x_ref[...] + (gate * n3).reshape(tm, d)


def _merge(yrg, yhg, yhg_spec, zrm, x, mod, g, wr, wh, wo, *, tm, nb):
    n, d = x.shape
    row = lambda i: (i, 0)
    const = lambda i: (0, 0)
    return pl.pallas_call(
        functools.partial(_merge_kernel, nb=nb, d=d),
        out_shape=jax.ShapeDtypeStruct((n, d), F32),
        grid=(n // tm,),
        in_specs=[pl.BlockSpec((tm, d), row), yhg_spec,
                  pl.BlockSpec((tm, d), lambda i: (i, 2)), pl.BlockSpec((tm, d), lambda i: (i, 3)),
                  pl.BlockSpec((tm, d), row), pl.BlockSpec(mod.shape, const),
                  pl.BlockSpec((1, d), const), pl.BlockSpec((d, d), const),
                  pl.BlockSpec((d, d), const), pl.BlockSpec((d, d), const)],
        out_specs=pl.BlockSpec((tm, d), row),
        compiler_params=_cparams(("parallel",)),
        name="merge",
    )(yrg, yhg, zrm, zrm, x, mod, g, wr, wh, wo)


def _ffn_kernel(*refs, nb, d, dff, tf, halo, nt):
    if halo:
        (x_ref, xp_ref, xn_ref, mod_ref, g1_ref, g2_ref, wu_ref, cw_ref, cb_ref, wd_ref,
         o_ref, act_scr) = refs
    else:
        (x_ref, mod_ref, g1_ref, g2_ref, wu_ref, cw_ref, cb_ref, wd_ref, o_ref, act_scr) = refs
    i = pl.program_id(0)
    tm = x_ref.shape[0]
    shift = mod_ref[:, 3 * d:4 * d]
    scale = mod_ref[:, 4 * d:5 * d]

    hm = _norm_mod(x_ref[...], g1_ref[...], shift, scale, nb).astype(BF16)
    if halo:
        hp = _norm_mod(xp_ref[...], g1_ref[...], shift, scale, nb)
        hn = _norm_mod(xn_ref[...], g1_ref[...], shift, scale, nb)
        hp = jnp.where(i > 0, hp, 0.0).astype(BF16)
        hn = jnp.where(i < nt - 1, hn, 0.0).astype(BF16)
        he = jnp.concatenate([hp, hm, hn], axis=0)

    for kb in range(dff // tf):
        gs = slice(kb * tf, (kb + 1) * tf)
        vs = slice(dff + kb * tf, dff + (kb + 1) * tf)
        if halo:
            ue = jnp.dot(he, wu_ref[:, gs], preferred_element_type=F32)
        else:
            u = jnp.dot(hm, wu_ref[:, gs], preferred_element_type=F32)
            z = jnp.zeros((nb, tf), F32)
            ue = jnp.concatenate([z, u, z], axis=0)
        gc = cb_ref[:, gs] + ue[0:tm] * cw_ref[0:1, gs] + ue[nb:nb + tm] * cw_ref[1:2, gs] \
            + ue[2 * nb:2 * nb + tm] * cw_ref[2:3, gs]
        uv = jnp.dot(hm, wu_ref[:, vs], preferred_element_type=F32)
        act_scr[:, gs] = (_silu(gc) * uv).astype(BF16)

    y = jnp.dot(act_scr[...], wd_ref[...], preferred_element_type=F32)
    n3, gate = _per_batch(_rms(y, g2_ref[...]), mod_ref[:, 5 * d:6 * d], nb)
    o_ref[...] = x_ref[...] + (gate * n3).reshape(tm, d)


def _ffn(x, mod, g1, g2, w_up, cw, cb, w_down, *, tm, tf, nb, halo):
    n, d = x.shape
    dff = w_down.shape[0]
    nt = n // tm
    hb = tm // nb
    row = lambda i: (i, 0)
    const = lambda i: (0, 0)

    def resident(shape):
        return pl.BlockSpec(shape, const, pipeline_mode=pl.Buffered(1))

    in_specs = [pl.BlockSpec((tm, d), row)]
    args = [x]
    if halo:
        in_specs += [pl.BlockSpec((nb, d), lambda i: (jnp.maximum(i * hb - 1, 0), 0)),
                     pl.BlockSpec((nb, d), lambda i: (jnp.minimum((i + 1) * hb, n // nb - 1), 0))]
        args += [x, x]
    in_specs += [pl.BlockSpec(mod.shape, const), pl.BlockSpec((1, d), const),
                 pl.BlockSpec((1, d), const), resident(w_up.shape),
                 pl.BlockSpec(cw.shape, const), pl.BlockSpec(cb.shape, const),
                 resident(w_down.shape)]
    args += [mod, g1, g2, w_up, cw, cb, w_down]
    return pl.pallas_call(
        functools.partial(_ffn_kernel, nb=nb, d=d, dff=dff, tf=tf, halo=halo, nt=nt),
        out_shape=jax.ShapeDtypeStruct((n, d), F32),
        grid=(nt,),
        in_specs=in_specs,
        out_specs=pl.BlockSpec((tm, d), row),
        scratch_shapes=[pltpu.VMEM((tm, dff), BF16)],
        compiler_params=_cparams(("parallel",)),
        name="ffn_ctx" if halo else "ffn",
    )(*args)


def _gate_weights(wa, wx):
    nd, heads, hd, _ = wa.shape
    per = RG_COLS // hd
    ncb = heads // per

    def bd(w):
        w = w.reshape(nd, ncb, per, hd, hd)
        eye = jnp.eye(per, dtype=w.dtype)
        return jnp.einsum('dcpij,pq->dcpiqj', w, eye).reshape(nd, ncb, RG_COLS, RG_COLS)

    return (0.5 * jnp.concatenate([bd(wa), bd(wx)], axis=-1)).astype(BF16)


def _forward(x, c, ctx, c_ctx, w_ada, b_ada, g_pre_mix, g_post_mix, g_pre_ffn, g_post_ffn, w_in,
             rg_conv_w, rg_conv_b, rg_wa, rg_ba, rg_wx, rg_bx, rg_lam, hg_lb_logits, hg_out_norm,
             w_proj_rg, w_proj_hg, w_out, ffn_w_up, ffn_conv_w, ffn_conv_b, ffn_w_down, *, grid_w):
    nb, seq, d = x.shape
    ctx_len = ctx.shape[1]
    depth = w_in.shape[0]
    rows_g = seq // grid_w
    dff = ffn_w_down.shape[1]
    tile = grid_w * nb
    ngroups = 5
    heads = d // HG_EXPAND
    tf = 256 if dff % 256 == 0 else dff

    p = jax.nn.softmax(hg_lb_logits.astype(F32), axis=0)
    cum = jnp.cumsum(p, axis=0)
    lb_all = cum - cum[0:1]
    hg_par = jnp.stack([lb_all, 1.0 - lb_all, jnp.zeros_like(lb_all), hg_out_norm], axis=1)
    hg_par = jnp.pad(hg_par, ((0, 0), (0, 4), (0, 0)))

    w_ada_b, w_in_b = w_ada.astype(BF16), w_in.astype(BF16)
    wr_b, wh_b, wo_b = w_proj_rg.astype(BF16), w_proj_hg.astype(BF16), w_out.astype(BF16)
    wup_b, wdn_b = ffn_w_up.astype(BF16), ffn_w_down.astype(BF16)

    cc = jnp.concatenate([c, jnp.broadcast_to(c_ctx[None], (nb, d))], axis=0)
    mod_all = _ada_mod(cc, w_ada_b, b_ada).reshape(depth, 2, nb, N_MOD * d)

    xl = jnp.swapaxes(x, 0, 1).reshape(seq * nb, d)
    xc = jnp.swapaxes(ctx, 0, 1).reshape(ctx_len * nb, d)

    for l in range(depth):
        need_ctx = l < depth - 1
        mod_l, mod_c = mod_all[l, 0], mod_all[l, 1]
        gpm = g_pre_mix[l].reshape(1, d)
        wg = _gate_weights(rg_wa[l], rg_wx[l])
        ncb = d // RG_COLS
        bg = 0.5 * jnp.concatenate([rg_ba[l].reshape(2, ncb, 1, RG_COLS),
                                    rg_bx[l].reshape(2, ncb, 1, RG_COLS)], axis=-1)
        conv_b = rg_conv_b[l].reshape(1, d)

        rm_groups, hg_groups = (0, 1, 7, 8), (2, 3, 4, 5, 6)
        zc_rm = _inproj(xc, mod_c, gpm, w_in_b[l], rm_groups,
                        jax.ShapeDtypeStruct((ctx_len * nb, 4 * d), BF16),
                        pl.BlockSpec((tile, d), lambda i, j: (i, j)), tile, nb)
        zl_rm = _inproj(xl, mod_l, gpm, w_in_b[l], rm_groups,
                        jax.ShapeDtypeStruct((seq * nb, 4 * d), BF16),
                        pl.BlockSpec((tile, d), lambda i, j: (i, j)), tile, nb)
        zc_hg = _inproj(xc, mod_c, gpm, w_in_b[l], hg_groups,
                        jax.ShapeDtypeStruct((ctx_len, nb, ngroups * d), BF16),
                        pl.BlockSpec((grid_w, nb, d), lambda i, j: (i, 0, j)), tile, nb)
        zl_hg = _inproj(xl, mod_l, gpm, w_in_b[l], hg_groups,
                        jax.ShapeDtypeStruct((grid_w, rows_g, nb, ngroups * d), BF16),
                        pl.BlockSpec((grid_w, None, nb, d), lambda i, j: (0, i, 0, j)), tile, nb)
        zl_hg = zl_hg.reshape(seq, nb, ngroups * d)

        zero_h = jnp.zeros((nb, d), F32)
        rg = functools.partial(_rg_dir, conv_w=rg_conv_w[l], conv_b=conv_b, nb=nb, tt=RG_STEPS, d=d)
        hf_c, st = rg(zc_rm, None, wg=wg[0], bg=bg[0], lam=rg_lam[l, 0:1], h0=zero_h, reverse=False)
        hf_l, _ = rg(zl_rm, None, wg=wg[0], bg=bg[0], lam=rg_lam[l, 0:1], h0=st, reverse=False)
        yrg_c, st = rg(zc_rm, hf_c, wg=wg[1], bg=bg[1], lam=rg_lam[l, 1:2], h0=zero_h, reverse=True)
        yrg_l, _ = rg(zl_rm, hf_l, wg=wg[1], bg=bg[1], lam=rg_lam[l, 1:2], h0=st, reverse=True)

        zero_s = jnp.zeros((nb, heads, HG_EXPAND, HG_EXPAND), F32)
        hg = functools.partial(_hg_dir, par=hg_par[l], d=d)
        of_c, st = hg(zc_hg, None, s0=zero_s, reverse=False)
        of_l, _ = hg(zl_hg, None, s0=st, reverse=False)
        yhg_c, st = hg(zc_hg, of_c, s0=zero_s, reverse=True)
        yhg_l, _ = hg(zl_hg, of_l, s0=st, reverse=True)

        gpo = g_post_mix[l].reshape(1, d)
        yhg_l4 = yhg_l.reshape(grid_w, rows_g, nb, d)
        xl = _merge(yrg_l, yhg_l4, pl.BlockSpec((grid_w, None, nb, d), lambda i: (0, i, 0, 0)),
                    zl_rm, xl, mod_l, gpo, wr_b[l], wh_b[l], wo_b[l], tm=tile, nb=nb)
        ffn = functools.partial(_ffn, g1=g_pre_ffn[l].reshape(1, d), g2=g_post_ffn[l].reshape(1, d),
                                w_up=wup_b[l], cw=ffn_conv_w[l], cb=ffn_conv_b[l].reshape(1, dff),
                                w_down=wdn_b[l], tm=tile, tf=tf, nb=nb)
        xl = ffn(xl, mod_l, halo=False)
        if need_ctx:
            xc = _merge(yrg_c, yhg_c, pl.BlockSpec((grid_w, nb, d), lambda i: (i, 0, 0)),
                        zc_rm, xc, mod_c, gpo, wr_b[l], wh_b[l], wo_b[l], tm=tile, nb=nb)
            xc = ffn(xc, mod_c, halo=True)

    return jnp.swapaxes(xl.reshape(seq, nb, d), 0, 1)


def kernel(x, c, ctx, c_ctx, w_ada, b_ada, g_pre_mix, g_post_mix, g_pre_ffn, g_post_ffn, w_in, rg_conv_w, rg_conv_b, rg_wa, rg_ba, rg_wx, rg_bx, rg_lam, hg_lb_logits, hg_out_norm, w_proj_rg, w_proj_hg, w_out, ffn_w_up, ffn_conv_w, ffn_conv_b, ffn_w_down):
    return _forward(x, c, ctx, c_ctx, w_ada, b_ada, g_pre_mix, g_post_mix, g_pre_ffn, g_post_ffn,
                    w_in, rg_conv_w, rg_conv_b, rg_wa, rg_ba, rg_wx, rg_bx, rg_lam, hg_lb_logits,
                    hg_out_norm, w_proj_rg, w_proj_hg, w_out, ffn_w_up, ffn_conv_w, ffn_conv_b,
                    ffn_w_down, grid_w=GRID_W)
```

```python
import functools

import jax
import jax.numpy as jnp
from jax import lax
from jax.experimental import pallas as pl
from jax.experimental.pallas import tpu as pltpu

GRID_W = 64
RG_HEADS = 16
RG_C = 8.0
HG_EXPAND = 128
N_MOD = 6
EPS = 1e-6
RG_CONV_PAD_L = 1
RG_CONV_PAD_R = 2

HG_CHUNK = 128
HG_DIAG = 8
HG_DIAG_CLAMP = 20.0
HG_F_FLOOR = 1e-37
HG_PITCH = 24
RG_TINY = 1e-30
MM_COLS = 256
RG_COLS = 256
VMEM_LIMIT = 56 * 1024 * 1024

LOG2_E = 1.4426950408889634

F32 = jnp.float32
BF16 = jnp.bfloat16


def _cparams(sem):
    return pltpu.CompilerParams(dimension_semantics=sem, vmem_limit_bytes=VMEM_LIMIT)


def _sigmoid(x):
    return 0.5 + 0.5 * jnp.tanh(0.5 * x)


def _silu(x):
    return x * _sigmoid(x)


def _gelu_tanh(x):
    c = 0.7978845608028654
    return 0.5 * x * (1.0 + jnp.tanh(c * (x + 0.044715 * (x * x * x))))


def _rms(x, g):
    ms = jnp.mean(x * x, axis=-1, keepdims=True)
    return x * lax.rsqrt(ms + EPS) * g


def _per_batch(x, vec, nb):
    r, d = x.shape
    return x.reshape(r // nb, nb, d), vec[None]


def _norm_mod(x, g, shift, scale, nb):
    y = _rms(x, g)
    y3, sc = _per_batch(y, scale, nb)
    h = y3 * (1.0 + sc) + shift[None]
    return h.reshape(x.shape)


def _ada_kernel(c_ref, w_ref, b_ref, o_ref):
    s = _silu(c_ref[...]).astype(BF16)
    o_ref[0] = jnp.dot(s, w_ref[0], preferred_element_type=F32) + b_ref[0]


def _ada_mod(cc, w_ada, b_ada):
    depth, d, n = w_ada.shape
    tn = 1536 if n % 1536 == 0 else n
    return pl.pallas_call(
        _ada_kernel,
        out_shape=jax.ShapeDtypeStruct((depth, cc.shape[0], n), F32),
        grid=(depth, n // tn),
        in_specs=[pl.BlockSpec(cc.shape, lambda l, j: (0, 0)),
                  pl.BlockSpec((1, d, tn), lambda l, j: (l, 0, j)),
                  pl.BlockSpec((1, 1, tn), lambda l, j: (l, 0, j))],
        out_specs=pl.BlockSpec((1, cc.shape[0], tn), lambda l, j: (l, 0, j)),
        compiler_params=_cparams(("parallel", "parallel")),
        name="ada_mod",
    )(cc, w_ada, b_ada.reshape(depth, 1, n))


def _zero_from(v):
    u = lax.bitcast_convert_type(v, jnp.uint32)
    return lax.bitcast_convert_type((u >> 16) >> 16, F32)


def _rg_chunk(xe, bias, cw_ref, wg_ref, bg_ref, lam_ref, h, *, reverse, nb, steps, cols):
    rows = steps * nb
    xl = bias + xe[0:rows] * cw_ref[0:1, :]
    for j in range(1, 4):
        xl = xl + xe[j * nb:j * nb + rows] * cw_ref[j:j + 1, :]
    th = jnp.tanh(jnp.dot(xl.astype(BF16), wg_ref[0], preferred_element_type=F32) + bg_ref[0])
    half_rate = (-0.5 * RG_C) * jax.nn.softplus(-lam_ref[...])
    log_a = half_rate + half_rate * th[:, :cols]
    ig = 0.5 + 0.5 * th[:, cols:]
    a = jnp.exp(log_a)
    om = (1.0 + a * a) * jnp.tanh(-log_a)
    b = (om * lax.rsqrt(jnp.maximum(om, RG_TINY))) * (ig * xl)
    hs = [None] * steps
    for t in (range(steps - 1, -1, -1) if reverse else range(steps)):
        h = a[t * nb:(t + 1) * nb] * h + b[t * nb:(t + 1) * nb]
        hs[t] = h
    return h, jnp.concatenate(hs, axis=0)


def _inproj_kernel(*refs, nb, d, rg, reverse, tt, nt):
    if not rg:
        x_ref, mod_ref, g_ref, w_ref, o_ref, h_scr = refs
    elif reverse:
        (x_ref, mod_ref, g_ref, w_ref, zx_ref, zxp_ref, zxn_ref, cw_ref, cb_ref, wg_ref, bg_ref,
         lam_ref, h0_ref, hf_ref, gg_ref, o_ref, y_ref, hfin_ref, h_scr, hst_scr) = refs
    else:
        (x_ref, mod_ref, g_ref, w_ref, zx_ref, zxp_ref, zxn_ref, cw_ref, cb_ref, wg_ref, bg_ref,
         lam_ref, h0_ref, o_ref, y_ref, hfin_ref, h_scr, hst_scr) = refs
    i = pl.program_id(0)
    j = pl.program_id(1)

    @pl.when(j == 0)
    def _():
        h = _norm_mod(x_ref[...], g_ref[...], mod_ref[:, 0:d], mod_ref[:, d:2 * d], nb)
        h_scr[...] = h.astype(BF16)

    if not rg:
        res = jnp.dot(h_scr[...], w_ref[...], preferred_element_type=F32)
        o_ref[...] = res.reshape(o_ref.shape).astype(o_ref.dtype)
        return

    @pl.when(i == 0)
    def _():
        hst_scr[j] = h0_ref[j]

    tile = (nt - 1 - i) if reverse else i
    cols = y_ref.shape[1]
    npiece = d // MM_COLS
    steps = tt // npiece
    sub = steps * nb
    oshape = o_ref.shape[:-1] + (MM_COLS,)
    prev = jnp.where(tile > 0, zxp_ref[...].astype(F32), 0.0)
    nxt = jnp.where(tile < nt - 1, zxn_ref[...].astype(F32), 0.0)
    h = hst_scr[j]
    bias = cb_ref[...]
    for n in range(npiece):
        res = jnp.dot(h_scr[...], w_ref[:, n * MM_COLS:(n + 1) * MM_COLS],
                      preferred_element_type=F32)
        o_ref[..., n * MM_COLS:(n + 1) * MM_COLS] = res.reshape(oshape).astype(o_ref.dtype)
        c = (npiece - 1 - n) if reverse else n
        r0 = c * sub
        lo = prev if c == 0 else zx_ref[r0 - nb:r0, :].astype(F32)
        hi = nxt if c == npiece - 1 else zx_ref[r0 + sub:r0 + sub + 2 * nb, :].astype(F32)
        xe = jnp.concatenate([lo, zx_ref[r0:r0 + sub, :].astype(F32), hi], axis=0)
        h, hs = _rg_chunk(xe, bias, cw_ref, wg_ref, bg_ref, lam_ref, h, reverse=reverse, nb=nb,
                          steps=steps, cols=cols)
        if reverse:
            gate = _gelu_tanh(gg_ref[...].reshape(tt * nb, cols)[r0:r0 + sub, :].astype(F32))
            y_ref[r0:r0 + sub, :] = (gate * (hf_ref[r0:r0 + sub, :] + hs)).astype(y_ref.dtype)
        else:
            y_ref[r0:r0 + sub, :] = hs
        bias = cb_ref[...] + _zero_from(res[sub - 8:sub, MM_COLS - 128:MM_COLS])[0:1, 0:1]
    hst_scr[j] = h
    hfin_ref[j] = h


def _group_map(col_groups):
    def wmap(i, j):
        idx = jnp.int32(col_groups[0])
        for k in range(1, len(col_groups)):
            idx = jnp.where(j == k, jnp.int32(col_groups[k]), idx)
        return (0, idx)
    return wmap


def _inproj(x, mod, g, w, col_groups, out_shape, out_spec, tm, nb):
    n, d = x.shape
    return pl.pallas_call(
        functools.partial(_inproj_kernel, nb=nb, d=d, rg=False, reverse=False, tt=0, nt=0),
        out_shape=out_shape,
        grid=(n // tm, len(col_groups)),
        in_specs=[pl.BlockSpec((tm, d), lambda i, j: (i, 0)),
                  pl.BlockSpec(mod.shape, lambda i, j: (0, 0)),
                  pl.BlockSpec((1, d), lambda i, j: (0, 0)),
                  pl.BlockSpec((d, d), _group_map(col_groups))],
        out_specs=out_spec,
        scratch_shapes=[pltpu.VMEM((tm, d), BF16)],
        compiler_params=_cparams(("parallel", "arbitrary")),
        name="inproj",
    )(x, mod, g, w)


def _inproj_rg(x, mod, g, w, col_groups, out_shape, out_spec_of, zx, hf, zgg, gg_spec_of, conv_w,
               conv_b, wg, bg, lam, h0, *, reverse, nb, tt):
    n, d = x.shape
    rows = tt * nb
    nt = n // rows
    ncb = d // RG_COLS
    assert len(col_groups) == ncb and tt % (d // MM_COLS) == 0
    ppb = rows // nb
    npb = rows // (2 * nb)
    n_next_blocks = n // (2 * nb)

    def tile_of(i):
        return (nt - 1 - i) if reverse else i

    in_specs = [
        pl.BlockSpec((rows, d), lambda i, j: (tile_of(i), 0)),
        pl.BlockSpec(mod.shape, lambda i, j: (0, 0)),
        pl.BlockSpec((1, d), lambda i, j: (0, 0)),
        pl.BlockSpec((d, d), _group_map(col_groups)),
        pl.BlockSpec((rows, RG_COLS), lambda i, j: (tile_of(i), j)),
        pl.BlockSpec((nb, RG_COLS), lambda i, j: (jnp.maximum(tile_of(i) * ppb - 1, 0), j)),
        pl.BlockSpec((2 * nb, RG_COLS),
                     lambda i, j: (jnp.minimum((tile_of(i) + 1) * npb, n_next_blocks - 1), j)),
        pl.BlockSpec((4, RG_COLS), lambda i, j: (0, j)),
        pl.BlockSpec((1, RG_COLS), lambda i, j: (0, j)),
        pl.BlockSpec((1, RG_COLS, 2 * RG_COLS), lambda i, j: (j, 0, 0)),
        pl.BlockSpec((1, 1, 2 * RG_COLS), lambda i, j: (j, 0, 0)),
        pl.BlockSpec((1, RG_COLS), lambda i, j: (0, j)),
        pl.BlockSpec((ncb, nb, RG_COLS), lambda i, j: (0, 0, 0)),
    ]
    args = [x, mod, g, w, zx, zx, zx, conv_w, conv_b, wg, bg, lam, h0]
    if reverse:
        in_specs += [pl.BlockSpec((rows, RG_COLS), lambda i, j: (tile_of(i), j)), gg_spec_of(tile_of)]
        args += [hf, zgg]
    return pl.pallas_call(
        functools.partial(_inproj_kernel, nb=nb, d=d, rg=True, reverse=reverse, tt=tt, nt=nt),
        out_shape=(out_shape, jax.ShapeDtypeStruct((n, d), BF16 if reverse else F32),
                   jax.ShapeDtypeStruct((ncb, nb, RG_COLS), F32)),
        grid=(nt, ncb),
        in_specs=in_specs,
        out_specs=(out_spec_of(tile_of),
                   pl.BlockSpec((rows, RG_COLS), lambda i, j: (tile_of(i), j)),
                   pl.BlockSpec((ncb, nb, RG_COLS), lambda i, j: (0, 0, 0))),
        scratch_shapes=[pltpu.VMEM((rows, d), BF16), pltpu.VMEM((ncb, nb, RG_COLS), F32)],
        compiler_params=_cparams(("arbitrary", "arbitrary")),
        name="inproj_rg_bwd" if reverse else "inproj_rg_fwd",
    )(*args)


def _cumsum_time(x, reverse, block):
    c = x.shape[0]
    x4 = x.reshape((c // block, block) + x.shape[1:])
    cols = [None] * block
    order = range(block - 1, -1, -1) if reverse else range(block)
    run = None
    for i in order:
        run = x4[:, i] if run is None else run + x4[:, i]
        cols[i] = run
    return jnp.stack(cols, axis=1).reshape(x.shape)


def _cumsum_chunk(x, reverse):
    c = x.shape[0]
    blk = 8
    x4 = _cumsum_time(x, reverse, blk).reshape((c // blk, blk) + x.shape[1:])
    nblk = c // blk
    tot = x4[:, 0] if reverse else x4[:, blk - 1]
    offs = [None] * nblk
    order = range(nblk - 1, -1, -1) if reverse else range(nblk)
    run = None
    for i in order:
        offs[i] = run
        run = tot[i] if run is None else run + tot[i]
    first = nblk - 1 if reverse else 0
    parts = [x4[i] if i == first else x4[i] + offs[i][None] for i in range(nblk)]
    return jnp.stack(parts, axis=0).reshape(x.shape)


def _pivot_time(c, block, p):
    c4 = c.reshape((c.shape[0] // block, block) + c.shape[1:])
    return jnp.broadcast_to(c4[:, p:p + 1], c4.shape).reshape(c.shape)


def _hg_levels(c):
    out, m = [], c // 2
    while m >= HG_DIAG:
        out.append(m)
        m //= 2
    return out


def _hg_kernel(*refs, reverse, final):
    if final:
        (zq_ref, zf_ref, v_ref, of_ref, og_ref, par_ref, s0_ref, y_ref, sout_ref,
         st_scr, tr_scr, o_scr, mm_scr) = refs
    else:
        (zq_ref, zf_ref, v_ref, par_ref, s0_ref, y_ref, sout_ref, st_scr, tr_scr, o_scr,
         mm_scr) = refs
    j = pl.program_id(1)
    nj = pl.num_programs(1)
    c, nb, kd = zq_ref.shape

    @pl.when(j == 0)
    def _():
        st_scr[...] = s0_ref[...]

    zq = zq_ref[...].astype(F32)
    zf = zf_ref[...].astype(F32)
    lb = par_ref[0:1, :][None]
    oml = par_ref[1:2, :][None]

    q = _silu(zq)
    e = jnp.exp(-jnp.abs(zf))
    r = 1.0 / (1.0 + e)
    pos = zf >= 0.0
    f = lb + oml * (jnp.where(pos, 1.0, e) * r)
    k = oml * (jnp.where(pos, e, 1.0) * r)
    g = jnp.log2(jnp.maximum(f, HG_F_FLOOR))
    cs = _cumsum_chunk(g, reverse)
    cd = _cumsum_time(jnp.maximum(g, -HG_DIAG_CLAMP * LOG2_E), reverse, HG_DIAG)
    ctot = cs[0] if reverse else cs[c - 1]
    decay = jnp.exp2(ctot)

    pad = jnp.zeros((c, HG_PITCH - nb, kd), F32)
    for n, arr in enumerate((q, k, cs, cd, v_ref[...].astype(F32))):
        tr_scr[n] = jnp.concatenate([arr, pad], axis=1).reshape(c * HG_PITCH, kd)

    ti = lax.broadcasted_iota(jnp.int32, (c, c), 0)
    si = lax.broadcasted_iota(jnp.int32, (c, c), 1)
    lev = ti ^ si
    valid = (ti <= si) if reverse else (ti >= si)
    nt_dims = (((1,), (1,)), ((), ()))
    tn_dims = (((0,), (0,)), ((), ()))
    levels = _hg_levels(c)

    for b in range(nb):
        rows = pl.ds(b, c, stride=HG_PITCH)
        q2, k2, cs2, cd2 = (tr_scr[n, rows, :] for n in range(4))
        qb = q2.astype(BF16)
        kb = k2.astype(BF16)
        mm_scr[0, b] = qb * jnp.exp2(cs2).astype(BF16)
        mm_scr[1, b] = kb * jnp.exp2(ctot[b:b + 1, :] - cs2).astype(BF16)
        mm_scr[2, b] = tr_scr[4, rows, :].astype(BF16)
        cdd = cd2 - _pivot_time(cd2, HG_DIAG, HG_DIAG // 2 if reverse else HG_DIAG // 2 - 1)
        a = lax.dot_general(qb * jnp.exp2(cdd).astype(BF16), kb * jnp.exp2(-cdd).astype(BF16),
                            nt_dims, preferred_element_type=F32)
        for m in levels[::-1]:
            piv = m if reverse else m - 1
            el = jnp.exp2(-jnp.abs((cs2 - _pivot_time(cs2, 2 * m, piv)).astype(BF16)))
            p = lax.dot_general(qb * el, kb * el, nt_dims, preferred_element_type=F32)
            a = jnp.where(lev < m, a, p)
        mm_scr[3, b] = jnp.where(valid, a, 0.0).astype(BF16)

    for b in range(nb):
        st = st_scr[b]
        vb = mm_scr[2, b]
        o = lax.dot_general(mm_scr[0, b], st.astype(BF16), nt_dims, preferred_element_type=F32)
        o = o + jnp.dot(mm_scr[3, b], vb, preferred_element_type=F32)
        st_scr[b] = st * decay[b:b + 1, :] + lax.dot_general(vb, mm_scr[1, b], tn_dims,
                                                             preferred_element_type=F32)
        o_scr[pl.ds(b, c, stride=HG_PITCH), :] = o

    o = o_scr[...].reshape(c, HG_PITCH, kd)[:, 0:nb, :]
    if final:
        gain = par_ref[3:4, :][None]
        y = _rms(o + of_ref[...], gain) * _silu(og_ref[...].astype(F32))
        y_ref[...] = y.astype(y_ref.dtype)
    else:
        y_ref[...] = o

    @pl.when(j == nj - 1)
    def _():
        sout_ref[...] = st_scr[...]


def _hg_dir(za, zb, of, par, s0, *, reverse, d):
    t, nb, _ = za.shape
    kd = HG_EXPAND
    heads = d // kd
    nj = t // HG_CHUNK
    final = reverse
    assert nb <= HG_PITCH

    def blk(j):
        return (nj - 1 - j) if reverse else j

    def zspec(gidx):
        return pl.BlockSpec((HG_CHUNK, nb, kd), lambda h, j: (blk(j), 0, gidx * heads + h))

    hspec = pl.BlockSpec((HG_CHUNK, nb, kd), lambda h, j: (blk(j), 0, h))
    sspec = pl.BlockSpec((nb, None, kd, kd), lambda h, j: (0, h, 0, 0))
    in_specs = [zspec(1), zspec(3 if reverse else 2), zspec(0)]
    args = [za, za, zb]
    scratch = [pltpu.VMEM((nb, kd, kd), F32), pltpu.VMEM((5, HG_CHUNK * HG_PITCH, kd), F32),
               pltpu.VMEM((HG_CHUNK * HG_PITCH, kd), F32),
               pltpu.VMEM((4, nb, HG_CHUNK, kd), BF16)]
    if final:
        in_specs += [hspec, zspec(1)]
        args += [of, zb]
    in_specs += [pl.BlockSpec((par.shape[0], kd), lambda h, j: (0, h)), sspec]
    args += [par, s0]
    return pl.pallas_call(
        functools.partial(_hg_kernel, reverse=reverse, final=final),
        out_shape=(jax.ShapeDtypeStruct((t, nb, d), BF16 if final else F32),
                   jax.ShapeDtypeStruct(s0.shape, F32)),
        grid=(heads, nj),
        in_specs=in_specs,
        out_specs=(hspec, sspec),
        scratch_shapes=scratch,
        compiler_params=_cparams(("parallel", "arbitrary")),
        name="hg_bwd" if reverse else "hg_fwd",
    )(*args)


def _merge_kernel(yrg_ref, yhg_ref, ga_ref, gb_ref, x_ref, mod_ref, g_ref, wr_ref, wh_ref, wo_ref,
                  o_ref, *, nb, d):
    tm = x_ref.shape[0]
    p_rg = jnp.dot(yrg_ref[...], wr_ref[...], preferred_element_type=F32)
    p_hg = jnp.dot(yhg_ref[...].reshape(tm, d), wh_ref[...], preferred_element_type=F32)
    ga = ga_ref[...].reshape(tm, d).astype(F32)
    gb = gb_ref[...].reshape(tm, d).astype(F32)
    m = _sigmoid(ga) * p_rg + _sigmoid(gb) * p_hg
    y = jnp.dot(m.astype(BF16), wo_ref[...], preferred_element_type=F32)
    n3, gate = _per_batch(_rms(y, g_ref[...]), mod_ref[:, 2 * d:3 * d], nb)
    o_ref[...] = x_ref[...] + (gate * n3).reshape(tm, d)


def _merge(yrg, yhg, zb, tile_spec, x, mod, g, wr, wh, wo, *, tm, nb):
    n, d = x.shape
    row = lambda i: (i, 0)
    const = lambda i: (0, 0)
    return pl.pallas_call(
        functools.partial(_merge_kernel, nb=nb, d=d),
        out_shape=jax.ShapeDtypeStruct((n, d), F32),
        grid=(n // tm,),
        in_specs=[pl.BlockSpec((tm, d), row), tile_spec(0), tile_spec(2), tile_spec(3),
                  pl.BlockSpec((tm, d), row), pl.BlockSpec(mod.shape, const),
                  pl.BlockSpec((1, d), const), pl.BlockSpec((d, d), const),
                  pl.BlockSpec((d, d), const), pl.BlockSpec((d, d), const)],
        out_specs=pl.BlockSpec((tm, d), row),
        compiler_params=_cparams(("parallel",)),
        name="merge",
    )(yrg, yhg, zb, zb, x, mod, g, wr, wh, wo)


def _ffn_kernel(*refs, nb, d, dff, tf, halo, nt):
    if halo:
        (x_ref, xp_ref, xn_ref, mod_ref, g1_ref, g2_ref, wu_ref, cw_ref, cb_ref, wd_ref,
         o_ref, act_scr) = refs
    else:
        (x_ref, mod_ref, g1_ref, g2_ref, wu_ref, cw_ref, cb_ref, wd_ref, o_ref, act_scr) = refs
    i = pl.program_id(0)
    tm = x_ref.shape[0]
    shift = mod_ref[:, 3 * d:4 * d]
    scale = mod_ref[:, 4 * d:5 * d]

    hm = _norm_mod(x_ref[...], g1_ref[...], shift, scale, nb).astype(BF16)
    if halo:
        hp = _norm_mod(xp_ref[...], g1_ref[...], shift, scale, nb)
        hn = _norm_mod(xn_ref[...], g1_ref[...], shift, scale, nb)
        hp = jnp.where(i > 0, hp, 0.0).astype(BF16)
        hn = jnp.where(i < nt - 1, hn, 0.0).astype(BF16)
        he = jnp.concatenate([hp, hm, hn], axis=0)

    for kb in range(dff // tf):
        gs = slice(kb * tf, (kb + 1) * tf)
        vs = slice(dff + kb * tf, dff + (kb + 1) * tf)
        if halo:
            ue = jnp.dot(he, wu_ref[:, gs], preferred_element_type=F32)
        else:
            u = jnp.dot(hm, wu_ref[:, gs], preferred_element_type=F32)
            z = jnp.zeros((nb, tf), F32)
            ue = jnp.concatenate([z, u, z], axis=0)
        gc = cb_ref[:, gs] + ue[0:tm] * cw_ref[0:1, gs] + ue[nb:nb + tm] * cw_ref[1:2, gs] \
            + ue[2 * nb:2 * nb + tm] * cw_ref[2:3, gs]
        uv = jnp.dot(hm, wu_ref[:, vs], preferred_element_type=F32)
        act_scr[:, gs] = (_silu(gc) * uv).astype(BF16)

    y = jnp.dot(act_scr[...], wd_ref[...], preferred_element_type=F32)
    n3, gate = _per_batch(_rms(y, g2_ref[...]), mod_ref[:, 5 * d:6 * d], nb)
    o_ref[...] = x_ref[...] + (gate * n3).reshape(tm, d)


def _ffn(x, mod, g1, g2, w_up, cw, cb, w_down, *, tm, tf, nb, halo):
    n, d = x.shape
    dff = w_down.shape[0]
    nt = n // tm
    hb = tm // nb
    row = lambda i: (i, 0)
    const = lambda i: (0, 0)

    def resident(shape):
        return pl.BlockSpec(shape, const, pipeline_mode=pl.Buffered(1))

    in_specs = [pl.BlockSpec((tm, d), row)]
    args = [x]
    if halo:
        in_specs += [pl.BlockSpec((nb, d), lambda i: (jnp.maximum(i * hb - 1, 0), 0)),
                     pl.BlockSpec((nb, d), lambda i: (jnp.minimum((i + 1) * hb, n // nb - 1), 0))]
        args += [x, x]
    in_specs += [pl.BlockSpec(mod.shape, const), pl.BlockSpec((1, d), const),
                 pl.BlockSpec((1, d), const), resident(w_up.shape),
                 pl.BlockSpec(cw.shape, const), pl.BlockSpec(cb.shape, const),
                 resident(w_down.shape)]
    args += [mod, g1, g2, w_up, cw, cb, w_down]
    return pl.pallas_call(
        functools.partial(_ffn_kernel, nb=nb, d=d, dff=dff, tf=tf, halo=halo, nt=nt),
        out_shape=jax.ShapeDtypeStruct((n, d), F32),
        grid=(nt,),
        in_specs=in_specs,
        out_specs=pl.BlockSpec((tm, d), row),
        scratch_shapes=[pltpu.VMEM((tm, dff), BF16)],
        compiler_params=_cparams(("parallel",)),
        name="ffn_ctx" if halo else "ffn",
    )(*args)


def _gate_weights(wa, wx):
    nd, heads, hd, _ = wa.shape
    per = RG_COLS // hd
    ncb = heads // per

    def bd(w):
        w = w.reshape(nd, ncb, per, hd, hd)
        eye = jnp.eye(per, dtype=w.dtype)
        return jnp.einsum('dcpij,pq->dcpiqj', w, eye).reshape(nd, ncb, RG_COLS, RG_COLS)

    return (0.5 * jnp.concatenate([bd(wa), bd(wx)], axis=-1)).astype(BF16)


def _forward(x, c, ctx, c_ctx, w_ada, b_ada, g_pre_mix, g_post_mix, g_pre_ffn, g_post_ffn, w_in,
             rg_conv_w, rg_conv_b, rg_wa, rg_ba, rg_wx, rg_bx, rg_lam, hg_lb_logits, hg_out_norm,
             w_proj_rg, w_proj_hg, w_out, ffn_w_up, ffn_conv_w, ffn_conv_b, ffn_w_down, *, grid_w):
    nb, seq, d = x.shape
    ctx_len = ctx.shape[1]
    depth = w_in.shape[0]
    rows_g = seq // grid_w
    dff = ffn_w_down.shape[1]
    tile = grid_w * nb
    heads = d // HG_EXPAND
    ncb = d // RG_COLS
    tf = 256 if dff % 256 == 0 else dff

    p = jax.nn.softmax(hg_lb_logits.astype(F32), axis=0)
    cum = jnp.cumsum(p, axis=0)
    lb_all = cum - cum[0:1]
    hg_par = jnp.stack([lb_all, 1.0 - lb_all, jnp.zeros_like(lb_all), hg_out_norm], axis=1)
    hg_par = jnp.pad(hg_par, ((0, 0), (0, 4), (0, 0)))

    w_ada_b, w_in_b = w_ada.astype(BF16), w_in.astype(BF16)
    wr_b, wh_b, wo_b = w_proj_rg.astype(BF16), w_proj_hg.astype(BF16), w_out.astype(BF16)
    wup_b, wdn_b = ffn_w_up.astype(BF16), ffn_w_down.astype(BF16)

    cc = jnp.concatenate([c, jnp.broadcast_to(c_ctx[None], (nb, d))], axis=0)
    mod_all = _ada_mod(cc, w_ada_b, b_ada).reshape(depth, 2, nb, N_MOD * d)

    xl = jnp.swapaxes(x, 0, 1).reshape(seq * nb, d)
    xc = jnp.swapaxes(ctx, 0, 1).reshape(ctx_len * nb, d)

    for l in range(depth):
        need_ctx = l < depth - 1
        mod_l, mod_c = mod_all[l, 0], mod_all[l, 1]
        gpm = g_pre_mix[l].reshape(1, d)
        gpo = g_post_mix[l].reshape(1, d)
        wg = _gate_weights(rg_wa[l], rg_wx[l])
        bg = 0.5 * jnp.concatenate([rg_ba[l].reshape(2, ncb, 1, RG_COLS),
                                    rg_bx[l].reshape(2, ncb, 1, RG_COLS)], axis=-1)
        rg_kw = dict(conv_w=rg_conv_w[l], conv_b=rg_conv_b[l].reshape(1, d), nb=nb, tt=grid_w)

        st_rg = [jnp.zeros((ncb, nb, RG_COLS), F32)] * 2
        st_hg = [jnp.zeros((nb, heads, HG_EXPAND, HG_EXPAND), F32)] * 2
        ctx_out = None
        for xs, mod_s, steps_s, latent in ((xc, mod_c, ctx_len, False), (xl, mod_l, seq, True)):
            if latent:
                shape4 = jax.ShapeDtypeStruct((grid_w, rows_g, nb, 4 * d), BF16)
                blk = lambda w, col: pl.BlockSpec((grid_w, None, nb, w), col)
                spec_of = lambda tile_of: blk(d, lambda i, j: (0, tile_of(i), 0, j))
                gg_of = lambda tile_of: blk(RG_COLS, lambda i, j: (0, tile_of(i), 0, j))
                tile_spec = lambda gidx: blk(d, lambda i: (0, i, 0, gidx))
            else:
                shape4 = jax.ShapeDtypeStruct((steps_s, nb, 4 * d), BF16)
                blk = lambda w, col: pl.BlockSpec((grid_w, nb, w), col)
                spec_of = lambda tile_of: blk(d, lambda i, j: (tile_of(i), 0, j))
                gg_of = lambda tile_of: blk(RG_COLS, lambda i, j: (tile_of(i), 0, j))
                tile_spec = lambda gidx: blk(d, lambda i: (i, 0, gidx))
            zx = _inproj(xs, mod_s, gpm, w_in_b[l], (0,), jax.ShapeDtypeStruct((steps_s * nb, d), BF16),
                         pl.BlockSpec((tile, d), lambda i, j: (i, 0)), tile, nb)
            za, hf, st_rg[0] = _inproj_rg(
                xs, mod_s, gpm, w_in_b[l], (1, 2, 3, 4), shape4, spec_of, zx, None, None, None,
                wg=wg[0], bg=bg[0], lam=rg_lam[l, 0:1], h0=st_rg[0], reverse=False, **rg_kw)
            zb, yrg, st_rg[1] = _inproj_rg(
                xs, mod_s, gpm, w_in_b[l], (5, 6, 7, 8), shape4, spec_of, zx, hf, za, gg_of,
                wg=wg[1], bg=bg[1], lam=rg_lam[l, 1:2], h0=st_rg[1], reverse=True, **rg_kw)
            za3 = za.reshape(steps_s, nb, 4 * d)
            zb3 = zb.reshape(steps_s, nb, 4 * d)
            of, st_hg[0] = _hg_dir(za3, zb3, None, hg_par[l], st_hg[0], reverse=False, d=d)
            yhg, st_hg[1] = _hg_dir(za3, zb3, of, hg_par[l], st_hg[1], reverse=True, d=d)
            if latent or need_ctx:
                yhg = yhg.reshape(shape4.shape[:-1] + (d,))
                out = _merge(yrg, yhg, zb, tile_spec, xs, mod_s, gpo, wr_b[l], wh_b[l], wo_b[l],
                             tm=tile, nb=nb)
                if latent:
                    xl = out
                else:
                    ctx_out = out
        ffn = functools.partial(_ffn, g1=g_pre_ffn[l].reshape(1, d), g2=g_post_ffn[l].reshape(1, d),
                                w_up=wup_b[l], cw=ffn_conv_w[l], cb=ffn_conv_b[l].reshape(1, dff),
                                w_down=wdn_b[l], tm=tile, tf=tf, nb=nb)
        xl = ffn(xl, mod_l, halo=False)
        if need_ctx:
            xc = ffn(ctx_out, mod_c, halo=True)

    return jnp.swapaxes(xl.reshape(seq, nb, d), 0, 1)


def kernel(x, c, ctx, c_ctx, w_ada, b_ada, g_pre_mix, g_post_mix, g_pre_ffn, g_post_ffn, w_in, rg_conv_w, rg_conv_b, rg_wa, rg_ba, rg_wx, rg_bx, rg_lam, hg_lb_logits, hg_out_norm, w_proj_rg, w_proj_hg, w_out, ffn_w_up, ffn_conv_w, ffn_conv_b, ffn_w_down):
    return _forward(x, c, ctx, c_ctx, w_ada, b_ada, g_pre_mix, g_post_mix, g_pre_ffn, g_post_ffn,
                    w_in, rg_conv_w, rg_conv_b, rg_wa, rg_ba, rg_wx, rg_bx, rg_lam, hg_lb_logits,
                    hg_out_norm, w_proj_rg, w_proj_hg, w_out, ffn_w_up, ffn_conv_w, ffn_conv_b,
                    ffn_w_down, grid_w=GRID_W)
```

```python
import functools

import jax
import jax.numpy as jnp
from jax import lax
from jax.experimental import pallas as pl
from jax.experimental.pallas import tpu as pltpu

GRID_W = 64
RG_HEADS = 16
RG_C = 8.0
HG_EXPAND = 128
N_MOD = 6
EPS = 1e-6
RG_CONV_PAD_L = 1
RG_CONV_PAD_R = 2

HG_CHUNK = 128
HG_STEP_CHUNKS = 2
HG_DIAG = 8
HG_DIAG_CLAMP = 20.0
HG_F_FLOOR = 1e-37
HG_PITCH = 24
RG_TINY = 1e-30
MM_COLS = 256
RG_COLS = 256
VMEM_LIMIT = 56 * 1024 * 1024

LOG2_E = 1.4426950408889634

F32 = jnp.float32
BF16 = jnp.bfloat16


def _cparams(sem):
    return pltpu.CompilerParams(dimension_semantics=sem, vmem_limit_bytes=VMEM_LIMIT)


def _sigmoid(x):
    return 0.5 + 0.5 * jnp.tanh(0.5 * x)


def _silu(x):
    return x * _sigmoid(x)


def _gelu_tanh(x):
    c = 0.7978845608028654
    return 0.5 * x * (1.0 + jnp.tanh(c * (x + 0.044715 * (x * x * x))))


def _rms(x, g):
    ms = jnp.mean(x * x, axis=-1, keepdims=True)
    return x * lax.rsqrt(ms + EPS) * g


def _per_batch(x, vec, nb):
    r, d = x.shape
    return x.reshape(r // nb, nb, d), vec[None]


def _norm_mod(x, g, shift, scale, nb):
    y = _rms(x, g)
    y3, sc = _per_batch(y, scale, nb)
    h = y3 * (1.0 + sc) + shift[None]
    return h.reshape(x.shape)


def _ada_kernel(c_ref, w_ref, b_ref, o_ref):
    s = _silu(c_ref[...]).astype(BF16)
    o_ref[0] = jnp.dot(s, w_ref[0], preferred_element_type=F32) + b_ref[0]


def _ada_mod(cc, w_ada, b_ada):
    depth, d, n = w_ada.shape
    tn = 1536 if n % 1536 == 0 else n
    return pl.pallas_call(
        _ada_kernel,
        out_shape=jax.ShapeDtypeStruct((depth, cc.shape[0], n), F32),
        grid=(depth, n // tn),
        in_specs=[pl.BlockSpec(cc.shape, lambda l, j: (0, 0)),
                  pl.BlockSpec((1, d, tn), lambda l, j: (l, 0, j)),
                  pl.BlockSpec((1, 1, tn), lambda l, j: (l, 0, j))],
        out_specs=pl.BlockSpec((1, cc.shape[0], tn), lambda l, j: (l, 0, j)),
        compiler_params=_cparams(("parallel", "parallel")),
        name="ada_mod",
    )(cc, w_ada, b_ada.reshape(depth, 1, n))


def _zero_from(v):
    u = lax.bitcast_convert_type(v, jnp.uint32)
    return lax.bitcast_convert_type((u >> 16) >> 16, F32)


def _rg_chunk(xe, bias, cw_ref, wg_ref, bg_ref, lam_ref, h, *, reverse, nb, steps, cols):
    rows = steps * nb
    xl = bias + xe[0:rows] * cw_ref[0:1, :]
    for j in range(1, 4):
        xl = xl + xe[j * nb:j * nb + rows] * cw_ref[j:j + 1, :]
    th = jnp.tanh(jnp.dot(xl.astype(BF16), wg_ref[0], preferred_element_type=F32) + bg_ref[0])
    half_rate = (-0.5 * RG_C) * jax.nn.softplus(-lam_ref[...])
    log_a = half_rate + half_rate * th[:, :cols]
    ig = 0.5 + 0.5 * th[:, cols:]
    a = jnp.exp(log_a)
    om = (1.0 + a * a) * jnp.tanh(-log_a)
    b = (om * lax.rsqrt(jnp.maximum(om, RG_TINY))) * (ig * xl)
    hs = [None] * steps
    for t in (range(steps - 1, -1, -1) if reverse else range(steps)):
        h = a[t * nb:(t + 1) * nb] * h + b[t * nb:(t + 1) * nb]
        hs[t] = h
    return h, jnp.concatenate(hs, axis=0)


def _inproj_kernel(*refs, nb, d, rg, reverse, tt, nt):
    if not rg:
        x_ref, mod_ref, g_ref, w_ref, o_ref, h_scr = refs
    elif reverse:
        (x_ref, mod_ref, g_ref, w_ref, zx_ref, zxp_ref, zxn_ref, cw_ref, cb_ref, wg_ref, bg_ref,
         lam_ref, h0_ref, hf_ref, gg_ref, o_ref, y_ref, hfin_ref, h_scr, hst_scr) = refs
    else:
        (x_ref, mod_ref, g_ref, w_ref, zx_ref, zxp_ref, zxn_ref, cw_ref, cb_ref, wg_ref, bg_ref,
         lam_ref, h0_ref, o_ref, y_ref, hfin_ref, h_scr, hst_scr) = refs
    i = pl.program_id(0)
    j = pl.program_id(1)

    @pl.when(j == 0)
    def _():
        h = _norm_mod(x_ref[...], g_ref[...], mod_ref[:, 0:d], mod_ref[:, d:2 * d], nb)
        h_scr[...] = h.astype(BF16)

    if not rg:
        res = jnp.dot(h_scr[...], w_ref[...], preferred_element_type=F32)
        o_ref[...] = res.reshape(o_ref.shape).astype(o_ref.dtype)
        return

    @pl.when(i == 0)
    def _():
        hst_scr[j] = h0_ref[j]

    tile = (nt - 1 - i) if reverse else i
    cols = y_ref.shape[1]
    npiece = d // MM_COLS
    steps = tt // npiece
    sub = steps * nb
    oshape = o_ref.shape[:-1] + (MM_COLS,)
    prev = jnp.where(tile > 0, zxp_ref[...].astype(F32), 0.0)
    nxt = jnp.where(tile < nt - 1, zxn_ref[...].astype(F32), 0.0)
    h = hst_scr[j]
    bias = cb_ref[...]
    for n in range(npiece):
        res = jnp.dot(h_scr[...], w_ref[:, n * MM_COLS:(n + 1) * MM_COLS],
                      preferred_element_type=F32)
        o_ref[..., n * MM_COLS:(n + 1) * MM_COLS] = res.reshape(oshape).astype(o_ref.dtype)
        c = (npiece - 1 - n) if reverse else n
        r0 = c * sub
        lo = prev if c == 0 else zx_ref[r0 - nb:r0, :].astype(F32)
        hi = nxt if c == npiece - 1 else zx_ref[r0 + sub:r0 + sub + 2 * nb, :].astype(F32)
        xe = jnp.concatenate([lo, zx_ref[r0:r0 + sub, :].astype(F32), hi], axis=0)
        h, hs = _rg_chunk(xe, bias, cw_ref, wg_ref, bg_ref, lam_ref, h, reverse=reverse, nb=nb,
                          steps=steps, cols=cols)
        if reverse:
            gate = _gelu_tanh(gg_ref[...].reshape(tt * nb, cols)[r0:r0 + sub, :].astype(F32))
            y_ref[r0:r0 + sub, :] = (gate * (hf_ref[r0:r0 + sub, :] + hs)).astype(y_ref.dtype)
        else:
            y_ref[r0:r0 + sub, :] = hs
        bias = cb_ref[...] + _zero_from(res[sub - 8:sub, MM_COLS - 128:MM_COLS])[0:1, 0:1]
    hst_scr[j] = h
    hfin_ref[j] = h


def _group_map(col_groups):
    def wmap(i, j):
        idx = jnp.int32(col_groups[0])
        for k in range(1, len(col_groups)):
            idx = jnp.where(j == k, jnp.int32(col_groups[k]), idx)
        return (0, idx)
    return wmap


def _inproj(x, mod, g, w, col_groups, out_shape, out_spec, tm, nb):
    n, d = x.shape
    return pl.pallas_call(
        functools.partial(_inproj_kernel, nb=nb, d=d, rg=False, reverse=False, tt=0, nt=0),
        out_shape=out_shape,
        grid=(n // tm, len(col_groups)),
        in_specs=[pl.BlockSpec((tm, d), lambda i, j: (i, 0)),
                  pl.BlockSpec(mod.shape, lambda i, j: (0, 0)),
                  pl.BlockSpec((1, d), lambda i, j: (0, 0)),
                  pl.BlockSpec((d, d), _group_map(col_groups))],
        out_specs=out_spec,
        scratch_shapes=[pltpu.VMEM((tm, d), BF16)],
        compiler_params=_cparams(("parallel", "arbitrary")),
        name="inproj",
    )(x, mod, g, w)


def _inproj_rg(x, mod, g, w, col_groups, out_shape, out_spec_of, zx, hf, zgg, gg_spec_of, conv_w,
               conv_b, wg, bg, lam, h0, *, reverse, nb, tt):
    n, d = x.shape
    rows = tt * nb
    nt = n // rows
    ncb = d // RG_COLS
    assert len(col_groups) == ncb and tt % (d // MM_COLS) == 0
    ppb = rows // nb
    npb = rows // (2 * nb)
    n_next_blocks = n // (2 * nb)

    def tile_of(i):
        return (nt - 1 - i) if reverse else i

    in_specs = [
        pl.BlockSpec((rows, d), lambda i, j: (tile_of(i), 0)),
        pl.BlockSpec(mod.shape, lambda i, j: (0, 0)),
        pl.BlockSpec((1, d), lambda i, j: (0, 0)),
        pl.BlockSpec((d, d), _group_map(col_groups)),
        pl.BlockSpec((rows, RG_COLS), lambda i, j: (tile_of(i), j)),
        pl.BlockSpec((nb, RG_COLS), lambda i, j: (jnp.maximum(tile_of(i) * ppb - 1, 0), j)),
        pl.BlockSpec((2 * nb, RG_COLS),
                     lambda i, j: (jnp.minimum((tile_of(i) + 1) * npb, n_next_blocks - 1), j)),
        pl.BlockSpec((4, RG_COLS), lambda i, j: (0, j)),
        pl.BlockSpec((1, RG_COLS), lambda i, j: (0, j)),
        pl.BlockSpec((1, RG_COLS, 2 * RG_COLS), lambda i, j: (j, 0, 0)),
        pl.BlockSpec((1, 1, 2 * RG_COLS), lambda i, j: (j, 0, 0)),
        pl.BlockSpec((1, RG_COLS), lambda i, j: (0, j)),
        pl.BlockSpec((ncb, nb, RG_COLS), lambda i, j: (0, 0, 0)),
    ]
    args = [x, mod, g, w, zx, zx, zx, conv_w, conv_b, wg, bg, lam, h0]
    if reverse:
        in_specs += [pl.BlockSpec((rows, RG_COLS), lambda i, j: (tile_of(i), j)), gg_spec_of(tile_of)]
        args += [hf, zgg]
    return pl.pallas_call(
        functools.partial(_inproj_kernel, nb=nb, d=d, rg=True, reverse=reverse, tt=tt, nt=nt),
        out_shape=(out_shape, jax.ShapeDtypeStruct((n, d), BF16 if reverse else F32),
                   jax.ShapeDtypeStruct((ncb, nb, RG_COLS), F32)),
        grid=(nt, ncb),
        in_specs=in_specs,
        out_specs=(out_spec_of(tile_of),
                   pl.BlockSpec((rows, RG_COLS), lambda i, j: (tile_of(i), j)),
                   pl.BlockSpec((ncb, nb, RG_COLS), lambda i, j: (0, 0, 0))),
        scratch_shapes=[pltpu.VMEM((rows, d), BF16), pltpu.VMEM((ncb, nb, RG_COLS), F32)],
        compiler_params=_cparams(("arbitrary", "arbitrary")),
        name="inproj_rg_bwd" if reverse else "inproj_rg_fwd",
    )(*args)


def _cumsum_time(x, reverse, block):
    c = x.shape[0]
    x4 = x.reshape((c // block, block) + x.shape[1:])
    cols = [None] * block
    order = range(block - 1, -1, -1) if reverse else range(block)
    run = None
    for i in order:
        run = x4[:, i] if run is None else run + x4[:, i]
        cols[i] = run
    return jnp.stack(cols, axis=1).reshape(x.shape)


def _cumsum_chunk(x, reverse):
    c = x.shape[0]
    blk = 8
    x4 = _cumsum_time(x, reverse, blk).reshape((c // blk, blk) + x.shape[1:])
    nblk = c // blk
    tot = x4[:, 0] if reverse else x4[:, blk - 1]
    offs = [None] * nblk
    order = range(nblk - 1, -1, -1) if reverse else range(nblk)
    run = None
    for i in order:
        offs[i] = run
        run = tot[i] if run is None else run + tot[i]
    first = nblk - 1 if reverse else 0
    parts = [x4[i] if i == first else x4[i] + offs[i][None] for i in range(nblk)]
    return jnp.stack(parts, axis=0).reshape(x.shape)


def _pivot_time(c, block, p):
    c4 = c.reshape((c.shape[0] // block, block) + c.shape[1:])
    return jnp.broadcast_to(c4[:, p:p + 1], c4.shape).reshape(c.shape)


def _hg_levels(c):
    out, m = [], c // 2
    while m >= HG_DIAG:
        out.append(m)
        m //= 2
    return out


def _hg_chunk(zq, zf, v, of, og, par_ref, st_scr, tr_scr, o_scr, mm_scr, *, reverse, final):
    c, nb, kd = zq.shape
    zq = zq.astype(F32)
    zf = zf.astype(F32)
    lb = par_ref[0:1, :][None]
    oml = par_ref[1:2, :][None]

    q = _silu(zq)
    e = jnp.exp(-jnp.abs(zf))
    r = 1.0 / (1.0 + e)
    pos = zf >= 0.0
    f = lb + oml * (jnp.where(pos, 1.0, e) * r)
    k = oml * (jnp.where(pos, e, 1.0) * r)
    g = jnp.log2(jnp.maximum(f, HG_F_FLOOR))
    cs = _cumsum_chunk(g, reverse)
    cd = _cumsum_time(jnp.maximum(g, -HG_DIAG_CLAMP * LOG2_E), reverse, HG_DIAG)
    ctot = cs[0] if reverse else cs[c - 1]
    decay = jnp.exp2(ctot)

    pad = jnp.zeros((c, HG_PITCH - nb, kd), F32)
    for n, arr in enumerate((q, k, cs, cd, v.astype(F32))):
        tr_scr[n] = jnp.concatenate([arr, pad], axis=1).reshape(c * HG_PITCH, kd)

    ti = lax.broadcasted_iota(jnp.int32, (c, c), 0)
    si = lax.broadcasted_iota(jnp.int32, (c, c), 1)
    lev = ti ^ si
    valid = (ti <= si) if reverse else (ti >= si)
    nt_dims = (((1,), (1,)), ((), ()))
    tn_dims = (((0,), (0,)), ((), ()))
    levels = _hg_levels(c)
    sign = jnp.uint16(0x8000)

    for b in range(nb):
        rows = pl.ds(b, c, stride=HG_PITCH)
        q2, k2, cs2, cd2 = (tr_scr[n, rows, :] for n in range(4))
        qb = q2.astype(BF16)
        kb = k2.astype(BF16)
        mm_scr[0, b] = qb * jnp.exp2(cs2).astype(BF16)
        mm_scr[1, b] = kb * jnp.exp2(ctot[b:b + 1, :] - cs2).astype(BF16)
        mm_scr[2, b] = tr_scr[4, rows, :].astype(BF16)
        cdd = cd2 - _pivot_time(cd2, HG_DIAG, HG_DIAG // 2 if reverse else HG_DIAG // 2 - 1)
        a = lax.dot_general(qb * jnp.exp2(cdd).astype(BF16), kb * jnp.exp2(-cdd).astype(BF16),
                            nt_dims, preferred_element_type=F32)
        for m in levels[::-1]:
            piv = m if reverse else m - 1
            dist = (cs2 - _pivot_time(cs2, 2 * m, piv)).astype(BF16)
            neg = lax.bitcast_convert_type(lax.bitcast_convert_type(dist, jnp.uint16) | sign, BF16)
            el = jnp.exp2(neg)
            p = lax.dot_general(qb * el, kb * el, nt_dims, preferred_element_type=F32)
            a = jnp.where(lev < m, a, p)
        mm_scr[3, b] = jnp.where(valid, a, 0.0).astype(BF16)

    for b in range(nb):
        st = st_scr[b]
        vb = mm_scr[2, b]
        o = lax.dot_general(mm_scr[0, b], st.astype(BF16), nt_dims, preferred_element_type=F32)
        o = o + jnp.dot(mm_scr[3, b], vb, preferred_element_type=F32)
        st_scr[b] = st * decay[b:b + 1, :] + lax.dot_general(vb, mm_scr[1, b], tn_dims,
                                                             preferred_element_type=F32)
        o_scr[pl.ds(b, c, stride=HG_PITCH), :] = o

    o = o_scr[...].reshape(c, HG_PITCH, kd)[:, 0:nb, :]
    if final:
        gain = par_ref[3:4, :][None]
        return (_rms(o + of, gain) * _silu(og.astype(F32))).astype(BF16)
    return o


def _hg_kernel(*refs, reverse, final):
    if final:
        (zq_ref, zf_ref, v_ref, of_ref, og_ref, par_ref, s0_ref, y_ref, sout_ref,
         st_scr, tr_scr, o_scr, mm_scr) = refs
    else:
        (zq_ref, zf_ref, v_ref, par_ref, s0_ref, y_ref, sout_ref, st_scr, tr_scr, o_scr,
         mm_scr) = refs
    j = pl.program_id(1)
    nj = pl.num_programs(1)

    @pl.when(j == 0)
    def _():
        st_scr[...] = s0_ref[...]

    nchunk = zq_ref.shape[0] // HG_CHUNK
    for cc in (range(nchunk - 1, -1, -1) if reverse else range(nchunk)):
        t0 = cc * HG_CHUNK
        sl = slice(t0, t0 + HG_CHUNK)
        y_ref[sl] = _hg_chunk(zq_ref[sl], zf_ref[sl], v_ref[sl],
                              of_ref[sl] if final else None, og_ref[sl] if final else None,
                              par_ref, st_scr, tr_scr, o_scr, mm_scr, reverse=reverse, final=final)

    @pl.when(j == nj - 1)
    def _():
        sout_ref[...] = st_scr[...]


def _hg_dir(za, zb, of, par, s0, *, reverse, d):
    t, nb, _ = za.shape
    kd = HG_EXPAND
    heads = d // kd
    tstep = HG_CHUNK * (HG_STEP_CHUNKS if t % (HG_CHUNK * HG_STEP_CHUNKS) == 0 else 1)
    nj = t // tstep
    final = reverse
    assert nb <= HG_PITCH

    def blk(j):
        return (nj - 1 - j) if reverse else j

    def zspec(gidx):
        return pl.BlockSpec((tstep, nb, kd), lambda h, j: (blk(j), 0, gidx * heads + h))

    hspec = pl.BlockSpec((tstep, nb, kd), lambda h, j: (blk(j), 0, h))
    sspec = pl.BlockSpec((nb, None, kd, kd), lambda h, j: (0, h, 0, 0))
    in_specs = [zspec(1), zspec(3 if reverse else 2), zspec(0)]
    args = [za, za, zb]
    scratch = [pltpu.VMEM((nb, kd, kd), F32), pltpu.VMEM((5, HG_CHUNK * HG_PITCH, kd), F32),
               pltpu.VMEM((HG_CHUNK * HG_PITCH, kd), F32),
               pltpu.VMEM((4, nb, HG_CHUNK, kd), BF16)]
    if final:
        in_specs += [hspec, zspec(1)]
        args += [of, zb]
    in_specs += [pl.BlockSpec((par.shape[0], kd), lambda h, j: (0, h)), sspec]
    args += [par, s0]
    return pl.pallas_call(
        functools.partial(_hg_kernel, reverse=reverse, final=final),
        out_shape=(jax.ShapeDtypeStruct((t, nb, d), BF16 if final else F32),
                   jax.ShapeDtypeStruct(s0.shape, F32)),
        grid=(heads, nj),
        in_specs=in_specs,
        out_specs=(hspec, sspec),
        scratch_shapes=scratch,
        compiler_params=_cparams(("parallel", "arbitrary")),
        name="hg_bwd" if reverse else "hg_fwd",
    )(*args)


def _merge_kernel(yrg_ref, yhg_ref, ga_ref, gb_ref, x_ref, mod_ref, g_ref, wr_ref, wh_ref, wo_ref,
                  o_ref, *, nb, d):
    tm = x_ref.shape[0]
    p_rg = jnp.dot(yrg_ref[...], wr_ref[...], preferred_element_type=F32)
    p_hg = jnp.dot(yhg_ref[...].reshape(tm, d), wh_ref[...], preferred_element_type=F32)
    ga = ga_ref[...].reshape(tm, d).astype(F32)
    gb = gb_ref[...].reshape(tm, d).astype(F32)
    m = _sigmoid(ga) * p_rg + _sigmoid(gb) * p_hg
    y = jnp.dot(m.astype(BF16), wo_ref[...], preferred_element_type=F32)
    n3, gate = _per_batch(_rms(y, g_ref[...]), mod_ref[:, 2 * d:3 * d], nb)
    o_ref[...] = x_ref[...] + (gate * n3).reshape(tm, d)


def _merge(yrg, yhg, zb, tile_spec, x, mod, g, wr, wh, wo, *, tm, nb):
    n, d = x.shape
    row = lambda i: (i, 0)
    const = lambda i: (0, 0)
    return pl.pallas_call(
        functools.partial(_merge_kernel, nb=nb, d=d),
        out_shape=jax.ShapeDtypeStruct((n, d), F32),
        grid=(n // tm,),
        in_specs=[pl.BlockSpec((tm, d), row), tile_spec(0), tile_spec(2), tile_spec(3),
                  pl.BlockSpec((tm, d), row), pl.BlockSpec(mod.shape, const),
                  pl.BlockSpec((1, d), const), pl.BlockSpec((d, d), const),
                  pl.BlockSpec((d, d), const), pl.BlockSpec((d, d), const)],
        out_specs=pl.BlockSpec((tm, d), row),
        compiler_params=_cparams(("parallel",)),
        name="merge",
    )(yrg, yhg, zb, zb, x, mod, g, wr, wh, wo)


def _ffn_kernel(*refs, nb, d, dff, tf, halo, nt):
    if halo:
        (x_ref, xp_ref, xn_ref, mod_ref, g1_ref, g2_ref, wu_ref, cw_ref, cb_ref, wd_ref,
         o_ref, act_scr) = refs
    else:
        (x_ref, mod_ref, g1_ref, g2_ref, wu_ref, cw_ref, cb_ref, wd_ref, o_ref, act_scr) = refs
    i = pl.program_id(0)
    tm = x_ref.shape[0]
    shift = mod_ref[:, 3 * d:4 * d]
    scale = mod_ref[:, 4 * d:5 * d]

    hm = _norm_mod(x_ref[...], g1_ref[...], shift, scale, nb).astype(BF16)
    if halo:
        hp = _norm_mod(xp_ref[...], g1_ref[...], shift, scale, nb)
        hn = _norm_mod(xn_ref[...], g1_ref[...], shift, scale, nb)
        hp = jnp.where(i > 0, hp, 0.0).astype(BF16)
        hn = jnp.where(i < nt - 1, hn, 0.0).astype(BF16)
        he = jnp.concatenate([hp, hm, hn], axis=0)

    for kb in range(dff // tf):
        gs = slice(kb * tf, (kb + 1) * tf)
        vs = slice(dff + kb * tf, dff + (kb + 1) * tf)
        if halo:
            ue = jnp.dot(he, wu_ref[:, gs], preferred_element_type=F32)
        else:
            u = jnp.dot(hm, wu_ref[:, gs], preferred_element_type=F32)
            z = jnp.zeros((nb, tf), F32)
            ue = jnp.concatenate([z, u, z], axis=0)
        gc = cb_ref[:, gs] + ue[0:tm] * cw_ref[0:1, gs] + ue[nb:nb + tm] * cw_ref[1:2, gs] \
            + ue[2 * nb:2 * nb + tm] * cw_ref[2:3, gs]
        uv = jnp.dot(hm, wu_ref[:, vs], preferred_element_type=F32)
        act_scr[:, gs] = (_silu(gc) * uv).astype(BF16)

    y = jnp.dot(act_scr[...], wd_ref[...], preferred_element_type=F32)
    n3, gate = _per_batch(_rms(y, g2_ref[...]), mod_ref[:, 5 * d:6 * d], nb)
    o_ref[...] = x_ref[...] + (gate * n3).reshape(tm, d)


def _ffn(x, mod, g1, g2, w_up, cw, cb, w_down, *, tm, tf, nb, halo):
    n, d = x.shape
    dff = w_down.shape[0]
    nt = n // tm
    hb = tm // nb
    row = lambda i: (i, 0)
    const = lambda i: (0, 0)

    def resident(shape):
        return pl.BlockSpec(shape, const, pipeline_mode=pl.Buffered(1))

    in_specs = [pl.BlockSpec((tm, d), row)]
    args = [x]
    if halo:
        in_specs += [pl.BlockSpec((nb, d), lambda i: (jnp.maximum(i * hb - 1, 0), 0)),
                     pl.BlockSpec((nb, d), lambda i: (jnp.minimum((i + 1) * hb, n // nb - 1), 0))]
        args += [x, x]
    in_specs += [pl.BlockSpec(mod.shape, const), pl.BlockSpec((1, d), const),
                 pl.BlockSpec((1, d), const), resident(w_up.shape),
                 pl.BlockSpec(cw.shape, const), pl.BlockSpec(cb.shape, const),
                 resident(w_down.shape)]
    args += [mod, g1, g2, w_up, cw, cb, w_down]
    return pl.pallas_call(
        functools.partial(_ffn_kernel, nb=nb, d=d, dff=dff, tf=tf, halo=halo, nt=nt),
        out_shape=jax.ShapeDtypeStruct((n, d), F32),
        grid=(nt,),
        in_specs=in_specs,
        out_specs=pl.BlockSpec((tm, d), row),
        scratch_shapes=[pltpu.VMEM((tm, dff), BF16)],
        compiler_params=_cparams(("parallel",)),
        name="ffn_ctx" if halo else "ffn",
    )(*args)


def _gate_weights(wa, wx):
    nd, heads, hd, _ = wa.shape
    per = RG_COLS // hd
    ncb = heads // per

    def bd(w):
        w = w.reshape(nd, ncb, per, hd, hd)
        eye = jnp.eye(per, dtype=w.dtype)
        return jnp.einsum('dcpij,pq->dcpiqj', w, eye).reshape(nd, ncb, RG_COLS, RG_COLS)

    return (0.5 * jnp.concatenate([bd(wa), bd(wx)], axis=-1)).astype(BF16)


def _forward(x, c, ctx, c_ctx, w_ada, b_ada, g_pre_mix, g_post_mix, g_pre_ffn, g_post_ffn, w_in,
             rg_conv_w, rg_conv_b, rg_wa, rg_ba, rg_wx, rg_bx, rg_lam, hg_lb_logits, hg_out_norm,
             w_proj_rg, w_proj_hg, w_out, ffn_w_up, ffn_conv_w, ffn_conv_b, ffn_w_down, *, grid_w):
    nb, seq, d = x.shape
    ctx_len = ctx.shape[1]
    depth = w_in.shape[0]
    rows_g = seq // grid_w
    dff = ffn_w_down.shape[1]
    tile = grid_w * nb
    heads = d // HG_EXPAND
    ncb = d // RG_COLS
    tf = 256 if dff % 256 == 0 else dff

    p = jax.nn.softmax(hg_lb_logits.astype(F32), axis=0)
    cum = jnp.cumsum(p, axis=0)
    lb_all = cum - cum[0:1]
    hg_par = jnp.stack([lb_all, 1.0 - lb_all, jnp.zeros_like(lb_all), hg_out_norm], axis=1)
    hg_par = jnp.pad(hg_par, ((0, 0), (0, 4), (0, 0)))

    w_ada_b, w_in_b = w_ada.astype(BF16), w_in.astype(BF16)
    wr_b, wh_b, wo_b = w_proj_rg.astype(BF16), w_proj_hg.astype(BF16), w_out.astype(BF16)
    wup_b, wdn_b = ffn_w_up.astype(BF16), ffn_w_down.astype(BF16)

    cc = jnp.concatenate([c, jnp.broadcast_to(c_ctx[None], (nb, d))], axis=0)
    mod_all = _ada_mod(cc, w_ada_b, b_ada).reshape(depth, 2, nb, N_MOD * d)

    xl = jnp.swapaxes(x, 0, 1).reshape(seq * nb, d)
    xc = jnp.swapaxes(ctx, 0, 1).reshape(ctx_len * nb, d)

    for l in range(depth):
        need_ctx = l < depth - 1
        mod_l, mod_c = mod_all[l, 0], mod_all[l, 1]
        gpm = g_pre_mix[l].reshape(1, d)
        gpo = g_post_mix[l].reshape(1, d)
        wg = _gate_weights(rg_wa[l], rg_wx[l])
        bg = 0.5 * jnp.concatenate([rg_ba[l].reshape(2, ncb, 1, RG_COLS),
                                    rg_bx[l].reshape(2, ncb, 1, RG_COLS)], axis=-1)
        rg_kw = dict(conv_w=rg_conv_w[l], conv_b=rg_conv_b[l].reshape(1, d), nb=nb, tt=grid_w)

        st_rg = [jnp.zeros((ncb, nb, RG_COLS), F32)] * 2
        st_hg = [jnp.zeros((nb, heads, HG_EXPAND, HG_EXPAND), F32)] * 2
        ctx_out = None
        for xs, mod_s, steps_s, latent in ((xc, mod_c, ctx_len, False), (xl, mod_l, seq, True)):
            if latent:
                shape4 = jax.ShapeDtypeStruct((grid_w, rows_g, nb, 4 * d), BF16)
                blk = lambda w, col: pl.BlockSpec((grid_w, None, nb, w), col)
                spec_of = lambda tile_of: blk(d, lambda i, j: (0, tile_of(i), 0, j))
                gg_of = lambda tile_of: blk(RG_COLS, lambda i, j: (0, tile_of(i), 0, j))
                tile_spec = lambda gidx: blk(d, lambda i: (0, i, 0, gidx))
            else:
                shape4 = jax.ShapeDtypeStruct((steps_s, nb, 4 * d), BF16)
                blk = lambda w, col: pl.BlockSpec((grid_w, nb, w), col)
                spec_of = lambda tile_of: blk(d, lambda i, j: (tile_of(i), 0, j))
                gg_of = lambda tile_of: blk(RG_COLS, lambda i, j: (tile_of(i), 0, j))
                tile_spec = lambda gidx: blk(d, lambda i: (i, 0, gidx))
            zx = _inproj(xs, mod_s, gpm, w_in_b[l], (0,), jax.ShapeDtypeStruct((steps_s * nb, d), BF16),
                         pl.BlockSpec((tile, d), lambda i, j: (i, 0)), tile, nb)
            za, hf, st_rg[0] = _inproj_rg(
                xs, mod_s, gpm, w_in_b[l], (1, 2, 3, 4), shape4, spec_of, zx, None, None, None,
                wg=wg[0], bg=bg[0], lam=rg_lam[l, 0:1], h0=st_rg[0], reverse=False, **rg_kw)
            zb, yrg, st_rg[1] = _inproj_rg(
                xs, mod_s, gpm, w_in_b[l], (5, 6, 7, 8), shape4, spec_of, zx, hf, za, gg_of,
                wg=wg[1], bg=bg[1], lam=rg_lam[l, 1:2], h0=st_rg[1], reverse=True, **rg_kw)
            za3 = za.reshape(steps_s, nb, 4 * d)
            zb3 = zb.reshape(steps_s, nb, 4 * d)
            of, st_hg[0] = _hg_dir(za3, zb3, None, hg_par[l], st_hg[0], reverse=False, d=d)
            yhg, st_hg[1] = _hg_dir(za3, zb3, of, hg_par[l], st_hg[1], reverse=True, d=d)
            if latent or need_ctx:
                yhg = yhg.reshape(shape4.shape[:-1] + (d,))
                out = _merge(yrg, yhg, zb, tile_spec, xs, mod_s, gpo, wr_b[l], wh_b[l], wo_b[l],
                             tm=tile, nb=nb)
                if latent:
                    xl = out
                else:
                    ctx_out = out
        ffn = functools.partial(_ffn, g1=g_pre_ffn[l].reshape(1, d), g2=g_post_ffn[l].reshape(1, d),
                                w_up=wup_b[l], cw=ffn_conv_w[l], cb=ffn_conv_b[l].reshape(1, dff),
                                w_down=wdn_b[l], tm=tile, tf=tf, nb=nb)
        xl = ffn(xl, mod_l, halo=False)
        if need_ctx:
            xc = ffn(ctx_out, mod_c, halo=True)

    return jnp.swapaxes(xl.reshape(seq, nb, d), 0, 1)


def kernel(x, c, ctx, c_ctx, w_ada, b_ada, g_pre_mix, g_post_mix, g_pre_ffn, g_post_ffn, w_in, rg_conv_w, rg_conv_b, rg_wa, rg_ba, rg_wx, rg_bx, rg_lam, hg_lb_logits, hg_out_norm, w_proj_rg, w_proj_hg, w_out, ffn_w_up, ffn_conv_w, ffn_conv_b, ffn_w_down):
    return _forward(x, c, ctx, c_ctx, w_ada, b_ada, g_pre_mix, g_post_mix, g_pre_ffn, g_post_ffn,
                    w_in, rg_conv_w, rg_conv_b, rg_wa, rg_ba, rg_wx, rg_bx, rg_lam, hg_lb_logits,
                    hg_out_norm, w_proj_rg, w_proj_hg, w_out, ffn_w_up, ffn_conv_w, ffn_conv_b,
                    ffn_w_down, grid_w=GRID_W)
```

```python
import functools

import jax
import jax.numpy as jnp
from jax import lax
from jax.experimental import pallas as pl
from jax.experimental.pallas import tpu as pltpu

GRID_W = 64
RG_HEADS = 16
RG_C = 8.0
HG_EXPAND = 128
N_MOD = 6
EPS = 1e-6
RG_CONV_PAD_L = 1
RG_CONV_PAD_R = 2

HG_CHUNK = 128
HG_STEP_CHUNKS = 1
HG_STEP_HEADS = 2
HG_DIAG = 8
HG_DIAG_CLAMP = 20.0
HG_F_FLOOR = 1e-37
HG_PITCH = 24
RG_TINY = 1e-30
MM_COLS = 256
RG_COLS = 256
VMEM_LIMIT = 56 * 1024 * 1024

LOG2_E = 1.4426950408889634

F32 = jnp.float32
BF16 = jnp.bfloat16


def _cparams(sem):
    return pltpu.CompilerParams(dimension_semantics=sem, vmem_limit_bytes=VMEM_LIMIT)


def _sigmoid(x):
    return 0.5 + 0.5 * jnp.tanh(0.5 * x)


def _silu(x):
    return x * _sigmoid(x)


def _gelu_tanh(x):
    c = 0.7978845608028654
    return 0.5 * x * (1.0 + jnp.tanh(c * (x + 0.044715 * (x * x * x))))


def _rms(x, g):
    ms = jnp.mean(x * x, axis=-1, keepdims=True)
    return x * lax.rsqrt(ms + EPS) * g


def _per_batch(x, vec, nb):
    r, d = x.shape
    return x.reshape(r // nb, nb, d), vec[None]


def _norm_mod(x, g, shift, scale, nb):
    y = _rms(x, g)
    y3, sc = _per_batch(y, scale, nb)
    h = y3 * (1.0 + sc) + shift[None]
    return h.reshape(x.shape)


def _ada_kernel(c_ref, w_ref, b_ref, o_ref):
    s = _silu(c_ref[...]).astype(BF16)
    o_ref[0] = jnp.dot(s, w_ref[0], preferred_element_type=F32) + b_ref[0]


def _ada_mod(cc, w_ada, b_ada):
    depth, d, n = w_ada.shape
    tn = 1536 if n % 1536 == 0 else n
    return pl.pallas_call(
        _ada_kernel,
        out_shape=jax.ShapeDtypeStruct((depth, cc.shape[0], n), F32),
        grid=(depth, n // tn),
        in_specs=[pl.BlockSpec(cc.shape, lambda l, j: (0, 0)),
                  pl.BlockSpec((1, d, tn), lambda l, j: (l, 0, j)),
                  pl.BlockSpec((1, 1, tn), lambda l, j: (l, 0, j))],
        out_specs=pl.BlockSpec((1, cc.shape[0], tn), lambda l, j: (l, 0, j)),
        compiler_params=_cparams(("parallel", "parallel")),
        name="ada_mod",
    )(cc, w_ada, b_ada.reshape(depth, 1, n))


def _zero_from(v):
    u = lax.bitcast_convert_type(v, jnp.uint32)
    return lax.bitcast_convert_type((u >> 16) >> 16, F32)


def _rg_chunk(xe, bias, cw_ref, wg_ref, bg_ref, lam_ref, h, *, reverse, nb, steps, cols):
    rows = steps * nb
    xl = bias + xe[0:rows] * cw_ref[0:1, :]
    for j in range(1, 4):
        xl = xl + xe[j * nb:j * nb + rows] * cw_ref[j:j + 1, :]
    th = jnp.tanh(jnp.dot(xl.astype(BF16), wg_ref[0], preferred_element_type=F32) + bg_ref[0])
    half_rate = (-0.5 * RG_C) * jax.nn.softplus(-lam_ref[...])
    log_a = half_rate + half_rate * th[:, :cols]
    ig = 0.5 + 0.5 * th[:, cols:]
    a = jnp.exp(log_a)
    om = (1.0 + a * a) * jnp.tanh(-log_a)
    b = (om * lax.rsqrt(jnp.maximum(om, RG_TINY))) * (ig * xl)
    hs = [None] * steps
    for t in (range(steps - 1, -1, -1) if reverse else range(steps)):
        h = a[t * nb:(t + 1) * nb] * h + b[t * nb:(t + 1) * nb]
        hs[t] = h
    return h, jnp.concatenate(hs, axis=0)


def _inproj_kernel(*refs, nb, d, rg, reverse, tt, nt):
    if not rg:
        x_ref, mod_ref, g_ref, w_ref, o_ref, h_scr = refs
    elif reverse:
        (x_ref, mod_ref, g_ref, w_ref, zx_ref, zxp_ref, zxn_ref, cw_ref, cb_ref, wg_ref, bg_ref,
         lam_ref, h0_ref, hf_ref, gg_ref, o_ref, y_ref, hfin_ref, h_scr, hst_scr) = refs
    else:
        (x_ref, mod_ref, g_ref, w_ref, zx_ref, zxp_ref, zxn_ref, cw_ref, cb_ref, wg_ref, bg_ref,
         lam_ref, h0_ref, o_ref, y_ref, hfin_ref, h_scr, hst_scr) = refs
    i = pl.program_id(0)
    j = pl.program_id(1)

    @pl.when(j == 0)
    def _():
        h = _norm_mod(x_ref[...], g_ref[...], mod_ref[:, 0:d], mod_ref[:, d:2 * d], nb)
        h_scr[...] = h.astype(BF16)

    if not rg:
        res = jnp.dot(h_scr[...], w_ref[...], preferred_element_type=F32)
        o_ref[...] = res.reshape(o_ref.shape).astype(o_ref.dtype)
        return

    @pl.when(i == 0)
    def _():
        hst_scr[j] = h0_ref[j]

    tile = (nt - 1 - i) if reverse else i
    cols = y_ref.shape[1]
    npiece = d // MM_COLS
    steps = tt // npiece
    sub = steps * nb
    oshape = o_ref.shape[:-1] + (MM_COLS,)
    prev = jnp.where(tile > 0, zxp_ref[...].astype(F32), 0.0)
    nxt = jnp.where(tile < nt - 1, zxn_ref[...].astype(F32), 0.0)
    h = hst_scr[j]
    bias = cb_ref[...]
    for n in range(npiece):
        res = jnp.dot(h_scr[...], w_ref[:, n * MM_COLS:(n + 1) * MM_COLS],
                      preferred_element_type=F32)
        o_ref[..., n * MM_COLS:(n + 1) * MM_COLS] = res.reshape(oshape).astype(o_ref.dtype)
        c = (npiece - 1 - n) if reverse else n
        r0 = c * sub
        lo = prev if c == 0 else zx_ref[r0 - nb:r0, :].astype(F32)
        hi = nxt if c == npiece - 1 else zx_ref[r0 + sub:r0 + sub + 2 * nb, :].astype(F32)
        xe = jnp.concatenate([lo, zx_ref[r0:r0 + sub, :].astype(F32), hi], axis=0)
        h, hs = _rg_chunk(xe, bias, cw_ref, wg_ref, bg_ref, lam_ref, h, reverse=reverse, nb=nb,
                          steps=steps, cols=cols)
        if reverse:
            gate = _gelu_tanh(gg_ref[...].reshape(tt * nb, cols)[r0:r0 + sub, :].astype(F32))
            y_ref[r0:r0 + sub, :] = (gate * (hf_ref[r0:r0 + sub, :] + hs)).astype(y_ref.dtype)
        else:
            y_ref[r0:r0 + sub, :] = hs
        bias = cb_ref[...] + _zero_from(res[sub - 8:sub, MM_COLS - 128:MM_COLS])[0:1, 0:1]
    hst_scr[j] = h
    hfin_ref[j] = h


def _group_map(col_groups):
    def wmap(i, j):
        idx = jnp.int32(col_groups[0])
        for k in range(1, len(col_groups)):
            idx = jnp.where(j == k, jnp.int32(col_groups[k]), idx)
        return (0, idx)
    return wmap


def _inproj(x, mod, g, w, col_groups, out_shape, out_spec, tm, nb):
    n, d = x.shape
    return pl.pallas_call(
        functools.partial(_inproj_kernel, nb=nb, d=d, rg=False, reverse=False, tt=0, nt=0),
        out_shape=out_shape,
        grid=(n // tm, len(col_groups)),
        in_specs=[pl.BlockSpec((tm, d), lambda i, j: (i, 0)),
                  pl.BlockSpec(mod.shape, lambda i, j: (0, 0)),
                  pl.BlockSpec((1, d), lambda i, j: (0, 0)),
                  pl.BlockSpec((d, d), _group_map(col_groups))],
        out_specs=out_spec,
        scratch_shapes=[pltpu.VMEM((tm, d), BF16)],
        compiler_params=_cparams(("parallel", "arbitrary")),
        name="inproj",
    )(x, mod, g, w)


def _inproj_rg(x, mod, g, w, col_groups, out_shape, out_spec_of, zx, hf, zgg, gg_spec_of, conv_w,
               conv_b, wg, bg, lam, h0, *, reverse, nb, tt):
    n, d = x.shape
    rows = tt * nb
    nt = n // rows
    ncb = d // RG_COLS
    assert len(col_groups) == ncb and tt % (d // MM_COLS) == 0
    ppb = rows // nb
    npb = rows // (2 * nb)
    n_next_blocks = n // (2 * nb)

    def tile_of(i):
        return (nt - 1 - i) if reverse else i

    in_specs = [
        pl.BlockSpec((rows, d), lambda i, j: (tile_of(i), 0)),
        pl.BlockSpec(mod.shape, lambda i, j: (0, 0)),
        pl.BlockSpec((1, d), lambda i, j: (0, 0)),
        pl.BlockSpec((d, d), _group_map(col_groups)),
        pl.BlockSpec((rows, RG_COLS), lambda i, j: (tile_of(i), j)),
        pl.BlockSpec((nb, RG_COLS), lambda i, j: (jnp.maximum(tile_of(i) * ppb - 1, 0), j)),
        pl.BlockSpec((2 * nb, RG_COLS),
                     lambda i, j: (jnp.minimum((tile_of(i) + 1) * npb, n_next_blocks - 1), j)),
        pl.BlockSpec((4, RG_COLS), lambda i, j: (0, j)),
        pl.BlockSpec((1, RG_COLS), lambda i, j: (0, j)),
        pl.BlockSpec((1, RG_COLS, 2 * RG_COLS), lambda i, j: (j, 0, 0)),
        pl.BlockSpec((1, 1, 2 * RG_COLS), lambda i, j: (j, 0, 0)),
        pl.BlockSpec((1, RG_COLS), lambda i, j: (0, j)),
        pl.BlockSpec((ncb, nb, RG_COLS), lambda i, j: (0, 0, 0)),
    ]
    args = [x, mod, g, w, zx, zx, zx, conv_w, conv_b, wg, bg, lam, h0]
    if reverse:
        in_specs += [pl.BlockSpec((rows, RG_COLS), lambda i, j: (tile_of(i), j)), gg_spec_of(tile_of)]
        args += [hf, zgg]
    return pl.pallas_call(
        functools.partial(_inproj_kernel, nb=nb, d=d, rg=True, reverse=reverse, tt=tt, nt=nt),
        out_shape=(out_shape, jax.ShapeDtypeStruct((n, d), BF16 if reverse else F32),
                   jax.ShapeDtypeStruct((ncb, nb, RG_COLS), F32)),
        grid=(nt, ncb),
        in_specs=in_specs,
        out_specs=(out_spec_of(tile_of),
                   pl.BlockSpec((rows, RG_COLS), lambda i, j: (tile_of(i), j)),
                   pl.BlockSpec((ncb, nb, RG_COLS), lambda i, j: (0, 0, 0))),
        scratch_shapes=[pltpu.VMEM((rows, d), BF16), pltpu.VMEM((ncb, nb, RG_COLS), F32)],
        compiler_params=_cparams(("arbitrary", "arbitrary")),
        name="inproj_rg_bwd" if reverse else "inproj_rg_fwd",
    )(*args)


def _cumsum_time(x, reverse, block):
    c = x.shape[0]
    x4 = x.reshape((c // block, block) + x.shape[1:])
    cols = [None] * block
    order = range(block - 1, -1, -1) if reverse else range(block)
    run = None
    for i in order:
        run = x4[:, i] if run is None else run + x4[:, i]
        cols[i] = run
    return jnp.stack(cols, axis=1).reshape(x.shape)


def _cumsum_chunk(x, reverse):
    c = x.shape[0]
    blk = 8
    x4 = _cumsum_time(x, reverse, blk).reshape((c // blk, blk) + x.shape[1:])
    nblk = c // blk
    tot = x4[:, 0] if reverse else x4[:, blk - 1]
    offs = [None] * nblk
    order = range(nblk - 1, -1, -1) if reverse else range(nblk)
    run = None
    for i in order:
        offs[i] = run
        run = tot[i] if run is None else run + tot[i]
    first = nblk - 1 if reverse else 0
    parts = [x4[i] if i == first else x4[i] + offs[i][None] for i in range(nblk)]
    return jnp.stack(parts, axis=0).reshape(x.shape)


def _pivot_time(c, block, p):
    c4 = c.reshape((c.shape[0] // block, block) + c.shape[1:])
    return jnp.broadcast_to(c4[:, p:p + 1], c4.shape).reshape(c.shape)


def _hg_levels(c):
    out, m = [], c // 2
    while m >= HG_DIAG:
        out.append(m)
        m //= 2
    return out


def _hg_chunk(zq, zf, v, of, og, par, st_scr, hh, tr_scr, o_scr, mm_scr, *, reverse, final):
    c, nb, kd = zq.shape
    zq = zq.astype(F32)
    zf = zf.astype(F32)
    lb = par[0:1, :][None]
    oml = par[1:2, :][None]

    q = _silu(zq)
    e = jnp.exp(-jnp.abs(zf))
    r = 1.0 / (1.0 + e)
    pos = zf >= 0.0
    f = lb + oml * (jnp.where(pos, 1.0, e) * r)
    k = oml * (jnp.where(pos, e, 1.0) * r)
    g = jnp.log2(jnp.maximum(f, HG_F_FLOOR))
    cs = _cumsum_chunk(g, reverse)
    cd = _cumsum_time(jnp.maximum(g, -HG_DIAG_CLAMP * LOG2_E), reverse, HG_DIAG)
    ctot = cs[0] if reverse else cs[c - 1]
    decay = jnp.exp2(ctot)

    pad = jnp.zeros((c, HG_PITCH - nb, kd), F32)
    for n, arr in enumerate((q, k, cs, cd, v.astype(F32))):
        tr_scr[n] = jnp.concatenate([arr, pad], axis=1).reshape(c * HG_PITCH, kd)

    ti = lax.broadcasted_iota(jnp.int32, (c, c), 0)
    si = lax.broadcasted_iota(jnp.int32, (c, c), 1)
    lev = ti ^ si
    valid = (ti <= si) if reverse else (ti >= si)
    nt_dims = (((1,), (1,)), ((), ()))
    tn_dims = (((0,), (0,)), ((), ()))
    levels = _hg_levels(c)
    sign = jnp.uint16(0x8000)

    for b in range(nb):
        rows = pl.ds(b, c, stride=HG_PITCH)
        q2, k2, cs2, cd2 = (tr_scr[n, rows, :] for n in range(4))
        qb = q2.astype(BF16)
        kb = k2.astype(BF16)
        mm_scr[0, b] = qb * jnp.exp2(cs2).astype(BF16)
        mm_scr[1, b] = kb * jnp.exp2(ctot[b:b + 1, :] - cs2).astype(BF16)
        mm_scr[2, b] = tr_scr[4, rows, :].astype(BF16)
        cdd = cd2 - _pivot_time(cd2, HG_DIAG, HG_DIAG // 2 if reverse else HG_DIAG // 2 - 1)
        a = lax.dot_general(qb * jnp.exp2(cdd).astype(BF16), kb * jnp.exp2(-cdd).astype(BF16),
                            nt_dims, preferred_element_type=F32)
        for m in levels[::-1]:
            piv = m if reverse else m - 1
            dist = (cs2 - _pivot_time(cs2, 2 * m, piv)).astype(BF16)
            neg = lax.bitcast_convert_type(lax.bitcast_convert_type(dist, jnp.uint16) | sign, BF16)
            el = jnp.exp2(neg)
            p = lax.dot_general(qb * el, kb * el, nt_dims, preferred_element_type=F32)
            a = jnp.where(lev < m, a, p)
        mm_scr[3, b] = jnp.where(valid, a, 0.0).astype(BF16)

    for b in range(nb):
        st = st_scr[hh, b]
        vb = mm_scr[2, b]
        o = lax.dot_general(mm_scr[0, b], st.astype(BF16), nt_dims, preferred_element_type=F32)
        o = o + jnp.dot(mm_scr[3, b], vb, preferred_element_type=F32)
        st_scr[hh, b] = st * decay[b:b + 1, :] + lax.dot_general(vb, mm_scr[1, b], tn_dims,
                                                                 preferred_element_type=F32)
        o_scr[pl.ds(b, c, stride=HG_PITCH), :] = o

    o = o_scr[...].reshape(c, HG_PITCH, kd)[:, 0:nb, :]
    if final:
        gain = par[3:4, :][None]
        return (_rms(o + of, gain) * _silu(og.astype(F32))).astype(BF16)
    return o


def _hg_kernel(*refs, reverse, final):
    if final:
        (zq_ref, zf_ref, v_ref, of_ref, og_ref, par_ref, s0_ref, y_ref, sout_ref,
         st_scr, tr_scr, o_scr, mm_scr) = refs
    else:
        (zq_ref, zf_ref, v_ref, par_ref, s0_ref, y_ref, sout_ref, st_scr, tr_scr, o_scr,
         mm_scr) = refs
    j = pl.program_id(1)
    nj = pl.num_programs(1)

    @pl.when(j == 0)
    def _():
        st_scr[...] = jnp.swapaxes(s0_ref[...], 0, 1)

    nchunk = zq_ref.shape[0] // HG_CHUNK
    kd = HG_EXPAND
    for hh in range(zq_ref.shape[2] // kd):
        ln = slice(hh * kd, (hh + 1) * kd)
        par = par_ref[:, ln]
        for cc in (range(nchunk - 1, -1, -1) if reverse else range(nchunk)):
            sl = slice(cc * HG_CHUNK, (cc + 1) * HG_CHUNK)
            y_ref[sl, :, ln] = _hg_chunk(
                zq_ref[sl, :, ln], zf_ref[sl, :, ln], v_ref[sl, :, ln],
                of_ref[sl, :, ln] if final else None, og_ref[sl, :, ln] if final else None,
                par, st_scr, hh, tr_scr, o_scr, mm_scr, reverse=reverse, final=final)

    @pl.when(j == nj - 1)
    def _():
        sout_ref[...] = jnp.swapaxes(st_scr[...], 0, 1)


def _hg_dir(za, zb, of, par, s0, *, reverse, d):
    t, nb, _ = za.shape
    kd = HG_EXPAND
    heads = d // kd
    tstep = HG_CHUNK * (HG_STEP_CHUNKS if t % (HG_CHUNK * HG_STEP_CHUNKS) == 0 else 1)
    nj = t // tstep
    final = reverse
    assert nb <= HG_PITCH

    def blk(j):
        return (nj - 1 - j) if reverse else j

    hs = HG_STEP_HEADS if heads % HG_STEP_HEADS == 0 else 1
    kw = hs * kd

    def zspec(gidx):
        return pl.BlockSpec((tstep, nb, kw), lambda h, j: (blk(j), 0, gidx * (heads // hs) + h))

    hspec = pl.BlockSpec((tstep, nb, kw), lambda h, j: (blk(j), 0, h))
    sspec = pl.BlockSpec((nb, hs, kd, kd), lambda h, j: (0, h, 0, 0))
    in_specs = [zspec(1), zspec(3 if reverse else 2), zspec(0)]
    args = [za, za, zb]
    scratch = [pltpu.VMEM((hs, nb, kd, kd), F32), pltpu.VMEM((5, HG_CHUNK * HG_PITCH, kd), F32),
               pltpu.VMEM((HG_CHUNK * HG_PITCH, kd), F32),
               pltpu.VMEM((4, nb, HG_CHUNK, kd), BF16)]
    if final:
        in_specs += [hspec, zspec(1)]
        args += [of, zb]
    in_specs += [pl.BlockSpec((par.shape[0], kw), lambda h, j: (0, h)), sspec]
    args += [par, s0]
    return pl.pallas_call(
        functools.partial(_hg_kernel, reverse=reverse, final=final),
        out_shape=(jax.ShapeDtypeStruct((t, nb, d), BF16 if final else F32),
                   jax.ShapeDtypeStruct(s0.shape, F32)),
        grid=(heads // hs, nj),
        in_specs=in_specs,
        out_specs=(hspec, sspec),
        scratch_shapes=scratch,
        compiler_params=_cparams(("parallel", "arbitrary")),
        name="hg_bwd" if reverse else "hg_fwd",
    )(*args)


def _merge_kernel(yrg_ref, yhg_ref, ga_ref, gb_ref, x_ref, mod_ref, g_ref, wr_ref, wh_ref, wo_ref,
                  o_ref, *, nb, d):
    tm = x_ref.shape[0]
    p_rg = jnp.dot(yrg_ref[...], wr_ref[...], preferred_element_type=F32)
    p_hg = jnp.dot(yhg_ref[...].reshape(tm, d), wh_ref[...], preferred_element_type=F32)
    ga = ga_ref[...].reshape(tm, d).astype(F32)
    gb = gb_ref[...].reshape(tm, d).astype(F32)
    m = _sigmoid(ga) * p_rg + _sigmoid(gb) * p_hg
    y = jnp.dot(m.astype(BF16), wo_ref[...], preferred_element_type=F32)
    n3, gate = _per_batch(_rms(y, g_ref[...]), mod_ref[:, 2 * d:3 * d], nb)
    o_ref[...] = x_ref[...] + (gate * n3).reshape(tm, d)


def _merge(yrg, yhg, zb, tile_spec, x, mod, g, wr, wh, wo, *, tm, nb):
    n, d = x.shape
    row = lambda i: (i, 0)
    const = lambda i: (0, 0)
    return pl.pallas_call(
        functools.partial(_merge_kernel, nb=nb, d=d),
        out_shape=jax.ShapeDtypeStruct((n, d), F32),
        grid=(n // tm,),
        in_specs=[pl.BlockSpec((tm, d), row), tile_spec(0), tile_spec(2), tile_spec(3),
                  pl.BlockSpec((tm, d), row), pl.BlockSpec(mod.shape, const),
                  pl.BlockSpec((1, d), const), pl.BlockSpec((d, d), const),
                  pl.BlockSpec((d, d), const), pl.BlockSpec((d, d), const)],
        out_specs=pl.BlockSpec((tm, d), row),
        compiler_params=_cparams(("parallel",)),
        name="merge",
    )(yrg, yhg, zb, zb, x, mod, g, wr, wh, wo)


def _ffn_kernel(*refs, nb, d, dff, tf, halo, nt):
    if halo:
        (x_ref, xp_ref, xn_ref, mod_ref, g1_ref, g2_ref, wu_ref, cw_ref, cb_ref, wd_ref,
         o_ref, act_scr) = refs
    else:
        (x_ref, mod_ref, g1_ref, g2_ref, wu_ref, cw_ref, cb_ref, wd_ref, o_ref, act_scr) = refs
    i = pl.program_id(0)
    tm = x_ref.shape[0]
    shift = mod_ref[:, 3 * d:4 * d]
    scale = mod_ref[:, 4 * d:5 * d]

    hm = _norm_mod(x_ref[...], g1_ref[...], shift, scale, nb).astype(BF16)
    if halo:
        hp = _norm_mod(xp_ref[...], g1_ref[...], shift, scale, nb)
        hn = _norm_mod(xn_ref[...], g1_ref[...], shift, scale, nb)
        hp = jnp.where(i > 0, hp, 0.0).astype(BF16)
        hn = jnp.where(i < nt - 1, hn, 0.0).astype(BF16)
        he = jnp.concatenate([hp, hm, hn], axis=0)

    for kb in range(dff // tf):
        gs = slice(kb * tf, (kb + 1) * tf)
        vs = slice(dff + kb * tf, dff + (kb + 1) * tf)
        if halo:
            ue = jnp.dot(he, wu_ref[:, gs], preferred_element_type=F32)
        else:
            u = jnp.dot(hm, wu_ref[:, gs], preferred_element_type=F32)
            z = jnp.zeros((nb, tf), F32)
            ue = jnp.concatenate([z, u, z], axis=0)
        gc = cb_ref[:, gs] + ue[0:tm] * cw_ref[0:1, gs] + ue[nb:nb + tm] * cw_ref[1:2, gs] \
            + ue[2 * nb:2 * nb + tm] * cw_ref[2:3, gs]
        uv = jnp.dot(hm, wu_ref[:, vs], preferred_element_type=F32)
        act_scr[:, gs] = (_silu(gc) * uv).astype(BF16)

    y = jnp.dot(act_scr[...], wd_ref[...], preferred_element_type=F32)
    n3, gate = _per_batch(_rms(y, g2_ref[...]), mod_ref[:, 5 * d:6 * d], nb)
    o_ref[...] = x_ref[...] + (gate * n3).reshape(tm, d)


def _ffn(x, mod, g1, g2, w_up, cw, cb, w_down, *, tm, tf, nb, halo):
    n, d = x.shape
    dff = w_down.shape[0]
    nt = n // tm
    hb = tm // nb
    row = lambda i: (i, 0)
    const = lambda i: (0, 0)

    def resident(shape):
        return pl.BlockSpec(shape, const, pipeline_mode=pl.Buffered(1))

    in_specs = [pl.BlockSpec((tm, d), row)]
    args = [x]
    if halo:
        in_specs += [pl.BlockSpec((nb, d), lambda i: (jnp.maximum(i * hb - 1, 0), 0)),
                     pl.BlockSpec((nb, d), lambda i: (jnp.minimum((i + 1) * hb, n // nb - 1), 0))]
        args += [x, x]
    in_specs += [pl.BlockSpec(mod.shape, const), pl.BlockSpec((1, d), const),
                 pl.BlockSpec((1, d), const), resident(w_up.shape),
                 pl.BlockSpec(cw.shape, const), pl.BlockSpec(cb.shape, const),
                 resident(w_down.shape)]
    args += [mod, g1, g2, w_up, cw, cb, w_down]
    return pl.pallas_call(
        functools.partial(_ffn_kernel, nb=nb, d=d, dff=dff, tf=tf, halo=halo, nt=nt),
        out_shape=jax.ShapeDtypeStruct((n, d), F32),
        grid=(nt,),
        in_specs=in_specs,
        out_specs=pl.BlockSpec((tm, d), row),
        scratch_shapes=[pltpu.VMEM((tm, dff), BF16)],
        compiler_params=_cparams(("parallel",)),
        name="ffn_ctx" if halo else "ffn",
    )(*args)


def _gate_weights(wa, wx):
    nd, heads, hd, _ = wa.shape
    per = RG_COLS // hd
    ncb = heads // per

    def bd(w):
        w = w.reshape(nd, ncb, per, hd, hd)
        eye = jnp.eye(per, dtype=w.dtype)
        return jnp.einsum('dcpij,pq->dcpiqj', w, eye).reshape(nd, ncb, RG_COLS, RG_COLS)

    return (0.5 * jnp.concatenate([bd(wa), bd(wx)], axis=-1)).astype(BF16)


def _forward(x, c, ctx, c_ctx, w_ada, b_ada, g_pre_mix, g_post_mix, g_pre_ffn, g_post_ffn, w_in,
             rg_conv_w, rg_conv_b, rg_wa, rg_ba, rg_wx, rg_bx, rg_lam, hg_lb_logits, hg_out_norm,
             w_proj_rg, w_proj_hg, w_out, ffn_w_up, ffn_conv_w, ffn_conv_b, ffn_w_down, *, grid_w):
    nb, seq, d = x.shape
    ctx_len = ctx.shape[1]
    depth = w_in.shape[0]
    rows_g = seq // grid_w
    dff = ffn_w_down.shape[1]
    tile = grid_w * nb
    heads = d // HG_EXPAND
    ncb = d // RG_COLS
    tf = 256 if dff % 256 == 0 else dff

    p = jax.nn.softmax(hg_lb_logits.astype(F32), axis=0)
    cum = jnp.cumsum(p, axis=0)
    lb_all = cum - cum[0:1]
    hg_par = jnp.stack([lb_all, 1.0 - lb_all, jnp.zeros_like(lb_all), hg_out_norm], axis=1)
    hg_par = jnp.pad(hg_par, ((0, 0), (0, 4), (0, 0)))

    w_ada_b, w_in_b = w_ada.astype(BF16), w_in.astype(BF16)
    wr_b, wh_b, wo_b = w_proj_rg.astype(BF16), w_proj_hg.astype(BF16), w_out.astype(BF16)
    wup_b, wdn_b = ffn_w_up.astype(BF16), ffn_w_down.astype(BF16)

    cc = jnp.concatenate([c, jnp.broadcast_to(c_ctx[None], (nb, d))], axis=0)
    mod_all = _ada_mod(cc, w_ada_b, b_ada).reshape(depth, 2, nb, N_MOD * d)

    xl = jnp.swapaxes(x, 0, 1).reshape(seq * nb, d)
    xc = jnp.swapaxes(ctx, 0, 1).reshape(ctx_len * nb, d)

    for l in range(depth):
        need_ctx = l < depth - 1
        mod_l, mod_c = mod_all[l, 0], mod_all[l, 1]
        gpm = g_pre_mix[l].reshape(1, d)
        gpo = g_post_mix[l].reshape(1, d)
        wg = _gate_weights(rg_wa[l], rg_wx[l])
        bg = 0.5 * jnp.concatenate([rg_ba[l].reshape(2, ncb, 1, RG_COLS),
                                    rg_bx[l].reshape(2, ncb, 1, RG_COLS)], axis=-1)
        rg_kw = dict(conv_w=rg_conv_w[l], conv_b=rg_conv_b[l].reshape(1, d), nb=nb, tt=grid_w)

        st_rg = [jnp.zeros((ncb, nb, RG_COLS), F32)] * 2
        st_hg = [jnp.zeros((nb, heads, HG_EXPAND, HG_EXPAND), F32)] * 2
        ctx_out = None
        for xs, mod_s, steps_s, latent in ((xc, mod_c, ctx_len, False), (xl, mod_l, seq, True)):
            if latent:
                shape4 = jax.ShapeDtypeStruct((grid_w, rows_g, nb, 4 * d), BF16)
                blk = lambda w, col: pl.BlockSpec((grid_w, None, nb, w), col)
                spec_of = lambda tile_of: blk(d, lambda i, j: (0, tile_of(i), 0, j))
                gg_of = lambda tile_of: blk(RG_COLS, lambda i, j: (0, tile_of(i), 0, j))
                tile_spec = lambda gidx: blk(d, lambda i: (0, i, 0, gidx))
            else:
                shape4 = jax.ShapeDtypeStruct((steps_s, nb, 4 * d), BF16)
                blk = lambda w, col: pl.BlockSpec((grid_w, nb, w), col)
                spec_of = lambda tile_of: blk(d, lambda i, j: (tile_of(i), 0, j))
                gg_of = lambda tile_of: blk(RG_COLS, lambda i, j: (tile_of(i), 0, j))
                tile_spec = lambda gidx: blk(d, lambda i: (i, 0, gidx))
            zx = _inproj(xs, mod_s, gpm, w_in_b[l], (0,), jax.ShapeDtypeStruct((steps_s * nb, d), BF16),
                         pl.BlockSpec((tile, d), lambda i, j: (i, 0)), tile, nb)
            za, hf, st_rg[0] = _inproj_rg(
                xs, mod_s, gpm, w_in_b[l], (1, 2, 3, 4), shape4, spec_of, zx, None, None, None,
                wg=wg[0], bg=bg[0], lam=rg_lam[l, 0:1], h0=st_rg[0], reverse=False, **rg_kw)
            zb, yrg, st_rg[1] = _inproj_rg(
                xs, mod_s, gpm, w_in_b[l], (5, 6, 7, 8), shape4, spec_of, zx, hf, za, gg_of,
                wg=wg[1], bg=bg[1], lam=rg_lam[l, 1:2], h0=st_rg[1], reverse=True, **rg_kw)
            za3 = za.reshape(steps_s, nb, 4 * d)
            zb3 = zb.reshape(steps_s, nb, 4 * d)
            of, st_hg[0] = _hg_dir(za3, zb3, None, hg_par[l], st_hg[0], reverse=False, d=d)
            yhg, st_hg[1] = _hg_dir(za3, zb3, of, hg_par[l], st_hg[1], reverse=True, d=d)
            if latent or need_ctx:
                yhg = yhg.reshape(shape4.shape[:-1] + (d,))
                out = _merge(yrg, yhg, zb, tile_spec, xs, mod_s, gpo, wr_b[l], wh_b[l], wo_b[l],
                             tm=tile, nb=nb)
                if latent:
                    xl = out
                else:
                    ctx_out = out
        ffn = functools.partial(_ffn, g1=g_pre_ffn[l].reshape(1, d), g2=g_post_ffn[l].reshape(1, d),
                                w_up=wup_b[l], cw=ffn_conv_w[l], cb=ffn_conv_b[l].reshape(1, dff),
                                w_down=wdn_b[l], tm=tile, tf=tf, nb=nb)
        xl = ffn(xl, mod_l, halo=False)
        if need_ctx:
            xc = ffn(ctx_out, mod_c, halo=True)

    return jnp.swapaxes(xl.reshape(seq, nb, d), 0, 1)


def kernel(x, c, ctx, c_ctx, w_ada, b_ada, g_pre_mix, g_post_mix, g_pre_ffn, g_post_ffn, w_in, rg_conv_w, rg_conv_b, rg_wa, rg_ba, rg_wx, rg_bx, rg_lam, hg_lb_logits, hg_out_norm, w_proj_rg, w_proj_hg, w_out, ffn_w_up, ffn_conv_w, ffn_conv_b, ffn_w_down):
    return _forward(x, c, ctx, c_ctx, w_ada, b_ada, g_pre_mix, g_post_mix, g_pre_ffn, g_post_ffn,
                    w_in, rg_conv_w, rg_conv_b, rg_wa, rg_ba, rg_wx, rg_bx, rg_lam, hg_lb_logits,
                    hg_out_norm, w_proj_rg, w_proj_hg, w_out, ffn_w_up, ffn_conv_w, ffn_conv_b,
                    ffn_w_down, grid_w=GRID_W)
```

```python
import functools

import jax
import jax.numpy as jnp
from jax import lax
from jax.experimental import pallas as pl
from jax.experimental.pallas import tpu as pltpu

GRID_W = 64
RG_HEADS = 16
RG_C = 8.0
HG_EXPAND = 128
N_MOD = 6
EPS = 1e-6
RG_CONV_PAD_L = 1
RG_CONV_PAD_R = 2

HG_CHUNK = 128
HG_DIAG = 8
HG_DIAG_CLAMP = 20.0
HG_F_FLOOR = 1e-37
HG_PITCH = 24
RG_TINY = 1e-30
MM_COLS = 256
RG_COLS = 256
VMEM_LIMIT = 56 * 1024 * 1024

LOG2_E = 1.4426950408889634

F32 = jnp.float32
BF16 = jnp.bfloat16


def _cparams(sem):
    return pltpu.CompilerParams(dimension_semantics=sem, vmem_limit_bytes=VMEM_LIMIT)


def _sigmoid(x):
    return 0.5 + 0.5 * jnp.tanh(0.5 * x)


def _silu(x):
    return x * _sigmoid(x)


def _gelu_tanh(x):
    c = 0.7978845608028654
    return 0.5 * x * (1.0 + jnp.tanh(c * (x + 0.044715 * (x * x * x))))


def _rms(x, g):
    ms = jnp.mean(x * x, axis=-1, keepdims=True)
    return x * lax.rsqrt(ms + EPS) * g


def _per_batch(x, vec, nb):
    r, d = x.shape
    return x.reshape(r // nb, nb, d), vec[None]


def _norm_mod(x, g, shift, scale, nb):
    y = _rms(x, g)
    y3, sc = _per_batch(y, scale, nb)
    h = y3 * (1.0 + sc) + shift[None]
    return h.reshape(x.shape)


def _ada_kernel(c_ref, w_ref, b_ref, o_ref):
    s = _silu(c_ref[...]).astype(BF16)
    o_ref[0] = jnp.dot(s, w_ref[0], preferred_element_type=F32) + b_ref[0]


def _ada_mod(cc, w_ada, b_ada):
    depth, d, n = w_ada.shape
    tn = 1536 if n % 1536 == 0 else n
    return pl.pallas_call(
        _ada_kernel,
        out_shape=jax.ShapeDtypeStruct((depth, cc.shape[0], n), F32),
        grid=(depth, n // tn),
        in_specs=[pl.BlockSpec(cc.shape, lambda l, j: (0, 0)),
                  pl.BlockSpec((1, d, tn), lambda l, j: (l, 0, j)),
                  pl.BlockSpec((1, 1, tn), lambda l, j: (l, 0, j))],
        out_specs=pl.BlockSpec((1, cc.shape[0], tn), lambda l, j: (l, 0, j)),
        compiler_params=_cparams(("parallel", "parallel")),
        name="ada_mod",
    )(cc, w_ada, b_ada.reshape(depth, 1, n))


def _zero_from(v):
    u = lax.bitcast_convert_type(v, jnp.uint32)
    return lax.bitcast_convert_type((u >> 16) >> 16, F32)


def _rg_chunk(xe, bias, cw_ref, wg_ref, bg_ref, lam_ref, h, *, reverse, nb, steps, cols):
    rows = steps * nb
    xl = bias + xe[0:rows] * cw_ref[0:1, :]
    for j in range(1, 4):
        xl = xl + xe[j * nb:j * nb + rows] * cw_ref[j:j + 1, :]
    th = jnp.tanh(jnp.dot(xl.astype(BF16), wg_ref[0], preferred_element_type=F32) + bg_ref[0])
    half_rate = (-0.5 * RG_C) * jax.nn.softplus(-lam_ref[...])
    log_a = half_rate + half_rate * th[:, :cols]
    ig = 0.5 + 0.5 * th[:, cols:]
    a = jnp.exp(log_a)
    om = (1.0 + a * a) * jnp.tanh(-log_a)
    b = (om * lax.rsqrt(jnp.maximum(om, RG_TINY))) * (ig * xl)
    hs = [None] * steps
    for t in (range(steps - 1, -1, -1) if reverse else range(steps)):
        h = a[t * nb:(t + 1) * nb] * h + b[t * nb:(t + 1) * nb]
        hs[t] = h
    return h, jnp.concatenate(hs, axis=0)


def _inproj_kernel(*refs, nb, d, rg, reverse, tt, nt):
    if not rg:
        x_ref, mod_ref, g_ref, w_ref, o_ref, h_ref = refs
        h = _norm_mod(x_ref[...], g_ref[...], mod_ref[:, 0:d], mod_ref[:, d:2 * d], nb).astype(BF16)
        h_ref[...] = h
        res = jnp.dot(h, w_ref[...], preferred_element_type=F32)
        o_ref[...] = res.reshape(o_ref.shape).astype(o_ref.dtype)
        return
    if reverse:
        (h_ref, w_ref, zx_ref, zxp_ref, zxn_ref, cw_ref, cb_ref, wg_ref, bg_ref,
         lam_ref, h0_ref, hf_ref, gg_ref, o_ref, y_ref, hfin_ref, hst_scr) = refs
    else:
        (h_ref, w_ref, zx_ref, zxp_ref, zxn_ref, cw_ref, cb_ref, wg_ref, bg_ref,
         lam_ref, h0_ref, o_ref, y_ref, hfin_ref, hst_scr) = refs
    i = pl.program_id(0)
    j = pl.program_id(1)

    @pl.when(i == 0)
    def _():
        hst_scr[j] = h0_ref[j]

    tile = (nt - 1 - i) if reverse else i
    cols = y_ref.shape[1]
    npiece = d // MM_COLS
    steps = tt // npiece
    sub = steps * nb
    oshape = o_ref.shape[:-1] + (MM_COLS,)
    prev = jnp.where(tile > 0, zxp_ref[...].astype(F32), 0.0)
    nxt = jnp.where(tile < nt - 1, zxn_ref[...].astype(F32), 0.0)
    h = hst_scr[j]
    bias = cb_ref[...]
    for n in range(npiece):
        res = jnp.dot(h_ref[...], w_ref[:, n * MM_COLS:(n + 1) * MM_COLS],
                      preferred_element_type=F32)
        o_ref[..., n * MM_COLS:(n + 1) * MM_COLS] = res.reshape(oshape).astype(o_ref.dtype)
        c = (npiece - 1 - n) if reverse else n
        r0 = c * sub
        lo = prev if c == 0 else zx_ref[r0 - nb:r0, :].astype(F32)
        hi = nxt if c == npiece - 1 else zx_ref[r0 + sub:r0 + sub + 2 * nb, :].astype(F32)
        xe = jnp.concatenate([lo, zx_ref[r0:r0 + sub, :].astype(F32), hi], axis=0)
        h, hs = _rg_chunk(xe, bias, cw_ref, wg_ref, bg_ref, lam_ref, h, reverse=reverse, nb=nb,
                          steps=steps, cols=cols)
        if reverse:
            gate = _gelu_tanh(gg_ref[...].reshape(tt * nb, cols)[r0:r0 + sub, :].astype(F32))
            y_ref[r0:r0 + sub, :] = (gate * (hf_ref[r0:r0 + sub, :] + hs)).astype(y_ref.dtype)
        else:
            y_ref[r0:r0 + sub, :] = hs
        bias = cb_ref[...] + _zero_from(res[sub - 8:sub, MM_COLS - 128:MM_COLS])[0:1, 0:1]
    hst_scr[j] = h
    hfin_ref[j] = h


def _group_map(col_groups):
    def wmap(i, j):
        idx = jnp.int32(col_groups[0])
        for k in range(1, len(col_groups)):
            idx = jnp.where(j == k, jnp.int32(col_groups[k]), idx)
        return (0, idx)
    return wmap


def _inproj(x, mod, g, w, group, tm, nb):
    n, d = x.shape
    row = lambda i: (i, 0)
    return pl.pallas_call(
        functools.partial(_inproj_kernel, nb=nb, d=d, rg=False, reverse=False, tt=0, nt=0),
        out_shape=(jax.ShapeDtypeStruct((n, d), BF16), jax.ShapeDtypeStruct((n, d), BF16)),
        grid=(n // tm,),
        in_specs=[pl.BlockSpec((tm, d), row),
                  pl.BlockSpec(mod.shape, lambda i: (0, 0)),
                  pl.BlockSpec((1, d), lambda i: (0, 0)),
                  pl.BlockSpec((d, d), lambda i: (0, group))],
        out_specs=(pl.BlockSpec((tm, d), row), pl.BlockSpec((tm, d), row)),
        compiler_params=_cparams(("parallel",)),
        name="inproj",
    )(x, mod, g, w)


def _inproj_rg(hx, w, col_groups, out_shape, out_spec_of, zx, hf, zgg, gg_spec_of, conv_w,
               conv_b, wg, bg, lam, h0, *, reverse, nb, tt):
    n, d = hx.shape
    rows = tt * nb
    nt = n // rows
    ncb = d // RG_COLS
    assert len(col_groups) == ncb and tt % (d // MM_COLS) == 0
    ppb = rows // nb
    npb = rows // (2 * nb)
    n_next_blocks = n // (2 * nb)

    def tile_of(i):
        return (nt - 1 - i) if reverse else i

    in_specs = [
        pl.BlockSpec((rows, d), lambda i, j: (tile_of(i), 0)),
        pl.BlockSpec((d, d), _group_map(col_groups)),
        pl.BlockSpec((rows, RG_COLS), lambda i, j: (tile_of(i), j)),
        pl.BlockSpec((nb, RG_COLS), lambda i, j: (jnp.maximum(tile_of(i) * ppb - 1, 0), j)),
        pl.BlockSpec((2 * nb, RG_COLS),
                     lambda i, j: (jnp.minimum((tile_of(i) + 1) * npb, n_next_blocks - 1), j)),
        pl.BlockSpec((4, RG_COLS), lambda i, j: (0, j)),
        pl.BlockSpec((1, RG_COLS), lambda i, j: (0, j)),
        pl.BlockSpec((1, RG_COLS, 2 * RG_COLS), lambda i, j: (j, 0, 0)),
        pl.BlockSpec((1, 1, 2 * RG_COLS), lambda i, j: (j, 0, 0)),
        pl.BlockSpec((1, RG_COLS), lambda i, j: (0, j)),
        pl.BlockSpec((ncb, nb, RG_COLS), lambda i, j: (0, 0, 0)),
    ]
    args = [hx, w, zx, zx, zx, conv_w, conv_b, wg, bg, lam, h0]
    if reverse:
        in_specs += [pl.BlockSpec((rows, RG_COLS), lambda i, j: (tile_of(i), j)), gg_spec_of(tile_of)]
        args += [hf, zgg]
    return pl.pallas_call(
        functools.partial(_inproj_kernel, nb=nb, d=d, rg=True, reverse=reverse, tt=tt, nt=nt),
        out_shape=(out_shape, jax.ShapeDtypeStruct((n, d), BF16 if reverse else F32),
                   jax.ShapeDtypeStruct((ncb, nb, RG_COLS), F32)),
        grid=(nt, ncb),
        in_specs=in_specs,
        out_specs=(out_spec_of(tile_of),
                   pl.BlockSpec((rows, RG_COLS), lambda i, j: (tile_of(i), j)),
                   pl.BlockSpec((ncb, nb, RG_COLS), lambda i, j: (0, 0, 0))),
        scratch_shapes=[pltpu.VMEM((ncb, nb, RG_COLS), F32)],
        compiler_params=_cparams(("arbitrary", "arbitrary")),
        name="inproj_rg_bwd" if reverse else "inproj_rg_fwd",
    )(*args)


def _cumsum_time(x, reverse, block):
    c = x.shape[0]
    x4 = x.reshape((c // block, block) + x.shape[1:])
    cols = [None] * block
    order = range(block - 1, -1, -1) if reverse else range(block)
    run = None
    for i in order:
        run = x4[:, i] if run is None else run + x4[:, i]
        cols[i] = run
    return jnp.stack(cols, axis=1).reshape(x.shape)


def _cumsum_chunk(x, reverse):
    c = x.shape[0]
    blk = 8
    x4 = _cumsum_time(x, reverse, blk).reshape((c // blk, blk) + x.shape[1:])
    nblk = c // blk
    tot = x4[:, 0] if reverse else x4[:, blk - 1]
    offs = [None] * nblk
    order = range(nblk - 1, -1, -1) if reverse else range(nblk)
    run = None
    for i in order:
        offs[i] = run
        run = tot[i] if run is None else run + tot[i]
    first = nblk - 1 if reverse else 0
    parts = [x4[i] if i == first else x4[i] + offs[i][None] for i in range(nblk)]
    return jnp.stack(parts, axis=0).reshape(x.shape)


def _pivot_time(c, block, p):
    c4 = c.reshape((c.shape[0] // block, block) + c.shape[1:])
    return jnp.broadcast_to(c4[:, p:p + 1], c4.shape).reshape(c.shape)


def _hg_levels(c):
    out, m = [], c // 2
    while m >= HG_DIAG:
        out.append(m)
        m //= 2
    return out


def _hg_kernel(*refs, reverse, final):
    if final:
        (zq_ref, zf_ref, v_ref, of_ref, og_ref, par_ref, s0_ref, y_ref, sout_ref,
         st_scr, tr_scr, o_scr, mm_scr) = refs
    else:
        (zq_ref, zf_ref, v_ref, par_ref, s0_ref, y_ref, sout_ref, st_scr, tr_scr, o_scr,
         mm_scr) = refs
    j = pl.program_id(1)
    nj = pl.num_programs(1)
    c, nb, kd = zq_ref.shape

    @pl.when(j == 0)
    def _():
        st_scr[...] = s0_ref[...]

    zq = zq_ref[...].astype(F32)
    zf = zf_ref[...].astype(F32)
    lb = par_ref[0:1, :][None]
    oml = par_ref[1:2, :][None]

    q = _silu(zq)
    e = jnp.exp(-jnp.abs(zf))
    r = 1.0 / (1.0 + e)
    pos = zf >= 0.0
    f = lb + oml * (jnp.where(pos, 1.0, e) * r)
    k = oml * (jnp.where(pos, e, 1.0) * r)
    g = jnp.log2(jnp.maximum(f, HG_F_FLOOR))
    cs = _cumsum_chunk(g, reverse)
    cd = _cumsum_time(jnp.maximum(g, -HG_DIAG_CLAMP * LOG2_E), reverse, HG_DIAG)
    ctot = cs[0] if reverse else cs[c - 1]
    decay = jnp.exp2(ctot)

    pad = jnp.zeros((c, HG_PITCH - nb, kd), F32)
    for n, arr in enumerate((q, k, cs, cd, v_ref[...].astype(F32))):
        tr_scr[n] = jnp.concatenate([arr, pad], axis=1).reshape(c * HG_PITCH, kd)

    ti = lax.broadcasted_iota(jnp.int32, (c, c), 0)
    si = lax.broadcasted_iota(jnp.int32, (c, c), 1)
    lev = ti ^ si
    valid = (ti <= si) if reverse else (ti >= si)
    nt_dims = (((1,), (1,)), ((), ()))
    tn_dims = (((0,), (0,)), ((), ()))
    levels = _hg_levels(c)

    for b in range(nb):
        rows = pl.ds(b, c, stride=HG_PITCH)
        q2, k2, cs2, cd2 = (tr_scr[n, rows, :] for n in range(4))
        qb = q2.astype(BF16)
        kb = k2.astype(BF16)
        mm_scr[0, b] = qb * jnp.exp2(cs2).astype(BF16)
        mm_scr[1, b] = kb * jnp.exp2(ctot[b:b + 1, :] - cs2).astype(BF16)
        mm_scr[2, b] = tr_scr[4, rows, :].astype(BF16)
        cdd = cd2 - _pivot_time(cd2, HG_DIAG, HG_DIAG // 2 if reverse else HG_DIAG // 2 - 1)
        a = lax.dot_general(qb * jnp.exp2(cdd).astype(BF16), kb * jnp.exp2(-cdd).astype(BF16),
                            nt_dims, preferred_element_type=F32)
        for m in levels[::-1]:
            piv = m if reverse else m - 1
            el = jnp.exp2(-jnp.abs((cs2 - _pivot_time(cs2, 2 * m, piv)).astype(BF16)))
            p = lax.dot_general(qb * el, kb * el, nt_dims, preferred_element_type=F32)
            a = jnp.where(lev < m, a, p)
        mm_scr[3, b] = jnp.where(valid, a, 0.0).astype(BF16)

    for b in range(nb):
        st = st_scr[b]
        vb = mm_scr[2, b]
        o = lax.dot_general(mm_scr[0, b], st.astype(BF16), nt_dims, preferred_element_type=F32)
        o = o + jnp.dot(mm_scr[3, b], vb, preferred_element_type=F32)
        st_scr[b] = st * decay[b:b + 1, :] + lax.dot_general(vb, mm_scr[1, b], tn_dims,
                                                             preferred_element_type=F32)
        o_scr[pl.ds(b, c, stride=HG_PITCH), :] = o

    o = o_scr[...].reshape(c, HG_PITCH, kd)[:, 0:nb, :]
    if final:
        gain = par_ref[3:4, :][None]
        y = _rms(o + of_ref[...], gain) * _silu(og_ref[...].astype(F32))
        y_ref[...] = y.astype(y_ref.dtype)
    else:
        y_ref[...] = o

    @pl.when(j == nj - 1)
    def _():
        sout_ref[...] = st_scr[...]


def _hg_dir(za, zb, of, par, s0, *, reverse, d):
    t, nb, _ = za.shape
    kd = HG_EXPAND
    heads = d // kd
    nj = t // HG_CHUNK
    final = reverse
    assert nb <= HG_PITCH

    def blk(j):
        return (nj - 1 - j) if reverse else j

    def zspec(gidx):
        return pl.BlockSpec((HG_CHUNK, nb, kd), lambda h, j: (blk(j), 0, gidx * heads + h))

    hspec = pl.BlockSpec((HG_CHUNK, nb, kd), lambda h, j: (blk(j), 0, h))
    sspec = pl.BlockSpec((nb, None, kd, kd), lambda h, j: (0, h, 0, 0))
    in_specs = [zspec(1), zspec(3 if reverse else 2), zspec(0)]
    args = [za, za, zb]
    scratch = [pltpu.VMEM((nb, kd, kd), F32), pltpu.VMEM((5, HG_CHUNK * HG_PITCH, kd), F32),
               pltpu.VMEM((HG_CHUNK * HG_PITCH, kd), F32),
               pltpu.VMEM((4, nb, HG_CHUNK, kd), BF16)]
    if final:
        in_specs += [hspec, zspec(1)]
        args += [of, zb]
    in_specs += [pl.BlockSpec((par.shape[0], kd), lambda h, j: (0, h)), sspec]
    args += [par, s0]
    return pl.pallas_call(
        functools.partial(_hg_kernel, reverse=reverse, final=final),
        out_shape=(jax.ShapeDtypeStruct((t, nb, d), BF16 if final else F32),
                   jax.ShapeDtypeStruct(s0.shape, F32)),
        grid=(heads, nj),
        in_specs=in_specs,
        out_specs=(hspec, sspec),
        scratch_shapes=scratch,
        compiler_params=_cparams(("parallel", "arbitrary")),
        name="hg_bwd" if reverse else "hg_fwd",
    )(*args)


def _merge_kernel(yrg_ref, yhg_ref, ga_ref, gb_ref, x_ref, mod_ref, g_ref, wr_ref, wh_ref, wo_ref,
                  o_ref, *, nb, d):
    tm = x_ref.shape[0]
    p_rg = jnp.dot(yrg_ref[...], wr_ref[...], preferred_element_type=F32)
    p_hg = jnp.dot(yhg_ref[...].reshape(tm, d), wh_ref[...], preferred_element_type=F32)
    ga = ga_ref[...].reshape(tm, d).astype(F32)
    gb = gb_ref[...].reshape(tm, d).astype(F32)
    m = _sigmoid(ga) * p_rg + _sigmoid(gb) * p_hg
    y = jnp.dot(m.astype(BF16), wo_ref[...], preferred_element_type=F32)
    n3, gate = _per_batch(_rms(y, g_ref[...]), mod_ref[:, 2 * d:3 * d], nb)
    o_ref[...] = x_ref[...] + (gate * n3).reshape(tm, d)


def _merge(yrg, yhg, zb, tile_spec, x, mod, g, wr, wh, wo, *, tm, nb):
    n, d = x.shape
    row = lambda i: (i, 0)
    const = lambda i: (0, 0)
    return pl.pallas_call(
        functools.partial(_merge_kernel, nb=nb, d=d),
        out_shape=jax.ShapeDtypeStruct((n, d), F32),
        grid=(n // tm,),
        in_specs=[pl.BlockSpec((tm, d), row), tile_spec(0), tile_spec(2), tile_spec(3),
                  pl.BlockSpec((tm, d), row), pl.BlockSpec(mod.shape, const),
                  pl.BlockSpec((1, d), const), pl.BlockSpec((d, d), const),
                  pl.BlockSpec((d, d), const), pl.BlockSpec((d, d), const)],
        out_specs=pl.BlockSpec((tm, d), row),
        compiler_params=_cparams(("parallel",)),
        name="merge",
    )(yrg, yhg, zb, zb, x, mod, g, wr, wh, wo)


def _ffn_kernel(*refs, nb, d, dff, tf, halo, nt):
    if halo:
        (x_ref, xp_ref, xn_ref, mod_ref, g1_ref, g2_ref, wu_ref, cw_ref, cb_ref, wd_ref,
         o_ref, act_scr) = refs
    else:
        (x_ref, mod_ref, g1_ref, g2_ref, wu_ref, cw_ref, cb_ref, wd_ref, o_ref, act_scr) = refs
    i = pl.program_id(0)
    tm = x_ref.shape[0]
    shift = mod_ref[:, 3 * d:4 * d]
    scale = mod_ref[:, 4 * d:5 * d]

    hm = _norm_mod(x_ref[...], g1_ref[...], shift, scale, nb).astype(BF16)
    if halo:
        hp = _norm_mod(xp_ref[...], g1_ref[...], shift, scale, nb)
        hn = _norm_mod(xn_ref[...], g1_ref[...], shift, scale, nb)
        hp = jnp.where(i > 0, hp, 0.0).astype(BF16)
        hn = jnp.where(i < nt - 1, hn, 0.0).astype(BF16)
        he = jnp.concatenate([hp, hm, hn], axis=0)

    for kb in range(dff // tf):
        gs = slice(kb * tf, (kb + 1) * tf)
        vs = slice(dff + kb * tf, dff + (kb + 1) * tf)
        if halo:
            ue = jnp.dot(he, wu_ref[:, gs], preferred_element_type=F32)
        else:
            u = jnp.dot(hm, wu_ref[:, gs], preferred_element_type=F32)
            z = jnp.zeros((nb, tf), F32)
            ue = jnp.concatenate([z, u, z], axis=0)
        gc = cb_ref[:, gs] + ue[0:tm] * cw_ref[0:1, gs] + ue[nb:nb + tm] * cw_ref[1:2, gs] \
            + ue[2 * nb:2 * nb + tm] * cw_ref[2:3, gs]
        uv = jnp.dot(hm, wu_ref[:, vs], preferred_element_type=F32)
        act_scr[:, gs] = (_silu(gc) * uv).astype(BF16)

    y = jnp.dot(act_scr[...], wd_ref[...], preferred_element_type=F32)
    n3, gate = _per_batch(_rms(y, g2_ref[...]), mod_ref[:, 5 * d:6 * d], nb)
    o_ref[...] = x_ref[...] + (gate * n3).reshape(tm, d)


def _ffn(x, mod, g1, g2, w_up, cw, cb, w_down, *, tm, tf, nb, halo):
    n, d = x.shape
    dff = w_down.shape[0]
    nt = n // tm
    hb = tm // nb
    row = lambda i: (i, 0)
    const = lambda i: (0, 0)

    def resident(shape):
        return pl.BlockSpec(shape, const, pipeline_mode=pl.Buffered(1))

    in_specs = [pl.BlockSpec((tm, d), row)]
    args = [x]
    if halo:
        in_specs += [pl.BlockSpec((nb, d), lambda i: (jnp.maximum(i * hb - 1, 0), 0)),
                     pl.BlockSpec((nb, d), lambda i: (jnp.minimum((i + 1) * hb, n // nb - 1), 0))]
        args += [x, x]
    in_specs += [pl.BlockSpec(mod.shape, const), pl.BlockSpec((1, d), const),
                 pl.BlockSpec((1, d), const), resident(w_up.shape),
                 pl.BlockSpec(cw.shape, const), pl.BlockSpec(cb.shape, const),
                 resident(w_down.shape)]
    args += [mod, g1, g2, w_up, cw, cb, w_down]
    return pl.pallas_call(
        functools.partial(_ffn_kernel, nb=nb, d=d, dff=dff, tf=tf, halo=halo, nt=nt),
        out_shape=jax.ShapeDtypeStruct((n, d), F32),
        grid=(nt,),
        in_specs=in_specs,
        out_specs=pl.BlockSpec((tm, d), row),
        scratch_shapes=[pltpu.VMEM((tm, dff), BF16)],
        compiler_params=_cparams(("parallel",)),
        name="ffn_ctx" if halo else "ffn",
    )(*args)


def _gate_weights(wa, wx):
    nd, heads, hd, _ = wa.shape
    per = RG_COLS // hd
    ncb = heads // per

    def bd(w):
        w = w.reshape(nd, ncb, per, hd, hd)
        eye = jnp.eye(per, dtype=w.dtype)
        return jnp.einsum('dcpij,pq->dcpiqj', w, eye).reshape(nd, ncb, RG_COLS, RG_COLS)

    return (0.5 * jnp.concatenate([bd(wa), bd(wx)], axis=-1)).astype(BF16)


def _forward(x, c, ctx, c_ctx, w_ada, b_ada, g_pre_mix, g_post_mix, g_pre_ffn, g_post_ffn, w_in,
             rg_conv_w, rg_conv_b, rg_wa, rg_ba, rg_wx, rg_bx, rg_lam, hg_lb_logits, hg_out_norm,
             w_proj_rg, w_proj_hg, w_out, ffn_w_up, ffn_conv_w, ffn_conv_b, ffn_w_down, *, grid_w):
    nb, seq, d = x.shape
    ctx_len = ctx.shape[1]
    depth = w_in.shape[0]
    rows_g = seq // grid_w
    dff = ffn_w_down.shape[1]
    tile = grid_w * nb
    heads = d // HG_EXPAND
    ncb = d // RG_COLS
    tf = 256 if dff % 256 == 0 else dff

    p = jax.nn.softmax(hg_lb_logits.astype(F32), axis=0)
    cum = jnp.cumsum(p, axis=0)
    lb_all = cum - cum[0:1]
    hg_par = jnp.stack([lb_all, 1.0 - lb_all, jnp.zeros_like(lb_all), hg_out_norm], axis=1)
    hg_par = jnp.pad(hg_par, ((0, 0), (0, 4), (0, 0)))

    w_ada_b, w_in_b = w_ada.astype(BF16), w_in.astype(BF16)
    wr_b, wh_b, wo_b = w_proj_rg.astype(BF16), w_proj_hg.astype(BF16), w_out.astype(BF16)
    wup_b, wdn_b = ffn_w_up.astype(BF16), ffn_w_down.astype(BF16)

    cc = jnp.concatenate([c, jnp.broadcast_to(c_ctx[None], (nb, d))], axis=0)
    mod_all = _ada_mod(cc, w_ada_b, b_ada).reshape(depth, 2, nb, N_MOD * d)

    xl = jnp.swapaxes(x, 0, 1).reshape(seq * nb, d)
    xc = jnp.swapaxes(ctx, 0, 1).reshape(ctx_len * nb, d)

    for l in range(depth):
        need_ctx = l < depth - 1
        mod_l, mod_c = mod_all[l, 0], mod_all[l, 1]
        gpm = g_pre_mix[l].reshape(1, d)
        gpo = g_post_mix[l].reshape(1, d)
        wg = _gate_weights(rg_wa[l], rg_wx[l])
        bg = 0.5 * jnp.concatenate([rg_ba[l].reshape(2, ncb, 1, RG_COLS),
                                    rg_bx[l].reshape(2, ncb, 1, RG_COLS)], axis=-1)
        rg_kw = dict(conv_w=rg_conv_w[l], conv_b=rg_conv_b[l].reshape(1, d), nb=nb, tt=grid_w)

        st_rg = [jnp.zeros((ncb, nb, RG_COLS), F32)] * 2
        st_hg = [jnp.zeros((nb, heads, HG_EXPAND, HG_EXPAND), F32)] * 2
        ctx_out = None
        for xs, mod_s, steps_s, latent in ((xc, mod_c, ctx_len, False), (xl, mod_l, seq, True)):
            if latent:
                shape4 = jax.ShapeDtypeStruct((grid_w, rows_g, nb, 4 * d), BF16)
                blk = lambda w, col: pl.BlockSpec((grid_w, None, nb, w), col)
                spec_of = lambda tile_of: blk(d, lambda i, j: (0, tile_of(i), 0, j))
                gg_of = lambda tile_of: blk(RG_COLS, lambda i, j: (0, tile_of(i), 0, j))
                tile_spec = lambda gidx: blk(d, lambda i: (0, i, 0, gidx))
            else:
                shape4 = jax.ShapeDtypeStruct((steps_s, nb, 4 * d), BF16)
                blk = lambda w, col: pl.BlockSpec((grid_w, nb, w), col)
                spec_of = lambda tile_of: blk(d, lambda i, j: (tile_of(i), 0, j))
                gg_of = lambda tile_of: blk(RG_COLS, lambda i, j: (tile_of(i), 0, j))
                tile_spec = lambda gidx: blk(d, lambda i: (i, 0, gidx))
            zx, hx = _inproj(xs, mod_s, gpm, w_in_b[l], 0, tile, nb)
            za, hf, st_rg[0] = _inproj_rg(
                hx, w_in_b[l], (1, 2, 3, 4), shape4, spec_of, zx, None, None, None,
                wg=wg[0], bg=bg[0], lam=rg_lam[l, 0:1], h0=st_rg[0], reverse=False, **rg_kw)
            zb, yrg, st_rg[1] = _inproj_rg(
                hx, w_in_b[l], (5, 6, 7, 8), shape4, spec_of, zx, hf, za, gg_of,
                wg=wg[1], bg=bg[1], lam=rg_lam[l, 1:2], h0=st_rg[1], reverse=True, **rg_kw)
            za3 = za.reshape(steps_s, nb, 4 * d)
            zb3 = zb.reshape(steps_s, nb, 4 * d)
            of, st_hg[0] = _hg_dir(za3, zb3, None, hg_par[l], st_hg[0], reverse=False, d=d)
            yhg, st_hg[1] = _hg_dir(za3, zb3, of, hg_par[l], st_hg[1], reverse=True, d=d)
            if latent or need_ctx:
                yhg = yhg.reshape(shape4.shape[:-1] + (d,))
                out = _merge(yrg, yhg, zb, tile_spec, xs, mod_s, gpo, wr_b[l], wh_b[l], wo_b[l],
                             tm=tile, nb=nb)
                if latent:
                    xl = out
                else:
                    ctx_out = out
        ffn = functools.partial(_ffn, g1=g_pre_ffn[l].reshape(1, d), g2=g_post_ffn[l].reshape(1, d),
                                w_up=wup_b[l], cw=ffn_conv_w[l], cb=ffn_conv_b[l].reshape(1, dff),
                                w_down=wdn_b[l], tm=tile, tf=tf, nb=nb)
        xl = ffn(xl, mod_l, halo=False)
        if need_ctx:
            xc = ffn(ctx_out, mod_c, halo=True)

    return jnp.swapaxes(xl.reshape(seq, nb, d), 0, 1)


def kernel(x, c, ctx, c_ctx, w_ada, b_ada, g_pre_mix, g_post_mix, g_pre_ffn, g_post_ffn, w_in, rg_conv_w, rg_conv_b, rg_wa, rg_ba, rg_wx, rg_bx, rg_lam, hg_lb_logits, hg_out_norm, w_proj_rg, w_proj_hg, w_out, ffn_w_up, ffn_conv_w, ffn_conv_b, ffn_w_down):
    return _forward(x, c, ctx, c_ctx, w_ada, b_ada, g_pre_mix, g_post_mix, g_pre_ffn, g_post_ffn,
                    w_in, rg_conv_w, rg_conv_b, rg_wa, rg_ba, rg_wx, rg_bx, rg_lam, hg_lb_logits,
                    hg_out_norm, w_proj_rg, w_proj_hg, w_out, ffn_w_up, ffn_conv_w, ffn_conv_b,
                    ffn_w_down, grid_w=GRID_W)
```

```python
import functools

import jax
import jax.numpy as jnp
from jax import lax
from jax.experimental import pallas as pl
from jax.experimental.pallas import tpu as pltpu

GRID_W = 64
RG_HEADS = 16
RG_C = 8.0
HG_EXPAND = 128
N_MOD = 6
EPS = 1e-6
RG_CONV_PAD_L = 1
RG_CONV_PAD_R = 2

HG_CHUNK = 128
HG_DIAG = 8
HG_DIAG_CLAMP = 20.0
HG_F_FLOOR = 1e-37
HG_PITCH = 24
RG_TINY = 1e-30
MM_COLS = 256
RG_GATE = 256
RG_COLS = 512
GROUPS_PER_STEP = 2
VMEM_LIMIT = 56 * 1024 * 1024

LOG2_E = 1.4426950408889634

F32 = jnp.float32
BF16 = jnp.bfloat16


def _cparams(sem):
    return pltpu.CompilerParams(dimension_semantics=sem, vmem_limit_bytes=VMEM_LIMIT)


def _sigmoid(x):
    return 0.5 + 0.5 * jnp.tanh(0.5 * x)


def _silu(x):
    return x * _sigmoid(x)


def _gelu_tanh(x):
    c = 0.7978845608028654
    return 0.5 * x * (1.0 + jnp.tanh(c * (x + 0.044715 * (x * x * x))))


def _rms(x, g):
    ms = jnp.mean(x * x, axis=-1, keepdims=True)
    return x * lax.rsqrt(ms + EPS) * g


def _per_batch(x, vec, nb):
    r, d = x.shape
    return x.reshape(r // nb, nb, d), vec[None]


def _norm_mod(x, g, shift, scale, nb):
    y = _rms(x, g)
    y3, sc = _per_batch(y, scale, nb)
    h = y3 * (1.0 + sc) + shift[None]
    return h.reshape(x.shape)


def _ada_kernel(c_ref, w_ref, b_ref, o_ref):
    s = _silu(c_ref[...]).astype(BF16)
    o_ref[0] = jnp.dot(s, w_ref[0], preferred_element_type=F32) + b_ref[0]


def _ada_mod(cc, w_ada, b_ada):
    depth, d, n = w_ada.shape
    tn = 1536 if n % 1536 == 0 else n
    return pl.pallas_call(
        _ada_kernel,
        out_shape=jax.ShapeDtypeStruct((depth, cc.shape[0], n), F32),
        grid=(depth, n // tn),
        in_specs=[pl.BlockSpec(cc.shape, lambda l, j: (0, 0)),
                  pl.BlockSpec((1, d, tn), lambda l, j: (l, 0, j)),
                  pl.BlockSpec((1, 1, tn), lambda l, j: (l, 0, j))],
        out_specs=pl.BlockSpec((1, cc.shape[0], tn), lambda l, j: (l, 0, j)),
        compiler_params=_cparams(("parallel", "parallel")),
        name="ada_mod",
    )(cc, w_ada, b_ada.reshape(depth, 1, n))


def _zero_from(v):
    u = lax.bitcast_convert_type(v, jnp.uint32)
    return lax.bitcast_convert_type((u >> 16) >> 16, F32)


def _rg_chunk(xe, bias, cw_ref, wg_ref, bg_ref, lam_ref, h, *, reverse, nb, steps, cols):
    rows = steps * nb
    xl = bias + xe[0:rows] * cw_ref[0:1, :]
    for j in range(1, 4):
        xl = xl + xe[j * nb:j * nb + rows] * cw_ref[j:j + 1, :]
    xb = xl.astype(BF16)
    tha, thx = [], []
    for s in range(cols // RG_GATE):
        th = jnp.tanh(jnp.dot(xb[:, s * RG_GATE:(s + 1) * RG_GATE], wg_ref[s],
                              preferred_element_type=F32) + bg_ref[s])
        tha.append(th[:, :RG_GATE])
        thx.append(th[:, RG_GATE:])
    half_rate = (-0.5 * RG_C) * jax.nn.softplus(-lam_ref[...])
    log_a = half_rate + half_rate * jnp.concatenate(tha, axis=1)
    ig = 0.5 + 0.5 * jnp.concatenate(thx, axis=1)
    a = jnp.exp(log_a)
    om = (1.0 + a * a) * jnp.tanh(-log_a)
    b = (om * lax.rsqrt(jnp.maximum(om, RG_TINY))) * (ig * xl)
    hs = [None] * steps
    for t in (range(steps - 1, -1, -1) if reverse else range(steps)):
        h = a[t * nb:(t + 1) * nb] * h + b[t * nb:(t + 1) * nb]
        hs[t] = h
    return h, jnp.concatenate(hs, axis=0)


def _inproj_kernel(*refs, nb, d, rg, reverse, tt, nt):
    if not rg:
        x_ref, mod_ref, g_ref, w_ref, o_ref, h_ref = refs
        h = _norm_mod(x_ref[...], g_ref[...], mod_ref[:, 0:d], mod_ref[:, d:2 * d], nb).astype(BF16)
        h_ref[...] = h
        res = jnp.dot(h, w_ref[...], preferred_element_type=F32)
        o_ref[...] = res.reshape(o_ref.shape).astype(o_ref.dtype)
        return
    gps = GROUPS_PER_STEP
    h_ref, w_refs, rest = refs[0], refs[1:1 + gps], refs[1 + gps:]
    if reverse:
        (zx_ref, zxp_ref, zxn_ref, cw_ref, cb_ref, wg_ref, bg_ref,
         lam_ref, h0_ref, hf_ref, gg_ref, o_ref, y_ref, hfin_ref, hst_scr) = rest
    else:
        (zx_ref, zxp_ref, zxn_ref, cw_ref, cb_ref, wg_ref, bg_ref,
         lam_ref, h0_ref, o_ref, y_ref, hfin_ref, hst_scr) = rest
    i = pl.program_id(0)
    j = pl.program_id(1)

    @pl.when(i == 0)
    def _():
        hst_scr[j] = h0_ref[j]

    tile = (nt - 1 - i) if reverse else i
    cols = y_ref.shape[1]
    ppg = d // MM_COLS
    npiece = gps * ppg
    steps = tt // npiece
    sub = steps * nb
    oshape = o_ref.shape[:-1] + (MM_COLS,)
    prev = jnp.where(tile > 0, zxp_ref[...].astype(F32), 0.0)
    nxt = jnp.where(tile < nt - 1, zxn_ref[...].astype(F32), 0.0)
    h = hst_scr[j]
    bias = cb_ref[...]
    for n in range(npiece):
        w_ref = w_refs[n // ppg]
        res = jnp.dot(h_ref[...], w_ref[:, (n % ppg) * MM_COLS:(n % ppg + 1) * MM_COLS],
                      preferred_element_type=F32)
        o_ref[..., n * MM_COLS:(n + 1) * MM_COLS] = res.reshape(oshape).astype(o_ref.dtype)
        c = (npiece - 1 - n) if reverse else n
        r0 = c * sub
        lo = prev if c == 0 else zx_ref[r0 - nb:r0, :].astype(F32)
        hi = nxt if c == npiece - 1 else zx_ref[r0 + sub:r0 + sub + 2 * nb, :].astype(F32)
        xe = jnp.concatenate([lo, zx_ref[r0:r0 + sub, :].astype(F32), hi], axis=0)
        h, hs = _rg_chunk(xe, bias, cw_ref, wg_ref, bg_ref, lam_ref, h, reverse=reverse, nb=nb,
                          steps=steps, cols=cols)
        if reverse:
            gate = _gelu_tanh(gg_ref[...].reshape(tt * nb, cols)[r0:r0 + sub, :].astype(F32))
            y_ref[r0:r0 + sub, :] = (gate * (hf_ref[r0:r0 + sub, :] + hs)).astype(y_ref.dtype)
        else:
            y_ref[r0:r0 + sub, :] = hs
        bias = cb_ref[...] + _zero_from(res[248:256, MM_COLS - 128:MM_COLS])[0:1, 0:1]
    hst_scr[j] = h
    hfin_ref[j] = h


def _group_map(col_groups):
    def wmap(i, j):
        idx = jnp.int32(col_groups[0])
        for k in range(1, len(col_groups)):
            idx = jnp.where(j == k, jnp.int32(col_groups[k]), idx)
        return (0, idx)
    return wmap


def _inproj(x, mod, g, w, group, tm, nb):
    n, d = x.shape
    row = lambda i: (i, 0)
    return pl.pallas_call(
        functools.partial(_inproj_kernel, nb=nb, d=d, rg=False, reverse=False, tt=0, nt=0),
        out_shape=(jax.ShapeDtypeStruct((n, d), BF16), jax.ShapeDtypeStruct((n, d), BF16)),
        grid=(n // tm,),
        in_specs=[pl.BlockSpec((tm, d), row),
                  pl.BlockSpec(mod.shape, lambda i: (0, 0)),
                  pl.BlockSpec((1, d), lambda i: (0, 0)),
                  pl.BlockSpec((d, d), lambda i: (0, group))],
        out_specs=(pl.BlockSpec((tm, d), row), pl.BlockSpec((tm, d), row)),
        compiler_params=_cparams(("parallel",)),
        name="inproj",
    )(x, mod, g, w)


def _inproj_rg(hx, w, col_groups, out_shape, out_spec_of, zx, hf, zgg, gg_spec_of, conv_w,
               conv_b, wg, bg, lam, h0, *, reverse, nb, tt):
    n, d = hx.shape
    rows = tt * nb
    nt = n // rows
    ncb = d // RG_COLS
    gps = GROUPS_PER_STEP
    nsb = RG_COLS // RG_GATE
    assert len(col_groups) == ncb * gps and tt % (gps * d // MM_COLS) == 0
    ppb = rows // nb
    npb = rows // (2 * nb)
    n_next_blocks = n // (2 * nb)

    def tile_of(i):
        return (nt - 1 - i) if reverse else i

    in_specs = [pl.BlockSpec((rows, d), lambda i, j: (tile_of(i), 0))]
    in_specs += [pl.BlockSpec((d, d), _group_map(col_groups[g::gps])) for g in range(gps)]
    in_specs += [
        pl.BlockSpec((rows, RG_COLS), lambda i, j: (tile_of(i), j)),
        pl.BlockSpec((nb, RG_COLS), lambda i, j: (jnp.maximum(tile_of(i) * ppb - 1, 0), j)),
        pl.BlockSpec((2 * nb, RG_COLS),
                     lambda i, j: (jnp.minimum((tile_of(i) + 1) * npb, n_next_blocks - 1), j)),
        pl.BlockSpec((4, RG_COLS), lambda i, j: (0, j)),
        pl.BlockSpec((1, RG_COLS), lambda i, j: (0, j)),
        pl.BlockSpec((nsb, RG_GATE, 2 * RG_GATE), lambda i, j: (j, 0, 0)),
        pl.BlockSpec((nsb, 1, 2 * RG_GATE), lambda i, j: (j, 0, 0)),
        pl.BlockSpec((1, RG_COLS), lambda i, j: (0, j)),
        pl.BlockSpec((ncb, nb, RG_COLS), lambda i, j: (0, 0, 0)),
    ]
    args = [hx] + [w] * gps + [zx, zx, zx, conv_w, conv_b, wg, bg, lam, h0]
    if reverse:
        in_specs += [pl.BlockSpec((rows, RG_COLS), lambda i, j: (tile_of(i), j)), gg_spec_of(tile_of)]
        args += [hf, zgg]
    return pl.pallas_call(
        functools.partial(_inproj_kernel, nb=nb, d=d, rg=True, reverse=reverse, tt=tt, nt=nt),
        out_shape=(out_shape, jax.ShapeDtypeStruct((n, d), BF16 if reverse else F32),
                   jax.ShapeDtypeStruct((ncb, nb, RG_COLS), F32)),
        grid=(nt, ncb),
        in_specs=in_specs,
        out_specs=(out_spec_of(tile_of),
                   pl.BlockSpec((rows, RG_COLS), lambda i, j: (tile_of(i), j)),
                   pl.BlockSpec((ncb, nb, RG_COLS), lambda i, j: (0, 0, 0))),
        scratch_shapes=[pltpu.VMEM((ncb, nb, RG_COLS), F32)],
        compiler_params=_cparams(("arbitrary", "arbitrary")),
        name="inproj_rg_bwd" if reverse else "inproj_rg_fwd",
    )(*args)


def _cumsum_time(x, reverse, block):
    c = x.shape[0]
    x4 = x.reshape((c // block, block) + x.shape[1:])
    cols = [None] * block
    order = range(block - 1, -1, -1) if reverse else range(block)
    run = None
    for i in order:
        run = x4[:, i] if run is None else run + x4[:, i]
        cols[i] = run
    return jnp.stack(cols, axis=1).reshape(x.shape)


def _cumsum_chunk(x, reverse):
    c = x.shape[0]
    blk = 8
    x4 = _cumsum_time(x, reverse, blk).reshape((c // blk, blk) + x.shape[1:])
    nblk = c // blk
    tot = x4[:, 0] if reverse else x4[:, blk - 1]
    offs = [None] * nblk
    order = range(nblk - 1, -1, -1) if reverse else range(nblk)
    run = None
    for i in order:
        offs[i] = run
        run = tot[i] if run is None else run + tot[i]
    first = nblk - 1 if reverse else 0
    parts = [x4[i] if i == first else x4[i] + offs[i][None] for i in range(nblk)]
    return jnp.stack(parts, axis=0).reshape(x.shape)


def _pivot_time(c, block, p):
    c4 = c.reshape((c.shape[0] // block, block) + c.shape[1:])
    return jnp.broadcast_to(c4[:, p:p + 1], c4.shape).reshape(c.shape)


def _hg_levels(c):
    out, m = [], c // 2
    while m >= HG_DIAG:
        out.append(m)
        m //= 2
    return out


def _hg_kernel(*refs, reverse, final):
    if final:
        (zq_ref, zf_ref, v_ref, of_ref, og_ref, par_ref, s0_ref, y_ref, sout_ref,
         st_scr, tr_scr, o_scr, mm_scr) = refs
    else:
        (zq_ref, zf_ref, v_ref, par_ref, s0_ref, y_ref, sout_ref, st_scr, tr_scr, o_scr,
         mm_scr) = refs
    j = pl.program_id(1)
    nj = pl.num_programs(1)
    c, nb, kd = zq_ref.shape

    @pl.when(j == 0)
    def _():
        st_scr[...] = s0_ref[...]

    zq = zq_ref[...].astype(F32)
    zf = zf_ref[...].astype(F32)
    lb = par_ref[0:1, :][None]
    oml = par_ref[1:2, :][None]

    q = _silu(zq)
    e = jnp.exp(-jnp.abs(zf))
    r = 1.0 / (1.0 + e)
    pos = zf >= 0.0
    f = lb + oml * (jnp.where(pos, 1.0, e) * r)
    k = oml * (jnp.where(pos, e, 1.0) * r)
    g = jnp.log2(jnp.maximum(f, HG_F_FLOOR))
    cs = _cumsum_chunk(g, reverse)
    cd = _cumsum_time(jnp.maximum(g, -HG_DIAG_CLAMP * LOG2_E), reverse, HG_DIAG)
    ctot = cs[0] if reverse else cs[c - 1]
    decay = jnp.exp2(ctot)

    pad = jnp.zeros((c, HG_PITCH - nb, kd), F32)
    for n, arr in enumerate((q, k, cs, cd, v_ref[...].astype(F32))):
        tr_scr[n] = jnp.concatenate([arr, pad], axis=1).reshape(c * HG_PITCH, kd)

    ti = lax.broadcasted_iota(jnp.int32, (c, c), 0)
    si = lax.broadcasted_iota(jnp.int32, (c, c), 1)
    lev = ti ^ si
    valid = (ti <= si) if reverse else (ti >= si)
    nt_dims = (((1,), (1,)), ((), ()))
    tn_dims = (((0,), (0,)), ((), ()))
    levels = _hg_levels(c)

    for b in range(nb):
        rows = pl.ds(b, c, stride=HG_PITCH)
        q2, k2, cs2, cd2 = (tr_scr[n, rows, :] for n in range(4))
        qb = q2.astype(BF16)
        kb = k2.astype(BF16)
        mm_scr[0, b] = qb * jnp.exp2(cs2).astype(BF16)
        mm_scr[1, b] = kb * jnp.exp2(ctot[b:b + 1, :] - cs2).astype(BF16)
        mm_scr[2, b] = tr_scr[4, rows, :].astype(BF16)
        cdd = cd2 - _pivot_time(cd2, HG_DIAG, HG_DIAG // 2 if reverse else HG_DIAG // 2 - 1)
        a = lax.dot_general(qb * jnp.exp2(cdd).astype(BF16), kb * jnp.exp2(-cdd).astype(BF16),
                            nt_dims, preferred_element_type=F32)
        for m in levels[::-1]:
            piv = m if reverse else m - 1
            el = jnp.exp2(-jnp.abs((cs2 - _pivot_time(cs2, 2 * m, piv)).astype(BF16)))
            p = lax.dot_general(qb * el, kb * el, nt_dims, preferred_element_type=F32)
            a = jnp.where(lev < m, a, p)
        mm_scr[3, b] = jnp.where(valid, a, 0.0).astype(BF16)

    for b in range(nb):
        st = st_scr[b]
        vb = mm_scr[2, b]
        o = lax.dot_general(mm_scr[0, b], st.astype(BF16), nt_dims, preferred_element_type=F32)
        o = o + jnp.dot(mm_scr[3, b], vb, preferred_element_type=F32)
        st_scr[b] = st * decay[b:b + 1, :] + lax.dot_general(vb, mm_scr[1, b], tn_dims,
                                                             preferred_element_type=F32)
        o_scr[pl.ds(b, c, stride=HG_PITCH), :] = o

    o = o_scr[...].reshape(c, HG_PITCH, kd)[:, 0:nb, :]
    if final:
        gain = par_ref[3:4, :][None]
        y = _rms(o + of_ref[...], gain) * _silu(og_ref[...].astype(F32))
        y_ref[...] = y.astype(y_ref.dtype)
    else:
        y_ref[...] = o

    @pl.when(j == nj - 1)
    def _():
        sout_ref[...] = st_scr[...]


def _hg_dir(za, zb, of, par, s0, *, reverse, d):
    t, nb, _ = za.shape
    kd = HG_EXPAND
    heads = d // kd
    nj = t // HG_CHUNK
    final = reverse
    assert nb <= HG_PITCH

    def blk(j):
        return (nj - 1 - j) if reverse else j

    def zspec(gidx):
        return pl.BlockSpec((HG_CHUNK, nb, kd), lambda h, j: (blk(j), 0, gidx * heads + h))

    hspec = pl.BlockSpec((HG_CHUNK, nb, kd), lambda h, j: (blk(j), 0, h))
    sspec = pl.BlockSpec((nb, None, kd, kd), lambda h, j: (0, h, 0, 0))
    in_specs = [zspec(1), zspec(3 if reverse else 2), zspec(0)]
    args = [za, za, zb]
    scratch = [pltpu.VMEM((nb, kd, kd), F32), pltpu.VMEM((5, HG_CHUNK * HG_PITCH, kd), F32),
               pltpu.VMEM((HG_CHUNK * HG_PITCH, kd), F32),
               pltpu.VMEM((4, nb, HG_CHUNK, kd), BF16)]
    if final:
        in_specs += [hspec, zspec(1)]
        args += [of, zb]
    in_specs += [pl.BlockSpec((par.shape[0], kd), lambda h, j: (0, h)), sspec]
    args += [par, s0]
    return pl.pallas_call(
        functools.partial(_hg_kernel, reverse=reverse, final=final),
        out_shape=(jax.ShapeDtypeStruct((t, nb, d), BF16 if final else F32),
                   jax.ShapeDtypeStruct(s0.shape, F32)),
        grid=(heads, nj),
        in_specs=in_specs,
        out_specs=(hspec, sspec),
        scratch_shapes=scratch,
        compiler_params=_cparams(("parallel", "arbitrary")),
        name="hg_bwd" if reverse else "hg_fwd",
    )(*args)


def _merge_kernel(yrg_ref, yhg_ref, ga_ref, gb_ref, x_ref, mod_ref, g_ref, wr_ref, wh_ref, wo_ref,
                  o_ref, *, nb, d):
    tm = x_ref.shape[0]
    p_rg = jnp.dot(yrg_ref[...], wr_ref[...], preferred_element_type=F32)
    p_hg = jnp.dot(yhg_ref[...].reshape(tm, d), wh_ref[...], preferred_element_type=F32)
    ga = ga_ref[...].reshape(tm, d).astype(F32)
    gb = gb_ref[...].reshape(tm, d).astype(F32)
    m = _sigmoid(ga) * p_rg + _sigmoid(gb) * p_hg
    y = jnp.dot(m.astype(BF16), wo_ref[...], preferred_element_type=F32)
    n3, gate = _per_batch(_rms(y, g_ref[...]), mod_ref[:, 2 * d:3 * d], nb)
    o_ref[...] = x_ref[...] + (gate * n3).reshape(tm, d)


def _merge(yrg, yhg, zb, tile_spec, x, mod, g, wr, wh, wo, *, tm, nb):
    n, d = x.shape
    row = lambda i: (i, 0)
    const = lambda i: (0, 0)
    return pl.pallas_call(
        functools.partial(_merge_kernel, nb=nb, d=d),
        out_shape=jax.ShapeDtypeStruct((n, d), F32),
        grid=(n // tm,),
        in_specs=[pl.BlockSpec((tm, d), row), tile_spec(0), tile_spec(2), tile_spec(3),
                  pl.BlockSpec((tm, d), row), pl.BlockSpec(mod.shape, const),
                  pl.BlockSpec((1, d), const), pl.BlockSpec((d, d), const),
                  pl.BlockSpec((d, d), const), pl.BlockSpec((d, d), const)],
        out_specs=pl.BlockSpec((tm, d), row),
        compiler_params=_cparams(("parallel",)),
        name="merge",
    )(yrg, yhg, zb, zb, x, mod, g, wr, wh, wo)


def _ffn_kernel(*refs, nb, d, dff, tf, halo, nt):
    if halo:
        (x_ref, xp_ref, xn_ref, mod_ref, g1_ref, g2_ref, wu_ref, cw_ref, cb_ref, wd_ref,
         o_ref, act_scr) = refs
    else:
        (x_ref, mod_ref, g1_ref, g2_ref, wu_ref, cw_ref, cb_ref, wd_ref, o_ref, act_scr) = refs
    i = pl.program_id(0)
    tm = x_ref.shape[0]
    shift = mod_ref[:, 3 * d:4 * d]
    scale = mod_ref[:, 4 * d:5 * d]

    hm = _norm_mod(x_ref[...], g1_ref[...], shift, scale, nb).astype(BF16)
    if halo:
        hp = _norm_mod(xp_ref[...], g1_ref[...], shift, scale, nb)
        hn = _norm_mod(xn_ref[...], g1_ref[...], shift, scale, nb)
        hp = jnp.where(i > 0, hp, 0.0).astype(BF16)
        hn = jnp.where(i < nt - 1, hn, 0.0).astype(BF16)
        he = jnp.concatenate([hp, hm, hn], axis=0)

    for kb in range(dff // tf):
        gs = slice(kb * tf, (kb + 1) * tf)
        vs = slice(dff + kb * tf, dff + (kb + 1) * tf)
        if halo:
            ue = jnp.dot(he, wu_ref[:, gs], preferred_element_type=F32)
        else:
            u = jnp.dot(hm, wu_ref[:, gs], preferred_element_type=F32)
            z = jnp.zeros((nb, tf), F32)
            ue = jnp.concatenate([z, u, z], axis=0)
        gc = cb_ref[:, gs] + ue[0:tm] * cw_ref[0:1, gs] + ue[nb:nb + tm] * cw_ref[1:2, gs] \
            + ue[2 * nb:2 * nb + tm] * cw_ref[2:3, gs]
        uv = jnp.dot(hm, wu_ref[:, vs], preferred_element_type=F32)
        act_scr[:, gs] = (_silu(gc) * uv).astype(BF16)

    y = jnp.dot(act_scr[...], wd_ref[...], preferred_element_type=F32)
    n3, gate = _per_batch(_rms(y, g2_ref[...]), mod_ref[:, 5 * d:6 * d], nb)
    o_ref[...] = x_ref[...] + (gate * n3).reshape(tm, d)


def _ffn(x, mod, g1, g2, w_up, cw, cb, w_down, *, tm, tf, nb, halo):
    n, d = x.shape
    dff = w_down.shape[0]
    nt = n // tm
    hb = tm // nb
    row = lambda i: (i, 0)
    const = lambda i: (0, 0)

    def resident(shape):
        return pl.BlockSpec(shape, const, pipeline_mode=pl.Buffered(1))

    in_specs = [pl.BlockSpec((tm, d), row)]
    args = [x]
    if halo:
        in_specs += [pl.BlockSpec((nb, d), lambda i: (jnp.maximum(i * hb - 1, 0), 0)),
                     pl.BlockSpec((nb, d), lambda i: (jnp.minimum((i + 1) * hb, n // nb - 1), 0))]
        args += [x, x]
    in_specs += [pl.BlockSpec(mod.shape, const), pl.BlockSpec((1, d), const),
                 pl.BlockSpec((1, d), const), resident(w_up.shape),
                 pl.BlockSpec(cw.shape, const), pl.BlockSpec(cb.shape, const),
                 resident(w_down.shape)]
    args += [mod, g1, g2, w_up, cw, cb, w_down]
    return pl.pallas_call(
        functools.partial(_ffn_kernel, nb=nb, d=d, dff=dff, tf=tf, halo=halo, nt=nt),
        out_shape=jax.ShapeDtypeStruct((n, d), F32),
        grid=(nt,),
        in_specs=in_specs,
        out_specs=pl.BlockSpec((tm, d), row),
        scratch_shapes=[pltpu.VMEM((tm, dff), BF16)],
        compiler_params=_cparams(("parallel",)),
        name="ffn_ctx" if halo else "ffn",
    )(*args)


def _gate_weights(wa, wx):
    nd, heads, hd, _ = wa.shape
    per = RG_GATE // hd
    ncb = heads // per

    def bd(w):
        w = w.reshape(nd, ncb, per, hd, hd)
        eye = jnp.eye(per, dtype=w.dtype)
        return jnp.einsum('dcpij,pq->dcpiqj', w, eye).reshape(nd, ncb, RG_GATE, RG_GATE)

    return (0.5 * jnp.concatenate([bd(wa), bd(wx)], axis=-1)).astype(BF16)


def _forward(x, c, ctx, c_ctx, w_ada, b_ada, g_pre_mix, g_post_mix, g_pre_ffn, g_post_ffn, w_in,
             rg_conv_w, rg_conv_b, rg_wa, rg_ba, rg_wx, rg_bx, rg_lam, hg_lb_logits, hg_out_norm,
             w_proj_rg, w_proj_hg, w_out, ffn_w_up, ffn_conv_w, ffn_conv_b, ffn_w_down, *, grid_w):
    nb, seq, d = x.shape
    ctx_len = ctx.shape[1]
    depth = w_in.shape[0]
    rows_g = seq // grid_w
    dff = ffn_w_down.shape[1]
    tile = grid_w * nb
    heads = d // HG_EXPAND
    ncb = d // RG_COLS
    tf = 256 if dff % 256 == 0 else dff

    p = jax.nn.softmax(hg_lb_logits.astype(F32), axis=0)
    cum = jnp.cumsum(p, axis=0)
    lb_all = cum - cum[0:1]
    hg_par = jnp.stack([lb_all, 1.0 - lb_all, jnp.zeros_like(lb_all), hg_out_norm], axis=1)
    hg_par = jnp.pad(hg_par, ((0, 0), (0, 4), (0, 0)))

    w_ada_b, w_in_b = w_ada.astype(BF16), w_in.astype(BF16)
    wr_b, wh_b, wo_b = w_proj_rg.astype(BF16), w_proj_hg.astype(BF16), w_out.astype(BF16)
    wup_b, wdn_b = ffn_w_up.astype(BF16), ffn_w_down.astype(BF16)

    cc = jnp.concatenate([c, jnp.broadcast_to(c_ctx[None], (nb, d))], axis=0)
    mod_all = _ada_mod(cc, w_ada_b, b_ada).reshape(depth, 2, nb, N_MOD * d)

    xl = jnp.swapaxes(x, 0, 1).reshape(seq * nb, d)
    xc = jnp.swapaxes(ctx, 0, 1).reshape(ctx_len * nb, d)

    for l in range(depth):
        need_ctx = l < depth - 1
        mod_l, mod_c = mod_all[l, 0], mod_all[l, 1]
        gpm = g_pre_mix[l].reshape(1, d)
        gpo = g_post_mix[l].reshape(1, d)
        wg = _gate_weights(rg_wa[l], rg_wx[l])
        bg = 0.5 * jnp.concatenate([rg_ba[l].reshape(2, d // RG_GATE, 1, RG_GATE),
                                    rg_bx[l].reshape(2, d // RG_GATE, 1, RG_GATE)], axis=-1)
        rg_kw = dict(conv_w=rg_conv_w[l], conv_b=rg_conv_b[l].reshape(1, d), nb=nb, tt=grid_w)

        st_rg = [jnp.zeros((ncb, nb, RG_COLS), F32)] * 2
        st_hg = [jnp.zeros((nb, heads, HG_EXPAND, HG_EXPAND), F32)] * 2
        ctx_out = None
        for xs, mod_s, steps_s, latent in ((xc, mod_c, ctx_len, False), (xl, mod_l, seq, True)):
            if latent:
                shape4 = jax.ShapeDtypeStruct((grid_w, rows_g, nb, 4 * d), BF16)
                blk = lambda w, col: pl.BlockSpec((grid_w, None, nb, w), col)
                spec_of = lambda tile_of: blk(GROUPS_PER_STEP * d, lambda i, j: (0, tile_of(i), 0, j))
                gg_of = lambda tile_of: blk(RG_COLS, lambda i, j: (0, tile_of(i), 0, j))
                tile_spec = lambda gidx: blk(d, lambda i: (0, i, 0, gidx))
            else:
                shape4 = jax.ShapeDtypeStruct((steps_s, nb, 4 * d), BF16)
                blk = lambda w, col: pl.BlockSpec((grid_w, nb, w), col)
                spec_of = lambda tile_of: blk(GROUPS_PER_STEP * d, lambda i, j: (tile_of(i), 0, j))
                gg_of = lambda tile_of: blk(RG_COLS, lambda i, j: (tile_of(i), 0, j))
                tile_spec = lambda gidx: blk(d, lambda i: (i, 0, gidx))
            zx, hx = _inproj(xs, mod_s, gpm, w_in_b[l], 0, tile, nb)
            za, hf, st_rg[0] = _inproj_rg(
                hx, w_in_b[l], (1, 2, 3, 4), shape4, spec_of, zx, None, None, None,
                wg=wg[0], bg=bg[0], lam=rg_lam[l, 0:1], h0=st_rg[0], reverse=False, **rg_kw)
            zb, yrg, st_rg[1] = _inproj_rg(
                hx, w_in_b[l], (5, 6, 7, 8), shape4, spec_of, zx, hf, za, gg_of,
                wg=wg[1], bg=bg[1], lam=rg_lam[l, 1:2], h0=st_rg[1], reverse=True, **rg_kw)
            za3 = za.reshape(steps_s, nb, 4 * d)
            zb3 = zb.reshape(steps_s, nb, 4 * d)
            of, st_hg[0] = _hg_dir(za3, zb3, None, hg_par[l], st_hg[0], reverse=False, d=d)
            yhg, st_hg[1] = _hg_dir(za3, zb3, of, hg_par[l], st_hg[1], reverse=True, d=d)
            if latent or need_ctx:
                yhg = yhg.reshape(shape4.shape[:-1] + (d,))
                out = _merge(yrg, yhg, zb, tile_spec, xs, mod_s, gpo, wr_b[l], wh_b[l], wo_b[l],
                             tm=tile, nb=nb)
                if latent:
                    xl = out
                else:
                    ctx_out = out
        ffn = functools.partial(_ffn, g1=g_pre_ffn[l].reshape(1, d), g2=g_post_ffn[l].reshape(1, d),
                                w_up=wup_b[l], cw=ffn_conv_w[l], cb=ffn_conv_b[l].reshape(1, dff),
                                w_down=wdn_b[l], tm=tile, tf=tf, nb=nb)
        xl = ffn(xl, mod_l, halo=False)
        if need_ctx:
            xc = ffn(ctx_out, mod_c, halo=True)

    return jnp.swapaxes(xl.reshape(seq, nb, d), 0, 1)


def kernel(x, c, ctx, c_ctx, w_ada, b_ada, g_pre_mix, g_post_mix, g_pre_ffn, g_post_ffn, w_in, rg_conv_w, rg_conv_b, rg_wa, rg_ba, rg_wx, rg_bx, rg_lam, hg_lb_logits, hg_out_norm, w_proj_rg, w_proj_hg, w_out, ffn_w_up, ffn_conv_w, ffn_conv_b, ffn_w_down):
    return _forward(x, c, ctx, c_ctx, w_ada, b_ada, g_pre_mix, g_post_mix, g_pre_ffn, g_post_ffn,
                    w_in, rg_conv_w, rg_conv_b, rg_wa, rg_ba, rg_wx, rg_bx, rg_lam, hg_lb_logits,
                    hg_out_norm, w_proj_rg, w_proj_hg, w_out, ffn_w_up, ffn_conv_w, ffn_conv_b,
                    ffn_w_down, grid_w=GRID_W)
```

```python
import functools

import jax
import jax.numpy as jnp
from jax import lax
from jax.experimental import pallas as pl
from jax.experimental.pallas import tpu as pltpu

GRID_W = 64
RG_HEADS = 16
RG_C = 8.0
HG_EXPAND = 128
N_MOD = 6
EPS = 1e-6
RG_CONV_PAD_L = 1
RG_CONV_PAD_R = 2

HG_CHUNK = 128
HG_DIAG = 8
HG_DIAG_CLAMP = 20.0
HG_F_FLOOR = 1e-37
HG_PITCH = 24
RG_TINY = 1e-30
MM_COLS = 256
RG_GATE = 256
RG_COLS = 512
GROUPS_PER_STEP = 2
VMEM_LIMIT = 56 * 1024 * 1024

LOG2_E = 1.4426950408889634

F32 = jnp.float32
BF16 = jnp.bfloat16


def _cparams(sem):
    return pltpu.CompilerParams(dimension_semantics=sem, vmem_limit_bytes=VMEM_LIMIT)


def _sigmoid(x):
    return 0.5 + 0.5 * jnp.tanh(0.5 * x)


def _silu(x):
    return x * _sigmoid(x)


def _gelu_tanh(x):
    c = 0.7978845608028654
    return 0.5 * x * (1.0 + jnp.tanh(c * (x + 0.044715 * (x * x * x))))


def _rms(x, g):
    ms = jnp.mean(x * x, axis=-1, keepdims=True)
    return x * lax.rsqrt(ms + EPS) * g


def _per_batch(x, vec, nb):
    r, d = x.shape
    return x.reshape(r // nb, nb, d), vec[None]


def _norm_mod(x, g, shift, scale, nb):
    y = _rms(x, g)
    y3, sc = _per_batch(y, scale, nb)
    h = y3 * (1.0 + sc) + shift[None]
    return h.reshape(x.shape)


def _ada_kernel(c_ref, w_ref, b_ref, o_ref):
    s = _silu(c_ref[...]).astype(BF16)
    o_ref[0] = jnp.dot(s, w_ref[0], preferred_element_type=F32) + b_ref[0]


def _ada_mod(cc, w_ada, b_ada):
    depth, d, n = w_ada.shape
    tn = 1536 if n % 1536 == 0 else n
    return pl.pallas_call(
        _ada_kernel,
        out_shape=jax.ShapeDtypeStruct((depth, cc.shape[0], n), F32),
        grid=(depth, n // tn),
        in_specs=[pl.BlockSpec(cc.shape, lambda l, j: (0, 0)),
                  pl.BlockSpec((1, d, tn), lambda l, j: (l, 0, j)),
                  pl.BlockSpec((1, 1, tn), lambda l, j: (l, 0, j))],
        out_specs=pl.BlockSpec((1, cc.shape[0], tn), lambda l, j: (l, 0, j)),
        compiler_params=_cparams(("parallel", "parallel")),
        name="ada_mod",
    )(cc, w_ada, b_ada.reshape(depth, 1, n))


def _zero_from(v):
    u = lax.bitcast_convert_type(v, jnp.uint32)
    return lax.bitcast_convert_type((u >> 16) >> 16, F32)


def _rg_chunk(xe, bias, cw_ref, wg_ref, bg_ref, lam_ref, h, *, reverse, nb, steps, cols):
    rows = steps * nb
    xl = bias + xe[0:rows] * cw_ref[0:1, :]
    for j in range(1, 4):
        xl = xl + xe[j * nb:j * nb + rows] * cw_ref[j:j + 1, :]
    xb = xl.astype(BF16)
    tha, thx = [], []
    for s in range(cols // RG_GATE):
        th = jnp.tanh(jnp.dot(xb[:, s * RG_GATE:(s + 1) * RG_GATE], wg_ref[s],
                              preferred_element_type=F32) + bg_ref[s])
        tha.append(th[:, :RG_GATE])
        thx.append(th[:, RG_GATE:])
    half_rate = (-0.5 * RG_C) * jax.nn.softplus(-lam_ref[...])
    log_a = half_rate + half_rate * jnp.concatenate(tha, axis=1)
    ig = 0.5 + 0.5 * jnp.concatenate(thx, axis=1)
    a = jnp.exp(log_a)
    om = (1.0 + a * a) * jnp.tanh(-log_a)
    b = (om * lax.rsqrt(jnp.maximum(om, RG_TINY))) * (ig * xl)
    hs = [None] * steps
    for t in (range(steps - 1, -1, -1) if reverse else range(steps)):
        h = a[t * nb:(t + 1) * nb] * h + b[t * nb:(t + 1) * nb]
        hs[t] = h
    return h, jnp.concatenate(hs, axis=0)


def _inproj_kernel(*refs, nb, d, rg, reverse, tt, nt):
    if not rg:
        x_ref, mod_ref, g_ref, w_ref, o_ref, h_ref = refs
        h = _norm_mod(x_ref[...], g_ref[...], mod_ref[:, 0:d], mod_ref[:, d:2 * d], nb).astype(BF16)
        h_ref[...] = h
        res = jnp.dot(h, w_ref[...], preferred_element_type=F32)
        o_ref[...] = res.reshape(o_ref.shape).astype(o_ref.dtype)
        return
    gps = GROUPS_PER_STEP
    h_ref, w_refs, rest = refs[0], refs[1:1 + gps], refs[1 + gps:]
    if reverse:
        (zx_ref, zxp_ref, zxn_ref, cw_ref, cb_ref, wg_ref, bg_ref,
         lam_ref, h0_ref, hf_ref, gg_ref, o_ref, y_ref, hfin_ref, hst_scr) = rest
    else:
        (zx_ref, zxp_ref, zxn_ref, cw_ref, cb_ref, wg_ref, bg_ref,
         lam_ref, h0_ref, o_ref, y_ref, hfin_ref, hst_scr) = rest
    i = pl.program_id(0)
    j = pl.program_id(1)

    @pl.when(i == 0)
    def _():
        hst_scr[j] = h0_ref[j]

    tile = (nt - 1 - i) if reverse else i
    cols = y_ref.shape[1]
    ppg = d // MM_COLS
    npiece = gps * ppg
    steps = tt // npiece
    sub = steps * nb
    oshape = o_ref.shape[:-1] + (MM_COLS,)
    prev = jnp.where(tile > 0, zxp_ref[...].astype(F32), 0.0)
    nxt = jnp.where(tile < nt - 1, zxn_ref[...].astype(F32), 0.0)
    h = hst_scr[j]
    bias = cb_ref[...]
    for n in range(npiece):
        w_ref = w_refs[n // ppg]
        res = jnp.dot(h_ref[...], w_ref[:, (n % ppg) * MM_COLS:(n % ppg + 1) * MM_COLS],
                      preferred_element_type=F32)
        o_ref[..., n * MM_COLS:(n + 1) * MM_COLS] = res.reshape(oshape).astype(o_ref.dtype)
        c = (npiece - 1 - n) if reverse else n
        r0 = c * sub
        lo = prev if c == 0 else zx_ref[r0 - nb:r0, :].astype(F32)
        hi = nxt if c == npiece - 1 else zx_ref[r0 + sub:r0 + sub + 2 * nb, :].astype(F32)
        xe = jnp.concatenate([lo, zx_ref[r0:r0 + sub, :].astype(F32), hi], axis=0)
        h, hs = _rg_chunk(xe, bias, cw_ref, wg_ref, bg_ref, lam_ref, h, reverse=reverse, nb=nb,
                          steps=steps, cols=cols)
        if reverse:
            gate = _gelu_tanh(gg_ref[...].reshape(tt * nb, cols)[r0:r0 + sub, :].astype(F32))
            y_ref[r0:r0 + sub, :] = (gate * (hf_ref[r0:r0 + sub, :] + hs)).astype(y_ref.dtype)
        else:
            y_ref[r0:r0 + sub, :] = hs
        bias = cb_ref[...] + _zero_from(res[248:256, MM_COLS - 128:MM_COLS])[0:1, 0:1]
    hst_scr[j] = h
    hfin_ref[j] = h


def _group_map(col_groups):
    def wmap(i, j):
        idx = jnp.int32(col_groups[0])
        for k in range(1, len(col_groups)):
            idx = jnp.where(j == k, jnp.int32(col_groups[k]), idx)
        return (0, idx)
    return wmap


def _inproj(x, mod, g, w, group, tm, nb):
    n, d = x.shape
    row = lambda i: (i, 0)
    return pl.pallas_call(
        functools.partial(_inproj_kernel, nb=nb, d=d, rg=False, reverse=False, tt=0, nt=0),
        out_shape=(jax.ShapeDtypeStruct((n, d), BF16), jax.ShapeDtypeStruct((n, d), BF16)),
        grid=(n // tm,),
        in_specs=[pl.BlockSpec((tm, d), row),
                  pl.BlockSpec(mod.shape, lambda i: (0, 0)),
                  pl.BlockSpec((1, d), lambda i: (0, 0)),
                  pl.BlockSpec((d, d), lambda i: (0, group))],
        out_specs=(pl.BlockSpec((tm, d), row), pl.BlockSpec((tm, d), row)),
        compiler_params=_cparams(("parallel",)),
        name="inproj",
    )(x, mod, g, w)


def _inproj_rg(hx, w, col_groups, out_shape, out_spec_of, zx, hf, zgg, gg_spec_of, conv_w,
               conv_b, wg, bg, lam, h0, *, reverse, nb, tt):
    n, d = hx.shape
    rows = tt * nb
    nt = n // rows
    ncb = d // RG_COLS
    gps = GROUPS_PER_STEP
    nsb = RG_COLS // RG_GATE
    assert len(col_groups) == ncb * gps and tt % (gps * d // MM_COLS) == 0
    ppb = rows // nb
    npb = rows // (2 * nb)
    n_next_blocks = n // (2 * nb)

    def tile_of(i):
        return (nt - 1 - i) if reverse else i

    in_specs = [pl.BlockSpec((rows, d), lambda i, j: (tile_of(i), 0))]
    in_specs += [pl.BlockSpec((d, d), _group_map(col_groups[g::gps])) for g in range(gps)]
    in_specs += [
        pl.BlockSpec((rows, RG_COLS), lambda i, j: (tile_of(i), j)),
        pl.BlockSpec((nb, RG_COLS), lambda i, j: (jnp.maximum(tile_of(i) * ppb - 1, 0), j)),
        pl.BlockSpec((2 * nb, RG_COLS),
                     lambda i, j: (jnp.minimum((tile_of(i) + 1) * npb, n_next_blocks - 1), j)),
        pl.BlockSpec((4, RG_COLS), lambda i, j: (0, j)),
        pl.BlockSpec((1, RG_COLS), lambda i, j: (0, j)),
        pl.BlockSpec((nsb, RG_GATE, 2 * RG_GATE), lambda i, j: (j, 0, 0)),
        pl.BlockSpec((nsb, 1, 2 * RG_GATE), lambda i, j: (j, 0, 0)),
        pl.BlockSpec((1, RG_COLS), lambda i, j: (0, j)),
        pl.BlockSpec((ncb, nb, RG_COLS), lambda i, j: (0, 0, 0)),
    ]
    args = [hx] + [w] * gps + [zx, zx, zx, conv_w, conv_b, wg, bg, lam, h0]
    if reverse:
        in_specs += [pl.BlockSpec((rows, RG_COLS), lambda i, j: (tile_of(i), j)), gg_spec_of(tile_of)]
        args += [hf, zgg]
    return pl.pallas_call(
        functools.partial(_inproj_kernel, nb=nb, d=d, rg=True, reverse=reverse, tt=tt, nt=nt),
        out_shape=(out_shape, jax.ShapeDtypeStruct((n, d), BF16 if reverse else F32),
                   jax.ShapeDtypeStruct((ncb, nb, RG_COLS), F32)),
        grid=(nt, ncb),
        in_specs=in_specs,
        out_specs=(out_spec_of(tile_of),
                   pl.BlockSpec((rows, RG_COLS), lambda i, j: (tile_of(i), j)),
                   pl.BlockSpec((ncb, nb, RG_COLS), lambda i, j: (0, 0, 0))),
        scratch_shapes=[pltpu.VMEM((ncb, nb, RG_COLS), F32)],
        compiler_params=_cparams(("arbitrary", "arbitrary")),
        name="inproj_rg_bwd" if reverse else "inproj_rg_fwd",
    )(*args)


def _cumsum_time(x, reverse, block):
    c = x.shape[0]
    x4 = x.reshape((c // block, block) + x.shape[1:])
    cols = [None] * block
    order = range(block - 1, -1, -1) if reverse else range(block)
    run = None
    for i in order:
        run = x4[:, i] if run is None else run + x4[:, i]
        cols[i] = run
    return jnp.stack(cols, axis=1).reshape(x.shape)


def _cumsum_chunk(x, reverse):
    c = x.shape[0]
    blk = 8
    x4 = _cumsum_time(x, reverse, blk).reshape((c // blk, blk) + x.shape[1:])
    nblk = c // blk
    tot = x4[:, 0] if reverse else x4[:, blk - 1]
    offs = [None] * nblk
    order = range(nblk - 1, -1, -1) if reverse else range(nblk)
    run = None
    for i in order:
        offs[i] = run
        run = tot[i] if run is None else run + tot[i]
    first = nblk - 1 if reverse else 0
    parts = [x4[i] if i == first else x4[i] + offs[i][None] for i in range(nblk)]
    return jnp.stack(parts, axis=0).reshape(x.shape)


def _pivot_time(c, block, p):
    c4 = c.reshape((c.shape[0] // block, block) + c.shape[1:])
    return jnp.broadcast_to(c4[:, p:p + 1], c4.shape).reshape(c.shape)


def _hg_levels(c):
    out, m = [], c // 2
    while m >= HG_DIAG:
        out.append(m)
        m //= 2
    return out


def _hg_kernel(*refs, reverse, final):
    if final:
        (zq_ref, zf_ref, v_ref, of_ref, og_ref, par_ref, s0_ref, y_ref, sout_ref,
         st_scr, tr_scr, o_scr, mm_scr) = refs
    else:
        (zq_ref, zf_ref, v_ref, par_ref, s0_ref, y_ref, sout_ref, st_scr, tr_scr, o_scr,
         mm_scr) = refs
    j = pl.program_id(1)
    nj = pl.num_programs(1)
    c, nb, kd = zq_ref.shape

    @pl.when(j == 0)
    def _():
        st_scr[...] = s0_ref[...]

    zq = zq_ref[...].astype(F32)
    zf = zf_ref[...].astype(F32)
    lb = par_ref[0:1, :][None]
    oml = par_ref[1:2, :][None]

    q = _silu(zq)
    e = jnp.exp(-jnp.abs(zf))
    r = 1.0 / (1.0 + e)
    pos = zf >= 0.0
    f = lb + oml * (jnp.where(pos, 1.0, e) * r)
    k = oml * (jnp.where(pos, e, 1.0) * r)
    g = jnp.log2(jnp.maximum(f, HG_F_FLOOR))
    cs = _cumsum_chunk(g, reverse)
    cd = _cumsum_time(jnp.maximum(g, -HG_DIAG_CLAMP * LOG2_E), reverse, HG_DIAG)
    ctot = cs[0] if reverse else cs[c - 1]
    decay = jnp.exp2(ctot)

    pad = jnp.zeros((c, HG_PITCH - nb, kd), F32)
    for n, arr in enumerate((q, k, cs, cd, v_ref[...].astype(F32))):
        tr_scr[n] = jnp.concatenate([arr, pad], axis=1).reshape(c * HG_PITCH, kd)

    ti = lax.broadcasted_iota(jnp.int32, (c, c), 0)
    si = lax.broadcasted_iota(jnp.int32, (c, c), 1)
    lev = ti ^ si
    valid = (ti <= si) if reverse else (ti >= si)
    nt_dims = (((1,), (1,)), ((), ()))
    tn_dims = (((0,), (0,)), ((), ()))
    levels = _hg_levels(c)

    for b in range(nb):
        rows = pl.ds(b, c, stride=HG_PITCH)
        q2, k2, cs2, cd2 = (tr_scr[n, rows, :] for n in range(4))
        qb = q2.astype(BF16)
        kb = k2.astype(BF16)
        mm_scr[0, b] = qb * jnp.exp2(cs2).astype(BF16)
        mm_scr[1, b] = kb * jnp.exp2(ctot[b:b + 1, :] - cs2).astype(BF16)
        mm_scr[2, b] = tr_scr[4, rows, :].astype(BF16)
        cdd = cd2 - _pivot_time(cd2, HG_DIAG, HG_DIAG // 2 if reverse else HG_DIAG // 2 - 1)
        a = lax.dot_general(qb * jnp.exp2(cdd).astype(BF16), kb * jnp.exp2(-cdd).astype(BF16),
                            nt_dims, preferred_element_type=F32)
        for m in levels[::-1]:
            piv = m if reverse else m - 1
            el = jnp.exp2(-jnp.abs((cs2 - _pivot_time(cs2, 2 * m, piv)).astype(BF16)))
            p = lax.dot_general(qb * el, kb * el, nt_dims, preferred_element_type=F32)
            a = jnp.where(lev < m, a, p)
        mm_scr[3, b] = jnp.where(valid, a, 0.0).astype(BF16)

    decay_t = jnp.transpose(decay)
    for b in range(nb):
        st = st_scr[b]
        vb = mm_scr[2, b]
        o = jnp.dot(mm_scr[0, b], st.astype(BF16), preferred_element_type=F32)
        o = o + jnp.dot(mm_scr[3, b], vb, preferred_element_type=F32)
        st_scr[b] = st * decay_t[:, b:b + 1] + lax.dot_general(mm_scr[1, b], vb, tn_dims,
                                                               preferred_element_type=F32)
        o_scr[pl.ds(b, c, stride=HG_PITCH), :] = o

    o = o_scr[...].reshape(c, HG_PITCH, kd)[:, 0:nb, :]
    if final:
        gain = par_ref[3:4, :][None]
        y = _rms(o + of_ref[...], gain) * _silu(og_ref[...].astype(F32))
        y_ref[...] = y.astype(y_ref.dtype)
    else:
        y_ref[...] = o

    @pl.when(j == nj - 1)
    def _():
        sout_ref[...] = st_scr[...]


def _hg_dir(za, zb, of, par, s0, *, reverse, d):
    t, nb, _ = za.shape
    kd = HG_EXPAND
    heads = d // kd
    nj = t // HG_CHUNK
    final = reverse
    assert nb <= HG_PITCH

    def blk(j):
        return (nj - 1 - j) if reverse else j

    def zspec(gidx):
        return pl.BlockSpec((HG_CHUNK, nb, kd), lambda h, j: (blk(j), 0, gidx * heads + h))

    hspec = pl.BlockSpec((HG_CHUNK, nb, kd), lambda h, j: (blk(j), 0, h))
    sspec = pl.BlockSpec((nb, None, kd, kd), lambda h, j: (0, h, 0, 0))
    in_specs = [zspec(1), zspec(3 if reverse else 2), zspec(0)]
    args = [za, za, zb]
    scratch = [pltpu.VMEM((nb, kd, kd), F32), pltpu.VMEM((5, HG_CHUNK * HG_PITCH, kd), F32),
               pltpu.VMEM((HG_CHUNK * HG_PITCH, kd), F32),
               pltpu.VMEM((4, nb, HG_CHUNK, kd), BF16)]
    if final:
        in_specs += [hspec, zspec(1)]
        args += [of, zb]
    in_specs += [pl.BlockSpec((par.shape[0], kd), lambda h, j: (0, h)), sspec]
    args += [par, s0]
    return pl.pallas_call(
        functools.partial(_hg_kernel, reverse=reverse, final=final),
        out_shape=(jax.ShapeDtypeStruct((t, nb, d), BF16 if final else F32),
                   jax.ShapeDtypeStruct(s0.shape, F32)),
        grid=(heads, nj),
        in_specs=in_specs,
        out_specs=(hspec, sspec),
        scratch_shapes=scratch,
        compiler_params=_cparams(("parallel", "arbitrary")),
        name="hg_bwd" if reverse else "hg_fwd",
    )(*args)


def _merge_kernel(yrg_ref, yhg_ref, ga_ref, gb_ref, x_ref, mod_ref, g_ref, wr_ref, wh_ref, wo_ref,
                  o_ref, *, nb, d):
    tm = x_ref.shape[0]
    p_rg = jnp.dot(yrg_ref[...], wr_ref[...], preferred_element_type=F32)
    p_hg = jnp.dot(yhg_ref[...].reshape(tm, d), wh_ref[...], preferred_element_type=F32)
    ga = ga_ref[...].reshape(tm, d).astype(F32)
    gb = gb_ref[...].reshape(tm, d).astype(F32)
    m = _sigmoid(ga) * p_rg + _sigmoid(gb) * p_hg
    y = jnp.dot(m.astype(BF16), wo_ref[...], preferred_element_type=F32)
    n3, gate = _per_batch(_rms(y, g_ref[...]), mod_ref[:, 2 * d:3 * d], nb)
    o_ref[...] = x_ref[...] + (gate * n3).reshape(tm, d)


def _merge(yrg, yhg, zb, tile_spec, x, mod, g, wr, wh, wo, *, tm, nb):
    n, d = x.shape
    row = lambda i: (i, 0)
    const = lambda i: (0, 0)
    return pl.pallas_call(
        functools.partial(_merge_kernel, nb=nb, d=d),
        out_shape=jax.ShapeDtypeStruct((n, d), F32),
        grid=(n // tm,),
        in_specs=[pl.BlockSpec((tm, d), row), tile_spec(0), tile_spec(2), tile_spec(3),
                  pl.BlockSpec((tm, d), row), pl.BlockSpec(mod.shape, const),
                  pl.BlockSpec((1, d), const), pl.BlockSpec((d, d), const),
                  pl.BlockSpec((d, d), const), pl.BlockSpec((d, d), const)],
        out_specs=pl.BlockSpec((tm, d), row),
        compiler_params=_cparams(("parallel",)),
        name="merge",
    )(yrg, yhg, zb, zb, x, mod, g, wr, wh, wo)


def _ffn_kernel(*refs, nb, d, dff, tf, halo, nt):
    if halo:
        (x_ref, xp_ref, xn_ref, mod_ref, g1_ref, g2_ref, wu_ref, cw_ref, cb_ref, wd_ref,
         o_ref, act_scr) = refs
    else:
        (x_ref, mod_ref, g1_ref, g2_ref, wu_ref, cw_ref, cb_ref, wd_ref, o_ref, act_scr) = refs
    i = pl.program_id(0)
    tm = x_ref.shape[0]
    shift = mod_ref[:, 3 * d:4 * d]
    scale = mod_ref[:, 4 * d:5 * d]

    hm = _norm_mod(x_ref[...], g1_ref[...], shift, scale, nb).astype(BF16)
    if halo:
        hp = _norm_mod(xp_ref[...], g1_ref[...], shift, scale, nb)
        hn = _norm_mod(xn_ref[...], g1_ref[...], shift, scale, nb)
        hp = jnp.where(i > 0, hp, 0.0).astype(BF16)
        hn = jnp.where(i < nt - 1, hn, 0.0).astype(BF16)
        he = jnp.concatenate([hp, hm, hn], axis=0)

    for kb in range(dff // tf):
        gs = slice(kb * tf, (kb + 1) * tf)
        vs = slice(dff + kb * tf, dff + (kb + 1) * tf)
        if halo:
            ue = jnp.dot(he, wu_ref[:, gs], preferred_element_type=F32)
        else:
            u = jnp.dot(hm, wu_ref[:, gs], preferred_element_type=F32)
            z = jnp.zeros((nb, tf), F32)
            ue = jnp.concatenate([z, u, z], axis=0)
        gc = cb_ref[:, gs] + ue[0:tm] * cw_ref[0:1, gs] + ue[nb:nb + tm] * cw_ref[1:2, gs] \
            + ue[2 * nb:2 * nb + tm] * cw_ref[2:3, gs]
        uv = jnp.dot(hm, wu_ref[:, vs], preferred_element_type=F32)
        act_scr[:, gs] = (_silu(gc) * uv).astype(BF16)

    y = jnp.dot(act_scr[...], wd_ref[...], preferred_element_type=F32)
    n3, gate = _per_batch(_rms(y, g2_ref[...]), mod_ref[:, 5 * d:6 * d], nb)
    o_ref[...] = x_ref[...] + (gate * n3).reshape(tm, d)


def _ffn(x, mod, g1, g2, w_up, cw, cb, w_down, *, tm, tf, nb, halo):
    n, d = x.shape
    dff = w_down.shape[0]
    nt = n // tm
    hb = tm // nb
    row = lambda i: (i, 0)
    const = lambda i: (0, 0)

    def resident(shape):
        return pl.BlockSpec(shape, const, pipeline_mode=pl.Buffered(1))

    in_specs = [pl.BlockSpec((tm, d), row)]
    args = [x]
    if halo:
        in_specs += [pl.BlockSpec((nb, d), lambda i: (jnp.maximum(i * hb - 1, 0), 0)),
                     pl.BlockSpec((nb, d), lambda i: (jnp.minimum((i + 1) * hb, n // nb - 1), 0))]
        args += [x, x]
    in_specs += [pl.BlockSpec(mod.shape, const), pl.BlockSpec((1, d), const),
                 pl.BlockSpec((1, d), const), resident(w_up.shape),
                 pl.BlockSpec(cw.shape, const), pl.BlockSpec(cb.shape, const),
                 resident(w_down.shape)]
    args += [mod, g1, g2, w_up, cw, cb, w_down]
    return pl.pallas_call(
        functools.partial(_ffn_kernel, nb=nb, d=d, dff=dff, tf=tf, halo=halo, nt=nt),
        out_shape=jax.ShapeDtypeStruct((n, d), F32),
        grid=(nt,),
        in_specs=in_specs,
        out_specs=pl.BlockSpec((tm, d), row),
        scratch_shapes=[pltpu.VMEM((tm, dff), BF16)],
        compiler_params=_cparams(("parallel",)),
        name="ffn_ctx" if halo else "ffn",
    )(*args)


def _gate_weights(wa, wx):
    nd, heads, hd, _ = wa.shape
    per = RG_GATE // hd
    ncb = heads // per

    def bd(w):
        w = w.reshape(nd, ncb, per, hd, hd)
        eye = jnp.eye(per, dtype=w.dtype)
        return jnp.einsum('dcpij,pq->dcpiqj', w, eye).reshape(nd, ncb, RG_GATE, RG_GATE)

    return (0.5 * jnp.concatenate([bd(wa), bd(wx)], axis=-1)).astype(BF16)


def _forward(x, c, ctx, c_ctx, w_ada, b_ada, g_pre_mix, g_post_mix, g_pre_ffn, g_post_ffn, w_in,
             rg_conv_w, rg_conv_b, rg_wa, rg_ba, rg_wx, rg_bx, rg_lam, hg_lb_logits, hg_out_norm,
             w_proj_rg, w_proj_hg, w_out, ffn_w_up, ffn_conv_w, ffn_conv_b, ffn_w_down, *, grid_w):
    nb, seq, d = x.shape
    ctx_len = ctx.shape[1]
    depth = w_in.shape[0]
    rows_g = seq // grid_w
    dff = ffn_w_down.shape[1]
    tile = grid_w * nb
    heads = d // HG_EXPAND
    ncb = d // RG_COLS
    tf = 256 if dff % 256 == 0 else dff

    p = jax.nn.softmax(hg_lb_logits.astype(F32), axis=0)
    cum = jnp.cumsum(p, axis=0)
    lb_all = cum - cum[0:1]
    hg_par = jnp.stack([lb_all, 1.0 - lb_all, jnp.zeros_like(lb_all), hg_out_norm], axis=1)
    hg_par = jnp.pad(hg_par, ((0, 0), (0, 4), (0, 0)))

    w_ada_b, w_in_b = w_ada.astype(BF16), w_in.astype(BF16)
    wr_b, wh_b, wo_b = w_proj_rg.astype(BF16), w_proj_hg.astype(BF16), w_out.astype(BF16)
    wup_b, wdn_b = ffn_w_up.astype(BF16), ffn_w_down.astype(BF16)

    cc = jnp.concatenate([c, jnp.broadcast_to(c_ctx[None], (nb, d))], axis=0)
    mod_all = _ada_mod(cc, w_ada_b, b_ada).reshape(depth, 2, nb, N_MOD * d)

    xl = jnp.swapaxes(x, 0, 1).reshape(seq * nb, d)
    xc = jnp.swapaxes(ctx, 0, 1).reshape(ctx_len * nb, d)

    for l in range(depth):
        need_ctx = l < depth - 1
        mod_l, mod_c = mod_all[l, 0], mod_all[l, 1]
        gpm = g_pre_mix[l].reshape(1, d)
        gpo = g_post_mix[l].reshape(1, d)
        wg = _gate_weights(rg_wa[l], rg_wx[l])
        bg = 0.5 * jnp.concatenate([rg_ba[l].reshape(2, d // RG_GATE, 1, RG_GATE),
                                    rg_bx[l].reshape(2, d // RG_GATE, 1, RG_GATE)], axis=-1)
        rg_kw = dict(conv_w=rg_conv_w[l], conv_b=rg_conv_b[l].reshape(1, d), nb=nb, tt=grid_w)

        st_rg = [jnp.zeros((ncb, nb, RG_COLS), F32)] * 2
        st_hg = [jnp.zeros((nb, heads, HG_EXPAND, HG_EXPAND), F32)] * 2
        ctx_out = None
        for xs, mod_s, steps_s, latent in ((xc, mod_c, ctx_len, False), (xl, mod_l, seq, True)):
            if latent:
                shape4 = jax.ShapeDtypeStruct((grid_w, rows_g, nb, 4 * d), BF16)
                blk = lambda w, col: pl.BlockSpec((grid_w, None, nb, w), col)
                spec_of = lambda tile_of: blk(GROUPS_PER_STEP * d, lambda i, j: (0, tile_of(i), 0, j))
                gg_of = lambda tile_of: blk(RG_COLS, lambda i, j: (0, tile_of(i), 0, j))
                tile_spec = lambda gidx: blk(d, lambda i: (0, i, 0, gidx))
            else:
                shape4 = jax.ShapeDtypeStruct((steps_s, nb, 4 * d), BF16)
                blk = lambda w, col: pl.BlockSpec((grid_w, nb, w), col)
                spec_of = lambda tile_of: blk(GROUPS_PER_STEP * d, lambda i, j: (tile_of(i), 0, j))
                gg_of = lambda tile_of: blk(RG_COLS, lambda i, j: (tile_of(i), 0, j))
                tile_spec = lambda gidx: blk(d, lambda i: (i, 0, gidx))
            zx, hx = _inproj(xs, mod_s, gpm, w_in_b[l], 0, tile, nb)
            za, hf, st_rg[0] = _inproj_rg(
                hx, w_in_b[l], (1, 2, 3, 4), shape4, spec_of, zx, None, None, None,
                wg=wg[0], bg=bg[0], lam=rg_lam[l, 0:1], h0=st_rg[0], reverse=False, **rg_kw)
            zb, yrg, st_rg[1] = _inproj_rg(
                hx, w_in_b[l], (5, 6, 7, 8), shape4, spec_of, zx, hf, za, gg_of,
                wg=wg[1], bg=bg[1], lam=rg_lam[l, 1:2], h0=st_rg[1], reverse=True, **rg_kw)
            za3 = za.reshape(steps_s, nb, 4 * d)
            zb3 = zb.reshape(steps_s, nb, 4 * d)
            of, st_hg[0] = _hg_dir(za3, zb3, None, hg_par[l], st_hg[0], reverse=False, d=d)
            yhg, st_hg[1] = _hg_dir(za3, zb3, of, hg_par[l], st_hg[1], reverse=True, d=d)
            if latent or need_ctx:
                yhg = yhg.reshape(shape4.shape[:-1] + (d,))
                out = _merge(yrg, yhg, zb, tile_spec, xs, mod_s, gpo, wr_b[l], wh_b[l], wo_b[l],
                             tm=tile, nb=nb)
                if latent:
                    xl = out
                else:
                    ctx_out = out
        ffn = functools.partial(_ffn, g1=g_pre_ffn[l].reshape(1, d), g2=g_post_ffn[l].reshape(1, d),
                                w_up=wup_b[l], cw=ffn_conv_w[l], cb=ffn_conv_b[l].reshape(1, dff),
                                w_down=wdn_b[l], tm=tile, tf=tf, nb=nb)
        xl = ffn(xl, mod_l, halo=False)
        if need_ctx:
            xc = ffn(ctx_out, mod_c, halo=True)

    return jnp.swapaxes(xl.reshape(seq, nb, d), 0, 1)


def kernel(x, c, ctx, c_ctx, w_ada, b_ada, g_pre_mix, g_post_mix, g_pre_ffn, g_post_ffn, w_in, rg_conv_w, rg_conv_b, rg_wa, rg_ba, rg_wx, rg_bx, rg_lam, hg_lb_logits, hg_out_norm, w_proj_rg, w_proj_hg, w_out, ffn_w_up, ffn_conv_w, ffn_conv_b, ffn_w_down):
    return _forward(x, c, ctx, c_ctx, w_ada, b_ada, g_pre_mix, g_post_mix, g_pre_ffn, g_post_ffn,
                    w_in, rg_conv_w, rg_conv_b, rg_wa, rg_ba, rg_wx, rg_bx, rg_lam, hg_lb_logits,
                    hg_out_norm, w_proj_rg, w_proj_hg, w_out, ffn_w_up, ffn_conv_w, ffn_conv_b,
                    ffn_w_down, grid_w=GRID_W)
```

```python
import functools

import jax
import jax.numpy as jnp
from jax import lax
from jax.experimental import pallas as pl
from jax.experimental.pallas import tpu as pltpu

GRID_W = 64
RG_HEADS = 16
RG_C = 8.0
HG_EXPAND = 128
N_MOD = 6
EPS = 1e-6
RG_CONV_PAD_L = 1
RG_CONV_PAD_R = 2

HG_CHUNK = 128
HG_DIAG = 8
HG_DIAG_CLAMP = 20.0
HG_F_FLOOR = 1e-37
HG_PITCH = 24
RG_TINY = 1e-30
MM_COLS = 256
RG_GATE = 256
RG_COLS = 512
GROUPS_PER_STEP = 2
VMEM_LIMIT = 56 * 1024 * 1024

LOG2_E = 1.4426950408889634

F32 = jnp.float32
BF16 = jnp.bfloat16


def _cparams(sem):
    return pltpu.CompilerParams(dimension_semantics=sem, vmem_limit_bytes=VMEM_LIMIT)


def _sigmoid(x):
    return 0.5 + 0.5 * jnp.tanh(0.5 * x)


def _silu(x):
    return x * _sigmoid(x)


def _gelu_tanh(x):
    c = 0.7978845608028654
    return 0.5 * x * (1.0 + jnp.tanh(c * (x + 0.044715 * (x * x * x))))


def _rms(x, g):
    ms = jnp.mean(x * x, axis=-1, keepdims=True)
    return x * lax.rsqrt(ms + EPS) * g


def _per_batch(x, vec, nb):
    r, d = x.shape
    return x.reshape(r // nb, nb, d), vec[None]


def _norm_mod(x, g, shift, scale, nb):
    y = _rms(x, g)
    y3, sc = _per_batch(y, scale, nb)
    h = y3 * (1.0 + sc) + shift[None]
    return h.reshape(x.shape)


def _ada_kernel(c_ref, w_ref, b_ref, o_ref):
    s = _silu(c_ref[...]).astype(BF16)
    o_ref[0] = jnp.dot(s, w_ref[0], preferred_element_type=F32) + b_ref[0]


def _ada_mod(cc, w_ada, b_ada):
    depth, d, n = w_ada.shape
    tn = 1536 if n % 1536 == 0 else n
    return pl.pallas_call(
        _ada_kernel,
        out_shape=jax.ShapeDtypeStruct((depth, cc.shape[0], n), F32),
        grid=(depth, n // tn),
        in_specs=[pl.BlockSpec(cc.shape, lambda l, j: (0, 0)),
                  pl.BlockSpec((1, d, tn), lambda l, j: (l, 0, j)),
                  pl.BlockSpec((1, 1, tn), lambda l, j: (l, 0, j))],
        out_specs=pl.BlockSpec((1, cc.shape[0], tn), lambda l, j: (l, 0, j)),
        compiler_params=_cparams(("parallel", "parallel")),
        name="ada_mod",
    )(cc, w_ada, b_ada.reshape(depth, 1, n))


def _zero_from(v):
    u = lax.bitcast_convert_type(v, jnp.uint32)
    return lax.bitcast_convert_type((u >> 16) >> 16, F32)


def _rg_chunk(xe, bias, cw_ref, wg_ref, bg_ref, lam_ref, h, *, reverse, nb, steps, cols):
    rows = steps * nb
    xl = bias + xe[0:rows] * cw_ref[0:1, :]
    for j in range(1, 4):
        xl = xl + xe[j * nb:j * nb + rows] * cw_ref[j:j + 1, :]
    xb = xl.astype(BF16)
    tha, thx = [], []
    for s in range(cols // RG_GATE):
        th = jnp.tanh(jnp.dot(xb[:, s * RG_GATE:(s + 1) * RG_GATE], wg_ref[s],
                              preferred_element_type=F32) + bg_ref[s])
        tha.append(th[:, :RG_GATE])
        thx.append(th[:, RG_GATE:])
    half_rate = (-0.5 * RG_C) * jax.nn.softplus(-lam_ref[...])
    log_a = half_rate + half_rate * jnp.concatenate(tha, axis=1)
    ig = 0.5 + 0.5 * jnp.concatenate(thx, axis=1)
    a = jnp.exp(log_a)
    om = (1.0 + a * a) * jnp.tanh(-log_a)
    b = (om * lax.rsqrt(jnp.maximum(om, RG_TINY))) * (ig * xl)
    hs = [None] * steps
    for t in (range(steps - 1, -1, -1) if reverse else range(steps)):
        h = a[t * nb:(t + 1) * nb] * h + b[t * nb:(t + 1) * nb]
        hs[t] = h
    return h, jnp.concatenate(hs, axis=0)


def _inproj_kernel(*refs, nb, d, rg, reverse, tt, nt):
    if not rg:
        x_ref, mod_ref, g_ref, w_ref, o_ref, h_ref = refs
        h = _norm_mod(x_ref[...], g_ref[...], mod_ref[:, 0:d], mod_ref[:, d:2 * d], nb).astype(BF16)
        h_ref[...] = h
        res = jnp.dot(h, w_ref[...], preferred_element_type=F32)
        o_ref[...] = res.reshape(o_ref.shape).astype(o_ref.dtype)
        return
    gps = GROUPS_PER_STEP
    h_ref, w_refs, rest = refs[0], refs[1:1 + gps], refs[1 + gps:]
    if reverse:
        (zx_ref, zxp_ref, zxn_ref, cw_ref, cb_ref, wg_ref, bg_ref,
         lam_ref, h0_ref, hf_ref, gg_ref, o_ref, y_ref, hfin_ref, hst_scr) = rest
    else:
        (zx_ref, zxp_ref, zxn_ref, cw_ref, cb_ref, wg_ref, bg_ref,
         lam_ref, h0_ref, o_ref, y_ref, hfin_ref, hst_scr) = rest
    i = pl.program_id(0)
    j = pl.program_id(1)

    @pl.when(i == 0)
    def _():
        hst_scr[j] = h0_ref[j]

    tile = (nt - 1 - i) if reverse else i
    cols = y_ref.shape[1]
    ppg = d // MM_COLS
    npiece = gps * ppg
    steps = tt // npiece
    sub = steps * nb
    oshape = o_ref.shape[:-1] + (MM_COLS,)
    prev = jnp.where(tile > 0, zxp_ref[...].astype(F32), 0.0)
    nxt = jnp.where(tile < nt - 1, zxn_ref[...].astype(F32), 0.0)
    h = hst_scr[j]
    bias = cb_ref[...]
    for n in range(npiece):
        w_ref = w_refs[n // ppg]
        res = jnp.dot(h_ref[...], w_ref[:, (n % ppg) * MM_COLS:(n % ppg + 1) * MM_COLS],
                      preferred_element_type=F32)
        o_ref[..., n * MM_COLS:(n + 1) * MM_COLS] = res.reshape(oshape).astype(o_ref.dtype)
        c = (npiece - 1 - n) if reverse else n
        r0 = c * sub
        lo = prev if c == 0 else zx_ref[r0 - nb:r0, :].astype(F32)
        hi = nxt if c == npiece - 1 else zx_ref[r0 + sub:r0 + sub + 2 * nb, :].astype(F32)
        xe = jnp.concatenate([lo, zx_ref[r0:r0 + sub, :].astype(F32), hi], axis=0)
        h, hs = _rg_chunk(xe, bias, cw_ref, wg_ref, bg_ref, lam_ref, h, reverse=reverse, nb=nb,
                          steps=steps, cols=cols)
        if reverse:
            gate = _gelu_tanh(gg_ref[...].reshape(tt * nb, cols)[r0:r0 + sub, :].astype(F32))
            y_ref[r0:r0 + sub, :] = (gate * (hf_ref[r0:r0 + sub, :] + hs)).astype(y_ref.dtype)
        else:
            y_ref[r0:r0 + sub, :] = hs
        bias = cb_ref[...] + _zero_from(res[248:256, MM_COLS - 128:MM_COLS])[0:1, 0:1]
    hst_scr[j] = h
    hfin_ref[j] = h


def _group_map(col_groups):
    def wmap(i, j):
        idx = jnp.int32(col_groups[0])
        for k in range(1, len(col_groups)):
            idx = jnp.where(j == k, jnp.int32(col_groups[k]), idx)
        return (0, idx)
    return wmap


def _inproj(x, mod, g, w, group, tm, nb):
    n, d = x.shape
    row = lambda i: (i, 0)
    return pl.pallas_call(
        functools.partial(_inproj_kernel, nb=nb, d=d, rg=False, reverse=False, tt=0, nt=0),
        out_shape=(jax.ShapeDtypeStruct((n, d), BF16), jax.ShapeDtypeStruct((n, d), BF16)),
        grid=(n // tm,),
        in_specs=[pl.BlockSpec((tm, d), row),
                  pl.BlockSpec(mod.shape, lambda i: (0, 0)),
                  pl.BlockSpec((1, d), lambda i: (0, 0)),
                  pl.BlockSpec((d, d), lambda i: (0, group))],
        out_specs=(pl.BlockSpec((tm, d), row), pl.BlockSpec((tm, d), row)),
        compiler_params=_cparams(("parallel",)),
        name="inproj",
    )(x, mod, g, w)


def _inproj_rg(hx, w, col_groups, out_shape, out_spec_of, zx, hf, zgg, gg_spec_of, conv_w,
               conv_b, wg, bg, lam, h0, *, reverse, nb, tt):
    n, d = hx.shape
    rows = tt * nb
    nt = n // rows
    ncb = d // RG_COLS
    gps = GROUPS_PER_STEP
    nsb = RG_COLS // RG_GATE
    assert len(col_groups) == ncb * gps and tt % (gps * d // MM_COLS) == 0
    ppb = rows // nb
    npb = rows // (2 * nb)
    n_next_blocks = n // (2 * nb)

    def tile_of(i):
        return (nt - 1 - i) if reverse else i

    in_specs = [pl.BlockSpec((rows, d), lambda i, j: (tile_of(i), 0))]
    in_specs += [pl.BlockSpec((d, d), _group_map(col_groups[g::gps])) for g in range(gps)]
    in_specs += [
        pl.BlockSpec((rows, RG_COLS), lambda i, j: (tile_of(i), j)),
        pl.BlockSpec((nb, RG_COLS), lambda i, j: (jnp.maximum(tile_of(i) * ppb - 1, 0), j)),
        pl.BlockSpec((2 * nb, RG_COLS),
                     lambda i, j: (jnp.minimum((tile_of(i) + 1) * npb, n_next_blocks - 1), j)),
        pl.BlockSpec((4, RG_COLS), lambda i, j: (0, j)),
        pl.BlockSpec((1, RG_COLS), lambda i, j: (0, j)),
        pl.BlockSpec((nsb, RG_GATE, 2 * RG_GATE), lambda i, j: (j, 0, 0)),
        pl.BlockSpec((nsb, 1, 2 * RG_GATE), lambda i, j: (j, 0, 0)),
        pl.BlockSpec((1, RG_COLS), lambda i, j: (0, j)),
        pl.BlockSpec((ncb, nb, RG_COLS), lambda i, j: (0, 0, 0)),
    ]
    args = [hx] + [w] * gps + [zx, zx, zx, conv_w, conv_b, wg, bg, lam, h0]
    if reverse:
        in_specs += [pl.BlockSpec((rows, RG_COLS), lambda i, j: (tile_of(i), j)), gg_spec_of(tile_of)]
        args += [hf, zgg]
    return pl.pallas_call(
        functools.partial(_inproj_kernel, nb=nb, d=d, rg=True, reverse=reverse, tt=tt, nt=nt),
        out_shape=(out_shape, jax.ShapeDtypeStruct((n, d), BF16 if reverse else F32),
                   jax.ShapeDtypeStruct((ncb, nb, RG_COLS), F32)),
        grid=(nt, ncb),
        in_specs=in_specs,
        out_specs=(out_spec_of(tile_of),
                   pl.BlockSpec((rows, RG_COLS), lambda i, j: (tile_of(i), j)),
                   pl.BlockSpec((ncb, nb, RG_COLS), lambda i, j: (0, 0, 0))),
        scratch_shapes=[pltpu.VMEM((ncb, nb, RG_COLS), F32)],
        compiler_params=_cparams(("arbitrary", "arbitrary")),
        name="inproj_rg_bwd" if reverse else "inproj_rg_fwd",
    )(*args)


def _cumsum_time(x, reverse, block):
    c = x.shape[0]
    x4 = x.reshape((c // block, block) + x.shape[1:])
    cols = [None] * block
    order = range(block - 1, -1, -1) if reverse else range(block)
    run = None
    for i in order:
        run = x4[:, i] if run is None else run + x4[:, i]
        cols[i] = run
    return jnp.stack(cols, axis=1).reshape(x.shape)


def _cumsum_chunk(x, reverse):
    c = x.shape[0]
    blk = 8
    x4 = _cumsum_time(x, reverse, blk).reshape((c // blk, blk) + x.shape[1:])
    nblk = c // blk
    tot = x4[:, 0] if reverse else x4[:, blk - 1]
    offs = [None] * nblk
    order = range(nblk - 1, -1, -1) if reverse else range(nblk)
    run = None
    for i in order:
        offs[i] = run
        run = tot[i] if run is None else run + tot[i]
    first = nblk - 1 if reverse else 0
    parts = [x4[i] if i == first else x4[i] + offs[i][None] for i in range(nblk)]
    return jnp.stack(parts, axis=0).reshape(x.shape)


def _pivot_time(c, block, p):
    c4 = c.reshape((c.shape[0] // block, block) + c.shape[1:])
    return jnp.broadcast_to(c4[:, p:p + 1], c4.shape).reshape(c.shape)


def _hg_levels(c):
    out, m = [], c // 2
    while m >= HG_DIAG:
        out.append(m)
        m //= 2
    return out


def _hg_kernel(*refs, reverse, final):
    if final:
        (zq_ref, zf_ref, v_ref, of_ref, og_ref, par_ref, s0_ref, y_ref, sout_ref,
         st_scr, tr_scr, o_scr, mm_scr) = refs
    else:
        (zq_ref, zf_ref, v_ref, par_ref, s0_ref, y_ref, sout_ref, st_scr, tr_scr, o_scr,
         mm_scr) = refs
    j = pl.program_id(1)
    nj = pl.num_programs(1)
    c, nb, kd = zq_ref.shape

    @pl.when(j == 0)
    def _():
        st_scr[...] = s0_ref[...]

    zq = zq_ref[...].astype(F32)
    zf = zf_ref[...].astype(F32)
    lb = par_ref[0:1, :][None]
    oml = par_ref[1:2, :][None]

    q = _silu(zq)
    e = jnp.exp(-jnp.abs(zf))
    r = 1.0 / (1.0 + e)
    pos = zf >= 0.0
    f = lb + oml * (jnp.where(pos, 1.0, e) * r)
    k = oml * (jnp.where(pos, e, 1.0) * r)
    g = jnp.log2(jnp.maximum(f, HG_F_FLOOR))
    cs = _cumsum_chunk(g, reverse)
    cd = _cumsum_time(jnp.maximum(g, -HG_DIAG_CLAMP * LOG2_E), reverse, HG_DIAG)
    ctot = cs[0] if reverse else cs[c - 1]
    decay = jnp.exp2(ctot)

    pad = jnp.zeros((c, HG_PITCH - nb, kd), F32)
    for n, arr in enumerate((q, k, cs, cd, v_ref[...].astype(F32))):
        tr_scr[n] = jnp.concatenate([arr, pad], axis=1).reshape(c * HG_PITCH, kd)

    ti = lax.broadcasted_iota(jnp.int32, (c, c), 0)
    si = lax.broadcasted_iota(jnp.int32, (c, c), 1)
    lev = ti ^ si
    valid = (ti <= si) if reverse else (ti >= si)
    nt_dims = (((1,), (1,)), ((), ()))
    tn_dims = (((0,), (0,)), ((), ()))
    levels = _hg_levels(c)

    for b in range(nb):
        rows = pl.ds(b, c, stride=HG_PITCH)
        q2, k2, cs2, cd2 = (tr_scr[n, rows, :] for n in range(4))
        qb = q2.astype(BF16)
        kb = k2.astype(BF16)
        mm_scr[0, b] = qb * jnp.exp2(cs2).astype(BF16)
        mm_scr[1, b] = kb * jnp.exp2(ctot[b:b + 1, :] - cs2).astype(BF16)
        mm_scr[2, b] = tr_scr[4, rows, :].astype(BF16)
        cdd = cd2 - _pivot_time(cd2, HG_DIAG, HG_DIAG // 2 if reverse else HG_DIAG // 2 - 1)
        kz = _zero_from(cs2[0:8, :])[0:1, 0:1].astype(BF16)
        a = jnp.dot(qb * jnp.exp2(cdd).astype(BF16),
                    jnp.transpose(kb * jnp.exp2(-cdd).astype(BF16)) + kz, preferred_element_type=F32)
        for m in levels[::-1]:
            piv = m if reverse else m - 1
            el = jnp.exp2(-jnp.abs((cs2 - _pivot_time(cs2, 2 * m, piv)).astype(BF16)))
            p = jnp.dot(qb * el, jnp.transpose(kb * el) + kz, preferred_element_type=F32)
            a = jnp.where(lev < m, a, p)
        mm_scr[3, b] = jnp.where(valid, a, 0.0).astype(BF16)

    decay_t = jnp.transpose(decay)
    for b in range(nb):
        st = st_scr[b]
        vb = mm_scr[2, b]
        o = jnp.dot(mm_scr[0, b], st.astype(BF16), preferred_element_type=F32)
        o = o + jnp.dot(mm_scr[3, b], vb, preferred_element_type=F32)
        st_scr[b] = st * decay_t[:, b:b + 1] + lax.dot_general(mm_scr[1, b], vb, tn_dims,
                                                               preferred_element_type=F32)
        o_scr[pl.ds(b, c, stride=HG_PITCH), :] = o

    o = o_scr[...].reshape(c, HG_PITCH, kd)[:, 0:nb, :]
    if final:
        gain = par_ref[3:4, :][None]
        y = _rms(o + of_ref[...], gain) * _silu(og_ref[...].astype(F32))
        y_ref[...] = y.astype(y_ref.dtype)
    else:
        y_ref[...] = o

    @pl.when(j == nj - 1)
    def _():
        sout_ref[...] = st_scr[...]


def _hg_dir(za, zb, of, par, s0, *, reverse, d):
    t, nb, _ = za.shape
    kd = HG_EXPAND
    heads = d // kd
    nj = t // HG_CHUNK
    final = reverse
    assert nb <= HG_PITCH

    def blk(j):
        return (nj - 1 - j) if reverse else j

    def zspec(gidx):
        return pl.BlockSpec((HG_CHUNK, nb, kd), lambda h, j: (blk(j), 0, gidx * heads + h))

    hspec = pl.BlockSpec((HG_CHUNK, nb, kd), lambda h, j: (blk(j), 0, h))
    sspec = pl.BlockSpec((nb, None, kd, kd), lambda h, j: (0, h, 0, 0))
    in_specs = [zspec(1), zspec(3 if reverse else 2), zspec(0)]
    args = [za, za, zb]
    scratch = [pltpu.VMEM((nb, kd, kd), F32), pltpu.VMEM((5, HG_CHUNK * HG_PITCH, kd), F32),
               pltpu.VMEM((HG_CHUNK * HG_PITCH, kd), F32),
               pltpu.VMEM((4, nb, HG_CHUNK, kd), BF16)]
    if final:
        in_specs += [hspec, zspec(1)]
        args += [of, zb]
    in_specs += [pl.BlockSpec((par.shape[0], kd), lambda h, j: (0, h)), sspec]
    args += [par, s0]
    return pl.pallas_call(
        functools.partial(_hg_kernel, reverse=reverse, final=final),
        out_shape=(jax.ShapeDtypeStruct((t, nb, d), BF16 if final else F32),
                   jax.ShapeDtypeStruct(s0.shape, F32)),
        grid=(heads, nj),
        in_specs=in_specs,
        out_specs=(hspec, sspec),
        scratch_shapes=scratch,
        compiler_params=_cparams(("parallel", "arbitrary")),
        name="hg_bwd" if reverse else "hg_fwd",
    )(*args)


def _merge_kernel(yrg_ref, yhg_ref, ga_ref, gb_ref, x_ref, mod_ref, g_ref, wr_ref, wh_ref, wo_ref,
                  o_ref, *, nb, d):
    tm = x_ref.shape[0]
    p_rg = jnp.dot(yrg_ref[...], wr_ref[...], preferred_element_type=F32)
    p_hg = jnp.dot(yhg_ref[...].reshape(tm, d), wh_ref[...], preferred_element_type=F32)
    ga = ga_ref[...].reshape(tm, d).astype(F32)
    gb = gb_ref[...].reshape(tm, d).astype(F32)
    m = _sigmoid(ga) * p_rg + _sigmoid(gb) * p_hg
    y = jnp.dot(m.astype(BF16), wo_ref[...], preferred_element_type=F32)
    n3, gate = _per_batch(_rms(y, g_ref[...]), mod_ref[:, 2 * d:3 * d], nb)
    o_ref[...] = x_ref[...] + (gate * n3).reshape(tm, d)


def _merge(yrg, yhg, zb, tile_spec, x, mod, g, wr, wh, wo, *, tm, nb):
    n, d = x.shape
    row = lambda i: (i, 0)
    const = lambda i: (0, 0)
    return pl.pallas_call(
        functools.partial(_merge_kernel, nb=nb, d=d),
        out_shape=jax.ShapeDtypeStruct((n, d), F32),
        grid=(n // tm,),
        in_specs=[pl.BlockSpec((tm, d), row), tile_spec(0), tile_spec(2), tile_spec(3),
                  pl.BlockSpec((tm, d), row), pl.BlockSpec(mod.shape, const),
                  pl.BlockSpec((1, d), const), pl.BlockSpec((d, d), const),
                  pl.BlockSpec((d, d), const), pl.BlockSpec((d, d), const)],
        out_specs=pl.BlockSpec((tm, d), row),
        compiler_params=_cparams(("parallel",)),
        name="merge",
    )(yrg, yhg, zb, zb, x, mod, g, wr, wh, wo)


def _ffn_kernel(*refs, nb, d, dff, tf, halo, nt):
    if halo:
        (x_ref, xp_ref, xn_ref, mod_ref, g1_ref, g2_ref, wu_ref, cw_ref, cb_ref, wd_ref,
         o_ref, act_scr) = refs
    else:
        (x_ref, mod_ref, g1_ref, g2_ref, wu_ref, cw_ref, cb_ref, wd_ref, o_ref, act_scr) = refs
    i = pl.program_id(0)
    tm = x_ref.shape[0]
    shift = mod_ref[:, 3 * d:4 * d]
    scale = mod_ref[:, 4 * d:5 * d]

    hm = _norm_mod(x_ref[...], g1_ref[...], shift, scale, nb).astype(BF16)
    if halo:
        hp = _norm_mod(xp_ref[...], g1_ref[...], shift, scale, nb)
        hn = _norm_mod(xn_ref[...], g1_ref[...], shift, scale, nb)
        hp = jnp.where(i > 0, hp, 0.0).astype(BF16)
        hn = jnp.where(i < nt - 1, hn, 0.0).astype(BF16)
        he = jnp.concatenate([hp, hm, hn], axis=0)

    for kb in range(dff // tf):
        gs = slice(kb * tf, (kb + 1) * tf)
        vs = slice(dff + kb * tf, dff + (kb + 1) * tf)
        if halo:
            ue = jnp.dot(he, wu_ref[:, gs], preferred_element_type=F32)
        else:
            u = jnp.dot(hm, wu_ref[:, gs], preferred_element_type=F32)
            z = jnp.zeros((nb, tf), F32)
            ue = jnp.concatenate([z, u, z], axis=0)
        gc = cb_ref[:, gs] + ue[0:tm] * cw_ref[0:1, gs] + ue[nb:nb + tm] * cw_ref[1:2, gs] \
            + ue[2 * nb:2 * nb + tm] * cw_ref[2:3, gs]
        uv = jnp.dot(hm, wu_ref[:, vs], preferred_element_type=F32)
        act_scr[:, gs] = (_silu(gc) * uv).astype(BF16)

    y = jnp.dot(act_scr[...], wd_ref[...], preferred_element_type=F32)
    n3, gate = _per_batch(_rms(y, g2_ref[...]), mod_ref[:, 5 * d:6 * d], nb)
    o_ref[...] = x_ref[...] + (gate * n3).reshape(tm, d)


def _ffn(x, mod, g1, g2, w_up, cw, cb, w_down, *, tm, tf, nb, halo):
    n, d = x.shape
    dff = w_down.shape[0]
    nt = n // tm
    hb = tm // nb
    row = lambda i: (i, 0)
    const = lambda i: (0, 0)

    def resident(shape):
        return pl.BlockSpec(shape, const, pipeline_mode=pl.Buffered(1))

    in_specs = [pl.BlockSpec((tm, d), row)]
    args = [x]
    if halo:
        in_specs += [pl.BlockSpec((nb, d), lambda i: (jnp.maximum(i * hb - 1, 0), 0)),
                     pl.BlockSpec((nb, d), lambda i: (jnp.minimum((i + 1) * hb, n // nb - 1), 0))]
        args += [x, x]
    in_specs += [pl.BlockSpec(mod.shape, const), pl.BlockSpec((1, d), const),
                 pl.BlockSpec((1, d), const), resident(w_up.shape),
                 pl.BlockSpec(cw.shape, const), pl.BlockSpec(cb.shape, const),
                 resident(w_down.shape)]
    args += [mod, g1, g2, w_up, cw, cb, w_down]
    return pl.pallas_call(
        functools.partial(_ffn_kernel, nb=nb, d=d, dff=dff, tf=tf, halo=halo, nt=nt),
        out_shape=jax.ShapeDtypeStruct((n, d), F32),
        grid=(nt,),
        in_specs=in_specs,
        out_specs=pl.BlockSpec((tm, d), row),
        scratch_shapes=[pltpu.VMEM((tm, dff), BF16)],
        compiler_params=_cparams(("parallel",)),
        name="ffn_ctx" if halo else "ffn",
    )(*args)


def _gate_weights(wa, wx):
    nd, heads, hd, _ = wa.shape
    per = RG_GATE // hd
    ncb = heads // per

    def bd(w):
        w = w.reshape(nd, ncb, per, hd, hd)
        eye = jnp.eye(per, dtype=w.dtype)
        return jnp.einsum('dcpij,pq->dcpiqj', w, eye).reshape(nd, ncb, RG_GATE, RG_GATE)

    return (0.5 * jnp.concatenate([bd(wa), bd(wx)], axis=-1)).astype(BF16)


def _forward(x, c, ctx, c_ctx, w_ada, b_ada, g_pre_mix, g_post_mix, g_pre_ffn, g_post_ffn, w_in,
             rg_conv_w, rg_conv_b, rg_wa, rg_ba, rg_wx, rg_bx, rg_lam, hg_lb_logits, hg_out_norm,
             w_proj_rg, w_proj_hg, w_out, ffn_w_up, ffn_conv_w, ffn_conv_b, ffn_w_down, *, grid_w):
    nb, seq, d = x.shape
    ctx_len = ctx.shape[1]
    depth = w_in.shape[0]
    rows_g = seq // grid_w
    dff = ffn_w_down.shape[1]
    tile = grid_w * nb
    heads = d // HG_EXPAND
    ncb = d // RG_COLS
    tf = 256 if dff % 256 == 0 else dff

    p = jax.nn.softmax(hg_lb_logits.astype(F32), axis=0)
    cum = jnp.cumsum(p, axis=0)
    lb_all = cum - cum[0:1]
    hg_par = jnp.stack([lb_all, 1.0 - lb_all, jnp.zeros_like(lb_all), hg_out_norm], axis=1)
    hg_par = jnp.pad(hg_par, ((0, 0), (0, 4), (0, 0)))

    w_ada_b, w_in_b = w_ada.astype(BF16), w_in.astype(BF16)
    wr_b, wh_b, wo_b = w_proj_rg.astype(BF16), w_proj_hg.astype(BF16), w_out.astype(BF16)
    wup_b, wdn_b = ffn_w_up.astype(BF16), ffn_w_down.astype(BF16)

    cc = jnp.concatenate([c, jnp.broadcast_to(c_ctx[None], (nb, d))], axis=0)
    mod_all = _ada_mod(cc, w_ada_b, b_ada).reshape(depth, 2, nb, N_MOD * d)

    xl = jnp.swapaxes(x, 0, 1).reshape(seq * nb, d)
    xc = jnp.swapaxes(ctx, 0, 1).reshape(ctx_len * nb, d)

    for l in range(depth):
        need_ctx = l < depth - 1
        mod_l, mod_c = mod_all[l, 0], mod_all[l, 1]
        gpm = g_pre_mix[l].reshape(1, d)
        gpo = g_post_mix[l].reshape(1, d)
        wg = _gate_weights(rg_wa[l], rg_wx[l])
        bg = 0.5 * jnp.concatenate([rg_ba[l].reshape(2, d // RG_GATE, 1, RG_GATE),
                                    rg_bx[l].reshape(2, d // RG_GATE, 1, RG_GATE)], axis=-1)
        rg_kw = dict(conv_w=rg_conv_w[l], conv_b=rg_conv_b[l].reshape(1, d), nb=nb, tt=grid_w)

        st_rg = [jnp.zeros((ncb, nb, RG_COLS), F32)] * 2
        st_hg = [jnp.zeros((nb, heads, HG_EXPAND, HG_EXPAND), F32)] * 2
        ctx_out = None
        for xs, mod_s, steps_s, latent in ((xc, mod_c, ctx_len, False), (xl, mod_l, seq, True)):
            if latent:
                shape4 = jax.ShapeDtypeStruct((grid_w, rows_g, nb, 4 * d), BF16)
                blk = lambda w, col: pl.BlockSpec((grid_w, None, nb, w), col)
                spec_of = lambda tile_of: blk(GROUPS_PER_STEP * d, lambda i, j: (0, tile_of(i), 0, j))
                gg_of = lambda tile_of: blk(RG_COLS, lambda i, j: (0, tile_of(i), 0, j))
                tile_spec = lambda gidx: blk(d, lambda i: (0, i, 0, gidx))
            else:
                shape4 = jax.ShapeDtypeStruct((steps_s, nb, 4 * d), BF16)
                blk = lambda w, col: pl.BlockSpec((grid_w, nb, w), col)
                spec_of = lambda tile_of: blk(GROUPS_PER_STEP * d, lambda i, j: (tile_of(i), 0, j))
                gg_of = lambda tile_of: blk(RG_COLS, lambda i, j: (tile_of(i), 0, j))
                tile_spec = lambda gidx: blk(d, lambda i: (i, 0, gidx))
            zx, hx = _inproj(xs, mod_s, gpm, w_in_b[l], 0, tile, nb)
            za, hf, st_rg[0] = _inproj_rg(
                hx, w_in_b[l], (1, 2, 3, 4), shape4, spec_of, zx, None, None, None,
                wg=wg[0], bg=bg[0], lam=rg_lam[l, 0:1], h0=st_rg[0], reverse=False, **rg_kw)
            zb, yrg, st_rg[1] = _inproj_rg(
                hx, w_in_b[l], (5, 6, 7, 8), shape4, spec_of, zx, hf, za, gg_of,
                wg=wg[1], bg=bg[1], lam=rg_lam[l, 1:2], h0=st_rg[1], reverse=True, **rg_kw)
            za3 = za.reshape(steps_s, nb, 4 * d)
            zb3 = zb.reshape(steps_s, nb, 4 * d)
            of, st_hg[0] = _hg_dir(za3, zb3, None, hg_par[l], st_hg[0], reverse=False, d=d)
            yhg, st_hg[1] = _hg_dir(za3, zb3, of, hg_par[l], st_hg[1], reverse=True, d=d)
            if latent or need_ctx:
                yhg = yhg.reshape(shape4.shape[:-1] + (d,))
                out = _merge(yrg, yhg, zb, tile_spec, xs, mod_s, gpo, wr_b[l], wh_b[l], wo_b[l],
                             tm=tile, nb=nb)
                if latent:
                    xl = out
                else:
                    ctx_out = out
        ffn = functools.partial(_ffn, g1=g_pre_ffn[l].reshape(1, d), g2=g_post_ffn[l].reshape(1, d),
                                w_up=wup_b[l], cw=ffn_conv_w[l], cb=ffn_conv_b[l].reshape(1, dff),
                                w_down=wdn_b[l], tm=tile, tf=tf, nb=nb)
        xl = ffn(xl, mod_l, halo=False)
        if need_ctx:
            xc = ffn(ctx_out, mod_c, halo=True)

    return jnp.swapaxes(xl.reshape(seq, nb, d), 0, 1)


def kernel(x, c, ctx, c_ctx, w_ada, b_ada, g_pre_mix, g_post_mix, g_pre_ffn, g_post_ffn, w_in, rg_conv_w, rg_conv_b, rg_wa, rg_ba, rg_wx, rg_bx, rg_lam, hg_lb_logits, hg_out_norm, w_proj_rg, w_proj_hg, w_out, ffn_w_up, ffn_conv_w, ffn_conv_b, ffn_w_down):
    return _forward(x, c, ctx, c_ctx, w_ada, b_ada, g_pre_mix, g_post_mix, g_pre_ffn, g_post_ffn,
                    w_in, rg_conv_w, rg_conv_b, rg_wa, rg_ba, rg_wx, rg_bx, rg_lam, hg_lb_logits,
                    hg_out_norm, w_proj_rg, w_proj_hg, w_out, ffn_w_up, ffn_conv_w, ffn_conv_b,
                    ffn_w_down, grid_w=GRID_W)
```

```python
import functools

import jax
import jax.numpy as jnp
from jax import lax
from jax.experimental import pallas as pl
from jax.experimental.pallas import tpu as pltpu

GRID_W = 64
RG_HEADS = 16
RG_C = 8.0
HG_EXPAND = 128
N_MOD = 6
EPS = 1e-6
RG_CONV_PAD_L = 1
RG_CONV_PAD_R = 2

HG_CHUNK = 128
HG_DIAG = 8
HG_DIAG_CLAMP = 20.0
HG_F_FLOOR = 1e-37
HG_PITCH = 24
RG_TINY = 1e-30
MM_COLS = 256
RG_GATE = 256
RG_COLS = 512
GROUPS_PER_STEP = 2
VMEM_LIMIT = 56 * 1024 * 1024

LOG2_E = 1.4426950408889634

F32 = jnp.float32
BF16 = jnp.bfloat16


def _cparams(sem):
    return pltpu.CompilerParams(dimension_semantics=sem, vmem_limit_bytes=VMEM_LIMIT)


def _sigmoid(x):
    return 0.5 + 0.5 * jnp.tanh(0.5 * x)


def _silu(x):
    return x * _sigmoid(x)


def _gelu_tanh(x):
    c = 0.7978845608028654
    return 0.5 * x * (1.0 + jnp.tanh(c * (x + 0.044715 * (x * x * x))))


def _rms(x, g):
    ms = jnp.mean(x * x, axis=-1, keepdims=True)
    return x * lax.rsqrt(ms + EPS) * g


def _per_batch(x, vec, nb):
    r, d = x.shape
    return x.reshape(r // nb, nb, d), vec[None]


def _norm_mod(x, g, shift, scale, nb):
    y = _rms(x, g)
    y3, sc = _per_batch(y, scale, nb)
    h = y3 * (1.0 + sc) + shift[None]
    return h.reshape(x.shape)


def _ada_kernel(c_ref, w_ref, b_ref, o_ref):
    s = _silu(c_ref[...]).astype(BF16)
    o_ref[0] = jnp.dot(s, w_ref[0], preferred_element_type=F32) + b_ref[0]


def _ada_mod(cc, w_ada, b_ada):
    depth, d, n = w_ada.shape
    tn = 1536 if n % 1536 == 0 else n
    return pl.pallas_call(
        _ada_kernel,
        out_shape=jax.ShapeDtypeStruct((depth, cc.shape[0], n), F32),
        grid=(depth, n // tn),
        in_specs=[pl.BlockSpec(cc.shape, lambda l, j: (0, 0)),
                  pl.BlockSpec((1, d, tn), lambda l, j: (l, 0, j)),
                  pl.BlockSpec((1, 1, tn), lambda l, j: (l, 0, j))],
        out_specs=pl.BlockSpec((1, cc.shape[0], tn), lambda l, j: (l, 0, j)),
        compiler_params=_cparams(("parallel", "parallel")),
        name="ada_mod",
    )(cc, w_ada, b_ada.reshape(depth, 1, n))


def _zero_from(v):
    u = lax.bitcast_convert_type(v, jnp.uint32)
    return lax.bitcast_convert_type((u >> 16) >> 16, F32)


def _rg_chunk(xe, bias, cw_ref, wg_ref, bg_ref, lam_ref, h, *, reverse, nb, steps, cols):
    rows = steps * nb
    xl = bias + xe[0:rows] * cw_ref[0:1, :]
    for j in range(1, 4):
        xl = xl + xe[j * nb:j * nb + rows] * cw_ref[j:j + 1, :]
    xb = xl.astype(BF16)
    tha, thx = [], []
    for s in range(cols // RG_GATE):
        th = jnp.tanh(jnp.dot(xb[:, s * RG_GATE:(s + 1) * RG_GATE], wg_ref[s],
                              preferred_element_type=F32) + bg_ref[s])
        tha.append(th[:, :RG_GATE])
        thx.append(th[:, RG_GATE:])
    half_rate = (-0.5 * RG_C) * jax.nn.softplus(-lam_ref[...])
    log_a = half_rate + half_rate * jnp.concatenate(tha, axis=1)
    ig = 0.5 + 0.5 * jnp.concatenate(thx, axis=1)
    a = jnp.exp(log_a)
    om = (1.0 + a * a) * jnp.tanh(-log_a)
    b = (om * lax.rsqrt(jnp.maximum(om, RG_TINY))) * (ig * xl)
    hs = [None] * steps
    for t in (range(steps - 1, -1, -1) if reverse else range(steps)):
        h = a[t * nb:(t + 1) * nb] * h + b[t * nb:(t + 1) * nb]
        hs[t] = h
    return h, jnp.concatenate(hs, axis=0)


def _inproj_kernel(*refs, nb, d, rg, reverse, tt, nt):
    if not rg:
        x_ref, mod_ref, g_ref, w_ref, o_ref, h_ref = refs
        h = _norm_mod(x_ref[...], g_ref[...], mod_ref[:, 0:d], mod_ref[:, d:2 * d], nb).astype(BF16)
        h_ref[...] = h
        res = jnp.dot(h, w_ref[...], preferred_element_type=F32)
        o_ref[...] = res.reshape(o_ref.shape).astype(o_ref.dtype)
        return
    gps = GROUPS_PER_STEP
    h_ref, w_refs, rest = refs[0], refs[1:1 + gps], refs[1 + gps:]
    if reverse:
        (zx_ref, zxp_ref, zxn_ref, cw_ref, cb_ref, wg_ref, bg_ref,
         lam_ref, h0_ref, hf_ref, gg_ref, o_ref, y_ref, hfin_ref, hst_scr) = rest
    else:
        (zx_ref, zxp_ref, zxn_ref, cw_ref, cb_ref, wg_ref, bg_ref,
         lam_ref, h0_ref, o_ref, y_ref, hfin_ref, hst_scr) = rest
    i = pl.program_id(0)
    j = pl.program_id(1)

    @pl.when(i == 0)
    def _():
        hst_scr[j] = h0_ref[j]

    tile = (nt - 1 - i) if reverse else i
    cols = y_ref.shape[1]
    ppg = d // MM_COLS
    npiece = gps * ppg
    steps = tt // npiece
    sub = steps * nb
    oshape = o_ref.shape[:-1] + (MM_COLS,)
    prev = jnp.where(tile > 0, zxp_ref[...].astype(F32), 0.0)
    nxt = jnp.where(tile < nt - 1, zxn_ref[...].astype(F32), 0.0)
    h = hst_scr[j]
    bias = cb_ref[...]
    for n in range(npiece):
        w_ref = w_refs[n // ppg]
        res = jnp.dot(h_ref[...], w_ref[:, (n % ppg) * MM_COLS:(n % ppg + 1) * MM_COLS],
                      preferred_element_type=F32)
        o_ref[..., n * MM_COLS:(n + 1) * MM_COLS] = res.reshape(oshape).astype(o_ref.dtype)
        c = (npiece - 1 - n) if reverse else n
        r0 = c * sub
        lo = prev if c == 0 else zx_ref[r0 - nb:r0, :].astype(F32)
        hi = nxt if c == npiece - 1 else zx_ref[r0 + sub:r0 + sub + 2 * nb, :].astype(F32)
        xe = jnp.concatenate([lo, zx_ref[r0:r0 + sub, :].astype(F32), hi], axis=0)
        h, hs = _rg_chunk(xe, bias, cw_ref, wg_ref, bg_ref, lam_ref, h, reverse=reverse, nb=nb,
                          steps=steps, cols=cols)
        if reverse:
            gate = _gelu_tanh(gg_ref[...].reshape(tt * nb, cols)[r0:r0 + sub, :].astype(F32))
            y_ref[r0:r0 + sub, :] = (gate * (hf_ref[r0:r0 + sub, :] + hs)).astype(y_ref.dtype)
        else:
            y_ref[r0:r0 + sub, :] = hs
        bias = cb_ref[...] + _zero_from(res[248:256, MM_COLS - 128:MM_COLS])[0:1, 0:1]
    hst_scr[j] = h
    hfin_ref[j] = h


def _group_map(col_groups):
    def wmap(i, j):
        idx = jnp.int32(col_groups[0])
        for k in range(1, len(col_groups)):
            idx = jnp.where(j == k, jnp.int32(col_groups[k]), idx)
        return (0, idx)
    return wmap


def _inproj(x, mod, g, w, group, tm, nb):
    n, d = x.shape
    row = lambda i: (i, 0)
    return pl.pallas_call(
        functools.partial(_inproj_kernel, nb=nb, d=d, rg=False, reverse=False, tt=0, nt=0),
        out_shape=(jax.ShapeDtypeStruct((n, d), BF16), jax.ShapeDtypeStruct((n, d), BF16)),
        grid=(n // tm,),
        in_specs=[pl.BlockSpec((tm, d), row),
                  pl.BlockSpec(mod.shape, lambda i: (0, 0)),
                  pl.BlockSpec((1, d), lambda i: (0, 0)),
                  pl.BlockSpec((d, d), lambda i: (0, group))],
        out_specs=(pl.BlockSpec((tm, d), row), pl.BlockSpec((tm, d), row)),
        compiler_params=_cparams(("parallel",)),
        name="inproj",
    )(x, mod, g, w)


def _inproj_rg(hx, w, col_groups, out_shape, out_spec_of, zx, hf, zgg, gg_spec_of, conv_w,
               conv_b, wg, bg, lam, h0, *, reverse, nb, tt):
    n, d = hx.shape
    rows = tt * nb
    nt = n // rows
    ncb = d // RG_COLS
    gps = GROUPS_PER_STEP
    nsb = RG_COLS // RG_GATE
    assert len(col_groups) == ncb * gps and tt % (gps * d // MM_COLS) == 0
    ppb = rows // nb
    npb = rows // (2 * nb)
    n_next_blocks = n // (2 * nb)

    def tile_of(i):
        return (nt - 1 - i) if reverse else i

    in_specs = [pl.BlockSpec((rows, d), lambda i, j: (tile_of(i), 0))]
    in_specs += [pl.BlockSpec((d, d), _group_map(col_groups[g::gps])) for g in range(gps)]
    in_specs += [
        pl.BlockSpec((rows, RG_COLS), lambda i, j: (tile_of(i), j)),
        pl.BlockSpec((nb, RG_COLS), lambda i, j: (jnp.maximum(tile_of(i) * ppb - 1, 0), j)),
        pl.BlockSpec((2 * nb, RG_COLS),
                     lambda i, j: (jnp.minimum((tile_of(i) + 1) * npb, n_next_blocks - 1), j)),
        pl.BlockSpec((4, RG_COLS), lambda i, j: (0, j)),
        pl.BlockSpec((1, RG_COLS), lambda i, j: (0, j)),
        pl.BlockSpec((nsb, RG_GATE, 2 * RG_GATE), lambda i, j: (j, 0, 0)),
        pl.BlockSpec((nsb, 1, 2 * RG_GATE), lambda i, j: (j, 0, 0)),
        pl.BlockSpec((1, RG_COLS), lambda i, j: (0, j)),
        pl.BlockSpec((ncb, nb, RG_COLS), lambda i, j: (0, 0, 0)),
    ]
    args = [hx] + [w] * gps + [zx, zx, zx, conv_w, conv_b, wg, bg, lam, h0]
    if reverse:
        in_specs += [pl.BlockSpec((rows, RG_COLS), lambda i, j: (tile_of(i), j)), gg_spec_of(tile_of)]
        args += [hf, zgg]
    return pl.pallas_call(
        functools.partial(_inproj_kernel, nb=nb, d=d, rg=True, reverse=reverse, tt=tt, nt=nt),
        out_shape=(out_shape, jax.ShapeDtypeStruct((n, d), BF16 if reverse else F32),
                   jax.ShapeDtypeStruct((ncb, nb, RG_COLS), F32)),
        grid=(nt, ncb),
        in_specs=in_specs,
        out_specs=(out_spec_of(tile_of),
                   pl.BlockSpec((rows, RG_COLS), lambda i, j: (tile_of(i), j)),
                   pl.BlockSpec((ncb, nb, RG_COLS), lambda i, j: (0, 0, 0))),
        scratch_shapes=[pltpu.VMEM((ncb, nb, RG_COLS), F32)],
        compiler_params=_cparams(("arbitrary", "arbitrary")),
        name="inproj_rg_bwd" if reverse else "inproj_rg_fwd",
    )(*args)


def _cumsum_time(x, reverse, block):
    c = x.shape[0]
    x4 = x.reshape((c // block, block) + x.shape[1:])
    cols = [None] * block
    order = range(block - 1, -1, -1) if reverse else range(block)
    run = None
    for i in order:
        run = x4[:, i] if run is None else run + x4[:, i]
        cols[i] = run
    return jnp.stack(cols, axis=1).reshape(x.shape)


def _cumsum_chunk(x, reverse):
    c = x.shape[0]
    blk = 8
    x4 = _cumsum_time(x, reverse, blk).reshape((c // blk, blk) + x.shape[1:])
    nblk = c // blk
    tot = x4[:, 0] if reverse else x4[:, blk - 1]
    offs = [None] * nblk
    order = range(nblk - 1, -1, -1) if reverse else range(nblk)
    run = None
    for i in order:
        offs[i] = run
        run = tot[i] if run is None else run + tot[i]
    first = nblk - 1 if reverse else 0
    parts = [x4[i] if i == first else x4[i] + offs[i][None] for i in range(nblk)]
    return jnp.stack(parts, axis=0).reshape(x.shape)


def _pivot_time(c, block, p):
    c4 = c.reshape((c.shape[0] // block, block) + c.shape[1:])
    return jnp.broadcast_to(c4[:, p:p + 1], c4.shape).reshape(c.shape)


def _hg_levels(c):
    out, m = [], c // 2
    while m >= HG_DIAG:
        out.append(m)
        m //= 2
    return out


def _hg_kernel(*refs, reverse, final):
    if final:
        (zq_ref, zf_ref, v_ref, of_ref, og_ref, par_ref, s0_ref, y_ref, sout_ref,
         st_scr, tr_scr, o_scr, mm_scr) = refs
    else:
        (zq_ref, zf_ref, v_ref, par_ref, s0_ref, y_ref, sout_ref, st_scr, tr_scr, o_scr,
         mm_scr) = refs
    j = pl.program_id(1)
    nj = pl.num_programs(1)
    c, nb, kd = zq_ref.shape

    @pl.when(j == 0)
    def _():
        st_scr[...] = s0_ref[...]

    zq = zq_ref[...].astype(F32)
    zf = zf_ref[...].astype(F32)
    lb = par_ref[0:1, :][None]
    oml = par_ref[1:2, :][None]

    q = _silu(zq)
    e = jnp.exp(-jnp.abs(zf))
    r = 1.0 / (1.0 + e)
    pos = zf >= 0.0
    f = lb + oml * (jnp.where(pos, 1.0, e) * r)
    k = oml * (jnp.where(pos, e, 1.0) * r)
    g = jnp.log2(jnp.maximum(f, HG_F_FLOOR))
    cs = _cumsum_chunk(g, reverse)
    cd = _cumsum_time(jnp.maximum(g, -HG_DIAG_CLAMP * LOG2_E), reverse, HG_DIAG)
    ctot = cs[0] if reverse else cs[c - 1]
    decay = jnp.exp2(ctot)

    pad = jnp.zeros((c, HG_PITCH - nb, kd), F32)
    for n, arr in enumerate((q, k, cs, cd, v_ref[...].astype(F32))):
        tr_scr[n] = jnp.concatenate([arr, pad], axis=1).reshape(c * HG_PITCH, kd)

    ti = lax.broadcasted_iota(jnp.int32, (c, c), 0)
    si = lax.broadcasted_iota(jnp.int32, (c, c), 1)
    lev = ti ^ si
    valid = (ti <= si) if reverse else (ti >= si)
    nt_dims = (((1,), (1,)), ((), ()))
    tn_dims = (((0,), (0,)), ((), ()))
    levels = _hg_levels(c)

    for b in range(nb):
        rows = pl.ds(b, c, stride=HG_PITCH)
        q2, k2, cs2, cd2 = (tr_scr[n, rows, :] for n in range(4))
        qb = q2.astype(BF16)
        kb = k2.astype(BF16)
        mm_scr[0, b] = qb * jnp.exp2(cs2).astype(BF16)
        mm_scr[1, b] = kb * jnp.exp2(ctot[b:b + 1, :] - cs2).astype(BF16)
        mm_scr[2, b] = tr_scr[4, rows, :].astype(BF16)
        cdd = cd2 - _pivot_time(cd2, HG_DIAG, HG_DIAG // 2 if reverse else HG_DIAG // 2 - 1)
        pairs = [(qb * jnp.exp2(cdd).astype(BF16), kb * jnp.exp2(-cdd).astype(BF16))]
        for m in levels[::-1]:
            piv = m if reverse else m - 1
            el = jnp.exp2(-jnp.abs((cs2 - _pivot_time(cs2, 2 * m, piv)).astype(BF16)))
            pairs.append((qb * el, kb * el))
        prods = []
        for i in range(0, len(pairs), 2):
            grp = pairs[i:i + 2]
            lhs = jnp.concatenate([g[0] for g in grp], axis=0)
            rhs = jnp.concatenate([g[1] for g in grp], axis=0)
            res = lax.dot_general(lhs, rhs, nt_dims, preferred_element_type=F32)
            prods += [res[n * c:(n + 1) * c, n * c:(n + 1) * c] for n in range(len(grp))]
        a = prods[0]
        for m, p in zip(levels[::-1], prods[1:]):
            a = jnp.where(lev < m, a, p)
        mm_scr[3, b] = jnp.where(valid, a, 0.0).astype(BF16)

    decay_t = jnp.transpose(decay)
    for b in range(nb):
        st = st_scr[b]
        vb = mm_scr[2, b]
        o = jnp.dot(mm_scr[0, b], st.astype(BF16), preferred_element_type=F32)
        o = o + jnp.dot(mm_scr[3, b], vb, preferred_element_type=F32)
        st_scr[b] = st * decay_t[:, b:b + 1] + lax.dot_general(mm_scr[1, b], vb, tn_dims,
                                                               preferred_element_type=F32)
        o_scr[pl.ds(b, c, stride=HG_PITCH), :] = o

    o = o_scr[...].reshape(c, HG_PITCH, kd)[:, 0:nb, :]
    if final:
        gain = par_ref[3:4, :][None]
        y = _rms(o + of_ref[...], gain) * _silu(og_ref[...].astype(F32))
        y_ref[...] = y.astype(y_ref.dtype)
    else:
        y_ref[...] = o

    @pl.when(j == nj - 1)
    def _():
        sout_ref[...] = st_scr[...]


def _hg_dir(za, zb, of, par, s0, *, reverse, d):
    t, nb, _ = za.shape
    kd = HG_EXPAND
    heads = d // kd
    nj = t // HG_CHUNK
    final = reverse
    assert nb <= HG_PITCH

    def blk(j):
        return (nj - 1 - j) if reverse else j

    def zspec(gidx):
        return pl.BlockSpec((HG_CHUNK, nb, kd), lambda h, j: (blk(j), 0, gidx * heads + h))

    hspec = pl.BlockSpec((HG_CHUNK, nb, kd), lambda h, j: (blk(j), 0, h))
    sspec = pl.BlockSpec((nb, None, kd, kd), lambda h, j: (0, h, 0, 0))
    in_specs = [zspec(1), zspec(3 if reverse else 2), zspec(0)]
    args = [za, za, zb]
    scratch = [pltpu.VMEM((nb, kd, kd), F32), pltpu.VMEM((5, HG_CHUNK * HG_PITCH, kd), F32),
               pltpu.VMEM((HG_CHUNK * HG_PITCH, kd), F32),
               pltpu.VMEM((4, nb, HG_CHUNK, kd), BF16)]
    if final:
        in_specs += [hspec, zspec(1)]
        args += [of, zb]
    in_specs += [pl.BlockSpec((par.shape[0], kd), lambda h, j: (0, h)), sspec]
    args += [par, s0]
    return pl.pallas_call(
        functools.partial(_hg_kernel, reverse=reverse, final=final),
        out_shape=(jax.ShapeDtypeStruct((t, nb, d), BF16 if final else F32),
                   jax.ShapeDtypeStruct(s0.shape, F32)),
        grid=(heads, nj),
        in_specs=in_specs,
        out_specs=(hspec, sspec),
        scratch_shapes=scratch,
        compiler_params=_cparams(("parallel", "arbitrary")),
        name="hg_bwd" if reverse else "hg_fwd",
    )(*args)


def _merge_kernel(yrg_ref, yhg_ref, ga_ref, gb_ref, x_ref, mod_ref, g_ref, wr_ref, wh_ref, wo_ref,
                  o_ref, *, nb, d):
    tm = x_ref.shape[0]
    p_rg = jnp.dot(yrg_ref[...], wr_ref[...], preferred_element_type=F32)
    p_hg = jnp.dot(yhg_ref[...].reshape(tm, d), wh_ref[...], preferred_element_type=F32)
    ga = ga_ref[...].reshape(tm, d).astype(F32)
    gb = gb_ref[...].reshape(tm, d).astype(F32)
    m = _sigmoid(ga) * p_rg + _sigmoid(gb) * p_hg
    y = jnp.dot(m.astype(BF16), wo_ref[...], preferred_element_type=F32)
    n3, gate = _per_batch(_rms(y, g_ref[...]), mod_ref[:, 2 * d:3 * d], nb)
    o_ref[...] = x_ref[...] + (gate * n3).reshape(tm, d)


def _merge(yrg, yhg, zb, tile_spec, x, mod, g, wr, wh, wo, *, tm, nb):
    n, d = x.shape
    row = lambda i: (i, 0)
    const = lambda i: (0, 0)
    return pl.pallas_call(
        functools.partial(_merge_kernel, nb=nb, d=d),
        out_shape=jax.ShapeDtypeStruct((n, d), F32),
        grid=(n // tm,),
        in_specs=[pl.BlockSpec((tm, d), row), tile_spec(0), tile_spec(2), tile_spec(3),
                  pl.BlockSpec((tm, d), row), pl.BlockSpec(mod.shape, const),
                  pl.BlockSpec((1, d), const), pl.BlockSpec((d, d), const),
                  pl.BlockSpec((d, d), const), pl.BlockSpec((d, d), const)],
        out_specs=pl.BlockSpec((tm, d), row),
        compiler_params=_cparams(("parallel",)),
        name="merge",
    )(yrg, yhg, zb, zb, x, mod, g, wr, wh, wo)


def _ffn_kernel(*refs, nb, d, dff, tf, halo, nt):
    if halo:
        (x_ref, xp_ref, xn_ref, mod_ref, g1_ref, g2_ref, wu_ref, cw_ref, cb_ref, wd_ref,
         o_ref, act_scr) = refs
    else:
        (x_ref, mod_ref, g1_ref, g2_ref, wu_ref, cw_ref, cb_ref, wd_ref, o_ref, act_scr) = refs
    i = pl.program_id(0)
    tm = x_ref.shape[0]
    shift = mod_ref[:, 3 * d:4 * d]
    scale = mod_ref[:, 4 * d:5 * d]

    hm = _norm_mod(x_ref[...], g1_ref[...], shift, scale, nb).astype(BF16)
    if halo:
        hp = _norm_mod(xp_ref[...], g1_ref[...], shift, scale, nb)
        hn = _norm_mod(xn_ref[...], g1_ref[...], shift, scale, nb)
        hp = jnp.where(i > 0, hp, 0.0).astype(BF16)
        hn = jnp.where(i < nt - 1, hn, 0.0).astype(BF16)
        he = jnp.concatenate([hp, hm, hn], axis=0)

    for kb in range(dff // tf):
        gs = slice(kb * tf, (kb + 1) * tf)
        vs = slice(dff + kb * tf, dff + (kb + 1) * tf)
        if halo:
            ue = jnp.dot(he, wu_ref[:, gs], preferred_element_type=F32)
        else:
            u = jnp.dot(hm, wu_ref[:, gs], preferred_element_type=F32)
            z = jnp.zeros((nb, tf), F32)
            ue = jnp.concatenate([z, u, z], axis=0)
        gc = cb_ref[:, gs] + ue[0:tm] * cw_ref[0:1, gs] + ue[nb:nb + tm] * cw_ref[1:2, gs] \
            + ue[2 * nb:2 * nb + tm] * cw_ref[2:3, gs]
        uv = jnp.dot(hm, wu_ref[:, vs], preferred_element_type=F32)
        act_scr[:, gs] = (_silu(gc) * uv).astype(BF16)

    y = jnp.dot(act_scr[...], wd_ref[...], preferred_element_type=F32)
    n3, gate = _per_batch(_rms(y, g2_ref[...]), mod_ref[:, 5 * d:6 * d], nb)
    o_ref[...] = x_ref[...] + (gate * n3).reshape(tm, d)


def _ffn(x, mod, g1, g2, w_up, cw, cb, w_down, *, tm, tf, nb, halo):
    n, d = x.shape
    dff = w_down.shape[0]
    nt = n // tm
    hb = tm // nb
    row = lambda i: (i, 0)
    const = lambda i: (0, 0)

    def resident(shape):
        return pl.BlockSpec(shape, const, pipeline_mode=pl.Buffered(1))

    in_specs = [pl.BlockSpec((tm, d), row)]
    args = [x]
    if halo:
        in_specs += [pl.BlockSpec((nb, d), lambda i: (jnp.maximum(i * hb - 1, 0), 0)),
                     pl.BlockSpec((nb, d), lambda i: (jnp.minimum((i + 1) * hb, n // nb - 1), 0))]
        args += [x, x]
    in_specs += [pl.BlockSpec(mod.shape, const), pl.BlockSpec((1, d), const),
                 pl.BlockSpec((1, d), const), resident(w_up.shape),
                 pl.BlockSpec(cw.shape, const), pl.BlockSpec(cb.shape, const),
                 resident(w_down.shape)]
    args += [mod, g1, g2, w_up, cw, cb, w_down]
    return pl.pallas_call(
        functools.partial(_ffn_kernel, nb=nb, d=d, dff=dff, tf=tf, halo=halo, nt=nt),
        out_shape=jax.ShapeDtypeStruct((n, d), F32),
        grid=(nt,),
        in_specs=in_specs,
        out_specs=pl.BlockSpec((tm, d), row),
        scratch_shapes=[pltpu.VMEM((tm, dff), BF16)],
        compiler_params=_cparams(("parallel",)),
        name="ffn_ctx" if halo else "ffn",
    )(*args)


def _gate_weights(wa, wx):
    nd, heads, hd, _ = wa.shape
    per = RG_GATE // hd
    ncb = heads // per

    def bd(w):
        w = w.reshape(nd, ncb, per, hd, hd)
        eye = jnp.eye(per, dtype=w.dtype)
        return jnp.einsum('dcpij,pq->dcpiqj', w, eye).reshape(nd, ncb, RG_GATE, RG_GATE)

    return (0.5 * jnp.concatenate([bd(wa), bd(wx)], axis=-1)).astype(BF16)


def _forward(x, c, ctx, c_ctx, w_ada, b_ada, g_pre_mix, g_post_mix, g_pre_ffn, g_post_ffn, w_in,
             rg_conv_w, rg_conv_b, rg_wa, rg_ba, rg_wx, rg_bx, rg_lam, hg_lb_logits, hg_out_norm,
             w_proj_rg, w_proj_hg, w_out, ffn_w_up, ffn_conv_w, ffn_conv_b, ffn_w_down, *, grid_w):
    nb, seq, d = x.shape
    ctx_len = ctx.shape[1]
    depth = w_in.shape[0]
    rows_g = seq // grid_w
    dff = ffn_w_down.shape[1]
    tile = grid_w * nb
    heads = d // HG_EXPAND
    ncb = d // RG_COLS
    tf = 256 if dff % 256 == 0 else dff

    p = jax.nn.softmax(hg_lb_logits.astype(F32), axis=0)
    cum = jnp.cumsum(p, axis=0)
    lb_all = cum - cum[0:1]
    hg_par = jnp.stack([lb_all, 1.0 - lb_all, jnp.zeros_like(lb_all), hg_out_norm], axis=1)
    hg_par = jnp.pad(hg_par, ((0, 0), (0, 4), (0, 0)))

    w_ada_b, w_in_b = w_ada.astype(BF16), w_in.astype(BF16)
    wr_b, wh_b, wo_b = w_proj_rg.astype(BF16), w_proj_hg.astype(BF16), w_out.astype(BF16)
    wup_b, wdn_b = ffn_w_up.astype(BF16), ffn_w_down.astype(BF16)

    cc = jnp.concatenate([c, jnp.broadcast_to(c_ctx[None], (nb, d))], axis=0)
    mod_all = _ada_mod(cc, w_ada_b, b_ada).reshape(depth, 2, nb, N_MOD * d)

    xl = jnp.swapaxes(x, 0, 1).reshape(seq * nb, d)
    xc = jnp.swapaxes(ctx, 0, 1).reshape(ctx_len * nb, d)

    for l in range(depth):
        need_ctx = l < depth - 1
        mod_l, mod_c = mod_all[l, 0], mod_all[l, 1]
        gpm = g_pre_mix[l].reshape(1, d)
        gpo = g_post_mix[l].reshape(1, d)
        wg = _gate_weights(rg_wa[l], rg_wx[l])
        bg = 0.5 * jnp.concatenate([rg_ba[l].reshape(2, d // RG_GATE, 1, RG_GATE),
                                    rg_bx[l].reshape(2, d // RG_GATE, 1, RG_GATE)], axis=-1)
        rg_kw = dict(conv_w=rg_conv_w[l], conv_b=rg_conv_b[l].reshape(1, d), nb=nb, tt=grid_w)

        st_rg = [jnp.zeros((ncb, nb, RG_COLS), F32)] * 2
        st_hg = [jnp.zeros((nb, heads, HG_EXPAND, HG_EXPAND), F32)] * 2
        ctx_out = None
        for xs, mod_s, steps_s, latent in ((xc, mod_c, ctx_len, False), (xl, mod_l, seq, True)):
            if latent:
                shape4 = jax.ShapeDtypeStruct((grid_w, rows_g, nb, 4 * d), BF16)
                blk = lambda w, col: pl.BlockSpec((grid_w, None, nb, w), col)
                spec_of = lambda tile_of: blk(GROUPS_PER_STEP * d, lambda i, j: (0, tile_of(i), 0, j))
                gg_of = lambda tile_of: blk(RG_COLS, lambda i, j: (0, tile_of(i), 0, j))
                tile_spec = lambda gidx: blk(d, lambda i: (0, i, 0, gidx))
            else:
                shape4 = jax.ShapeDtypeStruct((steps_s, nb, 4 * d), BF16)
                blk = lambda w, col: pl.BlockSpec((grid_w, nb, w), col)
                spec_of = lambda tile_of: blk(GROUPS_PER_STEP * d, lambda i, j: (tile_of(i), 0, j))
                gg_of = lambda tile_of: blk(RG_COLS, lambda i, j: (tile_of(i), 0, j))
                tile_spec = lambda gidx: blk(d, lambda i: (i, 0, gidx))
            zx, hx = _inproj(xs, mod_s, gpm, w_in_b[l], 0, tile, nb)
            za, hf, st_rg[0] = _inproj_rg(
                hx, w_in_b[l], (1, 2, 3, 4), shape4, spec_of, zx, None, None, None,
                wg=wg[0], bg=bg[0], lam=rg_lam[l, 0:1], h0=st_rg[0], reverse=False, **rg_kw)
            zb, yrg, st_rg[1] = _inproj_rg(
                hx, w_in_b[l], (5, 6, 7, 8), shape4, spec_of, zx, hf, za, gg_of,
                wg=wg[1], bg=bg[1], lam=rg_lam[l, 1:2], h0=st_rg[1], reverse=True, **rg_kw)
            za3 = za.reshape(steps_s, nb, 4 * d)
            zb3 = zb.reshape(steps_s, nb, 4 * d)
            of, st_hg[0] = _hg_dir(za3, zb3, None, hg_par[l], st_hg[0], reverse=False, d=d)
            yhg, st_hg[1] = _hg_dir(za3, zb3, of, hg_par[l], st_hg[1], reverse=True, d=d)
            if latent or need_ctx:
                yhg = yhg.reshape(shape4.shape[:-1] + (d,))
                out = _merge(yrg, yhg, zb, tile_spec, xs, mod_s, gpo, wr_b[l], wh_b[l], wo_b[l],
                             tm=tile, nb=nb)
                if latent:
                    xl = out
                else:
                    ctx_out = out
        ffn = functools.partial(_ffn, g1=g_pre_ffn[l].reshape(1, d), g2=g_post_ffn[l].reshape(1, d),
                                w_up=wup_b[l], cw=ffn_conv_w[l], cb=ffn_conv_b[l].reshape(1, dff),
                                w_down=wdn_b[l], tm=tile, tf=tf, nb=nb)
        xl = ffn(xl, mod_l, halo=False)
        if need_ctx:
            xc = ffn(ctx_out, mod_c, halo=True)

    return jnp.swapaxes(xl.reshape(seq, nb, d), 0, 1)


def kernel(x, c, ctx, c_ctx, w_ada, b_ada, g_pre_mix, g_post_mix, g_pre_ffn, g_post_ffn, w_in, rg_conv_w, rg_conv_b, rg_wa, rg_ba, rg_wx, rg_bx, rg_lam, hg_lb_logits, hg_out_norm, w_proj_rg, w_proj_hg, w_out, ffn_w_up, ffn_conv_w, ffn_conv_b, ffn_w_down):
    return _forward(x, c, ctx, c_ctx, w_ada, b_ada, g_pre_mix, g_post_mix, g_pre_ffn, g_post_ffn,
                    w_in, rg_conv_w, rg_conv_b, rg_wa, rg_ba, rg_wx, rg_bx, rg_lam, hg_lb_logits,
                    hg_out_norm, w_proj_rg, w_proj_hg, w_out, ffn_w_up, ffn_conv_w, ffn_conv_b,
                    ffn_w_down, grid_w=GRID_W)
```

```python
import functools

import jax
import jax.numpy as jnp
from jax import lax
from jax.experimental import pallas as pl
from jax.experimental.pallas import tpu as pltpu

GRID_W = 64
RG_HEADS = 16
RG_C = 8.0
HG_EXPAND = 128
N_MOD = 6
EPS = 1e-6
RG_CONV_PAD_L = 1
RG_CONV_PAD_R = 2

HG_CHUNK = 128
HG_DIAG = 8
HG_DIAG_CLAMP = 20.0
HG_F_FLOOR = 1e-37
HG_PITCH = 24
RG_TINY = 1e-30
MM_COLS = 256
RG_GATE = 256
RG_COLS = 512
GROUPS_PER_STEP = 2
VMEM_LIMIT = 56 * 1024 * 1024

LOG2_E = 1.4426950408889634

F32 = jnp.float32
BF16 = jnp.bfloat16


def _cparams(sem):
    return pltpu.CompilerParams(dimension_semantics=sem, vmem_limit_bytes=VMEM_LIMIT)


def _sigmoid(x):
    return 0.5 + 0.5 * jnp.tanh(0.5 * x)


def _silu(x):
    return x * _sigmoid(x)


def _gelu_tanh(x):
    c = 0.7978845608028654
    return 0.5 * x * (1.0 + jnp.tanh(c * (x + 0.044715 * (x * x * x))))


def _rms(x, g):
    ms = jnp.mean(x * x, axis=-1, keepdims=True)
    return x * lax.rsqrt(ms + EPS) * g


def _per_batch(x, vec, nb):
    r, d = x.shape
    return x.reshape(r // nb, nb, d), vec[None]


def _norm_mod(x, g, shift, scale, nb):
    y = _rms(x, g)
    y3, sc = _per_batch(y, scale, nb)
    h = y3 * (1.0 + sc) + shift[None]
    return h.reshape(x.shape)


def _ada_kernel(c_ref, w_ref, b_ref, o_ref):
    s = _silu(c_ref[...]).astype(BF16)
    o_ref[0] = jnp.dot(s, w_ref[0], preferred_element_type=F32) + b_ref[0]


def _ada_mod(cc, w_ada, b_ada):
    depth, d, n = w_ada.shape
    tn = 1536 if n % 1536 == 0 else n
    return pl.pallas_call(
        _ada_kernel,
        out_shape=jax.ShapeDtypeStruct((depth, cc.shape[0], n), F32),
        grid=(depth, n // tn),
        in_specs=[pl.BlockSpec(cc.shape, lambda l, j: (0, 0)),
                  pl.BlockSpec((1, d, tn), lambda l, j: (l, 0, j)),
                  pl.BlockSpec((1, 1, tn), lambda l, j: (l, 0, j))],
        out_specs=pl.BlockSpec((1, cc.shape[0], tn), lambda l, j: (l, 0, j)),
        compiler_params=_cparams(("parallel", "parallel")),
        name="ada_mod",
    )(cc, w_ada, b_ada.reshape(depth, 1, n))


def _zero_from(v):
    u = lax.bitcast_convert_type(v, jnp.uint32)
    return lax.bitcast_convert_type((u >> 16) >> 16, F32)


def _rg_chunk(xe, bias, cw_ref, wg_ref, bg_ref, lam_ref, h, *, reverse, nb, steps, cols):
    rows = steps * nb
    xl = bias + xe[0:rows] * cw_ref[0:1, :]
    for j in range(1, 4):
        xl = xl + xe[j * nb:j * nb + rows] * cw_ref[j:j + 1, :]
    xb = xl.astype(BF16)
    tha, thx = [], []
    for s in range(cols // RG_GATE):
        th = jnp.tanh(jnp.dot(xb[:, s * RG_GATE:(s + 1) * RG_GATE], wg_ref[s],
                              preferred_element_type=F32) + bg_ref[s])
        tha.append(th[:, :RG_GATE])
        thx.append(th[:, RG_GATE:])
    half_rate = (-0.5 * RG_C) * jax.nn.softplus(-lam_ref[...])
    log_a = half_rate + half_rate * jnp.concatenate(tha, axis=1)
    ig = 0.5 + 0.5 * jnp.concatenate(thx, axis=1)
    a = jnp.exp(log_a)
    om = (1.0 + a * a) * jnp.tanh(-log_a)
    b = (om * lax.rsqrt(jnp.maximum(om, RG_TINY))) * (ig * xl)
    hs = [None] * steps
    for t in (range(steps - 1, -1, -1) if reverse else range(steps)):
        h = a[t * nb:(t + 1) * nb] * h + b[t * nb:(t + 1) * nb]
        hs[t] = h
    return h, jnp.concatenate(hs, axis=0)


def _inproj_kernel(*refs, nb, d, rg, reverse, tt, nt):
    if not rg:
        x_ref, mod_ref, g_ref, w_ref, o_ref, h_ref = refs
        h = _norm_mod(x_ref[...], g_ref[...], mod_ref[:, 0:d], mod_ref[:, d:2 * d], nb).astype(BF16)
        h_ref[...] = h
        res = jnp.dot(h, w_ref[...], preferred_element_type=F32)
        o_ref[...] = res.reshape(o_ref.shape).astype(o_ref.dtype)
        return
    gps = GROUPS_PER_STEP
    h_ref, w_refs, rest = refs[0], refs[1:1 + gps], refs[1 + gps:]
    if reverse:
        (zx_ref, zxp_ref, zxn_ref, cw_ref, cb_ref, wg_ref, bg_ref,
         lam_ref, h0_ref, hf_ref, gg_ref, o_ref, y_ref, hfin_ref, hst_scr) = rest
    else:
        (zx_ref, zxp_ref, zxn_ref, cw_ref, cb_ref, wg_ref, bg_ref,
         lam_ref, h0_ref, o_ref, y_ref, hfin_ref, hst_scr) = rest
    i = pl.program_id(0)
    j = pl.program_id(1)

    @pl.when(i == 0)
    def _():
        hst_scr[j] = h0_ref[j]

    tile = (nt - 1 - i) if reverse else i
    cols = y_ref.shape[1]
    ppg = d // MM_COLS
    npiece = gps * ppg
    steps = tt // npiece
    sub = steps * nb
    oshape = o_ref.shape[:-1] + (MM_COLS,)
    prev = jnp.where(tile > 0, zxp_ref[...].astype(F32), 0.0)
    nxt = jnp.where(tile < nt - 1, zxn_ref[...].astype(F32), 0.0)
    h = hst_scr[j]
    bias = cb_ref[...]
    for n in range(npiece):
        w_ref = w_refs[n // ppg]
        res = jnp.dot(h_ref[...], w_ref[:, (n % ppg) * MM_COLS:(n % ppg + 1) * MM_COLS],
                      preferred_element_type=F32)
        o_ref[..., n * MM_COLS:(n + 1) * MM_COLS] = res.reshape(oshape).astype(o_ref.dtype)
        c = (npiece - 1 - n) if reverse else n
        r0 = c * sub
        lo = prev if c == 0 else zx_ref[r0 - nb:r0, :].astype(F32)
        hi = nxt if c == npiece - 1 else zx_ref[r0 + sub:r0 + sub + 2 * nb, :].astype(F32)
        xe = jnp.concatenate([lo, zx_ref[r0:r0 + sub, :].astype(F32), hi], axis=0)
        h, hs = _rg_chunk(xe, bias, cw_ref, wg_ref, bg_ref, lam_ref, h, reverse=reverse, nb=nb,
                          steps=steps, cols=cols)
        if reverse:
            gate = _gelu_tanh(gg_ref[...].reshape(tt * nb, cols)[r0:r0 + sub, :].astype(F32))
            y_ref[r0:r0 + sub, :] = (gate * (hf_ref[r0:r0 + sub, :] + hs)).astype(y_ref.dtype)
        else:
            y_ref[r0:r0 + sub, :] = hs
        bias = cb_ref[...] + _zero_from(res[248:256, MM_COLS - 128:MM_COLS])[0:1, 0:1]
    hst_scr[j] = h
    hfin_ref[j] = h


def _group_map(col_groups):
    def wmap(i, j):
        idx = jnp.int32(col_groups[0])
        for k in range(1, len(col_groups)):
            idx = jnp.where(j == k, jnp.int32(col_groups[k]), idx)
        return (0, idx)
    return wmap


def _inproj(x, mod, g, w, group, tm, nb):
    n, d = x.shape
    row = lambda i: (i, 0)
    return pl.pallas_call(
        functools.partial(_inproj_kernel, nb=nb, d=d, rg=False, reverse=False, tt=0, nt=0),
        out_shape=(jax.ShapeDtypeStruct((n, d), BF16), jax.ShapeDtypeStruct((n, d), BF16)),
        grid=(n // tm,),
        in_specs=[pl.BlockSpec((tm, d), row),
                  pl.BlockSpec(mod.shape, lambda i: (0, 0)),
                  pl.BlockSpec((1, d), lambda i: (0, 0)),
                  pl.BlockSpec((d, d), lambda i: (0, group))],
        out_specs=(pl.BlockSpec((tm, d), row), pl.BlockSpec((tm, d), row)),
        compiler_params=_cparams(("parallel",)),
        name="inproj",
    )(x, mod, g, w)


def _inproj_rg(hx, w, col_groups, out_shape, out_spec_of, zx, hf, zgg, gg_spec_of, conv_w,
               conv_b, wg, bg, lam, h0, *, reverse, nb, tt):
    n, d = hx.shape
    rows = tt * nb
    nt = n // rows
    ncb = d // RG_COLS
    gps = GROUPS_PER_STEP
    nsb = RG_COLS // RG_GATE
    assert len(col_groups) == ncb * gps and tt % (gps * d // MM_COLS) == 0
    ppb = rows // nb
    npb = rows // (2 * nb)
    n_next_blocks = n // (2 * nb)

    def tile_of(i):
        return (nt - 1 - i) if reverse else i

    in_specs = [pl.BlockSpec((rows, d), lambda i, j: (tile_of(i), 0))]
    in_specs += [pl.BlockSpec((d, d), _group_map(col_groups[g::gps])) for g in range(gps)]
    in_specs += [
        pl.BlockSpec((rows, RG_COLS), lambda i, j: (tile_of(i), j)),
        pl.BlockSpec((nb, RG_COLS), lambda i, j: (jnp.maximum(tile_of(i) * ppb - 1, 0), j)),
        pl.BlockSpec((2 * nb, RG_COLS),
                     lambda i, j: (jnp.minimum((tile_of(i) + 1) * npb, n_next_blocks - 1), j)),
        pl.BlockSpec((4, RG_COLS), lambda i, j: (0, j)),
        pl.BlockSpec((1, RG_COLS), lambda i, j: (0, j)),
        pl.BlockSpec((nsb, RG_GATE, 2 * RG_GATE), lambda i, j: (j, 0, 0)),
        pl.BlockSpec((nsb, 1, 2 * RG_GATE), lambda i, j: (j, 0, 0)),
        pl.BlockSpec((1, RG_COLS), lambda i, j: (0, j)),
        pl.BlockSpec((ncb, nb, RG_COLS), lambda i, j: (0, 0, 0)),
    ]
    args = [hx] + [w] * gps + [zx, zx, zx, conv_w, conv_b, wg, bg, lam, h0]
    if reverse:
        in_specs += [pl.BlockSpec((rows, RG_COLS), lambda i, j: (tile_of(i), j)), gg_spec_of(tile_of)]
        args += [hf, zgg]
    return pl.pallas_call(
        functools.partial(_inproj_kernel, nb=nb, d=d, rg=True, reverse=reverse, tt=tt, nt=nt),
        out_shape=(out_shape, jax.ShapeDtypeStruct((n, d), BF16 if reverse else F32),
                   jax.ShapeDtypeStruct((ncb, nb, RG_COLS), F32)),
        grid=(nt, ncb),
        in_specs=in_specs,
        out_specs=(out_spec_of(tile_of),
                   pl.BlockSpec((rows, RG_COLS), lambda i, j: (tile_of(i), j)),
                   pl.BlockSpec((ncb, nb, RG_COLS), lambda i, j: (0, 0, 0))),
        scratch_shapes=[pltpu.VMEM((ncb, nb, RG_COLS), F32)],
        compiler_params=_cparams(("arbitrary", "arbitrary")),
        name="inproj_rg_bwd" if reverse else "inproj_rg_fwd",
    )(*args)


def _cumsum_time(x, reverse, block):
    c = x.shape[0]
    x4 = x.reshape((c // block, block) + x.shape[1:])
    cols = [None] * block
    order = range(block - 1, -1, -1) if reverse else range(block)
    run = None
    for i in order:
        run = x4[:, i] if run is None else run + x4[:, i]
        cols[i] = run
    return jnp.stack(cols, axis=1).reshape(x.shape)


def _cumsum_chunk(x, reverse):
    c = x.shape[0]
    blk = 8
    x4 = _cumsum_time(x, reverse, blk).reshape((c // blk, blk) + x.shape[1:])
    nblk = c // blk
    tot = x4[:, 0] if reverse else x4[:, blk - 1]
    offs = [None] * nblk
    order = range(nblk - 1, -1, -1) if reverse else range(nblk)
    run = None
    for i in order:
        offs[i] = run
        run = tot[i] if run is None else run + tot[i]
    first = nblk - 1 if reverse else 0
    parts = [x4[i] if i == first else x4[i] + offs[i][None] for i in range(nblk)]
    return jnp.stack(parts, axis=0).reshape(x.shape)


def _pivot_time(c, block, p):
    c4 = c.reshape((c.shape[0] // block, block) + c.shape[1:])
    return jnp.broadcast_to(c4[:, p:p + 1], c4.shape).reshape(c.shape)


def _hg_levels(c):
    out, m = [], c // 2
    while m >= HG_DIAG:
        out.append(m)
        m //= 2
    return out


def _hg_kernel(*refs, reverse, final):
    if final:
        (zq_ref, zf_ref, v_ref, of_ref, og_ref, par_ref, s0_ref, y_ref, sout_ref,
         st_scr, tr_scr, o_scr, mm_scr) = refs
    else:
        (zq_ref, zf_ref, v_ref, par_ref, s0_ref, y_ref, sout_ref, st_scr, tr_scr, o_scr,
         mm_scr) = refs
    j = pl.program_id(1)
    nj = pl.num_programs(1)
    c, nb, kd = zq_ref.shape

    @pl.when(j == 0)
    def _():
        st_scr[...] = s0_ref[...]

    zq = zq_ref[...].astype(F32)
    zf = zf_ref[...].astype(F32)
    lb = par_ref[0:1, :][None]
    oml = par_ref[1:2, :][None]

    q = _silu(zq)
    e = jnp.exp(-jnp.abs(zf))
    r = 1.0 / (1.0 + e)
    pos = zf >= 0.0
    f = lb + oml * (jnp.where(pos, 1.0, e) * r)
    k = oml * (jnp.where(pos, e, 1.0) * r)
    g = jnp.log2(jnp.maximum(f, HG_F_FLOOR))
    cs = _cumsum_chunk(g, reverse)
    cd = _cumsum_time(jnp.maximum(g, -HG_DIAG_CLAMP * LOG2_E), reverse, HG_DIAG)
    ctot = cs[0] if reverse else cs[c - 1]
    decay = jnp.exp2(ctot)

    pad = jnp.zeros((c, HG_PITCH - nb, kd), F32)
    for n, arr in enumerate((q, k, cs, cd, v_ref[...].astype(F32))):
        tr_scr[n] = jnp.concatenate([arr, pad], axis=1).reshape(c * HG_PITCH, kd)

    ti = lax.broadcasted_iota(jnp.int32, (c, c), 0)
    si = lax.broadcasted_iota(jnp.int32, (c, c), 1)
    lev = ti ^ si
    valid = (ti <= si) if reverse else (ti >= si)
    nt_dims = (((1,), (1,)), ((), ()))
    tn_dims = (((0,), (0,)), ((), ()))
    levels = _hg_levels(c)

    for b in range(nb):
        rows = pl.ds(b, c, stride=HG_PITCH)
        q2, k2, cs2, cd2 = (tr_scr[n, rows, :] for n in range(4))
        qb = q2.astype(BF16)
        kb = k2.astype(BF16)
        mm_scr[0, b] = qb * jnp.exp2(cs2).astype(BF16)
        mm_scr[1, b] = kb * jnp.exp2(ctot[b:b + 1, :] - cs2).astype(BF16)
        mm_scr[2, b] = tr_scr[4, rows, :].astype(BF16)
        cdd = cd2 - _pivot_time(cd2, HG_DIAG, HG_DIAG // 2 if reverse else HG_DIAG // 2 - 1)
        a = lax.dot_general(qb * jnp.exp2(cdd).astype(BF16), kb * jnp.exp2(-cdd).astype(BF16),
                            nt_dims, preferred_element_type=F32)
        for m in levels[::-1]:
            piv = m if reverse else m - 1
            el = jnp.exp2(-jnp.abs((cs2 - _pivot_time(cs2, 2 * m, piv)).astype(BF16)))
            if 2 * m == c:
                tq = slice(0, m) if reverse else slice(m, c)
                sk = slice(m, c) if reverse else slice(0, m)
                blk = lax.dot_general(qb[tq] * el[tq], kb[sk] * el[sk], nt_dims,
                                      preferred_element_type=F32)
                zero = jnp.zeros((m, m), F32)
                rows = jnp.concatenate([zero, blk] if reverse else [blk, zero], axis=1)
                p = jnp.concatenate([rows, jnp.zeros((m, c), F32)] if reverse
                                    else [jnp.zeros((m, c), F32), rows], axis=0)
            else:
                p = lax.dot_general(qb * el, kb * el, nt_dims, preferred_element_type=F32)
            a = jnp.where(lev < m, a, p)
        mm_scr[3, b] = jnp.where(valid, a, 0.0).astype(BF16)

    decay_t = jnp.transpose(decay)
    for b in range(nb):
        st = st_scr[b]
        vb = mm_scr[2, b]
        o = jnp.dot(mm_scr[0, b], st.astype(BF16), preferred_element_type=F32)
        o = o + jnp.dot(mm_scr[3, b], vb, preferred_element_type=F32)
        st_scr[b] = st * decay_t[:, b:b + 1] + lax.dot_general(mm_scr[1, b], vb, tn_dims,
                                                               preferred_element_type=F32)
        o_scr[pl.ds(b, c, stride=HG_PITCH), :] = o

    o = o_scr[...].reshape(c, HG_PITCH, kd)[:, 0:nb, :]
    if final:
        gain = par_ref[3:4, :][None]
        y = _rms(o + of_ref[...], gain) * _silu(og_ref[...].astype(F32))
        y_ref[...] = y.astype(y_ref.dtype)
    else:
        y_ref[...] = o

    @pl.when(j == nj - 1)
    def _():
        sout_ref[...] = st_scr[...]


def _hg_dir(za, zb, of, par, s0, *, reverse, d):
    t, nb, _ = za.shape
    kd = HG_EXPAND
    heads = d // kd
    nj = t // HG_CHUNK
    final = reverse
    assert nb <= HG_PITCH

    def blk(j):
        return (nj - 1 - j) if reverse else j

    def zspec(gidx):
        return pl.BlockSpec((HG_CHUNK, nb, kd), lambda h, j: (blk(j), 0, gidx * heads + h))

    hspec = pl.BlockSpec((HG_CHUNK, nb, kd), lambda h, j: (blk(j), 0, h))
    sspec = pl.BlockSpec((nb, None, kd, kd), lambda h, j: (0, h, 0, 0))
    in_specs = [zspec(1), zspec(3 if reverse else 2), zspec(0)]
    args = [za, za, zb]
    scratch = [pltpu.VMEM((nb, kd, kd), F32), pltpu.VMEM((5, HG_CHUNK * HG_PITCH, kd), F32),
               pltpu.VMEM((HG_CHUNK * HG_PITCH, kd), F32),
               pltpu.VMEM((4, nb, HG_CHUNK, kd), BF16)]
    if final:
        in_specs += [hspec, zspec(1)]
        args += [of, zb]
    in_specs += [pl.BlockSpec((par.shape[0], kd), lambda h, j: (0, h)), sspec]
    args += [par, s0]
    return pl.pallas_call(
        functools.partial(_hg_kernel, reverse=reverse, final=final),
        out_shape=(jax.ShapeDtypeStruct((t, nb, d), BF16 if final else F32),
                   jax.ShapeDtypeStruct(s0.shape, F32)),
        grid=(heads, nj),
        in_specs=in_specs,
        out_specs=(hspec, sspec),
        scratch_shapes=scratch,
        compiler_params=_cparams(("parallel", "arbitrary")),
        name="hg_bwd" if reverse else "hg_fwd",
    )(*args)


def _merge_kernel(yrg_ref, yhg_ref, ga_ref, gb_ref, x_ref, mod_ref, g_ref, wr_ref, wh_ref, wo_ref,
                  o_ref, *, nb, d):
    tm = x_ref.shape[0]
    p_rg = jnp.dot(yrg_ref[...], wr_ref[...], preferred_element_type=F32)
    p_hg = jnp.dot(yhg_ref[...].reshape(tm, d), wh_ref[...], preferred_element_type=F32)
    ga = ga_ref[...].reshape(tm, d).astype(F32)
    gb = gb_ref[...].reshape(tm, d).astype(F32)
    m = _sigmoid(ga) * p_rg + _sigmoid(gb) * p_hg
    y = jnp.dot(m.astype(BF16), wo_ref[...], preferred_element_type=F32)
    n3, gate = _per_batch(_rms(y, g_ref[...]), mod_ref[:, 2 * d:3 * d], nb)
    o_ref[...] = x_ref[...] + (gate * n3).reshape(tm, d)


def _merge(yrg, yhg, zb, tile_spec, x, mod, g, wr, wh, wo, *, tm, nb):
    n, d = x.shape
    row = lambda i: (i, 0)
    const = lambda i: (0, 0)
    return pl.pallas_call(
        functools.partial(_merge_kernel, nb=nb, d=d),
        out_shape=jax.ShapeDtypeStruct((n, d), F32),
        grid=(n // tm,),
        in_specs=[pl.BlockSpec((tm, d), row), tile_spec(0), tile_spec(2), tile_spec(3),
                  pl.BlockSpec((tm, d), row), pl.BlockSpec(mod.shape, const),
                  pl.BlockSpec((1, d), const), pl.BlockSpec((d, d), const),
                  pl.BlockSpec((d, d), const), pl.BlockSpec((d, d), const)],
        out_specs=pl.BlockSpec((tm, d), row),
        compiler_params=_cparams(("parallel",)),
        name="merge",
    )(yrg, yhg, zb, zb, x, mod, g, wr, wh, wo)


def _ffn_kernel(*refs, nb, d, dff, tf, halo, nt):
    if halo:
        (x_ref, xp_ref, xn_ref, mod_ref, g1_ref, g2_ref, wu_ref, cw_ref, cb_ref, wd_ref,
         o_ref, act_scr) = refs
    else:
        (x_ref, mod_ref, g1_ref, g2_ref, wu_ref, cw_ref, cb_ref, wd_ref, o_ref, act_scr) = refs
    i = pl.program_id(0)
    tm = x_ref.shape[0]
    shift = mod_ref[:, 3 * d:4 * d]
    scale = mod_ref[:, 4 * d:5 * d]

    hm = _norm_mod(x_ref[...], g1_ref[...], shift, scale, nb).astype(BF16)
    if halo:
        hp = _norm_mod(xp_ref[...], g1_ref[...], shift, scale, nb)
        hn = _norm_mod(xn_ref[...], g1_ref[...], shift, scale, nb)
        hp = jnp.where(i > 0, hp, 0.0).astype(BF16)
        hn = jnp.where(i < nt - 1, hn, 0.0).astype(BF16)
        he = jnp.concatenate([hp, hm, hn], axis=0)

    for kb in range(dff // tf):
        gs = slice(kb * tf, (kb + 1) * tf)
        vs = slice(dff + kb * tf, dff + (kb + 1) * tf)
        if halo:
            ue = jnp.dot(he, wu_ref[:, gs], preferred_element_type=F32)
        else:
            u = jnp.dot(hm, wu_ref[:, gs], preferred_element_type=F32)
            z = jnp.zeros((nb, tf), F32)
            ue = jnp.concatenate([z, u, z], axis=0)
        gc = cb_ref[:, gs] + ue[0:tm] * cw_ref[0:1, gs] + ue[nb:nb + tm] * cw_ref[1:2, gs] \
            + ue[2 * nb:2 * nb + tm] * cw_ref[2:3, gs]
        uv = jnp.dot(hm, wu_ref[:, vs], preferred_element_type=F32)
        act_scr[:, gs] = (_silu(gc) * uv).astype(BF16)

    y = jnp.dot(act_scr[...], wd_ref[...], preferred_element_type=F32)
    n3, gate = _per_batch(_rms(y, g2_ref[...]), mod_ref[:, 5 * d:6 * d], nb)
    o_ref[...] = x_ref[...] + (gate * n3).reshape(tm, d)


def _ffn(x, mod, g1, g2, w_up, cw, cb, w_down, *, tm, tf, nb, halo):
    n, d = x.shape
    dff = w_down.shape[0]
    nt = n // tm
    hb = tm // nb
    row = lambda i: (i, 0)
    const = lambda i: (0, 0)

    def resident(shape):
        return pl.BlockSpec(shape, const, pipeline_mode=pl.Buffered(1))

    in_specs = [pl.BlockSpec((tm, d), row)]
    args = [x]
    if halo:
        in_specs += [pl.BlockSpec((nb, d), lambda i: (jnp.maximum(i * hb - 1, 0), 0)),
                     pl.BlockSpec((nb, d), lambda i: (jnp.minimum((i + 1) * hb, n // nb - 1), 0))]
        args += [x, x]
    in_specs += [pl.BlockSpec(mod.shape, const), pl.BlockSpec((1, d), const),
                 pl.BlockSpec((1, d), const), resident(w_up.shape),
                 pl.BlockSpec(cw.shape, const), pl.BlockSpec(cb.shape, const),
                 resident(w_down.shape)]
    args += [mod, g1, g2, w_up, cw, cb, w_down]
    return pl.pallas_call(
        functools.partial(_ffn_kernel, nb=nb, d=d, dff=dff, tf=tf, halo=halo, nt=nt),
        out_shape=jax.ShapeDtypeStruct((n, d), F32),
        grid=(nt,),
        in_specs=in_specs,
        out_specs=pl.BlockSpec((tm, d), row),
        scratch_shapes=[pltpu.VMEM((tm, dff), BF16)],
        compiler_params=_cparams(("parallel",)),
        name="ffn_ctx" if halo else "ffn",
    )(*args)


def _gate_weights(wa, wx):
    nd, heads, hd, _ = wa.shape
    per = RG_GATE // hd
    ncb = heads // per

    def bd(w):
        w = w.reshape(nd, ncb, per, hd, hd)
        eye = jnp.eye(per, dtype=w.dtype)
        return jnp.einsum('dcpij,pq->dcpiqj', w, eye).reshape(nd, ncb, RG_GATE, RG_GATE)

    return (0.5 * jnp.concatenate([bd(wa), bd(wx)], axis=-1)).astype(BF16)


def _forward(x, c, ctx, c_ctx, w_ada, b_ada, g_pre_mix, g_post_mix, g_pre_ffn, g_post_ffn, w_in,
             rg_conv_w, rg_conv_b, rg_wa, rg_ba, rg_wx, rg_bx, rg_lam, hg_lb_logits, hg_out_norm,
             w_proj_rg, w_proj_hg, w_out, ffn_w_up, ffn_conv_w, ffn_conv_b, ffn_w_down, *, grid_w):
    nb, seq, d = x.shape
    ctx_len = ctx.shape[1]
    depth = w_in.shape[0]
    rows_g = seq // grid_w
    dff = ffn_w_down.shape[1]
    tile = grid_w * nb
    heads = d // HG_EXPAND
    ncb = d // RG_COLS
    tf = 256 if dff % 256 == 0 else dff

    p = jax.nn.softmax(hg_lb_logits.astype(F32), axis=0)
    cum = jnp.cumsum(p, axis=0)
    lb_all = cum - cum[0:1]
    hg_par = jnp.stack([lb_all, 1.0 - lb_all, jnp.zeros_like(lb_all), hg_out_norm], axis=1)
    hg_par = jnp.pad(hg_par, ((0, 0), (0, 4), (0, 0)))

    w_ada_b, w_in_b = w_ada.astype(BF16), w_in.astype(BF16)
    wr_b, wh_b, wo_b = w_proj_rg.astype(BF16), w_proj_hg.astype(BF16), w_out.astype(BF16)
    wup_b, wdn_b = ffn_w_up.astype(BF16), ffn_w_down.astype(BF16)

    cc = jnp.concatenate([c, jnp.broadcast_to(c_ctx[None], (nb, d))], axis=0)
    mod_all = _ada_mod(cc, w_ada_b, b_ada).reshape(depth, 2, nb, N_MOD * d)

    xl = jnp.swapaxes(x, 0, 1).reshape(seq * nb, d)
    xc = jnp.swapaxes(ctx, 0, 1).reshape(ctx_len * nb, d)

    for l in range(depth):
        need_ctx = l < depth - 1
        mod_l, mod_c = mod_all[l, 0], mod_all[l, 1]
        gpm = g_pre_mix[l].reshape(1, d)
        gpo = g_post_mix[l].reshape(1, d)
        wg = _gate_weights(rg_wa[l], rg_wx[l])
        bg = 0.5 * jnp.concatenate([rg_ba[l].reshape(2, d // RG_GATE, 1, RG_GATE),
                                    rg_bx[l].reshape(2, d // RG_GATE, 1, RG_GATE)], axis=-1)
        rg_kw = dict(conv_w=rg_conv_w[l], conv_b=rg_conv_b[l].reshape(1, d), nb=nb, tt=grid_w)

        st_rg = [jnp.zeros((ncb, nb, RG_COLS), F32)] * 2
        st_hg = [jnp.zeros((nb, heads, HG_EXPAND, HG_EXPAND), F32)] * 2
        ctx_out = None
        for xs, mod_s, steps_s, latent in ((xc, mod_c, ctx_len, False), (xl, mod_l, seq, True)):
            if latent:
                shape4 = jax.ShapeDtypeStruct((grid_w, rows_g, nb, 4 * d), BF16)
                blk = lambda w, col: pl.BlockSpec((grid_w, None, nb, w), col)
                spec_of = lambda tile_of: blk(GROUPS_PER_STEP * d, lambda i, j: (0, tile_of(i), 0, j))
                gg_of = lambda tile_of: blk(RG_COLS, lambda i, j: (0, tile_of(i), 0, j))
                tile_spec = lambda gidx: blk(d, lambda i: (0, i, 0, gidx))
            else:
                shape4 = jax.ShapeDtypeStruct((steps_s, nb, 4 * d), BF16)
                blk = lambda w, col: pl.BlockSpec((grid_w, nb, w), col)
                spec_of = lambda tile_of: blk(GROUPS_PER_STEP * d, lambda i, j: (tile_of(i), 0, j))
                gg_of = lambda tile_of: blk(RG_COLS, lambda i, j: (tile_of(i), 0, j))
                tile_spec = lambda gidx: blk(d, lambda i: (i, 0, gidx))
            zx, hx = _inproj(xs, mod_s, gpm, w_in_b[l], 0, tile, nb)
            za, hf, st_rg[0] = _inproj_rg(
                hx, w_in_b[l], (1, 2, 3, 4), shape4, spec_of, zx, None, None, None,
                wg=wg[0], bg=bg[0], lam=rg_lam[l, 0:1], h0=st_rg[0], reverse=False, **rg_kw)
            zb, yrg, st_rg[1] = _inproj_rg(
                hx, w_in_b[l], (5, 6, 7, 8), shape4, spec_of, zx, hf, za, gg_of,
                wg=wg[1], bg=bg[1], lam=rg_lam[l, 1:2], h0=st_rg[1], reverse=True, **rg_kw)
            za3 = za.reshape(steps_s, nb, 4 * d)
            zb3 = zb.reshape(steps_s, nb, 4 * d)
            of, st_hg[0] = _hg_dir(za3, zb3, None, hg_par[l], st_hg[0], reverse=False, d=d)
            yhg, st_hg[1] = _hg_dir(za3, zb3, of, hg_par[l], st_hg[1], reverse=True, d=d)
            if latent or need_ctx:
                yhg = yhg.reshape(shape4.shape[:-1] + (d,))
                out = _merge(yrg, yhg, zb, tile_spec, xs, mod_s, gpo, wr_b[l], wh_b[l], wo_b[l],
                             tm=tile, nb=nb)
                if latent:
                    xl = out
                else:
                    ctx_out = out
        ffn = functools.partial(_ffn, g1=g_pre_ffn[l].reshape(1, d), g2=g_post_ffn[l].reshape(1, d),
                                w_up=wup_b[l], cw=ffn_conv_w[l], cb=ffn_conv_b[l].reshape(1, dff),
                                w_down=wdn_b[l], tm=tile, tf=tf, nb=nb)
        xl = ffn(xl, mod_l, halo=False)
        if need_ctx:
            xc = ffn(ctx_out, mod_c, halo=True)

    return jnp.swapaxes(xl.reshape(seq, nb, d), 0, 1)


def kernel(x, c, ctx, c_ctx, w_ada, b_ada, g_pre_mix, g_post_mix, g_pre_ffn, g_post_ffn, w_in, rg_conv_w, rg_conv_b, rg_wa, rg_ba, rg_wx, rg_bx, rg_lam, hg_lb_logits, hg_out_norm, w_proj_rg, w_proj_hg, w_out, ffn_w_up, ffn_conv_w, ffn_conv_b, ffn_w_down):
    return _forward(x, c, ctx, c_ctx, w_ada, b_ada, g_pre_mix, g_post_mix, g_pre_ffn, g_post_ffn,
                    w_in, rg_conv_w, rg_conv_b, rg_wa, rg_ba, rg_wx, rg_bx, rg_lam, hg_lb_logits,
                    hg_out_norm, w_proj_rg, w_proj_hg, w_out, ffn_w_up, ffn_conv_w, ffn_conv_b,
                    ffn_w_down, grid_w=GRID_W)
```

```python
import functools

import jax
import jax.numpy as jnp
from jax import lax
from jax.experimental import pallas as pl
from jax.experimental.pallas import tpu as pltpu

GRID_W = 64
RG_HEADS = 16
RG_C = 8.0
HG_EXPAND = 128
N_MOD = 6
EPS = 1e-6
RG_CONV_PAD_L = 1
RG_CONV_PAD_R = 2

HG_CHUNK = 128
HG_DIAG = 8
HG_DIAG_CLAMP = 20.0
HG_F_FLOOR = 1e-37
HG_PITCH = 24
RG_TINY = 1e-30
MM_COLS = 256
RG_GATE = 256
RG_COLS = 512
GROUPS_PER_STEP = 2
VMEM_LIMIT = 56 * 1024 * 1024

LOG2_E = 1.4426950408889634

F32 = jnp.float32
BF16 = jnp.bfloat16


def _cparams(sem):
    return pltpu.CompilerParams(dimension_semantics=sem, vmem_limit_bytes=VMEM_LIMIT)


def _sigmoid(x):
    return 0.5 + 0.5 * jnp.tanh(0.5 * x)


def _silu(x):
    return x * _sigmoid(x)


def _gelu_tanh(x):
    c = 0.7978845608028654
    return 0.5 * x * (1.0 + jnp.tanh(c * (x + 0.044715 * (x * x * x))))


def _rms(x, g):
    ms = jnp.mean(x * x, axis=-1, keepdims=True)
    return x * lax.rsqrt(ms + EPS) * g


def _per_batch(x, vec, nb):
    r, d = x.shape
    return x.reshape(r // nb, nb, d), vec[None]


def _norm_mod(x, g, shift, scale, nb):
    y = _rms(x, g)
    y3, sc = _per_batch(y, scale, nb)
    h = y3 * (1.0 + sc) + shift[None]
    return h.reshape(x.shape)


def _ada_kernel(c_ref, w_ref, b_ref, o_ref):
    s = _silu(c_ref[...]).astype(BF16)
    o_ref[0] = jnp.dot(s, w_ref[0], preferred_element_type=F32) + b_ref[0]


def _ada_mod(cc, w_ada, b_ada):
    depth, d, n = w_ada.shape
    tn = 1536 if n % 1536 == 0 else n
    return pl.pallas_call(
        _ada_kernel,
        out_shape=jax.ShapeDtypeStruct((depth, cc.shape[0], n), F32),
        grid=(depth, n // tn),
        in_specs=[pl.BlockSpec(cc.shape, lambda l, j: (0, 0)),
                  pl.BlockSpec((1, d, tn), lambda l, j: (l, 0, j)),
                  pl.BlockSpec((1, 1, tn), lambda l, j: (l, 0, j))],
        out_specs=pl.BlockSpec((1, cc.shape[0], tn), lambda l, j: (l, 0, j)),
        compiler_params=_cparams(("parallel", "parallel")),
        name="ada_mod",
    )(cc, w_ada, b_ada.reshape(depth, 1, n))


def _zero_from(v):
    u = lax.bitcast_convert_type(v, jnp.uint32)
    return lax.bitcast_convert_type((u >> 16) >> 16, F32)


def _rg_chunk(xe, bias, cw_ref, wg_ref, bg_ref, lam_ref, h, *, reverse, nb, steps, cols):
    rows = steps * nb
    xl = bias + xe[0:rows] * cw_ref[0:1, :]
    for j in range(1, 4):
        xl = xl + xe[j * nb:j * nb + rows] * cw_ref[j:j + 1, :]
    xb = xl.astype(BF16)
    tha, thx = [], []
    for s in range(cols // RG_GATE):
        th = jnp.tanh(jnp.dot(xb[:, s * RG_GATE:(s + 1) * RG_GATE], wg_ref[s],
                              preferred_element_type=F32) + bg_ref[s])
        tha.append(th[:, :RG_GATE])
        thx.append(th[:, RG_GATE:])
    half_rate = (-0.5 * RG_C) * jax.nn.softplus(-lam_ref[...])
    log_a = half_rate + half_rate * jnp.concatenate(tha, axis=1)
    ig = 0.5 + 0.5 * jnp.concatenate(thx, axis=1)
    a = jnp.exp(log_a)
    om = (1.0 + a * a) * jnp.tanh(-log_a)
    b = (om * lax.rsqrt(jnp.maximum(om, RG_TINY))) * (ig * xl)
    hs = [None] * steps
    for t in (range(steps - 1, -1, -1) if reverse else range(steps)):
        h = a[t * nb:(t + 1) * nb] * h + b[t * nb:(t + 1) * nb]
        hs[t] = h
    return h, jnp.concatenate(hs, axis=0)


def _inproj_kernel(*refs, nb, d, rg, reverse, tt, nt):
    if not rg:
        x_ref, mod_ref, g_ref, w_ref, o_ref, h_ref = refs
        h = _norm_mod(x_ref[...], g_ref[...], mod_ref[:, 0:d], mod_ref[:, d:2 * d], nb).astype(BF16)
        h_ref[...] = h
        res = jnp.dot(h, w_ref[...], preferred_element_type=F32)
        o_ref[...] = res.reshape(o_ref.shape).astype(o_ref.dtype)
        return
    gps = GROUPS_PER_STEP
    h_ref, w_refs, rest = refs[0], refs[1:1 + gps], refs[1 + gps:]
    if reverse:
        (zx_ref, zxp_ref, zxn_ref, cw_ref, cb_ref, wg_ref, bg_ref,
         lam_ref, h0_ref, hf_ref, gg_ref, o_ref, y_ref, hfin_ref, hst_scr) = rest
    else:
        (zx_ref, zxp_ref, zxn_ref, cw_ref, cb_ref, wg_ref, bg_ref,
         lam_ref, h0_ref, o_ref, y_ref, hfin_ref, hst_scr) = rest
    i = pl.program_id(0)
    j = pl.program_id(1)

    @pl.when(i == 0)
    def _():
        hst_scr[j] = h0_ref[j]

    tile = (nt - 1 - i) if reverse else i
    cols = y_ref.shape[1]
    ppg = d // MM_COLS
    npiece = gps * ppg
    steps = tt // npiece
    sub = steps * nb
    oshape = o_ref.shape[:-1] + (MM_COLS,)
    prev = jnp.where(tile > 0, zxp_ref[...].astype(F32), 0.0)
    nxt = jnp.where(tile < nt - 1, zxn_ref[...].astype(F32), 0.0)
    h = hst_scr[j]
    bias = cb_ref[...]
    for n in range(npiece):
        w_ref = w_refs[n // ppg]
        res = jnp.dot(h_ref[...], w_ref[:, (n % ppg) * MM_COLS:(n % ppg + 1) * MM_COLS],
                      preferred_element_type=F32)
        o_ref[..., n * MM_COLS:(n + 1) * MM_COLS] = res.reshape(oshape).astype(o_ref.dtype)
        c = (npiece - 1 - n) if reverse else n
        r0 = c * sub
        lo = prev if c == 0 else zx_ref[r0 - nb:r0, :].astype(F32)
        hi = nxt if c == npiece - 1 else zx_ref[r0 + sub:r0 + sub + 2 * nb, :].astype(F32)
        xe = jnp.concatenate([lo, zx_ref[r0:r0 + sub, :].astype(F32), hi], axis=0)
        h, hs = _rg_chunk(xe, bias, cw_ref, wg_ref, bg_ref, lam_ref, h, reverse=reverse, nb=nb,
                          steps=steps, cols=cols)
        if reverse:
            gate = _gelu_tanh(gg_ref[...].reshape(tt * nb, cols)[r0:r0 + sub, :].astype(F32))
            y_ref[r0:r0 + sub, :] = (gate * (hf_ref[r0:r0 + sub, :] + hs)).astype(y_ref.dtype)
        else:
            y_ref[r0:r0 + sub, :] = hs
        bias = cb_ref[...] + _zero_from(res[248:256, MM_COLS - 128:MM_COLS])[0:1, 0:1]
    hst_scr[j] = h
    hfin_ref[j] = h


def _group_map(col_groups):
    def wmap(i, j):
        idx = jnp.int32(col_groups[0])
        for k in range(1, len(col_groups)):
            idx = jnp.where(j == k, jnp.int32(col_groups[k]), idx)
        return (0, idx)
    return wmap


def _inproj(x, mod, g, w, group, tm, nb):
    n, d = x.shape
    row = lambda i: (i, 0)
    return pl.pallas_call(
        functools.partial(_inproj_kernel, nb=nb, d=d, rg=False, reverse=False, tt=0, nt=0),
        out_shape=(jax.ShapeDtypeStruct((n, d), BF16), jax.ShapeDtypeStruct((n, d), BF16)),
        grid=(n // tm,),
        in_specs=[pl.BlockSpec((tm, d), row),
                  pl.BlockSpec(mod.shape, lambda i: (0, 0)),
                  pl.BlockSpec((1, d), lambda i: (0, 0)),
                  pl.BlockSpec((d, d), lambda i: (0, group))],
        out_specs=(pl.BlockSpec((tm, d), row), pl.BlockSpec((tm, d), row)),
        compiler_params=_cparams(("parallel",)),
        name="inproj",
    )(x, mod, g, w)


def _inproj_rg(hx, w, col_groups, out_shape, out_spec_of, zx, hf, zgg, gg_spec_of, conv_w,
               conv_b, wg, bg, lam, h0, *, reverse, nb, tt):
    n, d = hx.shape
    rows = tt * nb
    nt = n // rows
    ncb = d // RG_COLS
    gps = GROUPS_PER_STEP
    nsb = RG_COLS // RG_GATE
    assert len(col_groups) == ncb * gps and tt % (gps * d // MM_COLS) == 0
    ppb = rows // nb
    npb = rows // (2 * nb)
    n_next_blocks = n // (2 * nb)

    def tile_of(i):
        return (nt - 1 - i) if reverse else i

    in_specs = [pl.BlockSpec((rows, d), lambda i, j: (tile_of(i), 0))]
    in_specs += [pl.BlockSpec((d, d), _group_map(col_groups[g::gps])) for g in range(gps)]
    in_specs += [
        pl.BlockSpec((rows, RG_COLS), lambda i, j: (tile_of(i), j)),
        pl.BlockSpec((nb, RG_COLS), lambda i, j: (jnp.maximum(tile_of(i) * ppb - 1, 0), j)),
        pl.BlockSpec((2 * nb, RG_COLS),
                     lambda i, j: (jnp.minimum((tile_of(i) + 1) * npb, n_next_blocks - 1), j)),
        pl.BlockSpec((4, RG_COLS), lambda i, j: (0, j)),
        pl.BlockSpec((1, RG_COLS), lambda i, j: (0, j)),
        pl.BlockSpec((nsb, RG_GATE, 2 * RG_GATE), lambda i, j: (j, 0, 0)),
        pl.BlockSpec((nsb, 1, 2 * RG_GATE), lambda i, j: (j, 0, 0)),
        pl.BlockSpec((1, RG_COLS), lambda i, j: (0, j)),
        pl.BlockSpec((ncb, nb, RG_COLS), lambda i, j: (0, 0, 0)),
    ]
    args = [hx] + [w] * gps + [zx, zx, zx, conv_w, conv_b, wg, bg, lam, h0]
    if reverse:
        in_specs += [pl.BlockSpec((rows, RG_COLS), lambda i, j: (tile_of(i), j)), gg_spec_of(tile_of)]
        args += [hf, zgg]
    return pl.pallas_call(
        functools.partial(_inproj_kernel, nb=nb, d=d, rg=True, reverse=reverse, tt=tt, nt=nt),
        out_shape=(out_shape, jax.ShapeDtypeStruct((n, d), BF16 if reverse else F32),
                   jax.ShapeDtypeStruct((ncb, nb, RG_COLS), F32)),
        grid=(nt, ncb),
        in_specs=in_specs,
        out_specs=(out_spec_of(tile_of),
                   pl.BlockSpec((rows, RG_COLS), lambda i, j: (tile_of(i), j)),
                   pl.BlockSpec((ncb, nb, RG_COLS), lambda i, j: (0, 0, 0))),
        scratch_shapes=[pltpu.VMEM((ncb, nb, RG_COLS), F32)],
        compiler_params=_cparams(("arbitrary", "arbitrary")),
        name="inproj_rg_bwd" if reverse else "inproj_rg_fwd",
    )(*args)


def _cumsum_time(x, reverse, block):
    c = x.shape[0]
    x4 = x.reshape((c // block, block) + x.shape[1:])
    cols = [None] * block
    order = range(block - 1, -1, -1) if reverse else range(block)
    run = None
    for i in order:
        run = x4[:, i] if run is None else run + x4[:, i]
        cols[i] = run
    return jnp.stack(cols, axis=1).reshape(x.shape)


def _cumsum_chunk(x, reverse):
    c = x.shape[0]
    blk = 8
    x4 = _cumsum_time(x, reverse, blk).reshape((c // blk, blk) + x.shape[1:])
    nblk = c // blk
    tot = x4[:, 0] if reverse else x4[:, blk - 1]
    offs = [None] * nblk
    order = range(nblk - 1, -1, -1) if reverse else range(nblk)
    run = None
    for i in order:
        offs[i] = run
        run = tot[i] if run is None else run + tot[i]
    first = nblk - 1 if reverse else 0
    parts = [x4[i] if i == first else x4[i] + offs[i][None] for i in range(nblk)]
    return jnp.stack(parts, axis=0).reshape(x.shape)


def _pivot_time(c, block, p):
    c4 = c.reshape((c.shape[0] // block, block) + c.shape[1:])
    return jnp.broadcast_to(c4[:, p:p + 1], c4.shape).reshape(c.shape)


def _hg_levels(c):
    out, m = [], c // 2
    while m >= HG_DIAG:
        out.append(m)
        m //= 2
    return out


def _hg_kernel(*refs, reverse, final):
    if final:
        (zq_ref, zf_ref, v_ref, of_ref, og_ref, par_ref, s0_ref, y_ref, sout_ref,
         st_scr, tr_scr, o_scr, mm_scr) = refs
    else:
        (zq_ref, zf_ref, v_ref, par_ref, s0_ref, y_ref, sout_ref, st_scr, tr_scr, o_scr,
         mm_scr) = refs
    j = pl.program_id(1)
    nj = pl.num_programs(1)
    c, nb, kd = zq_ref.shape

    @pl.when(j == 0)
    def _():
        st_scr[...] = s0_ref[...]

    zq = zq_ref[...].astype(F32)
    zf = zf_ref[...].astype(F32)
    lb = par_ref[0:1, :][None]
    oml = par_ref[1:2, :][None]

    q = _silu(zq)
    e = jnp.exp(-jnp.abs(zf))
    r = 1.0 / (1.0 + e)
    pos = zf >= 0.0
    f = lb + oml * (jnp.where(pos, 1.0, e) * r)
    k = oml * (jnp.where(pos, e, 1.0) * r)
    g = jnp.log2(jnp.maximum(f, HG_F_FLOOR))
    cs = _cumsum_chunk(g, reverse)
    cd = _cumsum_time(jnp.maximum(g, -HG_DIAG_CLAMP * LOG2_E), reverse, HG_DIAG)
    ctot = cs[0] if reverse else cs[c - 1]
    decay = jnp.exp2(ctot)

    pad = jnp.zeros((c, HG_PITCH - nb, kd), F32)
    for n, arr in enumerate((q, k, cs, cd, v_ref[...].astype(F32))):
        tr_scr[n] = jnp.concatenate([arr, pad], axis=1).reshape(c * HG_PITCH, kd)

    ti = lax.broadcasted_iota(jnp.int32, (c, c), 0)
    si = lax.broadcasted_iota(jnp.int32, (c, c), 1)
    lev = ti ^ si
    valid = (ti <= si) if reverse else (ti >= si)
    nt_dims = (((1,), (1,)), ((), ()))
    tn_dims = (((0,), (0,)), ((), ()))
    levels = _hg_levels(c)

    for b in range(nb):
        rows = pl.ds(b, c, stride=HG_PITCH)
        q2, k2, cs2, cd2 = (tr_scr[n, rows, :] for n in range(4))
        qb = q2.astype(BF16)
        kb = k2.astype(BF16)
        mm_scr[0, b] = qb * jnp.exp2(cs2).astype(BF16)
        mm_scr[1, b] = kb * jnp.exp2(ctot[b:b + 1, :] - cs2).astype(BF16)
        mm_scr[2, b] = tr_scr[4, rows, :].astype(BF16)
        cdd = cd2 - _pivot_time(cd2, HG_DIAG, HG_DIAG // 2 if reverse else HG_DIAG // 2 - 1)
        a = lax.dot_general(qb * jnp.exp2(cdd).astype(BF16), kb * jnp.exp2(-cdd).astype(BF16),
                            nt_dims, preferred_element_type=F32)
        for m in levels[::-1]:
            piv = m if reverse else m - 1
            el = jnp.exp2(-jnp.abs((cs2 - _pivot_time(cs2, 2 * m, piv)).astype(BF16)))
            p = lax.dot_general(qb * el, kb * el, nt_dims, preferred_element_type=F32)
            a = jnp.where(lev < m, a, p)
        mm_scr[3, b] = jnp.where(valid, a, 0.0).astype(BF16)

    decay_t = jnp.transpose(decay)
    for b in range(nb):
        st = st_scr[b]
        vb = mm_scr[2, b]
        o = jnp.dot(jnp.concatenate([mm_scr[0, b], mm_scr[3, b]], axis=1),
                    jnp.concatenate([st.astype(BF16), vb], axis=0), preferred_element_type=F32)
        st_scr[b] = st * decay_t[:, b:b + 1] + lax.dot_general(mm_scr[1, b], vb, tn_dims,
                                                               preferred_element_type=F32)
        o_scr[pl.ds(b, c, stride=HG_PITCH), :] = o

    o = o_scr[...].reshape(c, HG_PITCH, kd)[:, 0:nb, :]
    if final:
        gain = par_ref[3:4, :][None]
        y = _rms(o + of_ref[...], gain) * _silu(og_ref[...].astype(F32))
        y_ref[...] = y.astype(y_ref.dtype)
    else:
        y_ref[...] = o

    @pl.when(j == nj - 1)
    def _():
        sout_ref[...] = st_scr[...]


def _hg_dir(za, zb, of, par, s0, *, reverse, d):
    t, nb, _ = za.shape
    kd = HG_EXPAND
    heads = d // kd
    nj = t // HG_CHUNK
    final = reverse
    assert nb <= HG_PITCH

    def blk(j):
        return (nj - 1 - j) if reverse else j

    def zspec(gidx):
        return pl.BlockSpec((HG_CHUNK, nb, kd), lambda h, j: (blk(j), 0, gidx * heads + h))

    hspec = pl.BlockSpec((HG_CHUNK, nb, kd), lambda h, j: (blk(j), 0, h))
    sspec = pl.BlockSpec((nb, None, kd, kd), lambda h, j: (0, h, 0, 0))
    in_specs = [zspec(1), zspec(3 if reverse else 2), zspec(0)]
    args = [za, za, zb]
    scratch = [pltpu.VMEM((nb, kd, kd), F32), pltpu.VMEM((5, HG_CHUNK * HG_PITCH, kd), F32),
               pltpu.VMEM((HG_CHUNK * HG_PITCH, kd), F32),
               pltpu.VMEM((4, nb, HG_CHUNK, kd), BF16)]
    if final:
        in_specs += [hspec, zspec(1)]
        args += [of, zb]
    in_specs += [pl.BlockSpec((par.shape[0], kd), lambda h, j: (0, h)), sspec]
    args += [par, s0]
    return pl.pallas_call(
        functools.partial(_hg_kernel, reverse=reverse, final=final),
        out_shape=(jax.ShapeDtypeStruct((t, nb, d), BF16 if final else F32),
                   jax.ShapeDtypeStruct(s0.shape, F32)),
        grid=(heads, nj),
        in_specs=in_specs,
        out_specs=(hspec, sspec),
        scratch_shapes=scratch,
        compiler_params=_cparams(("parallel", "arbitrary")),
        name="hg_bwd" if reverse else "hg_fwd",
    )(*args)


def _merge_kernel(yrg_ref, yhg_ref, ga_ref, gb_ref, x_ref, mod_ref, g_ref, wr_ref, wh_ref, wo_ref,
                  o_ref, *, nb, d):
    tm = x_ref.shape[0]
    p_rg = jnp.dot(yrg_ref[...], wr_ref[...], preferred_element_type=F32)
    p_hg = jnp.dot(yhg_ref[...].reshape(tm, d), wh_ref[...], preferred_element_type=F32)
    ga = ga_ref[...].reshape(tm, d).astype(F32)
    gb = gb_ref[...].reshape(tm, d).astype(F32)
    m = _sigmoid(ga) * p_rg + _sigmoid(gb) * p_hg
    y = jnp.dot(m.astype(BF16), wo_ref[...], preferred_element_type=F32)
    n3, gate = _per_batch(_rms(y, g_ref[...]), mod_ref[:, 2 * d:3 * d], nb)
    o_ref[...] = x_ref[...] + (gate * n3).reshape(tm, d)


def _merge(yrg, yhg, zb, tile_spec, x, mod, g, wr, wh, wo, *, tm, nb):
    n, d = x.shape
    row = lambda i: (i, 0)
    const = lambda i: (0, 0)
    return pl.pallas_call(
        functools.partial(_merge_kernel, nb=nb, d=d),
        out_shape=jax.ShapeDtypeStruct((n, d), F32),
        grid=(n // tm,),
        in_specs=[pl.BlockSpec((tm, d), row), tile_spec(0), tile_spec(2), tile_spec(3),
                  pl.BlockSpec((tm, d), row), pl.BlockSpec(mod.shape, const),
                  pl.BlockSpec((1, d), const), pl.BlockSpec((d, d), const),
                  pl.BlockSpec((d, d), const), pl.BlockSpec((d, d), const)],
        out_specs=pl.BlockSpec((tm, d), row),
        compiler_params=_cparams(("parallel",)),
        name="merge",
    )(yrg, yhg, zb, zb, x, mod, g, wr, wh, wo)


def _ffn_kernel(*refs, nb, d, dff, tf, halo, nt):
    if halo:
        (x_ref, xp_ref, xn_ref, mod_ref, g1_ref, g2_ref, wu_ref, cw_ref, cb_ref, wd_ref,
         o_ref, act_scr) = refs
    else:
        (x_ref, mod_ref, g1_ref, g2_ref, wu_ref, cw_ref, cb_ref, wd_ref, o_ref, act_scr) = refs
    i = pl.program_id(0)
    tm = x_ref.shape[0]
    shift = mod_ref[:, 3 * d:4 * d]
    scale = mod_ref[:, 4 * d:5 * d]

    hm = _norm_mod(x_ref[...], g1_ref[...], shift, scale, nb).astype(BF16)
    if halo:
        hp = _norm_mod(xp_ref[...], g1_ref[...], shift, scale, nb)
        hn = _norm_mod(xn_ref[...], g1_ref[...], shift, scale, nb)
        hp = jnp.where(i > 0, hp, 0.0).astype(BF16)
        hn = jnp.where(i < nt - 1, hn, 0.0).astype(BF16)
        he = jnp.concatenate([hp, hm, hn], axis=0)

    for kb in range(dff // tf):
        gs = slice(kb * tf, (kb + 1) * tf)
        vs = slice(dff + kb * tf, dff + (kb + 1) * tf)
        if halo:
            ue = jnp.dot(he, wu_ref[:, gs], preferred_element_type=F32)
        else:
            u = jnp.dot(hm, wu_ref[:, gs], preferred_element_type=F32)
            z = jnp.zeros((nb, tf), F32)
            ue = jnp.concatenate([z, u, z], axis=0)
        gc = cb_ref[:, gs] + ue[0:tm] * cw_ref[0:1, gs] + ue[nb:nb + tm] * cw_ref[1:2, gs] \
            + ue[2 * nb:2 * nb + tm] * cw_ref[2:3, gs]
        uv = jnp.dot(hm, wu_ref[:, vs], preferred_element_type=F32)
        act_scr[:, gs] = (_silu(gc) * uv).astype(BF16)

    y = jnp.dot(act_scr[...], wd_ref[...], preferred_element_type=F32)
    n3, gate = _per_batch(_rms(y, g2_ref[...]), mod_ref[:, 5 * d:6 * d], nb)
    o_ref[...] = x_ref[...] + (gate * n3).reshape(tm, d)


def _ffn(x, mod, g1, g2, w_up, cw, cb, w_down, *, tm, tf, nb, halo):
    n, d = x.shape
    dff = w_down.shape[0]
    nt = n // tm
    hb = tm // nb
    row = lambda i: (i, 0)
    const = lambda i: (0, 0)

    def resident(shape):
        return pl.BlockSpec(shape, const, pipeline_mode=pl.Buffered(1))

    in_specs = [pl.BlockSpec((tm, d), row)]
    args = [x]
    if halo:
        in_specs += [pl.BlockSpec((nb, d), lambda i: (jnp.maximum(i * hb - 1, 0), 0)),
                     pl.BlockSpec((nb, d), lambda i: (jnp.minimum((i + 1) * hb, n // nb - 1), 0))]
        args += [x, x]
    in_specs += [pl.BlockSpec(mod.shape, const), pl.BlockSpec((1, d), const),
                 pl.BlockSpec((1, d), const), resident(w_up.shape),
                 pl.BlockSpec(cw.shape, const), pl.BlockSpec(cb.shape, const),
                 resident(w_down.shape)]
    args += [mod, g1, g2, w_up, cw, cb, w_down]
    return pl.pallas_call(
        functools.partial(_ffn_kernel, nb=nb, d=d, dff=dff, tf=tf, halo=halo, nt=nt),
        out_shape=jax.ShapeDtypeStruct((n, d), F32),
        grid=(nt,),
        in_specs=in_specs,
        out_specs=pl.BlockSpec((tm, d), row),
        scratch_shapes=[pltpu.VMEM((tm, dff), BF16)],
        compiler_params=_cparams(("parallel",)),
        name="ffn_ctx" if halo else "ffn",
    )(*args)


def _gate_weights(wa, wx):
    nd, heads, hd, _ = wa.shape
    per = RG_GATE // hd
    ncb = heads // per

    def bd(w):
        w = w.reshape(nd, ncb, per, hd, hd)
        eye = jnp.eye(per, dtype=w.dtype)
        return jnp.einsum('dcpij,pq->dcpiqj', w, eye).reshape(nd, ncb, RG_GATE, RG_GATE)

    return (0.5 * jnp.concatenate([bd(wa), bd(wx)], axis=-1)).astype(BF16)


def _forward(x, c, ctx, c_ctx, w_ada, b_ada, g_pre_mix, g_post_mix, g_pre_ffn, g_post_ffn, w_in,
             rg_conv_w, rg_conv_b, rg_wa, rg_ba, rg_wx, rg_bx, rg_lam, hg_lb_logits, hg_out_norm,
             w_proj_rg, w_proj_hg, w_out, ffn_w_up, ffn_conv_w, ffn_conv_b, ffn_w_down, *, grid_w):
    nb, seq, d = x.shape
    ctx_len = ctx.shape[1]
    depth = w_in.shape[0]
    rows_g = seq // grid_w
    dff = ffn_w_down.shape[1]
    tile = grid_w * nb
    heads = d // HG_EXPAND
    ncb = d // RG_COLS
    tf = 256 if dff % 256 == 0 else dff

    p = jax.nn.softmax(hg_lb_logits.astype(F32), axis=0)
    cum = jnp.cumsum(p, axis=0)
    lb_all = cum - cum[0:1]
    hg_par = jnp.stack([lb_all, 1.0 - lb_all, jnp.zeros_like(lb_all), hg_out_norm], axis=1)
    hg_par = jnp.pad(hg_par, ((0, 0), (0, 4), (0, 0)))

    w_ada_b, w_in_b = w_ada.astype(BF16), w_in.astype(BF16)
    wr_b, wh_b, wo_b = w_proj_rg.astype(BF16), w_proj_hg.astype(BF16), w_out.astype(BF16)
    wup_b, wdn_b = ffn_w_up.astype(BF16), ffn_w_down.astype(BF16)

    cc = jnp.concatenate([c, jnp.broadcast_to(c_ctx[None], (nb, d))], axis=0)
    mod_all = _ada_mod(cc, w_ada_b, b_ada).reshape(depth, 2, nb, N_MOD * d)

    xl = jnp.swapaxes(x, 0, 1).reshape(seq * nb, d)
    xc = jnp.swapaxes(ctx, 0, 1).reshape(ctx_len * nb, d)

    for l in range(depth):
        need_ctx = l < depth - 1
        mod_l, mod_c = mod_all[l, 0], mod_all[l, 1]
        gpm = g_pre_mix[l].reshape(1, d)
        gpo = g_post_mix[l].reshape(1, d)
        wg = _gate_weights(rg_wa[l], rg_wx[l])
        bg = 0.5 * jnp.concatenate([rg_ba[l].reshape(2, d // RG_GATE, 1, RG_GATE),
                                    rg_bx[l].reshape(2, d // RG_GATE, 1, RG_GATE)], axis=-1)
        rg_kw = dict(conv_w=rg_conv_w[l], conv_b=rg_conv_b[l].reshape(1, d), nb=nb, tt=grid_w)

        st_rg = [jnp.zeros((ncb, nb, RG_COLS), F32)] * 2
        st_hg = [jnp.zeros((nb, heads, HG_EXPAND, HG_EXPAND), F32)] * 2
        ctx_out = None
        for xs, mod_s, steps_s, latent in ((xc, mod_c, ctx_len, False), (xl, mod_l, seq, True)):
            if latent:
                shape4 = jax.ShapeDtypeStruct((grid_w, rows_g, nb, 4 * d), BF16)
                blk = lambda w, col: pl.BlockSpec((grid_w, None, nb, w), col)
                spec_of = lambda tile_of: blk(GROUPS_PER_STEP * d, lambda i, j: (0, tile_of(i), 0, j))
                gg_of = lambda tile_of: blk(RG_COLS, lambda i, j: (0, tile_of(i), 0, j))
                tile_spec = lambda gidx: blk(d, lambda i: (0, i, 0, gidx))
            else:
                shape4 = jax.ShapeDtypeStruct((steps_s, nb, 4 * d), BF16)
                blk = lambda w, col: pl.BlockSpec((grid_w, nb, w), col)
                spec_of = lambda tile_of: blk(GROUPS_PER_STEP * d, lambda i, j: (tile_of(i), 0, j))
                gg_of = lambda tile_of: blk(RG_COLS, lambda i, j: (tile_of(i), 0, j))
                tile_spec = lambda gidx: blk(d, lambda i: (i, 0, gidx))
            zx, hx = _inproj(xs, mod_s, gpm, w_in_b[l], 0, tile, nb)
            za, hf, st_rg[0] = _inproj_rg(
                hx, w_in_b[l], (1, 2, 3, 4), shape4, spec_of, zx, None, None, None,
                wg=wg[0], bg=bg[0], lam=rg_lam[l, 0:1], h0=st_rg[0], reverse=False, **rg_kw)
            zb, yrg, st_rg[1] = _inproj_rg(
                hx, w_in_b[l], (5, 6, 7, 8), shape4, spec_of, zx, hf, za, gg_of,
                wg=wg[1], bg=bg[1], lam=rg_lam[l, 1:2], h0=st_rg[1], reverse=True, **rg_kw)
            za3 = za.reshape(steps_s, nb, 4 * d)
            zb3 = zb.reshape(steps_s, nb, 4 * d)
            of, st_hg[0] = _hg_dir(za3, zb3, None, hg_par[l], st_hg[0], reverse=False, d=d)
            yhg, st_hg[1] = _hg_dir(za3, zb3, of, hg_par[l], st_hg[1], reverse=True, d=d)
            if latent or need_ctx:
                yhg = yhg.reshape(shape4.shape[:-1] + (d,))
                out = _merge(yrg, yhg, zb, tile_spec, xs, mod_s, gpo, wr_b[l], wh_b[l], wo_b[l],
                             tm=tile, nb=nb)
                if latent:
                    xl = out
                else:
                    ctx_out = out
        ffn = functools.partial(_ffn, g1=g_pre_ffn[l].reshape(1, d), g2=g_post_ffn[l].reshape(1, d),
                                w_up=wup_b[l], cw=ffn_conv_w[l], cb=ffn_conv_b[l].reshape(1, dff),
                                w_down=wdn_b[l], tm=tile, tf=tf, nb=nb)
        xl = ffn(xl, mod_l, halo=False)
        if need_ctx:
            xc = ffn(ctx_out, mod_c, halo=True)

    return jnp.swapaxes(xl.reshape(seq, nb, d), 0, 1)


def kernel(x, c, ctx, c_ctx, w_ada, b_ada, g_pre_mix, g_post_mix, g_pre_ffn, g_post_ffn, w_in, rg_conv_w, rg_conv_b, rg_wa, rg_ba, rg_wx, rg_bx, rg_lam, hg_lb_logits, hg_out_norm, w_proj_rg, w_proj_hg, w_out, ffn_w_up, ffn_conv_w, ffn_conv_b, ffn_w_down):
    return _forward(x, c, ctx, c_ctx, w_ada, b_ada, g_pre_mix, g_post_mix, g_pre_ffn, g_post_ffn,
                    w_in, rg_conv_w, rg_conv_b, rg_wa, rg_ba, rg_wx, rg_bx, rg_lam, hg_lb_logits,
                    hg_out_norm, w_proj_rg, w_proj_hg, w_out, ffn_w_up, ffn_conv_w, ffn_conv_b,
                    ffn_w_down, grid_w=GRID_W)
```

```python
import functools

import jax
import jax.numpy as jnp
from jax import lax
from jax.experimental import pallas as pl
from jax.experimental.pallas import tpu as pltpu

GRID_W = 64
RG_HEADS = 16
RG_C = 8.0
HG_EXPAND = 128
N_MOD = 6
EPS = 1e-6
RG_CONV_PAD_L = 1
RG_CONV_PAD_R = 2

HG_CHUNK = 128
HG_DIAG = 8
HG_DIAG_CLAMP = 20.0
HG_F_FLOOR = 1e-37
HG_PITCH = 24
RG_TINY = 1e-30
MM_COLS = 256
RG_GATE = 256
RG_COLS = 512
GROUPS_PER_STEP = 2
VMEM_LIMIT = 56 * 1024 * 1024

LOG2_E = 1.4426950408889634

F32 = jnp.float32
BF16 = jnp.bfloat16


def _cparams(sem):
    return pltpu.CompilerParams(dimension_semantics=sem, vmem_limit_bytes=VMEM_LIMIT)


def _sigmoid(x):
    return 0.5 + 0.5 * jnp.tanh(0.5 * x)


def _silu(x):
    return x * _sigmoid(x)


def _gelu_tanh(x):
    c = 0.7978845608028654
    return 0.5 * x * (1.0 + jnp.tanh(c * (x + 0.044715 * (x * x * x))))


def _rms(x, g):
    ms = jnp.mean(x * x, axis=-1, keepdims=True)
    return x * lax.rsqrt(ms + EPS) * g


def _per_batch(x, vec, nb):
    r, d = x.shape
    return x.reshape(r // nb, nb, d), vec[None]


def _norm_mod(x, g, shift, scale, nb):
    y = _rms(x, g)
    y3, sc = _per_batch(y, scale, nb)
    h = y3 * (1.0 + sc) + shift[None]
    return h.reshape(x.shape)


def _ada_kernel(c_ref, w_ref, b_ref, o_ref):
    s = _silu(c_ref[...]).astype(BF16)
    o_ref[0] = jnp.dot(s, w_ref[0], preferred_element_type=F32) + b_ref[0]


def _ada_mod(cc, w_ada, b_ada):
    depth, d, n = w_ada.shape
    tn = 1536 if n % 1536 == 0 else n
    return pl.pallas_call(
        _ada_kernel,
        out_shape=jax.ShapeDtypeStruct((depth, cc.shape[0], n), F32),
        grid=(depth, n // tn),
        in_specs=[pl.BlockSpec(cc.shape, lambda l, j: (0, 0)),
                  pl.BlockSpec((1, d, tn), lambda l, j: (l, 0, j)),
                  pl.BlockSpec((1, 1, tn), lambda l, j: (l, 0, j))],
        out_specs=pl.BlockSpec((1, cc.shape[0], tn), lambda l, j: (l, 0, j)),
        compiler_params=_cparams(("parallel", "parallel")),
        name="ada_mod",
    )(cc, w_ada, b_ada.reshape(depth, 1, n))


def _zero_from(v):
    u = lax.bitcast_convert_type(v, jnp.uint32)
    return lax.bitcast_convert_type((u >> 16) >> 16, F32)


def _rg_conv(xe, bias, cw_ref, rows, nb):
    xl = bias + xe[0:rows] * cw_ref[0:1, :]
    for j in range(1, 4):
        xl = xl + xe[j * nb:j * nb + rows] * cw_ref[j:j + 1, :]
    return xl


def _rg_chunk(xl, wg_ref, bg_ref, lam_ref, h, *, reverse, nb, steps, cols):
    xb = xl.astype(BF16)
    tha, thx = [], []
    for s in range(cols // RG_GATE):
        th = jnp.tanh(jnp.dot(xb[:, s * RG_GATE:(s + 1) * RG_GATE], wg_ref[s],
                              preferred_element_type=F32) + bg_ref[s])
        tha.append(th[:, :RG_GATE])
        thx.append(th[:, RG_GATE:])
    half_rate = (-0.5 * RG_C) * jax.nn.softplus(-lam_ref[...])
    log_a = half_rate + half_rate * jnp.concatenate(tha, axis=1)
    ig = 0.5 + 0.5 * jnp.concatenate(thx, axis=1)
    a = jnp.exp(log_a)
    om = (1.0 + a * a) * jnp.tanh(-log_a)
    b = (om * lax.rsqrt(jnp.maximum(om, RG_TINY))) * (ig * xl)
    hs = [None] * steps
    for t in (range(steps - 1, -1, -1) if reverse else range(steps)):
        h = a[t * nb:(t + 1) * nb] * h + b[t * nb:(t + 1) * nb]
        hs[t] = h
    return h, jnp.concatenate(hs, axis=0)


def _inproj_kernel(*refs, nb, d, rg, reverse, tt, nt):
    if not rg:
        x_ref, mod_ref, g_ref, w_ref, o_ref, h_ref = refs
        h = _norm_mod(x_ref[...], g_ref[...], mod_ref[:, 0:d], mod_ref[:, d:2 * d], nb).astype(BF16)
        h_ref[...] = h
        res = jnp.dot(h, w_ref[...], preferred_element_type=F32)
        o_ref[...] = res.reshape(o_ref.shape).astype(o_ref.dtype)
        return
    gps = GROUPS_PER_STEP
    h_ref, w_refs, rest = refs[0], refs[1:1 + gps], refs[1 + gps:]
    if reverse:
        (xl_ref, cb_ref, wg_ref, bg_ref, lam_ref, h0_ref, hf_ref, gg_ref,
         o_ref, y_ref, hfin_ref, hst_scr) = rest
    else:
        (zx_ref, zxp_ref, zxn_ref, cw_ref, cb_ref, wg_ref, bg_ref,
         lam_ref, h0_ref, o_ref, y_ref, xl_ref, hfin_ref, hst_scr) = rest
    i = pl.program_id(0)
    j = pl.program_id(1)

    @pl.when(i == 0)
    def _():
        hst_scr[j] = h0_ref[j]

    tile = (nt - 1 - i) if reverse else i
    cols = y_ref.shape[1]
    ppg = d // MM_COLS
    npiece = gps * ppg
    steps = tt // npiece
    sub = steps * nb
    oshape = o_ref.shape[:-1] + (MM_COLS,)
    if not reverse:
        prev = jnp.where(tile > 0, zxp_ref[...].astype(F32), 0.0)
        nxt = jnp.where(tile < nt - 1, zxn_ref[...].astype(F32), 0.0)
    h = hst_scr[j]
    bias = cb_ref[...]
    for n in range(npiece):
        w_ref = w_refs[n // ppg]
        res = jnp.dot(h_ref[...], w_ref[:, (n % ppg) * MM_COLS:(n % ppg + 1) * MM_COLS],
                      preferred_element_type=F32)
        o_ref[..., n * MM_COLS:(n + 1) * MM_COLS] = res.reshape(oshape).astype(o_ref.dtype)
        c = (npiece - 1 - n) if reverse else n
        r0 = c * sub
        if reverse:
            xl = xl_ref[r0:r0 + sub, :].astype(F32) + (bias - cb_ref[...])
        else:
            lo = prev if c == 0 else zx_ref[r0 - nb:r0, :].astype(F32)
            hi = nxt if c == npiece - 1 else zx_ref[r0 + sub:r0 + sub + 2 * nb, :].astype(F32)
            xe = jnp.concatenate([lo, zx_ref[r0:r0 + sub, :].astype(F32), hi], axis=0)
            xl = _rg_conv(xe, bias, cw_ref, sub, nb)
            xl_ref[r0:r0 + sub, :] = xl.astype(xl_ref.dtype)
        h, hs = _rg_chunk(xl, wg_ref, bg_ref, lam_ref, h, reverse=reverse, nb=nb,
                          steps=steps, cols=cols)
        if reverse:
            gate = _gelu_tanh(gg_ref[...].reshape(tt * nb, cols)[r0:r0 + sub, :].astype(F32))
            y_ref[r0:r0 + sub, :] = (gate * (hf_ref[r0:r0 + sub, :] + hs)).astype(y_ref.dtype)
        else:
            y_ref[r0:r0 + sub, :] = hs
        bias = cb_ref[...] + _zero_from(res[248:256, MM_COLS - 128:MM_COLS])[0:1, 0:1]
    hst_scr[j] = h
    hfin_ref[j] = h


def _group_map(col_groups):
    def wmap(i, j):
        idx = jnp.int32(col_groups[0])
        for k in range(1, len(col_groups)):
            idx = jnp.where(j == k, jnp.int32(col_groups[k]), idx)
        return (0, idx)
    return wmap


def _inproj(x, mod, g, w, group, tm, nb):
    n, d = x.shape
    row = lambda i: (i, 0)
    return pl.pallas_call(
        functools.partial(_inproj_kernel, nb=nb, d=d, rg=False, reverse=False, tt=0, nt=0),
        out_shape=(jax.ShapeDtypeStruct((n, d), BF16), jax.ShapeDtypeStruct((n, d), BF16)),
        grid=(n // tm,),
        in_specs=[pl.BlockSpec((tm, d), row),
                  pl.BlockSpec(mod.shape, lambda i: (0, 0)),
                  pl.BlockSpec((1, d), lambda i: (0, 0)),
                  pl.BlockSpec((d, d), lambda i: (0, group))],
        out_specs=(pl.BlockSpec((tm, d), row), pl.BlockSpec((tm, d), row)),
        compiler_params=_cparams(("parallel",)),
        name="inproj",
    )(x, mod, g, w)


def _inproj_rg(hx, w, col_groups, out_shape, out_spec_of, zx, hf, zgg, gg_spec_of, conv_w,
               conv_b, wg, bg, lam, h0, *, reverse, nb, tt):
    n, d = hx.shape
    rows = tt * nb
    nt = n // rows
    ncb = d // RG_COLS
    gps = GROUPS_PER_STEP
    nsb = RG_COLS // RG_GATE
    assert len(col_groups) == ncb * gps and tt % (gps * d // MM_COLS) == 0
    ppb = rows // nb
    npb = rows // (2 * nb)
    n_next_blocks = n // (2 * nb)

    def tile_of(i):
        return (nt - 1 - i) if reverse else i

    tile_blk = pl.BlockSpec((rows, RG_COLS), lambda i, j: (tile_of(i), j))
    in_specs = [pl.BlockSpec((rows, d), lambda i, j: (tile_of(i), 0))]
    in_specs += [pl.BlockSpec((d, d), _group_map(col_groups[g::gps])) for g in range(gps)]
    if reverse:
        in_specs += [tile_blk]
        args = [hx] + [w] * gps + [zx]
    else:
        in_specs += [
            tile_blk,
            pl.BlockSpec((nb, RG_COLS), lambda i, j: (jnp.maximum(tile_of(i) * ppb - 1, 0), j)),
            pl.BlockSpec((2 * nb, RG_COLS),
                         lambda i, j: (jnp.minimum((tile_of(i) + 1) * npb, n_next_blocks - 1), j)),
            pl.BlockSpec((4, RG_COLS), lambda i, j: (0, j)),
        ]
        args = [hx] + [w] * gps + [zx, zx, zx, conv_w]
    in_specs += [
        pl.BlockSpec((1, RG_COLS), lambda i, j: (0, j)),
        pl.BlockSpec((nsb, RG_GATE, 2 * RG_GATE), lambda i, j: (j, 0, 0)),
        pl.BlockSpec((nsb, 1, 2 * RG_GATE), lambda i, j: (j, 0, 0)),
        pl.BlockSpec((1, RG_COLS), lambda i, j: (0, j)),
        pl.BlockSpec((ncb, nb, RG_COLS), lambda i, j: (0, 0, 0)),
    ]
    args += [conv_b, wg, bg, lam, h0]
    state_shape = jax.ShapeDtypeStruct((ncb, nb, RG_COLS), F32)
    state_blk = pl.BlockSpec((ncb, nb, RG_COLS), lambda i, j: (0, 0, 0))
    if reverse:
        in_specs += [tile_blk, gg_spec_of(tile_of)]
        args += [hf, zgg]
        out_shape_all = (out_shape, jax.ShapeDtypeStruct((n, d), BF16), state_shape)
        out_specs = (out_spec_of(tile_of), tile_blk, state_blk)
    else:
        out_shape_all = (out_shape, jax.ShapeDtypeStruct((n, d), F32),
                         jax.ShapeDtypeStruct((n, d), BF16), state_shape)
        out_specs = (out_spec_of(tile_of), tile_blk, tile_blk, state_blk)
    return pl.pallas_call(
        functools.partial(_inproj_kernel, nb=nb, d=d, rg=True, reverse=reverse, tt=tt, nt=nt),
        out_shape=out_shape_all,
        grid=(nt, ncb),
        in_specs=in_specs,
        out_specs=out_specs,
        scratch_shapes=[pltpu.VMEM((ncb, nb, RG_COLS), F32)],
        compiler_params=_cparams(("arbitrary", "arbitrary")),
        name="inproj_rg_bwd" if reverse else "inproj_rg_fwd",
    )(*args)


def _cumsum_time(x, reverse, block):
    c = x.shape[0]
    x4 = x.reshape((c // block, block) + x.shape[1:])
    cols = [None] * block
    order = range(block - 1, -1, -1) if reverse else range(block)
    run = None
    for i in order:
        run = x4[:, i] if run is None else run + x4[:, i]
        cols[i] = run
    return jnp.stack(cols, axis=1).reshape(x.shape)


def _cumsum_chunk(x, reverse):
    c = x.shape[0]
    blk = 8
    x4 = _cumsum_time(x, reverse, blk).reshape((c // blk, blk) + x.shape[1:])
    nblk = c // blk
    tot = x4[:, 0] if reverse else x4[:, blk - 1]
    offs = [None] * nblk
    order = range(nblk - 1, -1, -1) if reverse else range(nblk)
    run = None
    for i in order:
        offs[i] = run
        run = tot[i] if run is None else run + tot[i]
    first = nblk - 1 if reverse else 0
    parts = [x4[i] if i == first else x4[i] + offs[i][None] for i in range(nblk)]
    return jnp.stack(parts, axis=0).reshape(x.shape)


def _pivot_time(c, block, p):
    c4 = c.reshape((c.shape[0] // block, block) + c.shape[1:])
    return jnp.broadcast_to(c4[:, p:p + 1], c4.shape).reshape(c.shape)


def _hg_levels(c):
    out, m = [], c // 2
    while m >= HG_DIAG:
        out.append(m)
        m //= 2
    return out


def _hg_kernel(*refs, reverse, final):
    if final:
        (zq_ref, zf_ref, v_ref, of_ref, og_ref, par_ref, s0_ref, y_ref, sout_ref,
         st_scr, tr_scr, o_scr, mm_scr) = refs
    else:
        (zq_ref, zf_ref, v_ref, par_ref, s0_ref, y_ref, sout_ref, st_scr, tr_scr, o_scr,
         mm_scr) = refs
    j = pl.program_id(1)
    nj = pl.num_programs(1)
    c, nb, kd = zq_ref.shape

    @pl.when(j == 0)
    def _():
        st_scr[...] = s0_ref[...]

    zq = zq_ref[...].astype(F32)
    zf = zf_ref[...].astype(F32)
    lb = par_ref[0:1, :][None]
    oml = par_ref[1:2, :][None]

    q = _silu(zq)
    e = jnp.exp(-jnp.abs(zf))
    r = 1.0 / (1.0 + e)
    pos = zf >= 0.0
    f = lb + oml * (jnp.where(pos, 1.0, e) * r)
    k = oml * (jnp.where(pos, e, 1.0) * r)
    g = jnp.log2(jnp.maximum(f, HG_F_FLOOR))
    cs = _cumsum_chunk(g, reverse)
    cd = _cumsum_time(jnp.maximum(g, -HG_DIAG_CLAMP * LOG2_E), reverse, HG_DIAG)
    ctot = cs[0] if reverse else cs[c - 1]
    decay = jnp.exp2(ctot)

    pad = jnp.zeros((c, HG_PITCH - nb, kd), F32)
    for n, arr in enumerate((q, k, cs, cd, v_ref[...].astype(F32))):
        tr_scr[n] = jnp.concatenate([arr, pad], axis=1).reshape(c * HG_PITCH, kd)

    ti = lax.broadcasted_iota(jnp.int32, (c, c), 0)
    si = lax.broadcasted_iota(jnp.int32, (c, c), 1)
    lev = ti ^ si
    valid = (ti <= si) if reverse else (ti >= si)
    nt_dims = (((1,), (1,)), ((), ()))
    tn_dims = (((0,), (0,)), ((), ()))
    levels = _hg_levels(c)

    for b in range(nb):
        rows = pl.ds(b, c, stride=HG_PITCH)
        q2, k2, cs2, cd2 = (tr_scr[n, rows, :] for n in range(4))
        qb = q2.astype(BF16)
        kb = k2.astype(BF16)
        mm_scr[0, b] = qb * jnp.exp2(cs2).astype(BF16)
        mm_scr[1, b] = kb * jnp.exp2(ctot[b:b + 1, :] - cs2).astype(BF16)
        mm_scr[2, b] = tr_scr[4, rows, :].astype(BF16)
        cdd = cd2 - _pivot_time(cd2, HG_DIAG, HG_DIAG // 2 if reverse else HG_DIAG // 2 - 1)
        a = lax.dot_general(qb * jnp.exp2(cdd).astype(BF16), kb * jnp.exp2(-cdd).astype(BF16),
                            nt_dims, preferred_element_type=F32)
        for m in levels[::-1]:
            piv = m if reverse else m - 1
            el = jnp.exp2(-jnp.abs((cs2 - _pivot_time(cs2, 2 * m, piv)).astype(BF16)))
            p = lax.dot_general(qb * el, kb * el, nt_dims, preferred_element_type=F32)
            a = jnp.where(lev < m, a, p)
        mm_scr[3, b] = jnp.where(valid, a, 0.0).astype(BF16)

    decay_t = jnp.transpose(decay)
    for b in range(nb):
        st = st_scr[b]
        vb = mm_scr[2, b]
        o = jnp.dot(jnp.concatenate([mm_scr[0, b], mm_scr[3, b]], axis=1),
                    jnp.concatenate([st.astype(BF16), vb], axis=0), preferred_element_type=F32)
        st_scr[b] = st * decay_t[:, b:b + 1] + lax.dot_general(mm_scr[1, b], vb, tn_dims,
                                                               preferred_element_type=F32)
        o_scr[pl.ds(b, c, stride=HG_PITCH), :] = o

    o = o_scr[...].reshape(c, HG_PITCH, kd)[:, 0:nb, :]
    if final:
        gain = par_ref[3:4, :][None]
        y = _rms(o + of_ref[...], gain) * _silu(og_ref[...].astype(F32))
        y_ref[...] = y.astype(y_ref.dtype)
    else:
        y_ref[...] = o

    @pl.when(j == nj - 1)
    def _():
        sout_ref[...] = st_scr[...]


def _hg_dir(za, zb, of, par, s0, *, reverse, d):
    t, nb, _ = za.shape
    kd = HG_EXPAND
    heads = d // kd
    nj = t // HG_CHUNK
    final = reverse
    assert nb <= HG_PITCH

    def blk(j):
        return (nj - 1 - j) if reverse else j

    def zspec(gidx):
        return pl.BlockSpec((HG_CHUNK, nb, kd), lambda h, j: (blk(j), 0, gidx * heads + h))

    hspec = pl.BlockSpec((HG_CHUNK, nb, kd), lambda h, j: (blk(j), 0, h))
    sspec = pl.BlockSpec((nb, None, kd, kd), lambda h, j: (0, h, 0, 0))
    in_specs = [zspec(1), zspec(3 if reverse else 2), zspec(0)]
    args = [za, za, zb]
    scratch = [pltpu.VMEM((nb, kd, kd), F32), pltpu.VMEM((5, HG_CHUNK * HG_PITCH, kd), F32),
               pltpu.VMEM((HG_CHUNK * HG_PITCH, kd), F32),
               pltpu.VMEM((4, nb, HG_CHUNK, kd), BF16)]
    if final:
        in_specs += [hspec, zspec(1)]
        args += [of, zb]
    in_specs += [pl.BlockSpec((par.shape[0], kd), lambda h, j: (0, h)), sspec]
    args += [par, s0]
    return pl.pallas_call(
        functools.partial(_hg_kernel, reverse=reverse, final=final),
        out_shape=(jax.ShapeDtypeStruct((t, nb, d), BF16 if final else F32),
                   jax.ShapeDtypeStruct(s0.shape, F32)),
        grid=(heads, nj),
        in_specs=in_specs,
        out_specs=(hspec, sspec),
        scratch_shapes=scratch,
        compiler_params=_cparams(("parallel", "arbitrary")),
        name="hg_bwd" if reverse else "hg_fwd",
    )(*args)


def _merge_kernel(yrg_ref, yhg_ref, ga_ref, gb_ref, x_ref, mod_ref, g_ref, wr_ref, wh_ref, wo_ref,
                  o_ref, *, nb, d):
    tm = x_ref.shape[0]
    p_rg = jnp.dot(yrg_ref[...], wr_ref[...], preferred_element_type=F32)
    p_hg = jnp.dot(yhg_ref[...].reshape(tm, d), wh_ref[...], preferred_element_type=F32)
    ga = ga_ref[...].reshape(tm, d).astype(F32)
    gb = gb_ref[...].reshape(tm, d).astype(F32)
    m = _sigmoid(ga) * p_rg + _sigmoid(gb) * p_hg
    y = jnp.dot(m.astype(BF16), wo_ref[...], preferred_element_type=F32)
    n3, gate = _per_batch(_rms(y, g_ref[...]), mod_ref[:, 2 * d:3 * d], nb)
    o_ref[...] = x_ref[...] + (gate * n3).reshape(tm, d)


def _merge(yrg, yhg, zb, tile_spec, x, mod, g, wr, wh, wo, *, tm, nb):
    n, d = x.shape
    row = lambda i: (i, 0)
    const = lambda i: (0, 0)
    return pl.pallas_call(
        functools.partial(_merge_kernel, nb=nb, d=d),
        out_shape=jax.ShapeDtypeStruct((n, d), F32),
        grid=(n // tm,),
        in_specs=[pl.BlockSpec((tm, d), row), tile_spec(0), tile_spec(2), tile_spec(3),
                  pl.BlockSpec((tm, d), row), pl.BlockSpec(mod.shape, const),
                  pl.BlockSpec((1, d), const), pl.BlockSpec((d, d), const),
                  pl.BlockSpec((d, d), const), pl.BlockSpec((d, d), const)],
        out_specs=pl.BlockSpec((tm, d), row),
        compiler_params=_cparams(("parallel",)),
        name="merge",
    )(yrg, yhg, zb, zb, x, mod, g, wr, wh, wo)


def _ffn_kernel(*refs, nb, d, dff, tf, halo, nt):
    if halo:
        (x_ref, xp_ref, xn_ref, mod_ref, g1_ref, g2_ref, wu_ref, cw_ref, cb_ref, wd_ref,
         o_ref, act_scr) = refs
    else:
        (x_ref, mod_ref, g1_ref, g2_ref, wu_ref, cw_ref, cb_ref, wd_ref, o_ref, act_scr) = refs
    i = pl.program_id(0)
    tm = x_ref.shape[0]
    shift = mod_ref[:, 3 * d:4 * d]
    scale = mod_ref[:, 4 * d:5 * d]

    hm = _norm_mod(x_ref[...], g1_ref[...], shift, scale, nb).astype(BF16)
    if halo:
        hp = _norm_mod(xp_ref[...], g1_ref[...], shift, scale, nb)
        hn = _norm_mod(xn_ref[...], g1_ref[...], shift, scale, nb)
        hp = jnp.where(i > 0, hp, 0.0).astype(BF16)
        hn = jnp.where(i < nt - 1, hn, 0.0).astype(BF16)
        he = jnp.concatenate([hp, hm, hn], axis=0)

    for kb in range(dff // tf):
        gs = slice(kb * tf, (kb + 1) * tf)
        vs = slice(dff + kb * tf, dff + (kb + 1) * tf)
        if halo:
            ue = jnp.dot(he, wu_ref[:, gs], preferred_element_type=F32)
        else:
            u = jnp.dot(hm, wu_ref[:, gs], preferred_element_type=F32)
            z = jnp.zeros((nb, tf), F32)
            ue = jnp.concatenate([z, u, z], axis=0)
        gc = cb_ref[:, gs] + ue[0:tm] * cw_ref[0:1, gs] + ue[nb:nb + tm] * cw_ref[1:2, gs] \
            + ue[2 * nb:2 * nb + tm] * cw_ref[2:3, gs]
        uv = jnp.dot(hm, wu_ref[:, vs], preferred_element_type=F32)
        act_scr[:, gs] = (_silu(gc) * uv).astype(BF16)

    y = jnp.dot(act_scr[...], wd_ref[...], preferred_element_type=F32)
    n3, gate = _per_batch(_rms(y, g2_ref[...]), mod_ref[:, 5 * d:6 * d], nb)
    o_ref[...] = x_ref[...] + (gate * n3).reshape(tm, d)


def _ffn(x, mod, g1, g2, w_up, cw, cb, w_down, *, tm, tf, nb, halo):
    n, d = x.shape
    dff = w_down.shape[0]
    nt = n // tm
    hb = tm // nb
    row = lambda i: (i, 0)
    const = lambda i: (0, 0)

    def resident(shape):
        return pl.BlockSpec(shape, const, pipeline_mode=pl.Buffered(1))

    in_specs = [pl.BlockSpec((tm, d), row)]
    args = [x]
    if halo:
        in_specs += [pl.BlockSpec((nb, d), lambda i: (jnp.maximum(i * hb - 1, 0), 0)),
                     pl.BlockSpec((nb, d), lambda i: (jnp.minimum((i + 1) * hb, n // nb - 1), 0))]
        args += [x, x]
    in_specs += [pl.BlockSpec(mod.shape, const), pl.BlockSpec((1, d), const),
                 pl.BlockSpec((1, d), const), resident(w_up.shape),
                 pl.BlockSpec(cw.shape, const), pl.BlockSpec(cb.shape, const),
                 resident(w_down.shape)]
    args += [mod, g1, g2, w_up, cw, cb, w_down]
    return pl.pallas_call(
        functools.partial(_ffn_kernel, nb=nb, d=d, dff=dff, tf=tf, halo=halo, nt=nt),
        out_shape=jax.ShapeDtypeStruct((n, d), F32),
        grid=(nt,),
        in_specs=in_specs,
        out_specs=pl.BlockSpec((tm, d), row),
        scratch_shapes=[pltpu.VMEM((tm, dff), BF16)],
        compiler_params=_cparams(("parallel",)),
        name="ffn_ctx" if halo else "ffn",
    )(*args)


def _gate_weights(wa, wx):
    nd, heads, hd, _ = wa.shape
    per = RG_GATE // hd
    ncb = heads // per

    def bd(w):
        w = w.reshape(nd, ncb, per, hd, hd)
        eye = jnp.eye(per, dtype=w.dtype)
        return jnp.einsum('dcpij,pq->dcpiqj', w, eye).reshape(nd, ncb, RG_GATE, RG_GATE)

    return (0.5 * jnp.concatenate([bd(wa), bd(wx)], axis=-1)).astype(BF16)


def _forward(x, c, ctx, c_ctx, w_ada, b_ada, g_pre_mix, g_post_mix, g_pre_ffn, g_post_ffn, w_in,
             rg_conv_w, rg_conv_b, rg_wa, rg_ba, rg_wx, rg_bx, rg_lam, hg_lb_logits, hg_out_norm,
             w_proj_rg, w_proj_hg, w_out, ffn_w_up, ffn_conv_w, ffn_conv_b, ffn_w_down, *, grid_w):
    nb, seq, d = x.shape
    ctx_len = ctx.shape[1]
    depth = w_in.shape[0]
    rows_g = seq // grid_w
    dff = ffn_w_down.shape[1]
    tile = grid_w * nb
    heads = d // HG_EXPAND
    ncb = d // RG_COLS
    tf = 256 if dff % 256 == 0 else dff

    p = jax.nn.softmax(hg_lb_logits.astype(F32), axis=0)
    cum = jnp.cumsum(p, axis=0)
    lb_all = cum - cum[0:1]
    hg_par = jnp.stack([lb_all, 1.0 - lb_all, jnp.zeros_like(lb_all), hg_out_norm], axis=1)
    hg_par = jnp.pad(hg_par, ((0, 0), (0, 4), (0, 0)))

    w_ada_b, w_in_b = w_ada.astype(BF16), w_in.astype(BF16)
    wr_b, wh_b, wo_b = w_proj_rg.astype(BF16), w_proj_hg.astype(BF16), w_out.astype(BF16)
    wup_b, wdn_b = ffn_w_up.astype(BF16), ffn_w_down.astype(BF16)

    cc = jnp.concatenate([c, jnp.broadcast_to(c_ctx[None], (nb, d))], axis=0)
    mod_all = _ada_mod(cc, w_ada_b, b_ada).reshape(depth, 2, nb, N_MOD * d)

    xl = jnp.swapaxes(x, 0, 1).reshape(seq * nb, d)
    xc = jnp.swapaxes(ctx, 0, 1).reshape(ctx_len * nb, d)

    for l in range(depth):
        need_ctx = l < depth - 1
        mod_l, mod_c = mod_all[l, 0], mod_all[l, 1]
        gpm = g_pre_mix[l].reshape(1, d)
        gpo = g_post_mix[l].reshape(1, d)
        wg = _gate_weights(rg_wa[l], rg_wx[l])
        bg = 0.5 * jnp.concatenate([rg_ba[l].reshape(2, d // RG_GATE, 1, RG_GATE),
                                    rg_bx[l].reshape(2, d // RG_GATE, 1, RG_GATE)], axis=-1)
        rg_kw = dict(conv_w=rg_conv_w[l], conv_b=rg_conv_b[l].reshape(1, d), nb=nb, tt=grid_w)

        st_rg = [jnp.zeros((ncb, nb, RG_COLS), F32)] * 2
        st_hg = [jnp.zeros((nb, heads, HG_EXPAND, HG_EXPAND), F32)] * 2
        ctx_out = None
        for xs, mod_s, steps_s, latent in ((xc, mod_c, ctx_len, False), (xl, mod_l, seq, True)):
            if latent:
                shape4 = jax.ShapeDtypeStruct((grid_w, rows_g, nb, 4 * d), BF16)
                blk = lambda w, col: pl.BlockSpec((grid_w, None, nb, w), col)
                spec_of = lambda tile_of: blk(GROUPS_PER_STEP * d, lambda i, j: (0, tile_of(i), 0, j))
                gg_of = lambda tile_of: blk(RG_COLS, lambda i, j: (0, tile_of(i), 0, j))
                tile_spec = lambda gidx: blk(d, lambda i: (0, i, 0, gidx))
            else:
                shape4 = jax.ShapeDtypeStruct((steps_s, nb, 4 * d), BF16)
                blk = lambda w, col: pl.BlockSpec((grid_w, nb, w), col)
                spec_of = lambda tile_of: blk(GROUPS_PER_STEP * d, lambda i, j: (tile_of(i), 0, j))
                gg_of = lambda tile_of: blk(RG_COLS, lambda i, j: (tile_of(i), 0, j))
                tile_spec = lambda gidx: blk(d, lambda i: (i, 0, gidx))
            zx, hx = _inproj(xs, mod_s, gpm, w_in_b[l], 0, tile, nb)
            za, hf, xlc, st_rg[0] = _inproj_rg(
                hx, w_in_b[l], (1, 2, 3, 4), shape4, spec_of, zx, None, None, None,
                wg=wg[0], bg=bg[0], lam=rg_lam[l, 0:1], h0=st_rg[0], reverse=False, **rg_kw)
            zb, yrg, st_rg[1] = _inproj_rg(
                hx, w_in_b[l], (5, 6, 7, 8), shape4, spec_of, xlc, hf, za, gg_of,
                wg=wg[1], bg=bg[1], lam=rg_lam[l, 1:2], h0=st_rg[1], reverse=True, **rg_kw)
            za3 = za.reshape(steps_s, nb, 4 * d)
            zb3 = zb.reshape(steps_s, nb, 4 * d)
            of, st_hg[0] = _hg_dir(za3, zb3, None, hg_par[l], st_hg[0], reverse=False, d=d)
            yhg, st_hg[1] = _hg_dir(za3, zb3, of, hg_par[l], st_hg[1], reverse=True, d=d)
            if latent or need_ctx:
                yhg = yhg.reshape(shape4.shape[:-1] + (d,))
                out = _merge(yrg, yhg, zb, tile_spec, xs, mod_s, gpo, wr_b[l], wh_b[l], wo_b[l],
                             tm=tile, nb=nb)
                if latent:
                    xl = out
                else:
                    ctx_out = out
        ffn = functools.partial(_ffn, g1=g_pre_ffn[l].reshape(1, d), g2=g_post_ffn[l].reshape(1, d),
                                w_up=wup_b[l], cw=ffn_conv_w[l], cb=ffn_conv_b[l].reshape(1, dff),
                                w_down=wdn_b[l], tm=tile, tf=tf, nb=nb)
        xl = ffn(xl, mod_l, halo=False)
        if need_ctx:
            xc = ffn(ctx_out, mod_c, halo=True)

    return jnp.swapaxes(xl.reshape(seq, nb, d), 0, 1)


def kernel(x, c, ctx, c_ctx, w_ada, b_ada, g_pre_mix, g_post_mix, g_pre_ffn, g_post_ffn, w_in, rg_conv_w, rg_conv_b, rg_wa, rg_ba, rg_wx, rg_bx, rg_lam, hg_lb_logits, hg_out_norm, w_proj_rg, w_proj_hg, w_out, ffn_w_up, ffn_conv_w, ffn_conv_b, ffn_w_down):
    return _forward(x, c, ctx, c_ctx, w_ada, b_ada, g_pre_mix, g_post_mix, g_pre_ffn, g_post_ffn,
                    w_in, rg_conv_w, rg_conv_b, rg_wa, rg_ba, rg_wx, rg_bx, rg_lam, hg_lb_logits,
                    hg_out_norm, w_proj_rg, w_proj_hg, w_out, ffn_w_up, ffn_conv_w, ffn_conv_b,
                    ffn_w_down, grid_w=GRID_W)
```

```python
import functools

import jax
import jax.numpy as jnp
from jax import lax
from jax.experimental import pallas as pl
from jax.experimental.pallas import tpu as pltpu

GRID_W = 64
RG_HEADS = 16
RG_C = 8.0
HG_EXPAND = 128
N_MOD = 6
EPS = 1e-6
RG_CONV_PAD_L = 1
RG_CONV_PAD_R = 2

HG_CHUNK = 128
HG_DIAG = 8
HG_DIAG_CLAMP = 20.0
HG_F_FLOOR = 1e-37
HG_PITCH = 24
RG_TINY = 1e-30
MM_COLS = 256
RG_GATE = 256
RG_COLS = 512
GROUPS_PER_STEP = 2
VMEM_LIMIT = 56 * 1024 * 1024

LOG2_E = 1.4426950408889634

F32 = jnp.float32
BF16 = jnp.bfloat16


def _cparams(sem):
    return pltpu.CompilerParams(dimension_semantics=sem, vmem_limit_bytes=VMEM_LIMIT)


def _sigmoid(x):
    return 0.5 + 0.5 * jnp.tanh(0.5 * x)


def _silu(x):
    return x * _sigmoid(x)


def _gelu_tanh(x):
    c = 0.7978845608028654
    return 0.5 * x * (1.0 + jnp.tanh(c * (x + 0.044715 * (x * x * x))))


def _rms(x, g):
    ms = jnp.mean(x * x, axis=-1, keepdims=True)
    return x * lax.rsqrt(ms + EPS) * g


def _per_batch(x, vec, nb):
    r, d = x.shape
    return x.reshape(r // nb, nb, d), vec[None]


def _norm_mod(x, g, shift, scale, nb):
    y = _rms(x, g)
    y3, sc = _per_batch(y, scale, nb)
    h = y3 * (1.0 + sc) + shift[None]
    return h.reshape(x.shape)


def _ada_kernel(c_ref, w_ref, b_ref, o_ref):
    s = _silu(c_ref[...]).astype(BF16)
    o_ref[0] = jnp.dot(s, w_ref[0], preferred_element_type=F32) + b_ref[0]


def _ada_mod(cc, w_ada, b_ada):
    depth, d, n = w_ada.shape
    tn = 1536 if n % 1536 == 0 else n
    return pl.pallas_call(
        _ada_kernel,
        out_shape=jax.ShapeDtypeStruct((depth, cc.shape[0], n), F32),
        grid=(depth, n // tn),
        in_specs=[pl.BlockSpec(cc.shape, lambda l, j: (0, 0)),
                  pl.BlockSpec((1, d, tn), lambda l, j: (l, 0, j)),
                  pl.BlockSpec((1, 1, tn), lambda l, j: (l, 0, j))],
        out_specs=pl.BlockSpec((1, cc.shape[0], tn), lambda l, j: (l, 0, j)),
        compiler_params=_cparams(("parallel", "parallel")),
        name="ada_mod",
    )(cc, w_ada, b_ada.reshape(depth, 1, n))


def _zero_from(v):
    u = lax.bitcast_convert_type(v, jnp.uint32)
    return lax.bitcast_convert_type((u >> 16) >> 16, F32)


def _rg_conv(xe, bias, cw_ref, rows, nb):
    xl = bias + xe[0:rows] * cw_ref[0:1, :]
    for j in range(1, 4):
        xl = xl + xe[j * nb:j * nb + rows] * cw_ref[j:j + 1, :]
    return xl


def _rg_chunk(xl, wg_ref, bg_ref, lam_ref, h, *, reverse, nb, steps, cols):
    xb = xl.astype(BF16)
    tha, thx = [], []
    for s in range(cols // RG_GATE):
        th = jnp.tanh(jnp.dot(xb[:, s * RG_GATE:(s + 1) * RG_GATE], wg_ref[s],
                              preferred_element_type=F32) + bg_ref[s])
        tha.append(th[:, :RG_GATE])
        thx.append(th[:, RG_GATE:])
    half_rate = (-0.5 * RG_C) * jax.nn.softplus(-lam_ref[...])
    log_a = half_rate + half_rate * jnp.concatenate(tha, axis=1)
    ig = 0.5 + 0.5 * jnp.concatenate(thx, axis=1)
    a = jnp.exp(log_a)
    om = (1.0 + a * a) * jnp.tanh(-log_a)
    b = (om * lax.rsqrt(jnp.maximum(om, RG_TINY))) * (ig * xl)
    hs = [None] * steps
    for t in (range(steps - 1, -1, -1) if reverse else range(steps)):
        h = a[t * nb:(t + 1) * nb] * h + b[t * nb:(t + 1) * nb]
        hs[t] = h
    return h, jnp.concatenate(hs, axis=0)


def _inproj_kernel(*refs, nb, d, rg, reverse, tt, nt):
    if not rg:
        x_ref, mod_ref, g_ref, w_ref, o_ref, h_ref = refs
        h = _norm_mod(x_ref[...], g_ref[...], mod_ref[:, 0:d], mod_ref[:, d:2 * d], nb).astype(BF16)
        h_ref[...] = h
        res = jnp.dot(h, w_ref[...], preferred_element_type=F32)
        o_ref[...] = res.reshape(o_ref.shape).astype(o_ref.dtype)
        return
    gps = GROUPS_PER_STEP
    h_ref, w_refs, rest = refs[0], refs[1:1 + gps], refs[1 + gps:]
    if reverse:
        (xl_ref, cb_ref, wg_ref, bg_ref, lam_ref, h0_ref, hf_ref, gg_ref,
         o_ref, y_ref, hfin_ref, hst_scr) = rest
    else:
        (zx_ref, zxp_ref, zxn_ref, cw_ref, cb_ref, wg_ref, bg_ref,
         lam_ref, h0_ref, o_ref, y_ref, xl_ref, hfin_ref, hst_scr) = rest
    i = pl.program_id(0)
    j = pl.program_id(1)

    @pl.when(i == 0)
    def _():
        hst_scr[j] = h0_ref[j]

    tile = (nt - 1 - i) if reverse else i
    cols = y_ref.shape[1]
    ppg = d // MM_COLS
    npiece = gps * ppg
    steps = tt // npiece
    sub = steps * nb
    oshape = o_ref.shape[:-1] + (MM_COLS,)
    if not reverse:
        prev = jnp.where(tile > 0, zxp_ref[...].astype(F32), 0.0)
        nxt = jnp.where(tile < nt - 1, zxn_ref[...].astype(F32), 0.0)
    h = hst_scr[j]
    bias = cb_ref[...]
    for n in range(npiece):
        w_ref = w_refs[n // ppg]
        res = jnp.dot(h_ref[...], w_ref[:, (n % ppg) * MM_COLS:(n % ppg + 1) * MM_COLS],
                      preferred_element_type=F32)
        o_ref[..., n * MM_COLS:(n + 1) * MM_COLS] = res.reshape(oshape).astype(o_ref.dtype)
        c = (npiece - 1 - n) if reverse else n
        r0 = c * sub
        if reverse:
            xl = xl_ref[r0:r0 + sub, :].astype(F32) + (bias - cb_ref[...])
        else:
            lo = prev if c == 0 else zx_ref[r0 - nb:r0, :].astype(F32)
            hi = nxt if c == npiece - 1 else zx_ref[r0 + sub:r0 + sub + 2 * nb, :].astype(F32)
            xe = jnp.concatenate([lo, zx_ref[r0:r0 + sub, :].astype(F32), hi], axis=0)
            xl = _rg_conv(xe, bias, cw_ref, sub, nb)
            xl_ref[r0:r0 + sub, :] = xl.astype(xl_ref.dtype)
        h, hs = _rg_chunk(xl, wg_ref, bg_ref, lam_ref, h, reverse=reverse, nb=nb,
                          steps=steps, cols=cols)
        if reverse:
            gate = _gelu_tanh(gg_ref[...].reshape(tt * nb, cols)[r0:r0 + sub, :].astype(F32))
            hf = hf_ref[r0:r0 + sub, :].astype(F32)
            y_ref[r0:r0 + sub, :] = (gate * (hf + hs)).astype(y_ref.dtype)
        else:
            y_ref[r0:r0 + sub, :] = hs.astype(y_ref.dtype)
        bias = cb_ref[...] + _zero_from(res[248:256, MM_COLS - 128:MM_COLS])[0:1, 0:1]
    hst_scr[j] = h
    hfin_ref[j] = h


def _group_map(col_groups):
    def wmap(i, j):
        idx = jnp.int32(col_groups[0])
        for k in range(1, len(col_groups)):
            idx = jnp.where(j == k, jnp.int32(col_groups[k]), idx)
        return (0, idx)
    return wmap


def _inproj(x, mod, g, w, group, tm, nb):
    n, d = x.shape
    row = lambda i: (i, 0)
    return pl.pallas_call(
        functools.partial(_inproj_kernel, nb=nb, d=d, rg=False, reverse=False, tt=0, nt=0),
        out_shape=(jax.ShapeDtypeStruct((n, d), BF16), jax.ShapeDtypeStruct((n, d), BF16)),
        grid=(n // tm,),
        in_specs=[pl.BlockSpec((tm, d), row),
                  pl.BlockSpec(mod.shape, lambda i: (0, 0)),
                  pl.BlockSpec((1, d), lambda i: (0, 0)),
                  pl.BlockSpec((d, d), lambda i: (0, group))],
        out_specs=(pl.BlockSpec((tm, d), row), pl.BlockSpec((tm, d), row)),
        compiler_params=_cparams(("parallel",)),
        name="inproj",
    )(x, mod, g, w)


def _inproj_rg(hx, w, col_groups, out_shape, out_spec_of, zx, hf, zgg, gg_spec_of, conv_w,
               conv_b, wg, bg, lam, h0, *, reverse, nb, tt):
    n, d = hx.shape
    rows = tt * nb
    nt = n // rows
    ncb = d // RG_COLS
    gps = GROUPS_PER_STEP
    nsb = RG_COLS // RG_GATE
    assert len(col_groups) == ncb * gps and tt % (gps * d // MM_COLS) == 0
    ppb = rows // nb
    npb = rows // (2 * nb)
    n_next_blocks = n // (2 * nb)

    def tile_of(i):
        return (nt - 1 - i) if reverse else i

    tile_blk = pl.BlockSpec((rows, RG_COLS), lambda i, j: (tile_of(i), j))
    in_specs = [pl.BlockSpec((rows, d), lambda i, j: (tile_of(i), 0))]
    in_specs += [pl.BlockSpec((d, d), _group_map(col_groups[g::gps])) for g in range(gps)]
    if reverse:
        in_specs += [tile_blk]
        args = [hx] + [w] * gps + [zx]
    else:
        in_specs += [
            tile_blk,
            pl.BlockSpec((nb, RG_COLS), lambda i, j: (jnp.maximum(tile_of(i) * ppb - 1, 0), j)),
            pl.BlockSpec((2 * nb, RG_COLS),
                         lambda i, j: (jnp.minimum((tile_of(i) + 1) * npb, n_next_blocks - 1), j)),
            pl.BlockSpec((4, RG_COLS), lambda i, j: (0, j)),
        ]
        args = [hx] + [w] * gps + [zx, zx, zx, conv_w]
    in_specs += [
        pl.BlockSpec((1, RG_COLS), lambda i, j: (0, j)),
        pl.BlockSpec((nsb, RG_GATE, 2 * RG_GATE), lambda i, j: (j, 0, 0)),
        pl.BlockSpec((nsb, 1, 2 * RG_GATE), lambda i, j: (j, 0, 0)),
        pl.BlockSpec((1, RG_COLS), lambda i, j: (0, j)),
        pl.BlockSpec((ncb, nb, RG_COLS), lambda i, j: (0, 0, 0)),
    ]
    args += [conv_b, wg, bg, lam, h0]
    state_shape = jax.ShapeDtypeStruct((ncb, nb, RG_COLS), F32)
    state_blk = pl.BlockSpec((ncb, nb, RG_COLS), lambda i, j: (0, 0, 0))
    if reverse:
        in_specs += [tile_blk, gg_spec_of(tile_of)]
        args += [hf, zgg]
        out_shape_all = (out_shape, jax.ShapeDtypeStruct((n, d), BF16), state_shape)
        out_specs = (out_spec_of(tile_of), tile_blk, state_blk)
    else:
        out_shape_all = (out_shape, jax.ShapeDtypeStruct((n, d), BF16),
                         jax.ShapeDtypeStruct((n, d), BF16), state_shape)
        out_specs = (out_spec_of(tile_of), tile_blk, tile_blk, state_blk)
    return pl.pallas_call(
        functools.partial(_inproj_kernel, nb=nb, d=d, rg=True, reverse=reverse, tt=tt, nt=nt),
        out_shape=out_shape_all,
        grid=(nt, ncb),
        in_specs=in_specs,
        out_specs=out_specs,
        scratch_shapes=[pltpu.VMEM((ncb, nb, RG_COLS), F32)],
        compiler_params=_cparams(("arbitrary", "arbitrary")),
        name="inproj_rg_bwd" if reverse else "inproj_rg_fwd",
    )(*args)


def _cumsum_time(x, reverse, block):
    c = x.shape[0]
    x4 = x.reshape((c // block, block) + x.shape[1:])
    cols = [None] * block
    order = range(block - 1, -1, -1) if reverse else range(block)
    run = None
    for i in order:
        run = x4[:, i] if run is None else run + x4[:, i]
        cols[i] = run
    return jnp.stack(cols, axis=1).reshape(x.shape)


def _cumsum_chunk(x, reverse):
    c = x.shape[0]
    blk = 8
    x4 = _cumsum_time(x, reverse, blk).reshape((c // blk, blk) + x.shape[1:])
    nblk = c // blk
    tot = x4[:, 0] if reverse else x4[:, blk - 1]
    offs = [None] * nblk
    order = range(nblk - 1, -1, -1) if reverse else range(nblk)
    run = None
    for i in order:
        offs[i] = run
        run = tot[i] if run is None else run + tot[i]
    first = nblk - 1 if reverse else 0
    parts = [x4[i] if i == first else x4[i] + offs[i][None] for i in range(nblk)]
    return jnp.stack(parts, axis=0).reshape(x.shape)


def _pivot_time(c, block, p):
    c4 = c.reshape((c.shape[0] // block, block) + c.shape[1:])
    return jnp.broadcast_to(c4[:, p:p + 1], c4.shape).reshape(c.shape)


def _hg_levels(c):
    out, m = [], c // 2
    while m >= HG_DIAG:
        out.append(m)
        m //= 2
    return out


def _hg_kernel(*refs, reverse, final):
    if final:
        (zq_ref, zf_ref, v_ref, of_ref, og_ref, par_ref, s0_ref, y_ref, sout_ref,
         st_scr, tr_scr, o_scr, mm_scr) = refs
    else:
        (zq_ref, zf_ref, v_ref, par_ref, s0_ref, y_ref, sout_ref, st_scr, tr_scr, o_scr,
         mm_scr) = refs
    j = pl.program_id(1)
    nj = pl.num_programs(1)
    c, nb, kd = zq_ref.shape

    @pl.when(j == 0)
    def _():
        st_scr[...] = s0_ref[...]

    zq = zq_ref[...].astype(F32)
    zf = zf_ref[...].astype(F32)
    lb = par_ref[0:1, :][None]
    oml = par_ref[1:2, :][None]

    q = _silu(zq)
    e = jnp.exp(-jnp.abs(zf))
    r = 1.0 / (1.0 + e)
    pos = zf >= 0.0
    f = lb + oml * (jnp.where(pos, 1.0, e) * r)
    k = oml * (jnp.where(pos, e, 1.0) * r)
    g = jnp.log2(jnp.maximum(f, HG_F_FLOOR))
    cs = _cumsum_chunk(g, reverse)
    cd = _cumsum_time(jnp.maximum(g, -HG_DIAG_CLAMP * LOG2_E), reverse, HG_DIAG)
    ctot = cs[0] if reverse else cs[c - 1]
    decay = jnp.exp2(ctot)

    pad = jnp.zeros((c, HG_PITCH - nb, kd), F32)
    for n, arr in enumerate((q, k, cs, cd, v_ref[...].astype(F32))):
        tr_scr[n] = jnp.concatenate([arr, pad], axis=1).reshape(c * HG_PITCH, kd)

    ti = lax.broadcasted_iota(jnp.int32, (c, c), 0)
    si = lax.broadcasted_iota(jnp.int32, (c, c), 1)
    lev = ti ^ si
    valid = (ti <= si) if reverse else (ti >= si)
    nt_dims = (((1,), (1,)), ((), ()))
    tn_dims = (((0,), (0,)), ((), ()))
    levels = _hg_levels(c)

    for b in range(nb):
        rows = pl.ds(b, c, stride=HG_PITCH)
        q2, k2, cs2, cd2 = (tr_scr[n, rows, :] for n in range(4))
        qb = q2.astype(BF16)
        kb = k2.astype(BF16)
        mm_scr[0, b] = qb * jnp.exp2(cs2).astype(BF16)
        mm_scr[1, b] = kb * jnp.exp2(ctot[b:b + 1, :] - cs2).astype(BF16)
        mm_scr[2, b] = tr_scr[4, rows, :].astype(BF16)
        cdd = cd2 - _pivot_time(cd2, HG_DIAG, HG_DIAG // 2 if reverse else HG_DIAG // 2 - 1)
        a = lax.dot_general(qb * jnp.exp2(cdd).astype(BF16), kb * jnp.exp2(-cdd).astype(BF16),
                            nt_dims, preferred_element_type=F32)
        for m in levels[::-1]:
            piv = m if reverse else m - 1
            el = jnp.exp2(-jnp.abs((cs2 - _pivot_time(cs2, 2 * m, piv)).astype(BF16)))
            p = lax.dot_general(qb * el, kb * el, nt_dims, preferred_element_type=F32)
            a = jnp.where(lev < m, a, p)
        mm_scr[3, b] = jnp.where(valid, a, 0.0).astype(BF16)

    decay_t = jnp.transpose(decay)
    for b in range(nb):
        st = st_scr[b]
        vb = mm_scr[2, b]
        o = jnp.dot(jnp.concatenate([mm_scr[0, b], mm_scr[3, b]], axis=1),
                    jnp.concatenate([st.astype(BF16), vb], axis=0), preferred_element_type=F32)
        st_scr[b] = st * decay_t[:, b:b + 1] + lax.dot_general(mm_scr[1, b], vb, tn_dims,
                                                               preferred_element_type=F32)
        o_scr[pl.ds(b, c, stride=HG_PITCH), :] = o

    o = o_scr[...].reshape(c, HG_PITCH, kd)[:, 0:nb, :]
    if final:
        gain = par_ref[3:4, :][None]
        y = _rms(o + of_ref[...], gain) * _silu(og_ref[...].astype(F32))
        y_ref[...] = y.astype(y_ref.dtype)
    else:
        y_ref[...] = o

    @pl.when(j == nj - 1)
    def _():
        sout_ref[...] = st_scr[...]


def _hg_dir(za, zb, of, par, s0, *, reverse, d):
    t, nb, _ = za.shape
    kd = HG_EXPAND
    heads = d // kd
    nj = t // HG_CHUNK
    final = reverse
    assert nb <= HG_PITCH

    def blk(j):
        return (nj - 1 - j) if reverse else j

    def zspec(gidx):
        return pl.BlockSpec((HG_CHUNK, nb, kd), lambda h, j: (blk(j), 0, gidx * heads + h))

    hspec = pl.BlockSpec((HG_CHUNK, nb, kd), lambda h, j: (blk(j), 0, h))
    sspec = pl.BlockSpec((nb, None, kd, kd), lambda h, j: (0, h, 0, 0))
    in_specs = [zspec(1), zspec(3 if reverse else 2), zspec(0)]
    args = [za, za, zb]
    scratch = [pltpu.VMEM((nb, kd, kd), F32), pltpu.VMEM((5, HG_CHUNK * HG_PITCH, kd), F32),
               pltpu.VMEM((HG_CHUNK * HG_PITCH, kd), F32),
               pltpu.VMEM((4, nb, HG_CHUNK, kd), BF16)]
    if final:
        in_specs += [hspec, zspec(1)]
        args += [of, zb]
    in_specs += [pl.BlockSpec((par.shape[0], kd), lambda h, j: (0, h)), sspec]
    args += [par, s0]
    return pl.pallas_call(
        functools.partial(_hg_kernel, reverse=reverse, final=final),
        out_shape=(jax.ShapeDtypeStruct((t, nb, d), BF16 if final else F32),
                   jax.ShapeDtypeStruct(s0.shape, F32)),
        grid=(heads, nj),
        in_specs=in_specs,
        out_specs=(hspec, sspec),
        scratch_shapes=scratch,
        compiler_params=_cparams(("parallel", "arbitrary")),
        name="hg_bwd" if reverse else "hg_fwd",
    )(*args)


def _merge_kernel(yrg_ref, yhg_ref, ga_ref, gb_ref, x_ref, mod_ref, g_ref, wr_ref, wh_ref, wo_ref,
                  o_ref, *, nb, d):
    tm = x_ref.shape[0]
    p_rg = jnp.dot(yrg_ref[...], wr_ref[...], preferred_element_type=F32)
    p_hg = jnp.dot(yhg_ref[...].reshape(tm, d), wh_ref[...], preferred_element_type=F32)
    ga = ga_ref[...].reshape(tm, d).astype(F32)
    gb = gb_ref[...].reshape(tm, d).astype(F32)
    m = _sigmoid(ga) * p_rg + _sigmoid(gb) * p_hg
    y = jnp.dot(m.astype(BF16), wo_ref[...], preferred_element_type=F32)
    n3, gate = _per_batch(_rms(y, g_ref[...]), mod_ref[:, 2 * d:3 * d], nb)
    o_ref[...] = x_ref[...] + (gate * n3).reshape(tm, d)


def _merge(yrg, yhg, zb, tile_spec, x, mod, g, wr, wh, wo, *, tm, nb):
    n, d = x.shape
    row = lambda i: (i, 0)
    const = lambda i: (0, 0)
    return pl.pallas_call(
        functools.partial(_merge_kernel, nb=nb, d=d),
        out_shape=jax.ShapeDtypeStruct((n, d), F32),
        grid=(n // tm,),
        in_specs=[pl.BlockSpec((tm, d), row), tile_spec(0), tile_spec(2), tile_spec(3),
                  pl.BlockSpec((tm, d), row), pl.BlockSpec(mod.shape, const),
                  pl.BlockSpec((1, d), const), pl.BlockSpec((d, d), const),
                  pl.BlockSpec((d, d), const), pl.BlockSpec((d, d), const)],
        out_specs=pl.BlockSpec((tm, d), row),
        compiler_params=_cparams(("parallel",)),
        name="merge",
    )(yrg, yhg, zb, zb, x, mod, g, wr, wh, wo)


def _ffn_kernel(*refs, nb, d, dff, tf, halo, nt):
    if halo:
        (x_ref, xp_ref, xn_ref, mod_ref, g1_ref, g2_ref, wu_ref, cw_ref, cb_ref, wd_ref,
         o_ref, act_scr) = refs
    else:
        (x_ref, mod_ref, g1_ref, g2_ref, wu_ref, cw_ref, cb_ref, wd_ref, o_ref, act_scr) = refs
    i = pl.program_id(0)
    tm = x_ref.shape[0]
    shift = mod_ref[:, 3 * d:4 * d]
    scale = mod_ref[:, 4 * d:5 * d]

    hm = _norm_mod(x_ref[...], g1_ref[...], shift, scale, nb).astype(BF16)
    if halo:
        hp = _norm_mod(xp_ref[...], g1_ref[...], shift, scale, nb)
        hn = _norm_mod(xn_ref[...], g1_ref[...], shift, scale, nb)
        hp = jnp.where(i > 0, hp, 0.0).astype(BF16)
        hn = jnp.where(i < nt - 1, hn, 0.0).astype(BF16)
        he = jnp.concatenate([hp, hm, hn], axis=0)

    for kb in range(dff // tf):
        gs = slice(kb * tf, (kb + 1) * tf)
        vs = slice(dff + kb * tf, dff + (kb + 1) * tf)
        if halo:
            ue = jnp.dot(he, wu_ref[:, gs], preferred_element_type=F32)
        else:
            u = jnp.dot(hm, wu_ref[:, gs], preferred_element_type=F32)
            z = jnp.zeros((nb, tf), F32)
            ue = jnp.concatenate([z, u, z], axis=0)
        gc = cb_ref[:, gs] + ue[0:tm] * cw_ref[0:1, gs] + ue[nb:nb + tm] * cw_ref[1:2, gs] \
            + ue[2 * nb:2 * nb + tm] * cw_ref[2:3, gs]
        uv = jnp.dot(hm, wu_ref[:, vs], preferred_element_type=F32)
        act_scr[:, gs] = (_silu(gc) * uv).astype(BF16)

    y = jnp.dot(act_scr[...], wd_ref[...], preferred_element_type=F32)
    n3, gate = _per_batch(_rms(y, g2_ref[...]), mod_ref[:, 5 * d:6 * d], nb)
    o_ref[...] = x_ref[...] + (gate * n3).reshape(tm, d)


def _ffn(x, mod, g1, g2, w_up, cw, cb, w_down, *, tm, tf, nb, halo):
    n, d = x.shape
    dff = w_down.shape[0]
    nt = n // tm
    hb = tm // nb
    row = lambda i: (i, 0)
    const = lambda i: (0, 0)

    def resident(shape):
        return pl.BlockSpec(shape, const, pipeline_mode=pl.Buffered(1))

    in_specs = [pl.BlockSpec((tm, d), row)]
    args = [x]
    if halo:
        in_specs += [pl.BlockSpec((nb, d), lambda i: (jnp.maximum(i * hb - 1, 0), 0)),
                     pl.BlockSpec((nb, d), lambda i: (jnp.minimum((i + 1) * hb, n // nb - 1), 0))]
        args += [x, x]
    in_specs += [pl.BlockSpec(mod.shape, const), pl.BlockSpec((1, d), const),
                 pl.BlockSpec((1, d), const), resident(w_up.shape),
                 pl.BlockSpec(cw.shape, const), pl.BlockSpec(cb.shape, const),
                 resident(w_down.shape)]
    args += [mod, g1, g2, w_up, cw, cb, w_down]
    return pl.pallas_call(
        functools.partial(_ffn_kernel, nb=nb, d=d, dff=dff, tf=tf, halo=halo, nt=nt),
        out_shape=jax.ShapeDtypeStruct((n, d), F32),
        grid=(nt,),
        in_specs=in_specs,
        out_specs=pl.BlockSpec((tm, d), row),
        scratch_shapes=[pltpu.VMEM((tm, dff), BF16)],
        compiler_params=_cparams(("parallel",)),
        name="ffn_ctx" if halo else "ffn",
    )(*args)


def _gate_weights(wa, wx):
    nd, heads, hd, _ = wa.shape
    per = RG_GATE // hd
    ncb = heads // per

    def bd(w):
        w = w.reshape(nd, ncb, per, hd, hd)
        eye = jnp.eye(per, dtype=w.dtype)
        return jnp.einsum('dcpij,pq->dcpiqj', w, eye).reshape(nd, ncb, RG_GATE, RG_GATE)

    return (0.5 * jnp.concatenate([bd(wa), bd(wx)], axis=-1)).astype(BF16)


def _forward(x, c, ctx, c_ctx, w_ada, b_ada, g_pre_mix, g_post_mix, g_pre_ffn, g_post_ffn, w_in,
             rg_conv_w, rg_conv_b, rg_wa, rg_ba, rg_wx, rg_bx, rg_lam, hg_lb_logits, hg_out_norm,
             w_proj_rg, w_proj_hg, w_out, ffn_w_up, ffn_conv_w, ffn_conv_b, ffn_w_down, *, grid_w):
    nb, seq, d = x.shape
    ctx_len = ctx.shape[1]
    depth = w_in.shape[0]
    rows_g = seq // grid_w
    dff = ffn_w_down.shape[1]
    tile = grid_w * nb
    heads = d // HG_EXPAND
    ncb = d // RG_COLS
    tf = 256 if dff % 256 == 0 else dff

    p = jax.nn.softmax(hg_lb_logits.astype(F32), axis=0)
    cum = jnp.cumsum(p, axis=0)
    lb_all = cum - cum[0:1]
    hg_par = jnp.stack([lb_all, 1.0 - lb_all, jnp.zeros_like(lb_all), hg_out_norm], axis=1)
    hg_par = jnp.pad(hg_par, ((0, 0), (0, 4), (0, 0)))

    w_ada_b, w_in_b = w_ada.astype(BF16), w_in.astype(BF16)
    wr_b, wh_b, wo_b = w_proj_rg.astype(BF16), w_proj_hg.astype(BF16), w_out.astype(BF16)
    wup_b, wdn_b = ffn_w_up.astype(BF16), ffn_w_down.astype(BF16)

    cc = jnp.concatenate([c, jnp.broadcast_to(c_ctx[None], (nb, d))], axis=0)
    mod_all = _ada_mod(cc, w_ada_b, b_ada).reshape(depth, 2, nb, N_MOD * d)

    xl = jnp.swapaxes(x, 0, 1).reshape(seq * nb, d)
    xc = jnp.swapaxes(ctx, 0, 1).reshape(ctx_len * nb, d)

    for l in range(depth):
        need_ctx = l < depth - 1
        mod_l, mod_c = mod_all[l, 0], mod_all[l, 1]
        gpm = g_pre_mix[l].reshape(1, d)
        gpo = g_post_mix[l].reshape(1, d)
        wg = _gate_weights(rg_wa[l], rg_wx[l])
        bg = 0.5 * jnp.concatenate([rg_ba[l].reshape(2, d // RG_GATE, 1, RG_GATE),
                                    rg_bx[l].reshape(2, d // RG_GATE, 1, RG_GATE)], axis=-1)
        rg_kw = dict(conv_w=rg_conv_w[l], conv_b=rg_conv_b[l].reshape(1, d), nb=nb, tt=grid_w)

        st_rg = [jnp.zeros((ncb, nb, RG_COLS), F32)] * 2
        st_hg = [jnp.zeros((nb, heads, HG_EXPAND, HG_EXPAND), F32)] * 2
        ctx_out = None
        for xs, mod_s, steps_s, latent in ((xc, mod_c, ctx_len, False), (xl, mod_l, seq, True)):
            if latent:
                shape4 = jax.ShapeDtypeStruct((grid_w, rows_g, nb, 4 * d), BF16)
                blk = lambda w, col: pl.BlockSpec((grid_w, None, nb, w), col)
                spec_of = lambda tile_of: blk(GROUPS_PER_STEP * d, lambda i, j: (0, tile_of(i), 0, j))
                gg_of = lambda tile_of: blk(RG_COLS, lambda i, j: (0, tile_of(i), 0, j))
                tile_spec = lambda gidx: blk(d, lambda i: (0, i, 0, gidx))
            else:
                shape4 = jax.ShapeDtypeStruct((steps_s, nb, 4 * d), BF16)
                blk = lambda w, col: pl.BlockSpec((grid_w, nb, w), col)
                spec_of = lambda tile_of: blk(GROUPS_PER_STEP * d, lambda i, j: (tile_of(i), 0, j))
                gg_of = lambda tile_of: blk(RG_COLS, lambda i, j: (tile_of(i), 0, j))
                tile_spec = lambda gidx: blk(d, lambda i: (i, 0, gidx))
            zx, hx = _inproj(xs, mod_s, gpm, w_in_b[l], 0, tile, nb)
            za, hf, xlc, st_rg[0] = _inproj_rg(
                hx, w_in_b[l], (1, 2, 3, 4), shape4, spec_of, zx, None, None, None,
                wg=wg[0], bg=bg[0], lam=rg_lam[l, 0:1], h0=st_rg[0], reverse=False, **rg_kw)
            zb, yrg, st_rg[1] = _inproj_rg(
                hx, w_in_b[l], (5, 6, 7, 8), shape4, spec_of, xlc, hf, za, gg_of,
                wg=wg[1], bg=bg[1], lam=rg_lam[l, 1:2], h0=st_rg[1], reverse=True, **rg_kw)
            za3 = za.reshape(steps_s, nb, 4 * d)
            zb3 = zb.reshape(steps_s, nb, 4 * d)
            of, st_hg[0] = _hg_dir(za3, zb3, None, hg_par[l], st_hg[0], reverse=False, d=d)
            yhg, st_hg[1] = _hg_dir(za3, zb3, of, hg_par[l], st_hg[1], reverse=True, d=d)
            if latent or need_ctx:
                yhg = yhg.reshape(shape4.shape[:-1] + (d,))
                out = _merge(yrg, yhg, zb, tile_spec, xs, mod_s, gpo, wr_b[l], wh_b[l], wo_b[l],
                             tm=tile, nb=nb)
                if latent:
                    xl = out
                else:
                    ctx_out = out
        ffn = functools.partial(_ffn, g1=g_pre_ffn[l].reshape(1, d), g2=g_post_ffn[l].reshape(1, d),
                                w_up=wup_b[l], cw=ffn_conv_w[l], cb=ffn_conv_b[l].reshape(1, dff),
                                w_down=wdn_b[l], tm=tile, tf=tf, nb=nb)
        xl = ffn(xl, mod_l, halo=False)
        if need_ctx:
            xc = ffn(ctx_out, mod_c, halo=True)

    return jnp.swapaxes(xl.reshape(seq, nb, d), 0, 1)


def kernel(x, c, ctx, c_ctx, w_ada, b_ada, g_pre_mix, g_post_mix, g_pre_ffn, g_post_ffn, w_in, rg_conv_w, rg_conv_b, rg_wa, rg_ba, rg_wx, rg_bx, rg_lam, hg_lb_logits, hg_out_norm, w_proj_rg, w_proj_hg, w_out, ffn_w_up, ffn_conv_w, ffn_conv_b, ffn_w_down):
    return _forward(x, c, ctx, c_ctx, w_ada, b_ada, g_pre_mix, g_post_mix, g_pre_ffn, g_post_ffn,
                    w_in, rg_conv_w, rg_conv_b, rg_wa, rg_ba, rg_wx, rg_bx, rg_lam, hg_lb_logits,
                    hg_out_norm, w_proj_rg, w_proj_hg, w_out, ffn_w_up, ffn_conv_w, ffn_conv_b,
                    ffn_w_down, grid_w=GRID_W)
```

```python
import functools

import jax
import jax.numpy as jnp
from jax import lax
from jax.experimental import pallas as pl
from jax.experimental.pallas import tpu as pltpu

GRID_W = 64
RG_HEADS = 16
RG_C = 8.0
HG_EXPAND = 128
N_MOD = 6
EPS = 1e-6
RG_CONV_PAD_L = 1
RG_CONV_PAD_R = 2

HG_CHUNK = 128
HG_DIAG = 8
HG_DIAG_CLAMP = 20.0
HG_F_FLOOR = 1e-37
HG_PITCH = 24
RG_TINY = 1e-30
RG_SPLIT = 2
MM_COLS = 256
RG_GATE = 256
RG_COLS = 512
GROUPS_PER_STEP = 2
VMEM_LIMIT = 56 * 1024 * 1024

LOG2_E = 1.4426950408889634

F32 = jnp.float32
BF16 = jnp.bfloat16


def _cparams(sem):
    return pltpu.CompilerParams(dimension_semantics=sem, vmem_limit_bytes=VMEM_LIMIT)


def _sigmoid(x):
    return 0.5 + 0.5 * jnp.tanh(0.5 * x)


def _silu(x):
    return x * _sigmoid(x)


def _gelu_tanh(x):
    c = 0.7978845608028654
    return 0.5 * x * (1.0 + jnp.tanh(c * (x + 0.044715 * (x * x * x))))


def _rms(x, g):
    ms = jnp.mean(x * x, axis=-1, keepdims=True)
    return x * lax.rsqrt(ms + EPS) * g


def _per_batch(x, vec, nb):
    r, d = x.shape
    return x.reshape(r // nb, nb, d), vec[None]


def _norm_mod(x, g, shift, scale, nb):
    y = _rms(x, g)
    y3, sc = _per_batch(y, scale, nb)
    h = y3 * (1.0 + sc) + shift[None]
    return h.reshape(x.shape)


def _ada_kernel(c_ref, w_ref, b_ref, o_ref):
    s = _silu(c_ref[...]).astype(BF16)
    o_ref[0] = jnp.dot(s, w_ref[0], preferred_element_type=F32) + b_ref[0]


def _ada_mod(cc, w_ada, b_ada):
    depth, d, n = w_ada.shape
    tn = 1536 if n % 1536 == 0 else n
    return pl.pallas_call(
        _ada_kernel,
        out_shape=jax.ShapeDtypeStruct((depth, cc.shape[0], n), F32),
        grid=(depth, n // tn),
        in_specs=[pl.BlockSpec(cc.shape, lambda l, j: (0, 0)),
                  pl.BlockSpec((1, d, tn), lambda l, j: (l, 0, j)),
                  pl.BlockSpec((1, 1, tn), lambda l, j: (l, 0, j))],
        out_specs=pl.BlockSpec((1, cc.shape[0], tn), lambda l, j: (l, 0, j)),
        compiler_params=_cparams(("parallel", "parallel")),
        name="ada_mod",
    )(cc, w_ada, b_ada.reshape(depth, 1, n))


def _zero_from(v):
    u = lax.bitcast_convert_type(v, jnp.uint32)
    return lax.bitcast_convert_type((u >> 16) >> 16, F32)


def _rg_conv(xe, bias, cw_ref, rows, nb):
    xl = bias + xe[0:rows] * cw_ref[0:1, :]
    for j in range(1, 4):
        xl = xl + xe[j * nb:j * nb + rows] * cw_ref[j:j + 1, :]
    return xl


def _rg_chunk(xl, wg_ref, bg_ref, lam_ref, h, *, reverse, nb, steps, cols):
    xb = xl.astype(BF16)
    tha, thx = [], []
    for s in range(cols // RG_GATE):
        th = jnp.tanh(jnp.dot(xb[:, s * RG_GATE:(s + 1) * RG_GATE], wg_ref[s],
                              preferred_element_type=F32) + bg_ref[s])
        tha.append(th[:, :RG_GATE])
        thx.append(th[:, RG_GATE:])
    half_rate = (-0.5 * RG_C) * jax.nn.softplus(-lam_ref[...])
    log_a = half_rate + half_rate * jnp.concatenate(tha, axis=1)
    ig = 0.5 + 0.5 * jnp.concatenate(thx, axis=1)
    a = jnp.exp(log_a)
    om = (1.0 + a * a) * jnp.tanh(-log_a)
    b = (om * lax.rsqrt(jnp.maximum(om, RG_TINY))) * (ig * xl)
    hs = [None] * steps
    for t in (range(steps - 1, -1, -1) if reverse else range(steps)):
        h = a[t * nb:(t + 1) * nb] * h + b[t * nb:(t + 1) * nb]
        hs[t] = h
    return h, jnp.concatenate(hs, axis=0)


def _inproj_kernel(*refs, nb, d, rg, reverse, tt, nt):
    if not rg:
        x_ref, mod_ref, g_ref, w_ref, o_ref, h_ref = refs
        h = _norm_mod(x_ref[...], g_ref[...], mod_ref[:, 0:d], mod_ref[:, d:2 * d], nb).astype(BF16)
        h_ref[...] = h
        res = jnp.dot(h, w_ref[...], preferred_element_type=F32)
        o_ref[...] = res.reshape(o_ref.shape).astype(o_ref.dtype)
        return
    gps = GROUPS_PER_STEP
    h_ref, w_refs, rest = refs[0], refs[1:1 + gps], refs[1 + gps:]
    if reverse:
        (xl_ref, cb_ref, wg_ref, bg_ref, lam_ref, h0_ref, hf_ref, gg_ref,
         o_ref, y_ref, hfin_ref, hst_scr) = rest
    else:
        (zx_ref, zxp_ref, zxn_ref, cw_ref, cb_ref, wg_ref, bg_ref,
         lam_ref, h0_ref, o_ref, y_ref, xl_ref, hfin_ref, hst_scr) = rest
    i = pl.program_id(0)
    j = pl.program_id(1)

    @pl.when(i == 0)
    def _():
        hst_scr[j] = h0_ref[j]

    tile = (nt - 1 - i) if reverse else i
    cols = y_ref.shape[1]
    ppg = d // MM_COLS
    npiece = gps * ppg
    nsub = npiece * RG_SPLIT
    steps = tt // nsub
    sub = steps * nb
    oshape = o_ref.shape[:-1] + (MM_COLS,)
    if not reverse:
        prev = jnp.where(tile > 0, zxp_ref[...].astype(F32), 0.0)
        nxt = jnp.where(tile < nt - 1, zxn_ref[...].astype(F32), 0.0)
    h = hst_scr[j]
    bias = cb_ref[...]
    for n in range(npiece):
        w_ref = w_refs[n // ppg]
        res = jnp.dot(h_ref[...], w_ref[:, (n % ppg) * MM_COLS:(n % ppg + 1) * MM_COLS],
                      preferred_element_type=F32)
        o_ref[..., n * MM_COLS:(n + 1) * MM_COLS] = res.reshape(oshape).astype(o_ref.dtype)
        for q in range(RG_SPLIT):
            k = n * RG_SPLIT + q
            c = (nsub - 1 - k) if reverse else k
            r0 = c * sub
            if reverse:
                xl = xl_ref[r0:r0 + sub, :].astype(F32) + (bias - cb_ref[...])
            else:
                lo = prev if c == 0 else zx_ref[r0 - nb:r0, :].astype(F32)
                hi = nxt if c == nsub - 1 else zx_ref[r0 + sub:r0 + sub + 2 * nb, :].astype(F32)
                xe = jnp.concatenate([lo, zx_ref[r0:r0 + sub, :].astype(F32), hi], axis=0)
                xl = _rg_conv(xe, bias, cw_ref, sub, nb)
                xl_ref[r0:r0 + sub, :] = xl.astype(xl_ref.dtype)
            h, hs = _rg_chunk(xl, wg_ref, bg_ref, lam_ref, h, reverse=reverse, nb=nb,
                              steps=steps, cols=cols)
            if reverse:
                gate = _gelu_tanh(gg_ref[...].reshape(tt * nb, cols)[r0:r0 + sub, :].astype(F32))
                y_ref[r0:r0 + sub, :] = (gate * (hf_ref[r0:r0 + sub, :] + hs)).astype(y_ref.dtype)
            else:
                y_ref[r0:r0 + sub, :] = hs
        bias = cb_ref[...] + _zero_from(res[248:256, MM_COLS - 128:MM_COLS])[0:1, 0:1]
    hst_scr[j] = h
    hfin_ref[j] = h


def _group_map(col_groups):
    def wmap(i, j):
        idx = jnp.int32(col_groups[0])
        for k in range(1, len(col_groups)):
            idx = jnp.where(j == k, jnp.int32(col_groups[k]), idx)
        return (0, idx)
    return wmap


def _inproj(x, mod, g, w, group, tm, nb):
    n, d = x.shape
    row = lambda i: (i, 0)
    return pl.pallas_call(
        functools.partial(_inproj_kernel, nb=nb, d=d, rg=False, reverse=False, tt=0, nt=0),
        out_shape=(jax.ShapeDtypeStruct((n, d), BF16), jax.ShapeDtypeStruct((n, d), BF16)),
        grid=(n // tm,),
        in_specs=[pl.BlockSpec((tm, d), row),
                  pl.BlockSpec(mod.shape, lambda i: (0, 0)),
                  pl.BlockSpec((1, d), lambda i: (0, 0)),
                  pl.BlockSpec((d, d), lambda i: (0, group))],
        out_specs=(pl.BlockSpec((tm, d), row), pl.BlockSpec((tm, d), row)),
        compiler_params=_cparams(("parallel",)),
        name="inproj",
    )(x, mod, g, w)


def _inproj_rg(hx, w, col_groups, out_shape, out_spec_of, zx, hf, zgg, gg_spec_of, conv_w,
               conv_b, wg, bg, lam, h0, *, reverse, nb, tt):
    n, d = hx.shape
    rows = tt * nb
    nt = n // rows
    ncb = d // RG_COLS
    gps = GROUPS_PER_STEP
    nsb = RG_COLS // RG_GATE
    assert len(col_groups) == ncb * gps and tt % (RG_SPLIT * gps * d // MM_COLS) == 0
    ppb = rows // nb
    npb = rows // (2 * nb)
    n_next_blocks = n // (2 * nb)

    def tile_of(i):
        return (nt - 1 - i) if reverse else i

    tile_blk = pl.BlockSpec((rows, RG_COLS), lambda i, j: (tile_of(i), j))
    in_specs = [pl.BlockSpec((rows, d), lambda i, j: (tile_of(i), 0))]
    in_specs += [pl.BlockSpec((d, d), _group_map(col_groups[g::gps])) for g in range(gps)]
    if reverse:
        in_specs += [tile_blk]
        args = [hx] + [w] * gps + [zx]
    else:
        in_specs += [
            tile_blk,
            pl.BlockSpec((nb, RG_COLS), lambda i, j: (jnp.maximum(tile_of(i) * ppb - 1, 0), j)),
            pl.BlockSpec((2 * nb, RG_COLS),
                         lambda i, j: (jnp.minimum((tile_of(i) + 1) * npb, n_next_blocks - 1), j)),
            pl.BlockSpec((4, RG_COLS), lambda i, j: (0, j)),
        ]
        args = [hx] + [w] * gps + [zx, zx, zx, conv_w]
    in_specs += [
        pl.BlockSpec((1, RG_COLS), lambda i, j: (0, j)),
        pl.BlockSpec((nsb, RG_GATE, 2 * RG_GATE), lambda i, j: (j, 0, 0)),
        pl.BlockSpec((nsb, 1, 2 * RG_GATE), lambda i, j: (j, 0, 0)),
        pl.BlockSpec((1, RG_COLS), lambda i, j: (0, j)),
        pl.BlockSpec((ncb, nb, RG_COLS), lambda i, j: (0, 0, 0)),
    ]
    args += [conv_b, wg, bg, lam, h0]
    state_shape = jax.ShapeDtypeStruct((ncb, nb, RG_COLS), F32)
    state_blk = pl.BlockSpec((ncb, nb, RG_COLS), lambda i, j: (0, 0, 0))
    if reverse:
        in_specs += [tile_blk, gg_spec_of(tile_of)]
        args += [hf, zgg]
        out_shape_all = (out_shape, jax.ShapeDtypeStruct((n, d), BF16), state_shape)
        out_specs = (out_spec_of(tile_of), tile_blk, state_blk)
    else:
        out_shape_all = (out_shape, jax.ShapeDtypeStruct((n, d), F32),
                         jax.ShapeDtypeStruct((n, d), BF16), state_shape)
        out_specs = (out_spec_of(tile_of), tile_blk, tile_blk, state_blk)
    return pl.pallas_call(
        functools.partial(_inproj_kernel, nb=nb, d=d, rg=True, reverse=reverse, tt=tt, nt=nt),
        out_shape=out_shape_all,
        grid=(nt, ncb),
        in_specs=in_specs,
        out_specs=out_specs,
        scratch_shapes=[pltpu.VMEM((ncb, nb, RG_COLS), F32)],
        compiler_params=_cparams(("arbitrary", "arbitrary")),
        name="inproj_rg_bwd" if reverse else "inproj_rg_fwd",
    )(*args)


def _cumsum_time(x, reverse, block):
    c = x.shape[0]
    x4 = x.reshape((c // block, block) + x.shape[1:])
    cols = [None] * block
    order = range(block - 1, -1, -1) if reverse else range(block)
    run = None
    for i in order:
        run = x4[:, i] if run is None else run + x4[:, i]
        cols[i] = run
    return jnp.stack(cols, axis=1).reshape(x.shape)


def _cumsum_chunk(x, reverse):
    c = x.shape[0]
    blk = 8
    x4 = _cumsum_time(x, reverse, blk).reshape((c // blk, blk) + x.shape[1:])
    nblk = c // blk
    tot = x4[:, 0] if reverse else x4[:, blk - 1]
    offs = [None] * nblk
    order = range(nblk - 1, -1, -1) if reverse else range(nblk)
    run = None
    for i in order:
        offs[i] = run
        run = tot[i] if run is None else run + tot[i]
    first = nblk - 1 if reverse else 0
    parts = [x4[i] if i == first else x4[i] + offs[i][None] for i in range(nblk)]
    return jnp.stack(parts, axis=0).reshape(x.shape)


def _pivot_time(c, block, p):
    c4 = c.reshape((c.shape[0] // block, block) + c.shape[1:])
    return jnp.broadcast_to(c4[:, p:p + 1], c4.shape).reshape(c.shape)


def _hg_levels(c):
    out, m = [], c // 2
    while m >= HG_DIAG:
        out.append(m)
        m //= 2
    return out


def _hg_kernel(*refs, reverse, final):
    if final:
        (zq_ref, zf_ref, v_ref, of_ref, og_ref, par_ref, s0_ref, y_ref, sout_ref,
         st_scr, tr_scr, o_scr, mm_scr) = refs
    else:
        (zq_ref, zf_ref, v_ref, par_ref, s0_ref, y_ref, sout_ref, st_scr, tr_scr, o_scr,
         mm_scr) = refs
    j = pl.program_id(1)
    nj = pl.num_programs(1)
    c, nb, kd = zq_ref.shape

    @pl.when(j == 0)
    def _():
        st_scr[...] = s0_ref[...]

    zq = zq_ref[...].astype(F32)
    zf = zf_ref[...].astype(F32)
    lb = par_ref[0:1, :][None]
    oml = par_ref[1:2, :][None]

    q = _silu(zq)
    e = jnp.exp(-jnp.abs(zf))
    r = 1.0 / (1.0 + e)
    pos = zf >= 0.0
    f = lb + oml * (jnp.where(pos, 1.0, e) * r)
    k = oml * (jnp.where(pos, e, 1.0) * r)
    g = jnp.log2(jnp.maximum(f, HG_F_FLOOR))
    cs = _cumsum_chunk(g, reverse)
    cd = _cumsum_time(jnp.maximum(g, -HG_DIAG_CLAMP * LOG2_E), reverse, HG_DIAG)
    ctot = cs[0] if reverse else cs[c - 1]
    decay = jnp.exp2(ctot)

    pad = jnp.zeros((c, HG_PITCH - nb, kd), F32)
    for n, arr in enumerate((q, k, cs, cd, v_ref[...].astype(F32))):
        tr_scr[n] = jnp.concatenate([arr, pad], axis=1).reshape(c * HG_PITCH, kd)

    ti = lax.broadcasted_iota(jnp.int32, (c, c), 0)
    si = lax.broadcasted_iota(jnp.int32, (c, c), 1)
    lev = ti ^ si
    valid = (ti <= si) if reverse else (ti >= si)
    nt_dims = (((1,), (1,)), ((), ()))
    tn_dims = (((0,), (0,)), ((), ()))
    levels = _hg_levels(c)

    for b in range(nb):
        rows = pl.ds(b, c, stride=HG_PITCH)
        q2, k2, cs2, cd2 = (tr_scr[n, rows, :] for n in range(4))
        qb = q2.astype(BF16)
        kb = k2.astype(BF16)
        mm_scr[0, b] = qb * jnp.exp2(cs2).astype(BF16)
        mm_scr[1, b] = kb * jnp.exp2(ctot[b:b + 1, :] - cs2).astype(BF16)
        mm_scr[2, b] = tr_scr[4, rows, :].astype(BF16)
        cdd = cd2 - _pivot_time(cd2, HG_DIAG, HG_DIAG // 2 if reverse else HG_DIAG // 2 - 1)
        a = lax.dot_general(qb * jnp.exp2(cdd).astype(BF16), kb * jnp.exp2(-cdd).astype(BF16),
                            nt_dims, preferred_element_type=F32)
        for m in levels[::-1]:
            piv = m if reverse else m - 1
            el = jnp.exp2(-jnp.abs((cs2 - _pivot_time(cs2, 2 * m, piv)).astype(BF16)))
            p = lax.dot_general(qb * el, kb * el, nt_dims, preferred_element_type=F32)
            a = jnp.where(lev < m, a, p)
        mm_scr[3, b] = jnp.where(valid, a, 0.0).astype(BF16)

    decay_t = jnp.transpose(decay)
    for b in range(nb):
        st = st_scr[b]
        vb = mm_scr[2, b]
        o = jnp.dot(jnp.concatenate([mm_scr[0, b], mm_scr[3, b]], axis=1),
                    jnp.concatenate([st.astype(BF16), vb], axis=0), preferred_element_type=F32)
        st_scr[b] = st * decay_t[:, b:b + 1] + lax.dot_general(mm_scr[1, b], vb, tn_dims,
                                                               preferred_element_type=F32)
        o_scr[pl.ds(b, c, stride=HG_PITCH), :] = o

    o = o_scr[...].reshape(c, HG_PITCH, kd)[:, 0:nb, :]
    if final:
        gain = par_ref[3:4, :][None]
        y = _rms(o + of_ref[...], gain) * _silu(og_ref[...].astype(F32))
        y_ref[...] = y.astype(y_ref.dtype)
    else:
        y_ref[...] = o

    @pl.when(j == nj - 1)
    def _():
        sout_ref[...] = st_scr[...]


def _hg_dir(za, zb, of, par, s0, *, reverse, d):
    t, nb, _ = za.shape
    kd = HG_EXPAND
    heads = d // kd
    nj = t // HG_CHUNK
    final = reverse
    assert nb <= HG_PITCH

    def blk(j):
        return (nj - 1 - j) if reverse else j

    def zspec(gidx):
        return pl.BlockSpec((HG_CHUNK, nb, kd), lambda h, j: (blk(j), 0, gidx * heads + h))

    hspec = pl.BlockSpec((HG_CHUNK, nb, kd), lambda h, j: (blk(j), 0, h))
    sspec = pl.BlockSpec((nb, None, kd, kd), lambda h, j: (0, h, 0, 0))
    in_specs = [zspec(1), zspec(3 if reverse else 2), zspec(0)]
    args = [za, za, zb]
    scratch = [pltpu.VMEM((nb, kd, kd), F32), pltpu.VMEM((5, HG_CHUNK * HG_PITCH, kd), F32),
               pltpu.VMEM((HG_CHUNK * HG_PITCH, kd), F32),
               pltpu.VMEM((4, nb, HG_CHUNK, kd), BF16)]
    if final:
        in_specs += [hspec, zspec(1)]
        args += [of, zb]
    in_specs += [pl.BlockSpec((par.shape[0], kd), lambda h, j: (0, h)), sspec]
    args += [par, s0]
    return pl.pallas_call(
        functools.partial(_hg_kernel, reverse=reverse, final=final),
        out_shape=(jax.ShapeDtypeStruct((t, nb, d), BF16 if final else F32),
                   jax.ShapeDtypeStruct(s0.shape, F32)),
        grid=(heads, nj),
        in_specs=in_specs,
        out_specs=(hspec, sspec),
        scratch_shapes=scratch,
        compiler_params=_cparams(("parallel", "arbitrary")),
        name="hg_bwd" if reverse else "hg_fwd",
    )(*args)


def _merge_kernel(yrg_ref, yhg_ref, ga_ref, gb_ref, x_ref, mod_ref, g_ref, wr_ref, wh_ref, wo_ref,
                  o_ref, *, nb, d):
    tm = x_ref.shape[0]
    p_rg = jnp.dot(yrg_ref[...], wr_ref[...], preferred_element_type=F32)
    p_hg = jnp.dot(yhg_ref[...].reshape(tm, d), wh_ref[...], preferred_element_type=F32)
    ga = ga_ref[...].reshape(tm, d).astype(F32)
    gb = gb_ref[...].reshape(tm, d).astype(F32)
    m = _sigmoid(ga) * p_rg + _sigmoid(gb) * p_hg
    y = jnp.dot(m.astype(BF16), wo_ref[...], preferred_element_type=F32)
    n3, gate = _per_batch(_rms(y, g_ref[...]), mod_ref[:, 2 * d:3 * d], nb)
    o_ref[...] = x_ref[...] + (gate * n3).reshape(tm, d)


def _merge(yrg, yhg, zb, tile_spec, x, mod, g, wr, wh, wo, *, tm, nb):
    n, d = x.shape
    row = lambda i: (i, 0)
    const = lambda i: (0, 0)
    return pl.pallas_call(
        functools.partial(_merge_kernel, nb=nb, d=d),
        out_shape=jax.ShapeDtypeStruct((n, d), F32),
        grid=(n // tm,),
        in_specs=[pl.BlockSpec((tm, d), row), tile_spec(0), tile_spec(2), tile_spec(3),
                  pl.BlockSpec((tm, d), row), pl.BlockSpec(mod.shape, const),
                  pl.BlockSpec((1, d), const), pl.BlockSpec((d, d), const),
                  pl.BlockSpec((d, d), const), pl.BlockSpec((d, d), const)],
        out_specs=pl.BlockSpec((tm, d), row),
        compiler_params=_cparams(("parallel",)),
        name="merge",
    )(yrg, yhg, zb, zb, x, mod, g, wr, wh, wo)


def _ffn_kernel(*refs, nb, d, dff, tf, halo, nt):
    if halo:
        (x_ref, xp_ref, xn_ref, mod_ref, g1_ref, g2_ref, wu_ref, cw_ref, cb_ref, wd_ref,
         o_ref, act_scr) = refs
    else:
        (x_ref, mod_ref, g1_ref, g2_ref, wu_ref, cw_ref, cb_ref, wd_ref, o_ref, act_scr) = refs
    i = pl.program_id(0)
    tm = x_ref.shape[0]
    shift = mod_ref[:, 3 * d:4 * d]
    scale = mod_ref[:, 4 * d:5 * d]

    hm = _norm_mod(x_ref[...], g1_ref[...], shift, scale, nb).astype(BF16)
    if halo:
        hp = _norm_mod(xp_ref[...], g1_ref[...], shift, scale, nb)
        hn = _norm_mod(xn_ref[...], g1_ref[...], shift, scale, nb)
        hp = jnp.where(i > 0, hp, 0.0).astype(BF16)
        hn = jnp.where(i < nt - 1, hn, 0.0).astype(BF16)
        he = jnp.concatenate([hp, hm, hn], axis=0)

    for kb in range(dff // tf):
        gs = slice(kb * tf, (kb + 1) * tf)
        vs = slice(dff + kb * tf, dff + (kb + 1) * tf)
        if halo:
            ue = jnp.dot(he, wu_ref[:, gs], preferred_element_type=F32)
        else:
            u = jnp.dot(hm, wu_ref[:, gs], preferred_element_type=F32)
            z = jnp.zeros((nb, tf), F32)
            ue = jnp.concatenate([z, u, z], axis=0)
        gc = cb_ref[:, gs] + ue[0:tm] * cw_ref[0:1, gs] + ue[nb:nb + tm] * cw_ref[1:2, gs] \
            + ue[2 * nb:2 * nb + tm] * cw_ref[2:3, gs]
        uv = jnp.dot(hm, wu_ref[:, vs], preferred_element_type=F32)
        act_scr[:, gs] = (_silu(gc) * uv).astype(BF16)

    y = jnp.dot(act_scr[...], wd_ref[...], preferred_element_type=F32)
    n3, gate = _per_batch(_rms(y, g2_ref[...]), mod_ref[:, 5 * d:6 * d], nb)
    o_ref[...] = x_ref[...] + (gate * n3).reshape(tm, d)


def _ffn(x, mod, g1, g2, w_up, cw, cb, w_down, *, tm, tf, nb, halo):
    n, d = x.shape
    dff = w_down.shape[0]
    nt = n // tm
    hb = tm // nb
    row = lambda i: (i, 0)
    const = lambda i: (0, 0)

    def resident(shape):
        return pl.BlockSpec(shape, const, pipeline_mode=pl.Buffered(1))

    in_specs = [pl.BlockSpec((tm, d), row)]
    args = [x]
    if halo:
        in_specs += [pl.BlockSpec((nb, d), lambda i: (jnp.maximum(i * hb - 1, 0), 0)),
                     pl.BlockSpec((nb, d), lambda i: (jnp.minimum((i + 1) * hb, n // nb - 1), 0))]
        args += [x, x]
    in_specs += [pl.BlockSpec(mod.shape, const), pl.BlockSpec((1, d), const),
                 pl.BlockSpec((1, d), const), resident(w_up.shape),
                 pl.BlockSpec(cw.shape, const), pl.BlockSpec(cb.shape, const),
                 resident(w_down.shape)]
    args += [mod, g1, g2, w_up, cw, cb, w_down]
    return pl.pallas_call(
        functools.partial(_ffn_kernel, nb=nb, d=d, dff=dff, tf=tf, halo=halo, nt=nt),
        out_shape=jax.ShapeDtypeStruct((n, d), F32),
        grid=(nt,),
        in_specs=in_specs,
        out_specs=pl.BlockSpec((tm, d), row),
        scratch_shapes=[pltpu.VMEM((tm, dff), BF16)],
        compiler_params=_cparams(("parallel",)),
        name="ffn_ctx" if halo else "ffn",
    )(*args)


def _gate_weights(wa, wx):
    nd, heads, hd, _ = wa.shape
    per = RG_GATE // hd
    ncb = heads // per

    def bd(w):
        w = w.reshape(nd, ncb, per, hd, hd)
        eye = jnp.eye(per, dtype=w.dtype)
        return jnp.einsum('dcpij,pq->dcpiqj', w, eye).reshape(nd, ncb, RG_GATE, RG_GATE)

    return (0.5 * jnp.concatenate([bd(wa), bd(wx)], axis=-1)).astype(BF16)


def _forward(x, c, ctx, c_ctx, w_ada, b_ada, g_pre_mix, g_post_mix, g_pre_ffn, g_post_ffn, w_in,
             rg_conv_w, rg_conv_b, rg_wa, rg_ba, rg_wx, rg_bx, rg_lam, hg_lb_logits, hg_out_norm,
             w_proj_rg, w_proj_hg, w_out, ffn_w_up, ffn_conv_w, ffn_conv_b, ffn_w_down, *, grid_w):
    nb, seq, d = x.shape
    ctx_len = ctx.shape[1]
    depth = w_in.shape[0]
    rows_g = seq // grid_w
    dff = ffn_w_down.shape[1]
    tile = grid_w * nb
    heads = d // HG_EXPAND
    ncb = d // RG_COLS
    tf = 256 if dff % 256 == 0 else dff

    p = jax.nn.softmax(hg_lb_logits.astype(F32), axis=0)
    cum = jnp.cumsum(p, axis=0)
    lb_all = cum - cum[0:1]
    hg_par = jnp.stack([lb_all, 1.0 - lb_all, jnp.zeros_like(lb_all), hg_out_norm], axis=1)
    hg_par = jnp.pad(hg_par, ((0, 0), (0, 4), (0, 0)))

    w_ada_b, w_in_b = w_ada.astype(BF16), w_in.astype(BF16)
    wr_b, wh_b, wo_b = w_proj_rg.astype(BF16), w_proj_hg.astype(BF16), w_out.astype(BF16)
    wup_b, wdn_b = ffn_w_up.astype(BF16), ffn_w_down.astype(BF16)

    cc = jnp.concatenate([c, jnp.broadcast_to(c_ctx[None], (nb, d))], axis=0)
    mod_all = _ada_mod(cc, w_ada_b, b_ada).reshape(depth, 2, nb, N_MOD * d)

    xl = jnp.swapaxes(x, 0, 1).reshape(seq * nb, d)
    xc = jnp.swapaxes(ctx, 0, 1).reshape(ctx_len * nb, d)

    for l in range(depth):
        need_ctx = l < depth - 1
        mod_l, mod_c = mod_all[l, 0], mod_all[l, 1]
        gpm = g_pre_mix[l].reshape(1, d)
        gpo = g_post_mix[l].reshape(1, d)
        wg = _gate_weights(rg_wa[l], rg_wx[l])
        bg = 0.5 * jnp.concatenate([rg_ba[l].reshape(2, d // RG_GATE, 1, RG_GATE),
                                    rg_bx[l].reshape(2, d // RG_GATE, 1, RG_GATE)], axis=-1)
        rg_kw = dict(conv_w=rg_conv_w[l], conv_b=rg_conv_b[l].reshape(1, d), nb=nb, tt=grid_w)

        st_rg = [jnp.zeros((ncb, nb, RG_COLS), F32)] * 2
        st_hg = [jnp.zeros((nb, heads, HG_EXPAND, HG_EXPAND), F32)] * 2
        ctx_out = None
        for xs, mod_s, steps_s, latent in ((xc, mod_c, ctx_len, False), (xl, mod_l, seq, True)):
            if latent:
                shape4 = jax.ShapeDtypeStruct((grid_w, rows_g, nb, 4 * d), BF16)
                blk = lambda w, col: pl.BlockSpec((grid_w, None, nb, w), col)
                spec_of = lambda tile_of: blk(GROUPS_PER_STEP * d, lambda i, j: (0, tile_of(i), 0, j))
                gg_of = lambda tile_of: blk(RG_COLS, lambda i, j: (0, tile_of(i), 0, j))
                tile_spec = lambda gidx: blk(d, lambda i: (0, i, 0, gidx))
            else:
                shape4 = jax.ShapeDtypeStruct((steps_s, nb, 4 * d), BF16)
                blk = lambda w, col: pl.BlockSpec((grid_w, nb, w), col)
                spec_of = lambda tile_of: blk(GROUPS_PER_STEP * d, lambda i, j: (tile_of(i), 0, j))
                gg_of = lambda tile_of: blk(RG_COLS, lambda i, j: (tile_of(i), 0, j))
                tile_spec = lambda gidx: blk(d, lambda i: (i, 0, gidx))
            zx, hx = _inproj(xs, mod_s, gpm, w_in_b[l], 0, tile, nb)
            za, hf, xlc, st_rg[0] = _inproj_rg(
                hx, w_in_b[l], (1, 2, 3, 4), shape4, spec_of, zx, None, None, None,
                wg=wg[0], bg=bg[0], lam=rg_lam[l, 0:1], h0=st_rg[0], reverse=False, **rg_kw)
            zb, yrg, st_rg[1] = _inproj_rg(
                hx, w_in_b[l], (5, 6, 7, 8), shape4, spec_of, xlc, hf, za, gg_of,
                wg=wg[1], bg=bg[1], lam=rg_lam[l, 1:2], h0=st_rg[1], reverse=True, **rg_kw)
            za3 = za.reshape(steps_s, nb, 4 * d)
            zb3 = zb.reshape(steps_s, nb, 4 * d)
            of, st_hg[0] = _hg_dir(za3, zb3, None, hg_par[l], st_hg[0], reverse=False, d=d)
            yhg, st_hg[1] = _hg_dir(za3, zb3, of, hg_par[l], st_hg[1], reverse=True, d=d)
            if latent or need_ctx:
                yhg = yhg.reshape(shape4.shape[:-1] + (d,))
                out = _merge(yrg, yhg, zb, tile_spec, xs, mod_s, gpo, wr_b[l], wh_b[l], wo_b[l],
                             tm=tile, nb=nb)
                if latent:
                    xl = out
                else:
                    ctx_out = out
        ffn = functools.partial(_ffn, g1=g_pre_ffn[l].reshape(1, d), g2=g_post_ffn[l].reshape(1, d),
                                w_up=wup_b[l], cw=ffn_conv_w[l], cb=ffn_conv_b[l].reshape(1, dff),
                                w_down=wdn_b[l], tm=tile, tf=tf, nb=nb)
        xl = ffn(xl, mod_l, halo=False)
        if need_ctx:
            xc = ffn(ctx_out, mod_c, halo=True)

    return jnp.swapaxes(xl.reshape(seq, nb, d), 0, 1)


def kernel(x, c, ctx, c_ctx, w_ada, b_ada, g_pre_mix, g_post_mix, g_pre_ffn, g_post_ffn, w_in, rg_conv_w, rg_conv_b, rg_wa, rg_ba, rg_wx, rg_bx, rg_lam, hg_lb_logits, hg_out_norm, w_proj_rg, w_proj_hg, w_out, ffn_w_up, ffn_conv_w, ffn_conv_b, ffn_w_down):
    return _forward(x, c, ctx, c_ctx, w_ada, b_ada, g_pre_mix, g_post_mix, g_pre_ffn, g_post_ffn,
                    w_in, rg_conv_w, rg_conv_b, rg_wa, rg_ba, rg_wx, rg_bx, rg_lam, hg_lb_logits,
                    hg_out_norm, w_proj_rg, w_proj_hg, w_out, ffn_w_up, ffn_conv_w, ffn_conv_b,
                    ffn_w_down, grid_w=GRID_W)
```

```python
import functools

import jax
import jax.numpy as jnp
from jax import lax
from jax.experimental import pallas as pl
from jax.experimental.pallas import tpu as pltpu

GRID_W = 64
RG_HEADS = 16
RG_C = 8.0
HG_EXPAND = 128
N_MOD = 6
EPS = 1e-6
RG_CONV_PAD_L = 1
RG_CONV_PAD_R = 2

HG_CHUNK = 128
HG_DIAG = 8
HG_DIAG_CLAMP = 20.0
HG_F_FLOOR = 1e-37
HG_PITCH = 24
RG_TINY = 1e-30
RG_SPLIT = 2
MM_COLS = 256
RG_GATE = 256
RG_COLS = 512
GROUPS_PER_STEP = 2
VMEM_LIMIT = 56 * 1024 * 1024

LOG2_E = 1.4426950408889634

F32 = jnp.float32
BF16 = jnp.bfloat16


def _cparams(sem):
    return pltpu.CompilerParams(dimension_semantics=sem, vmem_limit_bytes=VMEM_LIMIT)


def _sigmoid(x):
    return 0.5 + 0.5 * jnp.tanh(0.5 * x)


def _silu(x):
    return x * _sigmoid(x)


def _gelu_tanh(x):
    c = 0.7978845608028654
    return 0.5 * x * (1.0 + jnp.tanh(c * (x + 0.044715 * (x * x * x))))


def _rms(x, g):
    ms = jnp.mean(x * x, axis=-1, keepdims=True)
    return x * lax.rsqrt(ms + EPS) * g


def _per_batch(x, vec, nb):
    r, d = x.shape
    return x.reshape(r // nb, nb, d), vec[None]


def _norm_mod(x, g, shift, scale, nb):
    y = _rms(x, g)
    y3, sc = _per_batch(y, scale, nb)
    h = y3 * (1.0 + sc) + shift[None]
    return h.reshape(x.shape)


def _ada_kernel(c_ref, w_ref, b_ref, o_ref):
    s = _silu(c_ref[...]).astype(BF16)
    o_ref[0] = jnp.dot(s, w_ref[0], preferred_element_type=F32) + b_ref[0]


def _ada_mod(cc, w_ada, b_ada):
    depth, d, n = w_ada.shape
    tn = 1536 if n % 1536 == 0 else n
    return pl.pallas_call(
        _ada_kernel,
        out_shape=jax.ShapeDtypeStruct((depth, cc.shape[0], n), F32),
        grid=(depth, n // tn),
        in_specs=[pl.BlockSpec(cc.shape, lambda l, j: (0, 0)),
                  pl.BlockSpec((1, d, tn), lambda l, j: (l, 0, j)),
                  pl.BlockSpec((1, 1, tn), lambda l, j: (l, 0, j))],
        out_specs=pl.BlockSpec((1, cc.shape[0], tn), lambda l, j: (l, 0, j)),
        compiler_params=_cparams(("parallel", "parallel")),
        name="ada_mod",
    )(cc, w_ada, b_ada.reshape(depth, 1, n))


def _zero_from(v):
    u = lax.bitcast_convert_type(v, jnp.uint32)
    return lax.bitcast_convert_type((u >> 16) >> 16, F32)


def _rg_conv(xe, bias, cw_ref, rows, nb):
    xl = bias + xe[0:rows] * cw_ref[0:1, :]
    for j in range(1, 4):
        xl = xl + xe[j * nb:j * nb + rows] * cw_ref[j:j + 1, :]
    return xl


def _rg_chunk(xl, wg_ref, bg_ref, lam_ref, h, *, reverse, nb, steps, cols):
    xb = xl.astype(BF16)
    tha, thx = [], []
    for s in range(cols // RG_GATE):
        th = jnp.tanh(jnp.dot(xb[:, s * RG_GATE:(s + 1) * RG_GATE], wg_ref[s],
                              preferred_element_type=F32) + bg_ref[s])
        tha.append(th[:, :RG_GATE])
        thx.append(th[:, RG_GATE:])
    half_rate = (-0.5 * RG_C) * jax.nn.softplus(-lam_ref[...])
    log_a = half_rate + half_rate * jnp.concatenate(tha, axis=1)
    ig = 0.5 + 0.5 * jnp.concatenate(thx, axis=1)
    a = jnp.exp(log_a)
    om = (1.0 + a * a) * jnp.tanh(-log_a)
    b = (om * lax.rsqrt(jnp.maximum(om, RG_TINY))) * (ig * xl)
    hs = [None] * steps
    for t in (range(steps - 1, -1, -1) if reverse else range(steps)):
        h = a[t * nb:(t + 1) * nb] * h + b[t * nb:(t + 1) * nb]
        hs[t] = h
    return h, jnp.concatenate(hs, axis=0)


def _inproj_kernel(*refs, nb, d, rg, reverse, tt, nt):
    if not rg:
        x_ref, mod_ref, g_ref, w_ref, o_ref, h_ref = refs
        h = _norm_mod(x_ref[...], g_ref[...], mod_ref[:, 0:d], mod_ref[:, d:2 * d], nb).astype(BF16)
        h_ref[...] = h
        res = jnp.dot(h, w_ref[...], preferred_element_type=F32)
        o_ref[...] = res.reshape(o_ref.shape).astype(o_ref.dtype)
        return
    gps = GROUPS_PER_STEP
    h_ref, w_refs, rest = refs[0], refs[1:1 + gps], refs[1 + gps:]
    if reverse:
        (xl_ref, cb_ref, wg_ref, bg_ref, lam_ref, h0_ref, hf_ref, gg_ref,
         o_ref, y_ref, hfin_ref, hst_scr) = rest
    else:
        (zx_ref, zxp_ref, zxn_ref, cw_ref, cb_ref, wg_ref, bg_ref,
         lam_ref, h0_ref, o_ref, y_ref, xl_ref, hfin_ref, hst_scr) = rest
    i = pl.program_id(0)
    j = pl.program_id(1)

    @pl.when(i == 0)
    def _():
        hst_scr[j] = h0_ref[j]

    tile = (nt - 1 - i) if reverse else i
    cols = y_ref.shape[1]
    ppg = d // MM_COLS
    npiece = gps * ppg
    nsub = npiece * RG_SPLIT
    steps = tt // nsub
    sub = steps * nb
    oshape = o_ref.shape[:-1] + (MM_COLS,)
    if not reverse:
        prev = jnp.where(tile > 0, zxp_ref[...].astype(F32), 0.0)
        nxt = jnp.where(tile < nt - 1, zxn_ref[...].astype(F32), 0.0)
    h = hst_scr[j]
    bias = cb_ref[...]
    for n in range(npiece):
        w_ref = w_refs[n // ppg]
        res = jnp.dot(h_ref[...], w_ref[:, (n % ppg) * MM_COLS:(n % ppg + 1) * MM_COLS],
                      preferred_element_type=F32)
        o_ref[..., n * MM_COLS:(n + 1) * MM_COLS] = res.reshape(oshape).astype(o_ref.dtype)
        for q in range(RG_SPLIT):
            k = n * RG_SPLIT + q
            c = (nsub - 1 - k) if reverse else k
            r0 = c * sub
            if reverse:
                xl = xl_ref[r0:r0 + sub, :].astype(F32) + (bias - cb_ref[...])
            else:
                lo = prev if c == 0 else zx_ref[r0 - nb:r0, :].astype(F32)
                hi = nxt if c == nsub - 1 else zx_ref[r0 + sub:r0 + sub + 2 * nb, :].astype(F32)
                xe = jnp.concatenate([lo, zx_ref[r0:r0 + sub, :].astype(F32), hi], axis=0)
                xl = _rg_conv(xe, bias, cw_ref, sub, nb)
                xl_ref[r0:r0 + sub, :] = xl.astype(xl_ref.dtype)
            h, hs = _rg_chunk(xl, wg_ref, bg_ref, lam_ref, h, reverse=reverse, nb=nb,
                              steps=steps, cols=cols)
            if reverse:
                gate = _gelu_tanh(gg_ref[...].reshape(tt * nb, cols)[r0:r0 + sub, :].astype(F32))
                y_ref[r0:r0 + sub, :] = (gate * (hf_ref[r0:r0 + sub, :] + hs)).astype(y_ref.dtype)
            else:
                y_ref[r0:r0 + sub, :] = hs
        bias = cb_ref[...] + _zero_from(res[248:256, MM_COLS - 128:MM_COLS])[0:1, 0:1]
    hst_scr[j] = h
    hfin_ref[j] = h


def _group_map(col_groups):
    def wmap(i, j):
        idx = jnp.int32(col_groups[0])
        for k in range(1, len(col_groups)):
            idx = jnp.where(j == k, jnp.int32(col_groups[k]), idx)
        return (0, idx)
    return wmap


def _inproj(x, mod, g, w, group, tm, nb):
    n, d = x.shape
    row = lambda i: (i, 0)
    return pl.pallas_call(
        functools.partial(_inproj_kernel, nb=nb, d=d, rg=False, reverse=False, tt=0, nt=0),
        out_shape=(jax.ShapeDtypeStruct((n, d), BF16), jax.ShapeDtypeStruct((n, d), BF16)),
        grid=(n // tm,),
        in_specs=[pl.BlockSpec((tm, d), row),
                  pl.BlockSpec(mod.shape, lambda i: (0, 0)),
                  pl.BlockSpec((1, d), lambda i: (0, 0)),
                  pl.BlockSpec((d, d), lambda i: (0, group))],
        out_specs=(pl.BlockSpec((tm, d), row), pl.BlockSpec((tm, d), row)),
        compiler_params=_cparams(("parallel",)),
        name="inproj",
    )(x, mod, g, w)


def _inproj_rg(hx, w, col_groups, out_shape, out_spec_of, zx, hf, zgg, gg_spec_of, conv_w,
               conv_b, wg, bg, lam, h0, *, reverse, nb, tt):
    n, d = hx.shape
    rows = tt * nb
    nt = n // rows
    ncb = d // RG_COLS
    gps = GROUPS_PER_STEP
    nsb = RG_COLS // RG_GATE
    assert len(col_groups) == ncb * gps and tt % (RG_SPLIT * gps * d // MM_COLS) == 0
    ppb = rows // nb
    npb = rows // (2 * nb)
    n_next_blocks = n // (2 * nb)

    def tile_of(i):
        return (nt - 1 - i) if reverse else i

    tile_blk = pl.BlockSpec((rows, RG_COLS), lambda i, j: (tile_of(i), j))
    in_specs = [pl.BlockSpec((rows, d), lambda i, j: (tile_of(i), 0))]
    in_specs += [pl.BlockSpec((d, d), _group_map(col_groups[g::gps])) for g in range(gps)]
    if reverse:
        in_specs += [tile_blk]
        args = [hx] + [w] * gps + [zx]
    else:
        in_specs += [
            tile_blk,
            pl.BlockSpec((nb, RG_COLS), lambda i, j: (jnp.maximum(tile_of(i) * ppb - 1, 0), j)),
            pl.BlockSpec((2 * nb, RG_COLS),
                         lambda i, j: (jnp.minimum((tile_of(i) + 1) * npb, n_next_blocks - 1), j)),
            pl.BlockSpec((4, RG_COLS), lambda i, j: (0, j)),
        ]
        args = [hx] + [w] * gps + [zx, zx, zx, conv_w]
    in_specs += [
        pl.BlockSpec((1, RG_COLS), lambda i, j: (0, j)),
        pl.BlockSpec((nsb, RG_GATE, 2 * RG_GATE), lambda i, j: (j, 0, 0)),
        pl.BlockSpec((nsb, 1, 2 * RG_GATE), lambda i, j: (j, 0, 0)),
        pl.BlockSpec((1, RG_COLS), lambda i, j: (0, j)),
        pl.BlockSpec((ncb, nb, RG_COLS), lambda i, j: (0, 0, 0)),
    ]
    args += [conv_b, wg, bg, lam, h0]
    state_shape = jax.ShapeDtypeStruct((ncb, nb, RG_COLS), F32)
    state_blk = pl.BlockSpec((ncb, nb, RG_COLS), lambda i, j: (0, 0, 0))
    if reverse:
        in_specs += [tile_blk, gg_spec_of(tile_of)]
        args += [hf, zgg]
        out_shape_all = (out_shape, jax.ShapeDtypeStruct((n, d), BF16), state_shape)
        out_specs = (out_spec_of(tile_of), tile_blk, state_blk)
    else:
        out_shape_all = (out_shape, jax.ShapeDtypeStruct((n, d), F32),
                         jax.ShapeDtypeStruct((n, d), BF16), state_shape)
        out_specs = (out_spec_of(tile_of), tile_blk, tile_blk, state_blk)
    return pl.pallas_call(
        functools.partial(_inproj_kernel, nb=nb, d=d, rg=True, reverse=reverse, tt=tt, nt=nt),
        out_shape=out_shape_all,
        grid=(nt, ncb),
        in_specs=in_specs,
        out_specs=out_specs,
        scratch_shapes=[pltpu.VMEM((ncb, nb, RG_COLS), F32)],
        compiler_params=_cparams(("arbitrary", "arbitrary")),
        name="inproj_rg_bwd" if reverse else "inproj_rg_fwd",
    )(*args)


def _cumsum_time(x, reverse, block):
    c = x.shape[0]
    x4 = x.reshape((c // block, block) + x.shape[1:])
    cols = [None] * block
    order = range(block - 1, -1, -1) if reverse else range(block)
    run = None
    for i in order:
        run = x4[:, i] if run is None else run + x4[:, i]
        cols[i] = run
    return jnp.stack(cols, axis=1).reshape(x.shape)


def _cumsum_chunk(x, reverse):
    c = x.shape[0]
    blk = 8
    x4 = _cumsum_time(x, reverse, blk).reshape((c // blk, blk) + x.shape[1:])
    nblk = c // blk
    tot = x4[:, 0] if reverse else x4[:, blk - 1]
    offs = [None] * nblk
    order = range(nblk - 1, -1, -1) if reverse else range(nblk)
    run = None
    for i in order:
        offs[i] = run
        run = tot[i] if run is None else run + tot[i]
    first = nblk - 1 if reverse else 0
    parts = [x4[i] if i == first else x4[i] + offs[i][None] for i in range(nblk)]
    return jnp.stack(parts, axis=0).reshape(x.shape)


def _pivot_time(c, block, p):
    c4 = c.reshape((c.shape[0] // block, block) + c.shape[1:])
    return jnp.broadcast_to(c4[:, p:p + 1], c4.shape).reshape(c.shape)


def _hg_levels(c):
    out, m = [], c // 2
    while m >= HG_DIAG:
        out.append(m)
        m //= 2
    return out


def _hg_kernel(*refs, reverse, final):
    if final:
        (zq_ref, zf_ref, v_ref, of_ref, og_ref, par_ref, s0_ref, y_ref, sout_ref,
         st_scr, tr_scr, o_scr, mm_scr) = refs
    else:
        (zq_ref, zf_ref, v_ref, par_ref, s0_ref, y_ref, sout_ref, st_scr, tr_scr, o_scr,
         mm_scr) = refs
    j = pl.program_id(1)
    nj = pl.num_programs(1)
    c, nb, kd = zq_ref.shape

    @pl.when(j == 0)
    def _():
        st_scr[...] = s0_ref[...]

    lb = par_ref[0:1, :][None]
    oml = par_ref[1:2, :][None]
    blk = HG_DIAG
    nblk = c // blk
    pad = jnp.zeros((blk, HG_PITCH - nb, kd), F32)
    run = None
    for i in (range(nblk - 1, -1, -1) if reverse else range(nblk)):
        sl = slice(i * blk, (i + 1) * blk)
        zq = zq_ref[sl].astype(F32)
        zf = zf_ref[sl].astype(F32)
        q = _silu(zq)
        e = jnp.exp(-jnp.abs(zf))
        r = 1.0 / (1.0 + e)
        pos = zf >= 0.0
        f = lb + oml * (jnp.where(pos, 1.0, e) * r)
        k = oml * (jnp.where(pos, e, 1.0) * r)
        g = jnp.log2(jnp.maximum(f, HG_F_FLOOR))
        loc = _cumsum_time(g, reverse, blk)
        cd = _cumsum_time(jnp.maximum(g, -HG_DIAG_CLAMP * LOG2_E), reverse, blk)
        cs = loc if run is None else loc + run[None]
        run = cs[0] if reverse else cs[blk - 1]
        rows = slice(i * blk * HG_PITCH, (i + 1) * blk * HG_PITCH)
        for n, arr in enumerate((q, k, cs, cd, v_ref[sl].astype(F32))):
            tr_scr[n, rows, :] = jnp.concatenate([arr, pad], axis=1).reshape(blk * HG_PITCH, kd)
    ctot = run
    decay = jnp.exp2(ctot)

    ti = lax.broadcasted_iota(jnp.int32, (c, c), 0)
    si = lax.broadcasted_iota(jnp.int32, (c, c), 1)
    lev = ti ^ si
    valid = (ti <= si) if reverse else (ti >= si)
    nt_dims = (((1,), (1,)), ((), ()))
    tn_dims = (((0,), (0,)), ((), ()))
    levels = _hg_levels(c)

    for b in range(nb):
        rows = pl.ds(b, c, stride=HG_PITCH)
        q2, k2, cs2, cd2 = (tr_scr[n, rows, :] for n in range(4))
        qb = q2.astype(BF16)
        kb = k2.astype(BF16)
        mm_scr[0, b] = qb * jnp.exp2(cs2).astype(BF16)
        mm_scr[1, b] = kb * jnp.exp2(ctot[b:b + 1, :] - cs2).astype(BF16)
        mm_scr[2, b] = tr_scr[4, rows, :].astype(BF16)
        cdd = cd2 - _pivot_time(cd2, HG_DIAG, HG_DIAG // 2 if reverse else HG_DIAG // 2 - 1)
        a = lax.dot_general(qb * jnp.exp2(cdd).astype(BF16), kb * jnp.exp2(-cdd).astype(BF16),
                            nt_dims, preferred_element_type=F32)
        for m in levels[::-1]:
            piv = m if reverse else m - 1
            el = jnp.exp2(-jnp.abs((cs2 - _pivot_time(cs2, 2 * m, piv)).astype(BF16)))
            p = lax.dot_general(qb * el, kb * el, nt_dims, preferred_element_type=F32)
            a = jnp.where(lev < m, a, p)
        mm_scr[3, b] = jnp.where(valid, a, 0.0).astype(BF16)

    decay_t = jnp.transpose(decay)
    for b in range(nb):
        st = st_scr[b]
        vb = mm_scr[2, b]
        o = jnp.dot(jnp.concatenate([mm_scr[0, b], mm_scr[3, b]], axis=1),
                    jnp.concatenate([st.astype(BF16), vb], axis=0), preferred_element_type=F32)
        st_scr[b] = st * decay_t[:, b:b + 1] + lax.dot_general(mm_scr[1, b], vb, tn_dims,
                                                               preferred_element_type=F32)
        o_scr[pl.ds(b, c, stride=HG_PITCH), :] = o

    o = o_scr[...].reshape(c, HG_PITCH, kd)[:, 0:nb, :]
    if final:
        gain = par_ref[3:4, :][None]
        y = _rms(o + of_ref[...], gain) * _silu(og_ref[...].astype(F32))
        y_ref[...] = y.astype(y_ref.dtype)
    else:
        y_ref[...] = o

    @pl.when(j == nj - 1)
    def _():
        sout_ref[...] = st_scr[...]


def _hg_dir(za, zb, of, par, s0, *, reverse, d):
    t, nb, _ = za.shape
    kd = HG_EXPAND
    heads = d // kd
    nj = t // HG_CHUNK
    final = reverse
    assert nb <= HG_PITCH

    def blk(j):
        return (nj - 1 - j) if reverse else j

    def zspec(gidx):
        return pl.BlockSpec((HG_CHUNK, nb, kd), lambda h, j: (blk(j), 0, gidx * heads + h))

    hspec = pl.BlockSpec((HG_CHUNK, nb, kd), lambda h, j: (blk(j), 0, h))
    sspec = pl.BlockSpec((nb, None, kd, kd), lambda h, j: (0, h, 0, 0))
    in_specs = [zspec(1), zspec(3 if reverse else 2), zspec(0)]
    args = [za, za, zb]
    scratch = [pltpu.VMEM((nb, kd, kd), F32), pltpu.VMEM((5, HG_CHUNK * HG_PITCH, kd), F32),
               pltpu.VMEM((HG_CHUNK * HG_PITCH, kd), F32),
               pltpu.VMEM((4, nb, HG_CHUNK, kd), BF16)]
    if final:
        in_specs += [hspec, zspec(1)]
        args += [of, zb]
    in_specs += [pl.BlockSpec((par.shape[0], kd), lambda h, j: (0, h)), sspec]
    args += [par, s0]
    return pl.pallas_call(
        functools.partial(_hg_kernel, reverse=reverse, final=final),
        out_shape=(jax.ShapeDtypeStruct((t, nb, d), BF16 if final else F32),
                   jax.ShapeDtypeStruct(s0.shape, F32)),
        grid=(heads, nj),
        in_specs=in_specs,
        out_specs=(hspec, sspec),
        scratch_shapes=scratch,
        compiler_params=_cparams(("parallel", "arbitrary")),
        name="hg_bwd" if reverse else "hg_fwd",
    )(*args)


def _merge_kernel(yrg_ref, yhg_ref, ga_ref, gb_ref, x_ref, mod_ref, g_ref, wr_ref, wh_ref, wo_ref,
                  o_ref, *, nb, d):
    tm = x_ref.shape[0]
    p_rg = jnp.dot(yrg_ref[...], wr_ref[...], preferred_element_type=F32)
    p_hg = jnp.dot(yhg_ref[...].reshape(tm, d), wh_ref[...], preferred_element_type=F32)
    ga = ga_ref[...].reshape(tm, d).astype(F32)
    gb = gb_ref[...].reshape(tm, d).astype(F32)
    m = _sigmoid(ga) * p_rg + _sigmoid(gb) * p_hg
    y = jnp.dot(m.astype(BF16), wo_ref[...], preferred_element_type=F32)
    n3, gate = _per_batch(_rms(y, g_ref[...]), mod_ref[:, 2 * d:3 * d], nb)
    o_ref[...] = x_ref[...] + (gate * n3).reshape(tm, d)


def _merge(yrg, yhg, zb, tile_spec, x, mod, g, wr, wh, wo, *, tm, nb):
    n, d = x.shape
    row = lambda i: (i, 0)
    const = lambda i: (0, 0)
    return pl.pallas_call(
        functools.partial(_merge_kernel, nb=nb, d=d),
        out_shape=jax.ShapeDtypeStruct((n, d), F32),
        grid=(n // tm,),
        in_specs=[pl.BlockSpec((tm, d), row), tile_spec(0), tile_spec(2), tile_spec(3),
                  pl.BlockSpec((tm, d), row), pl.BlockSpec(mod.shape, const),
                  pl.BlockSpec((1, d), const), pl.BlockSpec((d, d), const),
                  pl.BlockSpec((d, d), const), pl.BlockSpec((d, d), const)],
        out_specs=pl.BlockSpec((tm, d), row),
        compiler_params=_cparams(("parallel",)),
        name="merge",
    )(yrg, yhg, zb, zb, x, mod, g, wr, wh, wo)


def _ffn_kernel(*refs, nb, d, dff, tf, halo, nt):
    if halo:
        (x_ref, xp_ref, xn_ref, mod_ref, g1_ref, g2_ref, wu_ref, cw_ref, cb_ref, wd_ref,
         o_ref, act_scr) = refs
    else:
        (x_ref, mod_ref, g1_ref, g2_ref, wu_ref, cw_ref, cb_ref, wd_ref, o_ref, act_scr) = refs
    i = pl.program_id(0)
    tm = x_ref.shape[0]
    shift = mod_ref[:, 3 * d:4 * d]
    scale = mod_ref[:, 4 * d:5 * d]

    hm = _norm_mod(x_ref[...], g1_ref[...], shift, scale, nb).astype(BF16)
    if halo:
        hp = _norm_mod(xp_ref[...], g1_ref[...], shift, scale, nb)
        hn = _norm_mod(xn_ref[...], g1_ref[...], shift, scale, nb)
        hp = jnp.where(i > 0, hp, 0.0).astype(BF16)
        hn = jnp.where(i < nt - 1, hn, 0.0).astype(BF16)
        he = jnp.concatenate([hp, hm, hn], axis=0)

    for kb in range(dff // tf):
        gs = slice(kb * tf, (kb + 1) * tf)
        vs = slice(dff + kb * tf, dff + (kb + 1) * tf)
        if halo:
            ue = jnp.dot(he, wu_ref[:, gs], preferred_element_type=F32)
        else:
            u = jnp.dot(hm, wu_ref[:, gs], preferred_element_type=F32)
            z = jnp.zeros((nb, tf), F32)
            ue = jnp.concatenate([z, u, z], axis=0)
        gc = cb_ref[:, gs] + ue[0:tm] * cw_ref[0:1, gs] + ue[nb:nb + tm] * cw_ref[1:2, gs] \
            + ue[2 * nb:2 * nb + tm] * cw_ref[2:3, gs]
        uv = jnp.dot(hm, wu_ref[:, vs], preferred_element_type=F32)
        act_scr[:, gs] = (_silu(gc) * uv).astype(BF16)

    y = jnp.dot(act_scr[...], wd_ref[...], preferred_element_type=F32)
    n3, gate = _per_batch(_rms(y, g2_ref[...]), mod_ref[:, 5 * d:6 * d], nb)
    o_ref[...] = x_ref[...] + (gate * n3).reshape(tm, d)


def _ffn(x, mod, g1, g2, w_up, cw, cb, w_down, *, tm, tf, nb, halo):
    n, d = x.shape
    dff = w_down.shape[0]
    nt = n // tm
    hb = tm // nb
    row = lambda i: (i, 0)
    const = lambda i: (0, 0)

    def resident(shape):
        return pl.BlockSpec(shape, const, pipeline_mode=pl.Buffered(1))

    in_specs = [pl.BlockSpec((tm, d), row)]
    args = [x]
    if halo:
        in_specs += [pl.BlockSpec((nb, d), lambda i: (jnp.maximum(i * hb - 1, 0), 0)),
                     pl.BlockSpec((nb, d), lambda i: (jnp.minimum((i + 1) * hb, n // nb - 1), 0))]
        args += [x, x]
    in_specs += [pl.BlockSpec(mod.shape, const), pl.BlockSpec((1, d), const),
                 pl.BlockSpec((1, d), const), resident(w_up.shape),
                 pl.BlockSpec(cw.shape, const), pl.BlockSpec(cb.shape, const),
                 resident(w_down.shape)]
    args += [mod, g1, g2, w_up, cw, cb, w_down]
    return pl.pallas_call(
        functools.partial(_ffn_kernel, nb=nb, d=d, dff=dff, tf=tf, halo=halo, nt=nt),
        out_shape=jax.ShapeDtypeStruct((n, d), F32),
        grid=(nt,),
        in_specs=in_specs,
        out_specs=pl.BlockSpec((tm, d), row),
        scratch_shapes=[pltpu.VMEM((tm, dff), BF16)],
        compiler_params=_cparams(("parallel",)),
        name="ffn_ctx" if halo else "ffn",
    )(*args)


def _gate_weights(wa, wx):
    nd, heads, hd, _ = wa.shape
    per = RG_GATE // hd
    ncb = heads // per

    def bd(w):
        w = w.reshape(nd, ncb, per, hd, hd)
        eye = jnp.eye(per, dtype=w.dtype)
        return jnp.einsum('dcpij,pq->dcpiqj', w, eye).reshape(nd, ncb, RG_GATE, RG_GATE)

    return (0.5 * jnp.concatenate([bd(wa), bd(wx)], axis=-1)).astype(BF16)


def _forward(x, c, ctx, c_ctx, w_ada, b_ada, g_pre_mix, g_post_mix, g_pre_ffn, g_post_ffn, w_in,
             rg_conv_w, rg_conv_b, rg_wa, rg_ba, rg_wx, rg_bx, rg_lam, hg_lb_logits, hg_out_norm,
             w_proj_rg, w_proj_hg, w_out, ffn_w_up, ffn_conv_w, ffn_conv_b, ffn_w_down, *, grid_w):
    nb, seq, d = x.shape
    ctx_len = ctx.shape[1]
    depth = w_in.shape[0]
    rows_g = seq // grid_w
    dff = ffn_w_down.shape[1]
    tile = grid_w * nb
    heads = d // HG_EXPAND
    ncb = d // RG_COLS
    tf = 256 if dff % 256 == 0 else dff

    p = jax.nn.softmax(hg_lb_logits.astype(F32), axis=0)
    cum = jnp.cumsum(p, axis=0)
    lb_all = cum - cum[0:1]
    hg_par = jnp.stack([lb_all, 1.0 - lb_all, jnp.zeros_like(lb_all), hg_out_norm], axis=1)
    hg_par = jnp.pad(hg_par, ((0, 0), (0, 4), (0, 0)))

    w_ada_b, w_in_b = w_ada.astype(BF16), w_in.astype(BF16)
    wr_b, wh_b, wo_b = w_proj_rg.astype(BF16), w_proj_hg.astype(BF16), w_out.astype(BF16)
    wup_b, wdn_b = ffn_w_up.astype(BF16), ffn_w_down.astype(BF16)

    cc = jnp.concatenate([c, jnp.broadcast_to(c_ctx[None], (nb, d))], axis=0)
    mod_all = _ada_mod(cc, w_ada_b, b_ada).reshape(depth, 2, nb, N_MOD * d)

    xl = jnp.swapaxes(x, 0, 1).reshape(seq * nb, d)
    xc = jnp.swapaxes(ctx, 0, 1).reshape(ctx_len * nb, d)

    for l in range(depth):
        need_ctx = l < depth - 1
        mod_l, mod_c = mod_all[l, 0], mod_all[l, 1]
        gpm = g_pre_mix[l].reshape(1, d)
        gpo = g_post_mix[l].reshape(1, d)
        wg = _gate_weights(rg_wa[l], rg_wx[l])
        bg = 0.5 * jnp.concatenate([rg_ba[l].reshape(2, d // RG_GATE, 1, RG_GATE),
                                    rg_bx[l].reshape(2, d // RG_GATE, 1, RG_GATE)], axis=-1)
        rg_kw = dict(conv_w=rg_conv_w[l], conv_b=rg_conv_b[l].reshape(1, d), nb=nb, tt=grid_w)

        st_rg = [jnp.zeros((ncb, nb, RG_COLS), F32)] * 2
        st_hg = [jnp.zeros((nb, heads, HG_EXPAND, HG_EXPAND), F32)] * 2
        ctx_out = None
        for xs, mod_s, steps_s, latent in ((xc, mod_c, ctx_len, False), (xl, mod_l, seq, True)):
            if latent:
                shape4 = jax.ShapeDtypeStruct((grid_w, rows_g, nb, 4 * d), BF16)
                blk = lambda w, col: pl.BlockSpec((grid_w, None, nb, w), col)
                spec_of = lambda tile_of: blk(GROUPS_PER_STEP * d, lambda i, j: (0, tile_of(i), 0, j))
                gg_of = lambda tile_of: blk(RG_COLS, lambda i, j: (0, tile_of(i), 0, j))
                tile_spec = lambda gidx: blk(d, lambda i: (0, i, 0, gidx))
            else:
                shape4 = jax.ShapeDtypeStruct((steps_s, nb, 4 * d), BF16)
                blk = lambda w, col: pl.BlockSpec((grid_w, nb, w), col)
                spec_of = lambda tile_of: blk(GROUPS_PER_STEP * d, lambda i, j: (tile_of(i), 0, j))
                gg_of = lambda tile_of: blk(RG_COLS, lambda i, j: (tile_of(i), 0, j))
                tile_spec = lambda gidx: blk(d, lambda i: (i, 0, gidx))
            zx, hx = _inproj(xs, mod_s, gpm, w_in_b[l], 0, tile, nb)
            za, hf, xlc, st_rg[0] = _inproj_rg(
                hx, w_in_b[l], (1, 2, 3, 4), shape4, spec_of, zx, None, None, None,
                wg=wg[0], bg=bg[0], lam=rg_lam[l, 0:1], h0=st_rg[0], reverse=False, **rg_kw)
            zb, yrg, st_rg[1] = _inproj_rg(
                hx, w_in_b[l], (5, 6, 7, 8), shape4, spec_of, xlc, hf, za, gg_of,
                wg=wg[1], bg=bg[1], lam=rg_lam[l, 1:2], h0=st_rg[1], reverse=True, **rg_kw)
            za3 = za.reshape(steps_s, nb, 4 * d)
            zb3 = zb.reshape(steps_s, nb, 4 * d)
            of, st_hg[0] = _hg_dir(za3, zb3, None, hg_par[l], st_hg[0], reverse=False, d=d)
            yhg, st_hg[1] = _hg_dir(za3, zb3, of, hg_par[l], st_hg[1], reverse=True, d=d)
            if latent or need_ctx:
                yhg = yhg.reshape(shape4.shape[:-1] + (d,))
                out = _merge(yrg, yhg, zb, tile_spec, xs, mod_s, gpo, wr_b[l], wh_b[l], wo_b[l],
                             tm=tile, nb=nb)
                if latent:
                    xl = out
                else:
                    ctx_out = out
        ffn = functools.partial(_ffn, g1=g_pre_ffn[l].reshape(1, d), g2=g_post_ffn[l].reshape(1, d),
                                w_up=wup_b[l], cw=ffn_conv_w[l], cb=ffn_conv_b[l].reshape(1, dff),
                                w_down=wdn_b[l], tm=tile, tf=tf, nb=nb)
        xl = ffn(xl, mod_l, halo=False)
        if need_ctx:
            xc = ffn(ctx_out, mod_c, halo=True)

    return jnp.swapaxes(xl.reshape(seq, nb, d), 0, 1)


def kernel(x, c, ctx, c_ctx, w_ada, b_ada, g_pre_mix, g_post_mix, g_pre_ffn, g_post_ffn, w_in, rg_conv_w, rg_conv_b, rg_wa, rg_ba, rg_wx, rg_bx, rg_lam, hg_lb_logits, hg_out_norm, w_proj_rg, w_proj_hg, w_out, ffn_w_up, ffn_conv_w, ffn_conv_b, ffn_w_down):
    return _forward(x, c, ctx, c_ctx, w_ada, b_ada, g_pre_mix, g_post_mix, g_pre_ffn, g_post_ffn,
                    w_in, rg_conv_w, rg_conv_b, rg_wa, rg_ba, rg_wx, rg_bx, rg_lam, hg_lb_logits,
                    hg_out_norm, w_proj_rg, w_proj_hg, w_out, ffn_w_up, ffn_conv_w, ffn_conv_b,
                    ffn_w_down, grid_w=GRID_W)
```
